```python
import math
import jax, jax.numpy as jnp
from jax import lax
import numpy as np

D_MODEL = 1024
BATCH = 4
SEQ = 4096
DEPTH = 1

SSM_WIDTH = D_MODEL // 2
SSM_GROUP = 16
SSM_GROUPS = SSM_WIDTH // SSM_GROUP
SSM_STATE = 64
NSA_HEADS = 8
NSA_KV_GROUPS = 2
NSA_HEAD_DIM = (D_MODEL // 2) // NSA_HEADS
NSA_WIDTH = NSA_HEADS * NSA_HEAD_DIM
KV_WIDTH = NSA_KV_GROUPS * NSA_HEAD_DIM
CMP_BLOCK = 32
CMP_STRIDE = 16
CMP_HIDDEN = 2 * NSA_HEAD_DIM
SEL_BLOCK = 64
SEL_TOPK = 16
WINDOW = 512
Q_BLOCK = 128
FORCE_BONUS = 1.0e4
PEER_HEADS = 8
PEER_NKEYS = 128
PEER_EXPERTS = PEER_NKEYS * PEER_NKEYS
PEER_KEY_DIM = 128
PEER_TOPK = 16
PEER_CHUNK = 128
IN_COLS = SSM_WIDTH + NSA_WIDTH + 6 * KV_WIDTH + 3 * NSA_HEADS + 2 * D_MODEL
DN_ALPHA = (2.0 * DEPTH) ** 0.25
DN_BETA = (8.0 * DEPTH) ** -0.25
LN_EPS = 1e-5
NEG = -1e30

kernel_name = "hybrid_s5_nsa_peer_deepnorm"


def layer_norm(x, g, b):
    xf = x.astype(jnp.float32)
    mu = xf.mean(-1, keepdims=True)
    var = jnp.square(xf - mu).mean(-1, keepdims=True)
    return ((xf - mu) * lax.rsqrt(var + LN_EPS) * g.astype(jnp.float32) + b.astype(jnp.float32)).astype(x.dtype)


def masked_softmax(s, mask):
    s = jnp.where(mask, s.astype(jnp.float32), NEG)
    m = jnp.max(s, axis=-1, keepdims=True)
    p = jnp.exp(s - m) * mask
    return p / jnp.maximum(p.sum(-1, keepdims=True), 1e-30)


def alibi_slopes():
    return jnp.exp2(-8.0 * jnp.arange(1, NSA_HEADS + 1, dtype=jnp.float32) / NSA_HEADS)


def s5_mixer(u, a_re, a_im, log_dt, b_re, b_im, c_re, c_im, d_skip, w_glu):
    Bsz, S, _ = u.shape
    f32 = jnp.float32
    uf = u.astype(f32).reshape(Bsz, S, SSM_GROUPS, SSM_GROUP)
    lam = lax.complex(a_re.astype(f32), a_im.astype(f32))
    dt = jnp.exp(log_dt.astype(f32))[:, None]
    lam_bar = jnp.exp(lam * dt)
    b = lax.complex(b_re.astype(f32), b_im.astype(f32))
    b_bar = ((lam_bar - 1.0) / lam)[:, :, None] * b
    bu = lax.complex(jnp.einsum('bsgh,gph->sbgp', uf, jnp.real(b_bar)),
                     jnp.einsum('bsgh,gph->sbgp', uf, jnp.imag(b_bar)))
    a = jnp.broadcast_to(lam_bar[None, None], (S, 1) + lam_bar.shape)

    def combine(left, right):
        a_l, b_l = left
        a_r, b_r = right
        return a_r * a_l, a_r * b_l + b_r

    _, xs = lax.associative_scan(combine, (a, bu), axis=0)
    y = (jnp.einsum('sbgp,ghp->bsgh', jnp.real(xs), c_re.astype(f32))
         - jnp.einsum('sbgp,ghp->bsgh', jnp.imag(xs), c_im.astype(f32)))
    y = y.reshape(Bsz, S, SSM_WIDTH) + d_skip.astype(f32) * uf.reshape(Bsz, S, SSM_WIDTH)
    y = jax.nn.gelu(y)
    ga, gb = jnp.split(y @ w_glu.astype(f32), 2, axis=-1)
    return (ga * jax.nn.sigmoid(gb)).astype(u.dtype)


def compress(kv, pe, w1, w2):
    S = kv.shape[2]
    n_cmp = (S - CMP_BLOCK) // CMP_STRIDE + 1
    pos = jnp.arange(n_cmp)[:, None] * CMP_STRIDE + jnp.arange(CMP_BLOCK)[None, :]
    blocks = kv[:, :, pos] + pe
    flat = blocks.reshape(blocks.shape[:3] + (CMP_BLOCK * NSA_HEAD_DIM,))
    return jax.nn.gelu(flat @ w1) @ w2


def nsa_mixer(q, k_cmp, v_cmp, k_slc, v_slc, k_win, v_win, gates,
              pe_k, w1_k, w2_k, pe_v, w1_v, w2_v):
    Bsz, S, _ = q.shape
    G, R, dk = NSA_KV_GROUPS, NSA_HEADS // NSA_KV_GROUPS, NSA_HEAD_DIM
    f32 = jnp.float32
    qh = q.reshape(Bsz, S, G, R, dk).transpose(0, 2, 3, 1, 4) * (dk ** -0.5)

    def kvh(t):
        return t.reshape(Bsz, S, G, dk).transpose(0, 2, 1, 3)

    gate = jax.nn.sigmoid(gates.astype(f32)).reshape(Bsz, S, G, R, 3).transpose(0, 2, 3, 1, 4)
    slopes = alibi_slopes().reshape(G, R)[None, :, :, None, None]

    kc = compress(kvh(k_cmp), pe_k, w1_k, w2_k)
    vc = compress(kvh(v_cmp), pe_v, w1_v, w2_v)
    n_cmp = kc.shape[2]
    c_start = jnp.arange(n_cmp) * CMP_STRIDE
    c_end = c_start + CMP_BLOCK - 1
    n_slc = S // SEL_BLOCK
    s_start = jnp.arange(n_slc) * SEL_BLOCK
    overlap = ((c_start[:, None] < s_start[None, :] + SEL_BLOCK)
               & (c_start[:, None] + CMP_BLOCK > s_start[None, :])).astype(f32)
    n_pick = min(SEL_TOPK, n_slc)
    ks_blk = kvh(k_slc).reshape(Bsz, G, n_slc, SEL_BLOCK, dk)
    vs_blk = kvh(v_slc).reshape(Bsz, G, n_slc, SEL_BLOCK, dk)
    pad = ((0, 0), (0, 0), (WINDOW, 0), (0, 0))
    kw_pad = jnp.pad(kvh(k_win), pad)
    vw_pad = jnp.pad(kvh(v_win), pad)
    gather = jax.vmap(jax.vmap(lambda blk, ix: blk[ix]))
    j_idx = jnp.arange(n_slc)

    def query_block(i):
        qs = i * Q_BLOCK
        t = qs + jnp.arange(Q_BLOCK)
        qb = lax.dynamic_slice_in_dim(qh, qs, Q_BLOCK, axis=3)
        gb = lax.dynamic_slice_in_dim(gate, qs, Q_BLOCK, axis=3)
        s_c = jnp.einsum('bgrqd,bgcd->bgrqc', qb, kc)
        dist_c = (t[:, None] - c_end[None, :]).astype(f32)
        p_c = masked_softmax(s_c - slopes * dist_c, dist_c >= 0)
        o_c = jnp.einsum('bgrqc,bgcd->bgrqd', p_c.astype(vc.dtype), vc)
        imp = jnp.einsum('bgrqc,cj->bgqj', p_c, overlap)
        cur = t // SEL_BLOCK
        forced = (j_idx[None, :] == 0) | (j_idx[None, :] == cur[:, None]) | (j_idx[None, :] == cur[:, None] - 1)
        valid = s_start[None, :] <= t[:, None]
        score = jnp.where(valid, imp + forced * FORCE_BONUS, NEG)
        _, idx = lax.top_k(score, n_pick)
        ks = gather(ks_blk, idx).reshape(Bsz, G, Q_BLOCK, n_pick * SEL_BLOCK, dk)
        vs = gather(vs_blk, idx).reshape(Bsz, G, Q_BLOCK, n_pick * SEL_BLOCK, dk)
        kpos = (idx[..., None] * SEL_BLOCK + jnp.arange(SEL_BLOCK)).reshape(Bsz, G, Q_BLOCK, n_pick * SEL_BLOCK)
        dist_s = (t[None, None, :, None] - kpos).astype(f32)[:, :, None]
        s_s = jnp.einsum('bgrqd,bgqkd->bgrqk', qb, ks)
        p_s = masked_softmax(s_s - slopes * dist_s, dist_s >= 0)
        o_s = jnp.einsum('bgrqk,bgqkd->bgrqd', p_s.astype(vs.dtype), vs)
        kw = lax.dynamic_slice_in_dim(kw_pad, qs, Q_BLOCK + WINDOW, axis=2)
        vw = lax.dynamic_slice_in_dim(vw_pad, qs, Q_BLOCK + WINDOW, axis=2)
        wpos = qs - WINDOW + jnp.arange(Q_BLOCK + WINDOW)
        dist_w = t[:, None] - wpos[None, :]
        mask_w = (dist_w >= 0) & (dist_w < WINDOW) & (wpos[None, :] >= 0)
        s_w = jnp.einsum('bgrqd,bgkd->bgrqk', qb, kw)
        p_w = masked_softmax(s_w - slopes * dist_w.astype(f32), mask_w)
        o_w = jnp.einsum('bgrqk,bgkd->bgrqd', p_w.astype(vw.dtype), vw)
        return gb[..., 0:1] * o_c + gb[..., 1:2] * o_s + gb[..., 2:3] * o_w

    o = lax.map(query_block, jnp.arange(S // Q_BLOCK))
    return o.transpose(1, 0, 4, 2, 3, 5).reshape(Bsz, S, NSA_WIDTH).astype(q.dtype)


def peer_ffn(x, w_q, subkeys, u_tab, v_tab):
    Bsz, S, D = x.shape
    xt = x.reshape((Bsz * S) // PEER_CHUNK, PEER_CHUNK, D)
    half = PEER_KEY_DIM // 2

    def chunk(xc):
        q = (xc @ w_q).reshape(PEER_CHUNK, PEER_HEADS, 2, half)
        s = jnp.einsum('thcd,hckd->thck', q, subkeys).astype(jnp.float32)
        sv, si = lax.top_k(s, PEER_TOPK)
        cand = sv[..., 0, :, None] + sv[..., 1, None, :]
        cidx = si[..., 0, :, None] * PEER_NKEYS + si[..., 1, None, :]
        best, pos = lax.top_k(cand.reshape(PEER_CHUNK, PEER_HEADS, -1), PEER_TOPK)
        experts = jnp.take_along_axis(cidx.reshape(PEER_CHUNK, PEER_HEADS, -1), pos, axis=-1)
        g = jax.nn.softmax(best, axis=-1)
        u = u_tab[experts]
        v = v_tab[experts]
        h = jax.nn.gelu(jnp.einsum('thkd,td->thk', u, xc).astype(jnp.float32))
        return jnp.einsum('thk,thkd->td', (g * h).astype(v.dtype), v)

    return lax.map(chunk, xt).reshape(Bsz, S, D).astype(x.dtype)


def hybrid_layer(x, w_in, b_in, ssm_a_re, ssm_a_im, ssm_log_dt, ssm_b_re, ssm_b_im,
                 ssm_c_re, ssm_c_im, ssm_d, w_glu, w_up_ssm,
                 nsa_pe_k, nsa_w1_k, nsa_w2_k, nsa_pe_v, nsa_w1_v, nsa_w2_v, w_up_nsa,
                 w_out, ln1_g, ln1_b, peer_w_q, peer_subkeys, peer_u, peer_v, ln2_g, ln2_b):
    sizes = [SSM_WIDTH, NSA_WIDTH] + [KV_WIDTH] * 6 + [3 * NSA_HEADS, 2 * D_MODEL]
    offsets = np.cumsum(sizes)[:-1].tolist()
    proj = x @ w_in + b_in
    u_ssm, q, k_cmp, v_cmp, k_slc, v_slc, k_win, v_win, nsa_gates, merge_logits = jnp.split(proj, offsets, axis=-1)
    y_a = s5_mixer(u_ssm, ssm_a_re, ssm_a_im, ssm_log_dt, ssm_b_re, ssm_b_im,
                   ssm_c_re, ssm_c_im, ssm_d, w_glu) @ w_up_ssm
    y_b = nsa_mixer(q, k_cmp, v_cmp, k_slc, v_slc, k_win, v_win, nsa_gates,
                    nsa_pe_k, nsa_w1_k, nsa_w2_k, nsa_pe_v, nsa_w1_v, nsa_w2_v) @ w_up_nsa
    g_a, g_b = jnp.split(jax.nn.sigmoid(merge_logits), 2, axis=-1)
    mix = (g_a * y_a + g_b * y_b) @ w_out
    x = layer_norm(DN_ALPHA * x + mix, ln1_g, ln1_b)
    y = peer_ffn(x, peer_w_q, peer_subkeys, peer_u, peer_v)
    return layer_norm(DN_ALPHA * x + y, ln2_g, ln2_b)


def setup_inputs(seed: int = 0) -> dict:
    key = jax.random.key(seed)
    ks = jax.random.split(key, 32)
    f32 = jnp.float32

    def nrm(k, shape, scale):
        return jax.random.normal(k, shape, f32) * scale

    L = DEPTH
    P, G, H = SSM_STATE, SSM_GROUPS, SSM_GROUP
    ldk = CMP_BLOCK * NSA_HEAD_DIM
    return {
        "x": nrm(ks[0], (BATCH, SEQ, D_MODEL), 1.0),
        "w_in": nrm(ks[1], (L, D_MODEL, IN_COLS), D_MODEL ** -0.5),
        "b_in": nrm(ks[2], (L, IN_COLS), 0.02),
        "ssm_a_re": -0.5 + nrm(ks[3], (L, G, P), 0.01),
        "ssm_a_im": jnp.pi * jnp.arange(P, dtype=f32)[None, None, :] + nrm(ks[4], (L, G, P), 0.01),
        "ssm_log_dt": jax.random.uniform(ks[5], (L, G), f32, math.log(1e-3), math.log(1e-1)),
        "ssm_b_re": nrm(ks[6], (L, G, P, H), (2.0 * H) ** -0.5),
        "ssm_b_im": nrm(ks[7], (L, G, P, H), (2.0 * H) ** -0.5),
        "ssm_c_re": nrm(ks[8], (L, G, H, P), P ** -0.5),
        "ssm_c_im": nrm(ks[9], (L, G, H, P), P ** -0.5),
        "ssm_d": nrm(ks[10], (L, SSM_WIDTH), 1.0),
        "w_glu": nrm(ks[11], (L, SSM_WIDTH, 2 * SSM_WIDTH), SSM_WIDTH ** -0.5),
        "w_up_ssm": nrm(ks[12], (L, SSM_WIDTH, D_MODEL), SSM_WIDTH ** -0.5),
        "nsa_pe_k": nrm(ks[13], (L, CMP_BLOCK, NSA_HEAD_DIM), 0.1),
        "nsa_w1_k": nrm(ks[14], (L, ldk, CMP_HIDDEN), ldk ** -0.5),
        "nsa_w2_k": nrm(ks[15], (L, CMP_HIDDEN, NSA_HEAD_DIM), CMP_HIDDEN ** -0.5),
        "nsa_pe_v": nrm(ks[16], (L, CMP_BLOCK, NSA_HEAD_DIM), 0.1),
        "nsa_w1_v": nrm(ks[17], (L, ldk, CMP_HIDDEN), ldk ** -0.5),
        "nsa_w2_v": nrm(ks[18], (L, CMP_HIDDEN, NSA_HEAD_DIM), CMP_HIDDEN ** -0.5),
        "w_up_nsa": nrm(ks[19], (L, NSA_WIDTH, D_MODEL), NSA_WIDTH ** -0.5),
        "w_out": nrm(ks[20], (L, D_MODEL, D_MODEL), DN_BETA * D_MODEL ** -0.5),
        "ln1_g": 1.0 + nrm(ks[21], (L, D_MODEL), 0.05),
        "ln1_b": nrm(ks[22], (L, D_MODEL), 0.02),
        "peer_w_q": nrm(ks[23], (L, D_MODEL, PEER_HEADS * PEER_KEY_DIM), D_MODEL ** -0.5),
        "peer_subkeys": nrm(ks[24], (L, PEER_HEADS, 2, PEER_NKEYS, PEER_KEY_DIM // 2), (PEER_KEY_DIM // 2) ** -0.5),
        "peer_u": nrm(ks[25], (L, PEER_EXPERTS, D_MODEL), D_MODEL ** -0.5),
        "peer_v": nrm(ks[26], (L, PEER_EXPERTS, D_MODEL), DN_BETA),
        "ln2_g": 1.0 + nrm(ks[27], (L, D_MODEL), 0.05),
        "ln2_b": nrm(ks[28], (L, D_MODEL), 0.02),
    }


def reference(x, w_in, b_in, ssm_a_re, ssm_a_im, ssm_log_dt, ssm_b_re, ssm_b_im,
              ssm_c_re, ssm_c_im, ssm_d, w_glu, w_up_ssm,
              nsa_pe_k, nsa_w1_k, nsa_w2_k, nsa_pe_v, nsa_w1_v, nsa_w2_v, w_up_nsa,
              w_out, ln1_g, ln1_b, peer_w_q, peer_subkeys, peer_u, peer_v, ln2_g, ln2_b):
    for l in range(DEPTH):
        x = hybrid_layer(x, w_in[l], b_in[l], ssm_a_re[l], ssm_a_im[l], ssm_log_dt[l],
                         ssm_b_re[l], ssm_b_im[l], ssm_c_re[l], ssm_c_im[l], ssm_d[l],
                         w_glu[l], w_up_ssm[l],
                         nsa_pe_k[l], nsa_w1_k[l], nsa_w2_k[l], nsa_pe_v[l], nsa_w1_v[l], nsa_w2_v[l],
                         w_up_nsa[l], w_out[l], ln1_g[l], ln1_b[l],
                         peer_w_q[l], peer_subkeys[l], peer_u[l], peer_v[l], ln2_g[l], ln2_b[l])
    return x
```

```python
import functools
import math

import jax
import jax.numpy as jnp
import numpy as np
from jax import lax
from jax.experimental import pallas as pl
from jax.experimental.pallas import tpu as pltpu

F32 = jnp.float32
BF16 = jnp.bfloat16

D_MODEL = 1024
SSM_WIDTH = 512
SSM_GROUP = 16
SSM_GROUPS = 32
SSM_STATE = 64
SSM_CHUNK = 8
LANE_GROUPS = 8
N_LANE_TILES = SSM_WIDTH // 128
NSA_HEADS = 8
NSA_KV_GROUPS = 2
NSA_REP = NSA_HEADS // NSA_KV_GROUPS
NSA_HEAD_DIM = 64
KV_WIDTH = NSA_KV_GROUPS * NSA_HEAD_DIM
CMP_BLOCK = 32
CMP_STRIDE = 16
CMP_HIDDEN = 128
SEL_BLOCK = 64
SEL_TOPK = 16
WINDOW = 512
Q_BLOCK = 128
FORCE_BONUS = 1.0e4
PEER_HEADS = 8
PEER_NKEYS = 128
PEER_EXPERTS = PEER_NKEYS * PEER_NKEYS
PEER_TOPK = 16
DN_ALPHA = 2.0 ** 0.25
LN_EPS = 1e-5
NEG = -1e30
VMEM_LIMIT = 56 * 1024 * 1024


def _cparams(sem):
    return pltpu.CompilerParams(dimension_semantics=sem, vmem_limit_bytes=VMEM_LIMIT)


def _const_spec(shape):
    nd = len(shape)
    return pl.BlockSpec(shape, lambda *_: (0,) * nd, pipeline_mode=pl.Buffered(1))


def _layer_norm(z, g, b):
    mu = jnp.mean(z, axis=-1, keepdims=True)
    var = jnp.mean(jnp.square(z - mu), axis=-1, keepdims=True)
    return (z - mu) * lax.rsqrt(var + LN_EPS) * g + b


def _proj_body(x_ref, wu, bu, wq, bq, wc, bc, wk, bk, wg, bg, wm, bm,
               u_o, q_o, c_o, k_o, g_o, m_o):
    xb = x_ref[...].astype(BF16)

    def lin(w, b):
        return jnp.dot(xb, w[...], preferred_element_type=F32) + b[...]

    u_o[...] = lin(wu, bu)
    q_o[...] = (lin(wq, bq) * (NSA_HEAD_DIM ** -0.5)).astype(BF16)
    c_o[...] = lin(wc, bc)
    k_o[...] = lin(wk, bk).astype(BF16)
    g_o[...] = lin(wg, bg)
    m_o[...] = jax.nn.sigmoid(lin(wm, bm))


def _stage_proj(x2, w_in, b_in):
    n = x2.shape[0]
    tm = 256
    o0 = SSM_WIDTH
    o1 = o0 + NSA_HEADS * NSA_HEAD_DIM
    o2 = o1 + 6 * KV_WIDTH
    o3 = o2 + 3 * NSA_HEADS
    w_u, b_u = w_in[:, :o0], b_in[:o0]
    w_q, b_q = w_in[:, o0:o1], b_in[o0:o1]
    w_kv, b_kv = w_in[:, o1:o2], b_in[o1:o2]
    w_g, b_g = w_in[:, o2:o3], b_in[o2:o3]
    w_m, b_m = w_in[:, o3:], b_in[o3:]

    def pad_q(a):
        a = a.reshape(a.shape[:-1] + (NSA_HEADS, NSA_HEAD_DIM))
        z = jnp.zeros_like(a)
        lo = jnp.concatenate([a, z], axis=-1)
        hi = jnp.concatenate([z, a], axis=-1)
        hsel = (jnp.arange(NSA_HEADS) // NSA_REP == 0)[:, None]
        return jnp.where(hsel, lo, hi).reshape(a.shape[:-2] + (NSA_HEADS * 128,))

    def pad_g(a):
        a = a.reshape(a.shape[:-1] + (NSA_KV_GROUPS, 3 * NSA_REP))
        a = jnp.concatenate([a, jnp.zeros(a.shape[:-1] + (128 - 3 * NSA_REP,), a.dtype)], axis=-1)
        return a.reshape(a.shape[:-2] + (NSA_KV_GROUPS * 128,))

    ws = [w_u, pad_q(w_q), w_kv[:, :2 * KV_WIDTH], w_kv[:, 2 * KV_WIDTH:], pad_g(w_g), w_m]
    bs = [b_u, pad_q(b_q), b_kv[:2 * KV_WIDTH], b_kv[2 * KV_WIDTH:], pad_g(b_g), b_m]
    odt = [F32, BF16, F32, BF16, F32, F32]
    args, in_specs = [x2], [pl.BlockSpec((tm, D_MODEL), lambda i: (i, 0))]
    for w, b in zip(ws, bs):
        args += [w.astype(BF16), b.reshape(1, -1).astype(F32)]
        in_specs += [_const_spec(w.shape), _const_spec((1, w.shape[1]))]
    out_shape = [jax.ShapeDtypeStruct((n, w.shape[1]), dt) for w, dt in zip(ws, odt)]
    out_specs = [pl.BlockSpec((tm, w.shape[1]), lambda i: (i, 0)) for w in ws]
    return pl.pallas_call(
        _proj_body, grid=(n // tm,), in_specs=in_specs, out_specs=out_specs, out_shape=out_shape,
        compiler_params=_cparams(("parallel",)), name="in_proj")(*args)


def _ssm_weights(a_re, a_im, log_dt, b_re, b_im, c_re, c_im):
    L, G, P, H, A, J = SSM_CHUNK, SSM_GROUPS, SSM_STATE, SSM_GROUP, LANE_GROUPS, N_LANE_TILES
    lam = lax.complex(a_re.astype(F32), a_im.astype(F32))
    dt = jnp.exp(log_dt.astype(F32))[:, None]
    lam_bar = jnp.exp(lam * dt)
    b_bar = ((lam_bar - 1.0) / lam)[:, :, None] * lax.complex(b_re.astype(F32), b_im.astype(F32))
    c = lax.complex(c_re.astype(F32), c_im.astype(F32))
    k = jnp.arange(L + 1, dtype=F32)
    pw = jnp.exp((lam * dt)[None] * k[:, None, None])
    eye = jnp.eye(A, dtype=F32)
    kern = jnp.real(jnp.einsum('ghp,kgp,gpj->kghj', c, pw[:L], b_bar))
    s_i, t_i = jnp.arange(L)[:, None], jnp.arange(L)[None, :]
    tau = jnp.clip(t_i - s_i, 0, L - 1)
    causal = (t_i >= s_i).astype(F32)
    wfull = kern[tau] * causal[:, :, None, None, None]
    wfull = wfull.reshape(L, L, J, A, H, H)
    w_intra = jnp.einsum('stjahi,ab->jsaitbh', wfull, eye).reshape(J, L * 128, L * 128)
    q = pw[:L][::-1][:, :, :, None] * b_bar[None]
    q = q.reshape(L, J, A, P, H)
    m2 = jnp.stack([jnp.real(q), jnp.imag(q)], axis=0)
    m2 = jnp.einsum('rsjapi,ab->jsairbp', m2, eye).reshape(J, L * 128, 2 * A * P)
    cl = c[None] * pw[1:][:, :, None, :]
    cl = cl.reshape(L, J, A, H, P)
    m1 = jnp.stack([jnp.real(cl), -jnp.imag(cl)], axis=0)
    m1 = jnp.einsum('rtjahp,ab->jraptbh', m1, eye).reshape(J, 2 * A * P, L * 128)
    w2 = jnp.concatenate([w_intra, m1], axis=1)
    lam8 = pw[L].reshape(G * P)
    return m2.astype(BF16), w2.astype(BF16), jnp.real(lam8), jnp.imag(lam8)


def _chunk_inputs(u_ref, j):
    return jnp.concatenate(
        [u_ref[:, s * SSM_WIDTH + j * 128: s * SSM_WIDTH + (j + 1) * 128] for s in range(SSM_CHUNK)], axis=1)


def _ssm_state_body(u_ref, m2_ref, zre_ref, zim_ref):
    half = LANE_GROUPS * SSM_STATE
    for j in range(N_LANE_TILES):
        z = jnp.dot(_chunk_inputs(u_ref, j).astype(BF16), m2_ref[j], preferred_element_type=F32)
        zre_ref[:, j * half:(j + 1) * half] = z[:, :half]
        zim_ref[:, j * half:(j + 1) * half] = z[:, half:]


def _ssm_scan_body(zre_ref, zim_ref, lre_ref, lim_ref, xre_ref, xim_ref, sre, sim):
    @pl.when(pl.program_id(0) == 0)
    def _():
        sre[...] = jnp.zeros_like(sre)
        sim[...] = jnp.zeros_like(sim)

    lr, li = lre_ref[...], lim_ref[...]

    def step(c, carry):
        xr, xi = carry
        xre_ref[c] = xr
        xim_ref[c] = xi
        return lr * xr - li * xi + zre_ref[c], lr * xi + li * xr + zim_ref[c]

    xr, xi = lax.fori_loop(0, zre_ref.shape[0], step, (sre[...], sim[...]))
    sre[...] = xr
    sim[...] = xi


def _ssm_out_body(u_ref, xre_ref, xim_ref, w2_ref, d_ref, wglu_ref, wup_ref, o_ref, y_sc):
    half = LANE_GROUPS * SSM_STATE
    for j in range(N_LANE_TILES):
        lhs = jnp.concatenate(
            [_chunk_inputs(u_ref, j), xre_ref[:, j * half:(j + 1) * half], xim_ref[:, j * half:(j + 1) * half]],
            axis=1).astype(BF16)
        yj = jnp.dot(lhs, w2_ref[j], preferred_element_type=F32)
        for t in range(SSM_CHUNK):
            y_sc[:, t * SSM_WIDTH + j * 128: t * SSM_WIDTH + (j + 1) * 128] = yj[:, t * 128:(t + 1) * 128]
    for t in range(SSM_CHUNK):
        cols = slice(t * SSM_WIDTH, (t + 1) * SSM_WIDTH)
        y = jax.nn.gelu(y_sc[:, cols] + d_ref[...] * u_ref[:, cols])
        gl = jnp.dot(y.astype(BF16), wglu_ref[...], preferred_element_type=F32)
        v = gl[:, :SSM_WIDTH] * jax.nn.sigmoid(gl[:, SSM_WIDTH:])
        o_ref[:, t * D_MODEL:(t + 1) * D_MODEL] = jnp.dot(v.astype(BF16), wup_ref[...], preferred_element_type=F32)


def _stage_ssm(u, bsz, seq, a_re, a_im, log_dt, b_re, b_im, c_re, c_im, d_skip, w_glu, w_up):
    n = bsz * seq
    L = SSM_CHUNK
    nck = seq // L
    gp = SSM_GROUPS * SSM_STATE
    m2, w2, l8re, l8im = _ssm_weights(a_re, a_im, log_dt, b_re, b_im, c_re, c_im)
    u2 = u.reshape(n // L, L * SSM_WIDTH)
    rs = min(nck, 512)
    zre, zim = pl.pallas_call(
        _ssm_state_body, grid=(bsz, nck // rs),
        in_specs=[pl.BlockSpec((rs, L * SSM_WIDTH), lambda b, i: (b * (nck // rs) + i, 0)),
                  _const_spec(m2.shape)],
        out_specs=[pl.BlockSpec((rs, gp), lambda b, i: (i, b))] * 2,
        out_shape=[jax.ShapeDtypeStruct((nck, bsz * gp), F32)] * 2,
        compiler_params=_cparams(("parallel", "parallel")), name="ssm_chunk_state")(u2, m2)
    lanes = bsz * gp // 8
    rep = lanes and (8 * lanes) // gp
    assert 8 * lanes == rep * gp and gp % lanes == 0
    lre = jnp.tile(l8re, rep).reshape(8, lanes)
    lim = jnp.tile(l8im, rep).reshape(8, lanes)
    cs = min(nck, 64)
    blk = pl.BlockSpec((cs, 8, lanes), lambda i: (i, 0, 0))
    xre, xim = pl.pallas_call(
        _ssm_scan_body, grid=(nck // cs,),
        in_specs=[blk, blk, _const_spec((8, lanes)), _const_spec((8, lanes))],
        out_specs=[blk, blk],
        out_shape=[jax.ShapeDtypeStruct((nck, 8, lanes), F32)] * 2,
        scratch_shapes=[pltpu.VMEM((8, lanes), F32)] * 2,
        compiler_params=_cparams(("arbitrary",)), name="ssm_carry_scan")(
            zre.reshape(nck, 8, lanes), zim.reshape(nck, 8, lanes), lre, lim)
    xre, xim = xre.reshape(nck, bsz * gp), xim.reshape(nck, bsz * gp)
    ro = min(nck, 128)
    d8 = d_skip.reshape(1, SSM_WIDTH).astype(F32)
    ya = pl.pallas_call(
        _ssm_out_body, grid=(bsz, nck // ro),
        in_specs=[pl.BlockSpec((ro, L * SSM_WIDTH), lambda b, i: (b * (nck // ro) + i, 0)),
                  pl.BlockSpec((ro, gp), lambda b, i: (i, b)),
                  pl.BlockSpec((ro, gp), lambda b, i: (i, b)),
                  _const_spec(w2.shape), _const_spec((1, SSM_WIDTH)),
                  _const_spec(w_glu.shape), _const_spec(w_up.shape)],
        out_specs=pl.BlockSpec((ro, L * D_MODEL), lambda b, i: (b * (nck // ro) + i, 0)),
        out_shape=jax.ShapeDtypeStruct((n // L, L * D_MODEL), F32),
        scratch_shapes=[pltpu.VMEM((ro, L * SSM_WIDTH), F32)],
        compiler_params=_cparams(("parallel", "parallel")), name="ssm_out")(
            u2, xre, xim, w2, d8, w_glu.astype(BF16), w_up.astype(BF16))
    return ya.reshape(n, D_MODEL)


def _compress_body(f_ref, pea_ref, peb_ref, w1a_ref, w1b_ref, w2_ref, o_ref):
    f = f_ref[...]
    a = jnp.dot((f + pea_ref[...]).astype(BF16), w1a_ref[...], preferred_element_type=F32)
    b = jnp.dot((f + peb_ref[...]).astype(BF16), w1b_ref[...], preferred_element_type=F32)
    pre = a + pltpu.roll(b, b.shape[0] - 1, 0)
    hid = jax.nn.gelu(pre)
    o_ref[...] = jnp.dot(hid.astype(BF16), w2_ref[...], preferred_element_type=F32).astype(BF16)


def _stage_compress(kvc, bsz, seq, pe_k, w1_k, w2_k, pe_v, w1_v, w2_v):
    hl = CMP_STRIDE
    nrow = seq // hl
    G, dk, hid = NSA_KV_GROUPS, NSA_HEAD_DIM, CMP_HIDDEN
    eye = jnp.eye(2 * G, dtype=F32)

    def big_w1(lo):
        wk = w1_k.reshape(CMP_BLOCK, dk, hid)[lo:lo + hl]
        wv = w1_v.reshape(CMP_BLOCK, dk, hid)[lo:lo + hl]
        w = jnp.stack([wk, wk, wv, wv], axis=1)
        return jnp.einsum('lcdj,ce->lcdej', w, eye).reshape(hl * 2 * G * dk, 2 * G * hid)

    def big_pe(lo):
        pk, pv = pe_k[lo:lo + hl], pe_v[lo:lo + hl]
        return jnp.stack([pk, pk, pv, pv], axis=1).reshape(1, hl * 2 * G * dk)

    w2 = jnp.stack([w2_k, w2_k, w2_v, w2_v], axis=0)
    w2 = jnp.einsum('cjd,ce->cjed', w2, eye).reshape(2 * G * hid, 2 * G * dk)
    f = kvc.reshape(bsz * nrow, hl * 2 * G * dk)
    width = hl * 2 * G * dk
    return pl.pallas_call(
        _compress_body, grid=(bsz,),
        in_specs=[pl.BlockSpec((nrow, width), lambda b: (b, 0)),
                  _const_spec((1, width)), _const_spec((1, width)),
                  _const_spec((width, 2 * G * hid)), _const_spec((width, 2 * G * hid)),
                  _const_spec((2 * G * hid, 2 * G * dk))],
        out_specs=pl.BlockSpec((nrow, 2 * G * dk), lambda b: (b, 0)),
        out_shape=jax.ShapeDtypeStruct((bsz * nrow, 2 * G * dk), BF16),
        compiler_params=_cparams(("parallel",)), name="nsa_compress")(
            f, big_pe(0).astype(F32), big_pe(hl).astype(F32),
            big_w1(0).astype(BF16), big_w1(hl).astype(BF16), w2.astype(BF16))


def _nt_dot(a, b):
    return lax.dot_general(a, b, (((1,), (1,)), ((), ())), preferred_element_type=F32)


def _nsa_body(q_ref, kcv_ref, kv_ref, g_ref, ov_ref, o_ref, acc_sc):
    i = pl.program_id(1)
    g = pl.program_id(2)
    qs = i * Q_BLOCK
    ncmp = kcv_ref.shape[0]
    rowi = lax.broadcasted_iota(jnp.int32, (Q_BLOCK, 128), 0)
    coli = lax.broadcasted_iota(jnp.int32, (Q_BLOCK, 128), 1)
    base = (rowi - coli).astype(F32)
    slopes = [jnp.where(g == 0, 2.0 ** -(r + 1), 2.0 ** -(r + 1 + NSA_REP)).astype(F32) for r in range(NSA_REP)]
    qh = [q_ref[:, r * 128:(r + 1) * 128] for r in range(NSA_REP)]

    kc = kcv_ref[:, :KV_WIDTH]
    vc = kcv_ref[:, KV_WIDTH:]
    t_c = qs + lax.broadcasted_iota(jnp.int32, (Q_BLOCK, ncmp), 0)
    c_end = lax.broadcasted_iota(jnp.int32, (Q_BLOCK, ncmp), 1) * CMP_STRIDE + (CMP_BLOCK - 1)
    dist_c = (t_c - c_end).astype(F32)
    vis_c = dist_c >= 0.0
    vis_cf = vis_c.astype(F32)
    o_cmp, p_all = [], []
    for r in range(NSA_REP):
        s = _nt_dot(qh[r], kc) - slopes[r] * dist_c
        s = jnp.where(vis_c, s, NEG)
        m = jnp.max(s, axis=-1, keepdims=True)
        p = jnp.exp(s - m) * vis_cf
        p = p / jnp.maximum(jnp.sum(p, axis=-1, keepdims=True), 1e-30)
        pb = p.astype(BF16)
        p_all.append(pb)
        o_cmp.append(jnp.dot(pb, vc, preferred_element_type=F32))

    imp = jnp.dot(jnp.concatenate(p_all, axis=1), ov_ref[...], preferred_element_type=F32)
    t_q = qs + rowi
    cur = t_q // SEL_BLOCK
    forced = (coli == 0) | (coli == cur) | (coli == cur - 1)
    score = jnp.where(coli * SEL_BLOCK <= t_q, imp + jnp.where(forced, FORCE_BONUS, 0.0), NEG)
    nsl = kv_ref.shape[0] // SEL_BLOCK
    st = score.T[:nsl]
    jrow = lax.broadcasted_iota(jnp.int32, st.shape, 0)
    rank = jnp.zeros(st.shape, F32)
    for k in range(nsl):
        rk = st[k:k + 1, :]
        tie = jnp.where(jrow > k, jnp.where(rk == st, 1.0, 0.0), 0.0)
        rank = rank + jnp.where(rk > st, 1.0, tie)
    sel_t = jnp.where(rank < float(min(SEL_TOPK, nsl)), 1.0, 0.0)
    if nsl < 128:
        sel_t = jnp.concatenate([sel_t, jnp.zeros((128 - nsl, Q_BLOCK), F32)], axis=0)
    sel = sel_t.T.astype(BF16)
    jcol = lax.broadcasted_iota(jnp.int32, (128, 128), 0)
    kcol = lax.broadcasted_iota(jnp.int32, (128, 128), 1) // SEL_BLOCK

    def attend(kt, carry, col0, window):
        k0 = pl.multiple_of(kt * 128, 128)
        kk = kv_ref[pl.ds(k0, 128), col0:col0 + KV_WIDTH]
        vv = kv_ref[pl.ds(k0, 128), col0 + KV_WIDTH:col0 + 2 * KV_WIDTH]
        dist = base + (qs - k0).astype(F32)
        if window:
            ok = (dist >= 0.0) & (dist < float(WINDOW))
        else:
            expand = jnp.where(jcol == kcol + kt * (128 // SEL_BLOCK), 1.0, 0.0).astype(BF16)
            picked = jnp.dot(sel, expand, preferred_element_type=F32)
            ok = (dist >= 0.0) & (picked > 0.5)
        out = []
        for r in range(NSA_REP):
            m_old, l_old = carry[2 * r], carry[2 * r + 1]
            s = jnp.where(ok, _nt_dot(qh[r], kk) - slopes[r] * dist, NEG)
            m_new = jnp.maximum(m_old, jnp.max(s, axis=-1, keepdims=True))
            alpha = jnp.exp(m_old - m_new)
            p = jnp.exp(s - m_new)
            acc_sc[r] = alpha * acc_sc[r] + jnp.dot(p.astype(BF16), vv, preferred_element_type=F32)
            out += [m_new, alpha * l_old + jnp.sum(p, axis=-1, keepdims=True)]
        return tuple(out)

    def run_branch(lo, col0, window):
        acc_sc[...] = jnp.zeros_like(acc_sc)
        init = tuple(jnp.full((Q_BLOCK, 1), NEG, F32) if c % 2 == 0 else jnp.zeros((Q_BLOCK, 1), F32)
                     for c in range(2 * NSA_REP))
        fin = lax.fori_loop(lo, i + 1, functools.partial(attend, col0=col0, window=window), init)
        return [acc_sc[r] / fin[2 * r + 1] for r in range(NSA_REP)]

    o_sel = run_branch(0, 0, False)
    o_win = run_branch(jnp.maximum(i - WINDOW // 128, 0), 2 * KV_WIDTH, True)

    gate = jax.nn.sigmoid(g_ref[...])
    for r in range(NSA_REP):
        o = (gate[:, 3 * r:3 * r + 1] * o_cmp[r] + gate[:, 3 * r + 1:3 * r + 2] * o_sel[r]
             + gate[:, 3 * r + 2:3 * r + 3] * o_win[r])
        for gg in range(NSA_KV_GROUPS):
            @pl.when(g == gg)
            def _():
                c0 = (gg * NSA_REP + r) * NSA_HEAD_DIM
                o_ref[:, c0:c0 + NSA_HEAD_DIM] = o[:, gg * NSA_HEAD_DIM:(gg + 1) * NSA_HEAD_DIM].astype(o_ref.dtype)


def _stage_attn(qp, kcv, kv4, gates, bsz, seq):
    n = bsz * seq
    nq = seq // Q_BLOCK
    ncmp = seq // CMP_STRIDE
    nsl = seq // SEL_BLOCK
    c_start = np.arange(ncmp) * CMP_STRIDE
    s_start = np.arange(nsl) * SEL_BLOCK
    ov = ((c_start[:, None] < s_start[None, :] + SEL_BLOCK) & (c_start[:, None] + CMP_BLOCK > s_start[None, :]))
    ov = np.pad(ov.astype(np.float32), ((0, 0), (0, 128 - nsl)))
    ov[ncmp - 1] = 0.0
    ov = jnp.asarray(np.tile(ov, (NSA_REP, 1)), BF16)
    return pl.pallas_call(
        _nsa_body, grid=(bsz, nq, NSA_KV_GROUPS),
        in_specs=[pl.BlockSpec((Q_BLOCK, NSA_REP * 128), lambda b, i, g: (b * nq + i, g)),
                  pl.BlockSpec((ncmp, 2 * KV_WIDTH), lambda b, i, g: (b, 0)),
                  pl.BlockSpec((seq, 4 * KV_WIDTH), lambda b, i, g: (b, 0)),
                  pl.BlockSpec((Q_BLOCK, 128), lambda b, i, g: (b * nq + i, g)),
                  _const_spec(ov.shape)],
        out_specs=pl.BlockSpec((Q_BLOCK, NSA_HEADS * NSA_HEAD_DIM), lambda b, i, g: (b * nq + i, 0)),
        out_shape=jax.ShapeDtypeStruct((n, NSA_HEADS * NSA_HEAD_DIM), BF16),
        scratch_shapes=[pltpu.VMEM((NSA_REP, Q_BLOCK, 128), F32)],
        compiler_params=_cparams(("parallel", "arbitrary", "arbitrary")), name="nsa_attention")(
            qp, kcv, kv4, gates, ov)


def _merge_body(x_ref, ya_ref, ob_ref, gm_ref, wnsa_ref, wout_ref, g_ref, b_ref, o_ref):
    yb = jnp.dot(ob_ref[...], wnsa_ref[...], preferred_element_type=F32)
    mix_in = gm_ref[:, :D_MODEL] * ya_ref[...] + gm_ref[:, D_MODEL:] * yb
    mix = jnp.dot(mix_in.astype(BF16), wout_ref[...], preferred_element_type=F32)
    o_ref[...] = _layer_norm(DN_ALPHA * x_ref[...] + mix, g_ref[...], b_ref[...])


def _stage_merge(x2, ya, ob, gm, w_up_nsa, w_out, ln_g, ln_b):
    n = x2.shape[0]
    tm = 256
    row = lambda w: pl.BlockSpec((tm, w), lambda i: (i, 0))
    return pl.pallas_call(
        _merge_body, grid=(n // tm,),
        in_specs=[row(D_MODEL), row(D_MODEL), row(ob.shape[1]), row(2 * D_MODEL),
                  _const_spec(w_up_nsa.shape), _const_spec(w_out.shape),
                  _const_spec((1, D_MODEL)), _const_spec((1, D_MODEL))],
        out_specs=row(D_MODEL), out_shape=jax.ShapeDtypeStruct((n, D_MODEL), F32),
        compiler_params=_cparams(("parallel",)), name="merge_ln")(
            x2, ya, ob, gm, w_up_nsa.astype(BF16), w_out.astype(BF16),
            ln_g.reshape(1, -1).astype(F32), ln_b.reshape(1, -1).astype(F32))


BIG_NEG = -3.0e38


def _top_rows(s, k):
    rid = lax.broadcasted_iota(jnp.int32, s.shape, 0).astype(F32)
    rank = jnp.full(s.shape, float(k), F32)
    vals, idxs = [], []
    for r in range(k):
        m = jnp.max(s, axis=0, keepdims=True)
        idx = jnp.min(jnp.where(s == m, rid, float(s.shape[0])), axis=0, keepdims=True)
        hit = rid == idx
        rank = jnp.where(hit, float(r), rank)
        s = jnp.where(hit, BIG_NEG, s)
        vals.append(m)
        idxs.append(idx)
    return jnp.concatenate(vals, axis=0), jnp.concatenate(idxs, axis=0), rank


def _peer_body(x_ref, wq_ref, sk_ref, u_ref, vt_ref, g_ref, b_ref, o_ref,
               xt_sc, acc_sc, n_sc, e0_sc, rk1_sc, e1_sc, p_sc, st_sc):
    e = pl.program_id(1)
    k = PEER_TOPK
    nk = PEER_NKEYS

    @pl.when(e == 0)
    def _():
        xt = x_ref[...].T.astype(BF16)
        xt_sc[...] = xt
        acc_sc[...] = jnp.zeros_like(acc_sc)
        qt = jnp.dot(wq_ref[...], xt, preferred_element_type=F32).astype(BF16)
        st_sc[...] = jnp.dot(sk_ref[...], qt, preferred_element_type=F32)

        @pl.loop(0, PEER_HEADS)
        def _(h):
            s0 = st_sc[pl.ds(pl.multiple_of(2 * h * nk, nk), nk), :]
            s1 = st_sc[pl.ds(pl.multiple_of((2 * h + 1) * nk, nk), nk), :]
            a, _, rk0 = _top_rows(s0, k)
            b, _, rk1 = _top_rows(s1, k)
            cand = jnp.concatenate([a[r:r + 1] + b for r in range(k)], axis=0)
            best, bidx, _ = _top_rows(cand, k)
            brank = jnp.floor(bidx * (1.0 / k))
            z = jnp.sum(jnp.exp(best - best[0:1]), axis=0, keepdims=True)
            n_i = jnp.zeros(s0.shape, F32)
            for r in range(k):
                nr = jnp.sum(jnp.where(brank == float(r), 1.0, 0.0), axis=0, keepdims=True)
                n_i = n_i + jnp.where(rk0 == float(r), nr, 0.0)
            n_sc[h] = n_i
            e0_sc[h] = jnp.exp(s0 - a[0:1]) / z
            rk1_sc[h] = rk1
            e1_sc[h] = jnp.exp(s1 - b[0:1])

    ht = jnp.dot(u_ref[...], xt_sc[...], preferred_element_type=F32)
    act = jax.nn.gelu(ht)
    rows = u_ref.shape[0] // nk
    for ii in range(rows):
        i = e * rows + ii
        w = jnp.zeros((nk, ht.shape[1]), F32)
        for h in range(PEER_HEADS):
            n_row = n_sc[h, pl.ds(i, 1), :]
            e0_row = e0_sc[h, pl.ds(i, 1), :]
            w = w + jnp.where(rk1_sc[h] < n_row, e0_row * e1_sc[h], 0.0)
        p_sc[ii * nk:(ii + 1) * nk, :] = (w * act[ii * nk:(ii + 1) * nk]).astype(BF16)
    acc_sc[...] += jnp.dot(vt_ref[...], p_sc[...], preferred_element_type=F32)

    @pl.when(e == pl.num_programs(1) - 1)
    def _():
        x = x_ref[...]
        o_ref[...] = _layer_norm(DN_ALPHA * x + acc_sc[...].T, g_ref[...], b_ref[...])


def _stage_peer(x1, w_q, subkeys, u_tab, v_tab, ln_g, ln_b, tt=512, et=512):
    n = x1.shape[0]
    tt = min(tt, n)
    H, nk, half = PEER_HEADS, PEER_NKEYS, subkeys.shape[-1]
    wq_t = w_q.T.astype(BF16)
    eye = jnp.eye(2 * H, dtype=F32)
    sk = subkeys.reshape(2 * H, nk, half)
    sk_t = jnp.einsum('ckd,ce->cked', sk, eye).reshape(2 * H * nk, 2 * H * half).astype(BF16)
    u_b = u_tab.astype(BF16)
    v_t = v_tab.T.astype(BF16)
    ne = u_tab.shape[0]
    sel_sc = pltpu.VMEM((H, nk, tt), F32)
    return pl.pallas_call(
        _peer_body, grid=(n // tt, ne // et),
        in_specs=[pl.BlockSpec((tt, D_MODEL), lambda t, e: (t, 0)),
                  _const_spec(wq_t.shape), _const_spec(sk_t.shape),
                  pl.BlockSpec((et, D_MODEL), lambda t, e: (e, 0)),
                  pl.BlockSpec((D_MODEL, et), lambda t, e: (0, e)),
                  _const_spec((1, D_MODEL)), _const_spec((1, D_MODEL))],
        out_specs=pl.BlockSpec((tt, D_MODEL), lambda t, e: (t, 0)),
        out_shape=jax.ShapeDtypeStruct((n, D_MODEL), F32),
        scratch_shapes=[pltpu.VMEM((D_MODEL, tt), BF16), pltpu.VMEM((D_MODEL, tt), F32),
                        sel_sc, sel_sc, sel_sc, sel_sc, pltpu.VMEM((et, tt), BF16),
                        pltpu.VMEM((2 * H * nk, tt), F32)],
        compiler_params=_cparams(("parallel", "arbitrary")), name="peer_ffn")(
            x1, wq_t, sk_t, u_b, v_t, ln_g.reshape(1, -1).astype(F32), ln_b.reshape(1, -1).astype(F32))


def _layer(x, w_in, b_in, ssm_a_re, ssm_a_im, ssm_log_dt, ssm_b_re, ssm_b_im, ssm_c_re, ssm_c_im, ssm_d,
           w_glu, w_up_ssm, nsa_pe_k, nsa_w1_k, nsa_w2_k, nsa_pe_v, nsa_w1_v, nsa_w2_v, w_up_nsa,
           w_out, ln1_g, ln1_b, peer_w_q, peer_subkeys, peer_u, peer_v, ln2_g, ln2_b):
    bsz, seq, _ = x.shape
    x2 = x.reshape(bsz * seq, D_MODEL)
    u, qp, kvc, kv4, gates, gm = _stage_proj(x2, w_in, b_in)
    ya = _stage_ssm(u, bsz, seq, ssm_a_re, ssm_a_im, ssm_log_dt, ssm_b_re, ssm_b_im, ssm_c_re, ssm_c_im,
                    ssm_d, w_glu, w_up_ssm)
    kcv = _stage_compress(kvc, bsz, seq, nsa_pe_k, nsa_w1_k, nsa_w2_k, nsa_pe_v, nsa_w1_v, nsa_w2_v)
    ob = _stage_attn(qp, kcv, kv4, gates, bsz, seq)
    x1 = _stage_merge(x2, ya, ob, gm, w_up_nsa, w_out, ln1_g, ln1_b)
    out = _stage_peer(x1, peer_w_q, peer_subkeys, peer_u, peer_v, ln2_g, ln2_b)
    return out.reshape(bsz, seq, D_MODEL)


def kernel(x, w_in, b_in, ssm_a_re, ssm_a_im, ssm_log_dt, ssm_b_re, ssm_b_im, ssm_c_re, ssm_c_im, ssm_d, w_glu,
           w_up_ssm, nsa_pe_k, nsa_w1_k, nsa_w2_k, nsa_pe_v, nsa_w1_v, nsa_w2_v, w_up_nsa, w_out, ln1_g, ln1_b,
           peer_w_q, peer_subkeys, peer_u, peer_v, ln2_g, ln2_b):
    params = (w_in, b_in, ssm_a_re, ssm_a_im, ssm_log_dt, ssm_b_re, ssm_b_im, ssm_c_re, ssm_c_im, ssm_d, w_glu,
              w_up_ssm, nsa_pe_k, nsa_w1_k, nsa_w2_k, nsa_pe_v, nsa_w1_v, nsa_w2_v, w_up_nsa, w_out, ln1_g, ln1_b,
              peer_w_q, peer_subkeys, peer_u, peer_v, ln2_g, ln2_b)
    for layer in range(w_in.shape[0]):
        x = _layer(x, *[p[layer] for p in params])
    return x
```

```python
import functools
import math

import jax
import jax.numpy as jnp
import numpy as np
from jax import lax
from jax.experimental import pallas as pl
from jax.experimental.pallas import tpu as pltpu

F32 = jnp.float32
BF16 = jnp.bfloat16

D_MODEL = 1024
SSM_WIDTH = 512
SSM_GROUP = 16
SSM_GROUPS = 32
SSM_STATE = 64
SSM_CHUNK = 8
LANE_GROUPS = 8
N_LANE_TILES = SSM_WIDTH // 128
NSA_HEADS = 8
NSA_KV_GROUPS = 2
NSA_REP = NSA_HEADS // NSA_KV_GROUPS
NSA_HEAD_DIM = 64
KV_WIDTH = NSA_KV_GROUPS * NSA_HEAD_DIM
CMP_BLOCK = 32
CMP_STRIDE = 16
CMP_HIDDEN = 128
SEL_BLOCK = 64
SEL_TOPK = 16
WINDOW = 512
Q_BLOCK = 128
FORCE_BONUS = 1.0e4
PEER_HEADS = 8
PEER_NKEYS = 128
PEER_EXPERTS = PEER_NKEYS * PEER_NKEYS
PEER_TOPK = 16
DN_ALPHA = 2.0 ** 0.25
LN_EPS = 1e-5
NEG = -1e30
VMEM_LIMIT = 56 * 1024 * 1024


def _cparams(sem):
    return pltpu.CompilerParams(dimension_semantics=sem, vmem_limit_bytes=VMEM_LIMIT)


def _const_spec(shape):
    nd = len(shape)
    return pl.BlockSpec(shape, lambda *_: (0,) * nd, pipeline_mode=pl.Buffered(1))


def _layer_norm(z, g, b):
    mu = jnp.mean(z, axis=-1, keepdims=True)
    var = jnp.mean(jnp.square(z - mu), axis=-1, keepdims=True)
    return (z - mu) * lax.rsqrt(var + LN_EPS) * g + b


KV_REC = 640
KS_OFF, VS_OFF, KW_OFF, VW_OFF = 0, 256, 384, 512
POS_HI, POS_LO, ONE_LANE, BLOCK_LANE0 = 64, 65, 64, 128
assert SEL_BLOCK == 64


def _position_features(pos, lane):
    hi, lo = pos >> 6, pos & 63
    f = jnp.where(lane == KS_OFF + POS_HI, hi, 0) + jnp.where(lane == KW_OFF + POS_HI, hi, 0)
    f = f + jnp.where(lane == KS_OFF + POS_LO, lo, 0) + jnp.where(lane == KW_OFF + POS_LO, lo, 0)
    f = f + jnp.where(lane == KS_OFF + BLOCK_LANE0 + hi, 1, 0)
    f = f + jnp.where(lane == VS_OFF + ONE_LANE, 1, 0) + jnp.where(lane == VW_OFF + ONE_LANE, 1, 0)
    return f.astype(F32)


def _proj_body(seq, x_ref, wu, bu, wq, bq, wc, bc, wk, bk, wg, bg, wm, bm,
               u_o, q_o, c_o, k_o, g_o, m_o):
    xb = x_ref[...].astype(BF16)
    tm = x_ref.shape[0]

    def lin(w, b):
        return jnp.dot(xb, w[...], preferred_element_type=F32) + b[...]

    u_o[...] = lin(wu, bu)
    q_o[...] = (lin(wq, bq) * (NSA_HEAD_DIM ** -0.5)).astype(BF16)
    c_o[...] = lin(wc, bc)
    pos0 = (pl.program_id(0) % (seq // tm)) * tm
    shape = (tm, NSA_KV_GROUPS * KV_REC)
    lane = lax.broadcasted_iota(jnp.int32, shape, 1)
    lane = jnp.where(lane >= KV_REC, lane - KV_REC, lane)
    feat = _position_features(pos0 + lax.broadcasted_iota(jnp.int32, shape, 0), lane)
    k_o[...] = (lin(wk, bk) + feat).astype(BF16)
    g_o[...] = lin(wg, bg)
    m_o[...] = jax.nn.sigmoid(lin(wm, bm))


def _stage_proj(x2, w_in, b_in, seq):
    n = x2.shape[0]
    tm = 256
    o0 = SSM_WIDTH
    o1 = o0 + NSA_HEADS * NSA_HEAD_DIM
    o2 = o1 + 6 * KV_WIDTH
    o3 = o2 + 3 * NSA_HEADS
    w_u, b_u = w_in[:, :o0], b_in[:o0]
    w_q, b_q = w_in[:, o0:o1], b_in[o0:o1]
    w_kv, b_kv = w_in[:, o1:o2], b_in[o1:o2]
    w_g, b_g = w_in[:, o2:o3], b_in[o2:o3]
    w_m, b_m = w_in[:, o3:], b_in[o3:]

    def pad_q(a, aux):
        a = a.reshape(a.shape[:-1] + (NSA_HEADS, NSA_HEAD_DIM))
        aux = jnp.broadcast_to(aux, a.shape[:-2] + aux.shape)
        return jnp.concatenate([a, aux], axis=-1).reshape(a.shape[:-2] + (NSA_HEADS * 128,))

    slope = 2.0 ** -(np.arange(NSA_HEADS, dtype=np.float32) + 1.0) * NSA_HEAD_DIM ** 0.5
    q_aux = np.zeros((NSA_HEADS, 128 - NSA_HEAD_DIM), np.float32)
    q_aux[:, POS_HI - NSA_HEAD_DIM] = slope * 64.0
    q_aux[:, POS_LO - NSA_HEAD_DIM] = slope

    def kv_records(a):
        parts = a.reshape(a.shape[:-1] + (6, NSA_KV_GROUPS, NSA_HEAD_DIM))
        z = lambda w: jnp.zeros(a.shape[:-1] + (w,), a.dtype)
        recs = []
        for g in range(NSA_KV_GROUPS):
            k_s, v_s, k_w, v_w = (parts[..., c, g, :] for c in (2, 3, 4, 5))
            recs += [k_s, z(256 - NSA_HEAD_DIM), v_s, z(128 - NSA_HEAD_DIM), k_w, z(128 - NSA_HEAD_DIM),
                     v_w, z(128 - NSA_HEAD_DIM)]
        return jnp.concatenate(recs, axis=-1)

    def pad_g(a):
        a = a.reshape(a.shape[:-1] + (NSA_KV_GROUPS, 3 * NSA_REP))
        a = jnp.concatenate([a, jnp.zeros(a.shape[:-1] + (128 - 3 * NSA_REP,), a.dtype)], axis=-1)
        return a.reshape(a.shape[:-2] + (NSA_KV_GROUPS * 128,))

    ws = [w_u, pad_q(w_q, jnp.zeros_like(q_aux)), w_kv[:, :2 * KV_WIDTH], kv_records(w_kv), pad_g(w_g), w_m]
    bs = [b_u, pad_q(b_q, jnp.asarray(q_aux)), b_kv[:2 * KV_WIDTH], kv_records(b_kv), pad_g(b_g), b_m]
    odt = [F32, BF16, F32, BF16, F32, F32]
    args, in_specs = [x2], [pl.BlockSpec((tm, D_MODEL), lambda i: (i, 0))]
    for w, b in zip(ws, bs):
        args += [w.astype(BF16), b.reshape(1, -1).astype(F32)]
        in_specs += [_const_spec(w.shape), _const_spec((1, w.shape[1]))]
    out_shape = [jax.ShapeDtypeStruct((n, w.shape[1]), dt) for w, dt in zip(ws, odt)]
    out_specs = [pl.BlockSpec((tm, w.shape[1]), lambda i: (i, 0)) for w in ws]
    return pl.pallas_call(
        functools.partial(_proj_body, seq), grid=(n // tm,), in_specs=in_specs, out_specs=out_specs, out_shape=out_shape,
        compiler_params=_cparams(("parallel",)), name="in_proj")(*args)


def _ssm_weights(a_re, a_im, log_dt, b_re, b_im, c_re, c_im):
    L, G, P, H, A, J = SSM_CHUNK, SSM_GROUPS, SSM_STATE, SSM_GROUP, LANE_GROUPS, N_LANE_TILES
    lam = lax.complex(a_re.astype(F32), a_im.astype(F32))
    dt = jnp.exp(log_dt.astype(F32))[:, None]
    lam_bar = jnp.exp(lam * dt)
    b_bar = ((lam_bar - 1.0) / lam)[:, :, None] * lax.complex(b_re.astype(F32), b_im.astype(F32))
    c = lax.complex(c_re.astype(F32), c_im.astype(F32))
    k = jnp.arange(L + 1, dtype=F32)
    pw = jnp.exp((lam * dt)[None] * k[:, None, None])
    eye = jnp.eye(A, dtype=F32)
    kern = jnp.real(jnp.einsum('ghp,kgp,gpj->kghj', c, pw[:L], b_bar))
    s_i, t_i = jnp.arange(L)[:, None], jnp.arange(L)[None, :]
    tau = jnp.clip(t_i - s_i, 0, L - 1)
    causal = (t_i >= s_i).astype(F32)
    wfull = kern[tau] * causal[:, :, None, None, None]
    wfull = wfull.reshape(L, L, J, A, H, H)
    w_intra = jnp.einsum('stjahi,ab->jsaitbh', wfull, eye).reshape(J, L * 128, L * 128)
    q = pw[:L][::-1][:, :, :, None] * b_bar[None]
    q = q.reshape(L, J, A, P, H)
    m2 = jnp.stack([jnp.real(q), jnp.imag(q)], axis=0)
    m2 = jnp.einsum('rsjapi,ab->jsairbp', m2, eye).reshape(J, L * 128, 2 * A * P)
    cl = c[None] * pw[1:][:, :, None, :]
    cl = cl.reshape(L, J, A, H, P)
    m1 = jnp.stack([jnp.real(cl), -jnp.imag(cl)], axis=0)
    m1 = jnp.einsum('rtjahp,ab->jraptbh', m1, eye).reshape(J, 2 * A * P, L * 128)
    w2 = jnp.concatenate([w_intra, m1], axis=1)
    lam8 = pw[L].reshape(G * P)
    return m2.astype(BF16), w2.astype(BF16), jnp.real(lam8), jnp.imag(lam8)


def _chunk_inputs(u_ref, j):
    return jnp.concatenate(
        [u_ref[:, s * SSM_WIDTH + j * 128: s * SSM_WIDTH + (j + 1) * 128] for s in range(SSM_CHUNK)], axis=1)


def _ssm_state_body(u_ref, m2_ref, zre_ref, zim_ref):
    half = LANE_GROUPS * SSM_STATE
    for j in range(N_LANE_TILES):
        z = jnp.dot(_chunk_inputs(u_ref, j).astype(BF16), m2_ref[j], preferred_element_type=F32)
        zre_ref[:, j * half:(j + 1) * half] = z[:, :half]
        zim_ref[:, j * half:(j + 1) * half] = z[:, half:]


def _ssm_scan_body(zre_ref, zim_ref, lre_ref, lim_ref, xre_ref, xim_ref, sre, sim):
    @pl.when(pl.program_id(0) == 0)
    def _():
        sre[...] = jnp.zeros_like(sre)
        sim[...] = jnp.zeros_like(sim)

    lr, li = lre_ref[...], lim_ref[...]

    def step(c, carry):
        xr, xi = carry
        xre_ref[c] = xr
        xim_ref[c] = xi
        return lr * xr - li * xi + zre_ref[c], lr * xi + li * xr + zim_ref[c]

    xr, xi = lax.fori_loop(0, zre_ref.shape[0], step, (sre[...], sim[...]))
    sre[...] = xr
    sim[...] = xi


def _ssm_out_body(u_ref, xre_ref, xim_ref, w2_ref, d_ref, wglu_ref, wup_ref, o_ref, y_sc):
    half = LANE_GROUPS * SSM_STATE
    for j in range(N_LANE_TILES):
        lhs = jnp.concatenate(
            [_chunk_inputs(u_ref, j), xre_ref[:, j * half:(j + 1) * half], xim_ref[:, j * half:(j + 1) * half]],
            axis=1).astype(BF16)
        yj = jnp.dot(lhs, w2_ref[j], preferred_element_type=F32)
        for t in range(SSM_CHUNK):
            y_sc[:, t * SSM_WIDTH + j * 128: t * SSM_WIDTH + (j + 1) * 128] = yj[:, t * 128:(t + 1) * 128]
    for t in range(SSM_CHUNK):
        cols = slice(t * SSM_WIDTH, (t + 1) * SSM_WIDTH)
        y = jax.nn.gelu(y_sc[:, cols] + d_ref[...] * u_ref[:, cols])
        gl = jnp.dot(y.astype(BF16), wglu_ref[...], preferred_element_type=F32)
        v = gl[:, :SSM_WIDTH] * jax.nn.sigmoid(gl[:, SSM_WIDTH:])
        o_ref[:, t * D_MODEL:(t + 1) * D_MODEL] = jnp.dot(v.astype(BF16), wup_ref[...], preferred_element_type=F32)


def _stage_ssm(u, bsz, seq, a_re, a_im, log_dt, b_re, b_im, c_re, c_im, d_skip, w_glu, w_up):
    n = bsz * seq
    L = SSM_CHUNK
    nck = seq // L
    gp = SSM_GROUPS * SSM_STATE
    m2, w2, l8re, l8im = _ssm_weights(a_re, a_im, log_dt, b_re, b_im, c_re, c_im)
    u2 = u.reshape(n // L, L * SSM_WIDTH)
    rs = min(nck, 512)
    zre, zim = pl.pallas_call(
        _ssm_state_body, grid=(bsz, nck // rs),
        in_specs=[pl.BlockSpec((rs, L * SSM_WIDTH), lambda b, i: (b * (nck // rs) + i, 0)),
                  _const_spec(m2.shape)],
        out_specs=[pl.BlockSpec((rs, gp), lambda b, i: (i, b))] * 2,
        out_shape=[jax.ShapeDtypeStruct((nck, bsz * gp), F32)] * 2,
        compiler_params=_cparams(("parallel", "parallel")), name="ssm_chunk_state")(u2, m2)
    lanes = bsz * gp // 8
    rep = lanes and (8 * lanes) // gp
    assert 8 * lanes == rep * gp and gp % lanes == 0
    lre = jnp.tile(l8re, rep).reshape(8, lanes)
    lim = jnp.tile(l8im, rep).reshape(8, lanes)
    cs = min(nck, 64)
    blk = pl.BlockSpec((cs, 8, lanes), lambda i: (i, 0, 0))
    xre, xim = pl.pallas_call(
        _ssm_scan_body, grid=(nck // cs,),
        in_specs=[blk, blk, _const_spec((8, lanes)), _const_spec((8, lanes))],
        out_specs=[blk, blk],
        out_shape=[jax.ShapeDtypeStruct((nck, 8, lanes), F32)] * 2,
        scratch_shapes=[pltpu.VMEM((8, lanes), F32)] * 2,
        compiler_params=_cparams(("arbitrary",)), name="ssm_carry_scan")(
            zre.reshape(nck, 8, lanes), zim.reshape(nck, 8, lanes), lre, lim)
    xre, xim = xre.reshape(nck, bsz * gp), xim.reshape(nck, bsz * gp)
    ro = min(nck, 128)
    d8 = d_skip.reshape(1, SSM_WIDTH).astype(F32)
    ya = pl.pallas_call(
        _ssm_out_body, grid=(bsz, nck // ro),
        in_specs=[pl.BlockSpec((ro, L * SSM_WIDTH), lambda b, i: (b * (nck // ro) + i, 0)),
                  pl.BlockSpec((ro, gp), lambda b, i: (i, b)),
                  pl.BlockSpec((ro, gp), lambda b, i: (i, b)),
                  _const_spec(w2.shape), _const_spec((1, SSM_WIDTH)),
                  _const_spec(w_glu.shape), _const_spec(w_up.shape)],
        out_specs=pl.BlockSpec((ro, L * D_MODEL), lambda b, i: (b * (nck // ro) + i, 0)),
        out_shape=jax.ShapeDtypeStruct((n // L, L * D_MODEL), F32),
        scratch_shapes=[pltpu.VMEM((ro, L * SSM_WIDTH), F32)],
        compiler_params=_cparams(("parallel", "parallel")), name="ssm_out")(
            u2, xre, xim, w2, d8, w_glu.astype(BF16), w_up.astype(BF16))
    return ya.reshape(n, D_MODEL)


def _compress_body(f_ref, pea_ref, peb_ref, w1a_ref, w1b_ref, w2_ref, o_ref):
    f = f_ref[...]
    a = jnp.dot((f + pea_ref[...]).astype(BF16), w1a_ref[...], preferred_element_type=F32)
    b = jnp.dot((f + peb_ref[...]).astype(BF16), w1b_ref[...], preferred_element_type=F32)
    pre = a + pltpu.roll(b, b.shape[0] - 1, 0)
    hid = jax.nn.gelu(pre)
    out = jnp.dot(hid.astype(BF16), w2_ref[...], preferred_element_type=F32)
    lane = lax.broadcasted_iota(jnp.int32, out.shape, 1) & 255
    c_end = lax.broadcasted_iota(jnp.int32, out.shape, 0) * CMP_STRIDE + (CMP_BLOCK - 1)
    feat = jnp.where(lane == POS_HI, c_end >> 6, 0) + jnp.where(lane == POS_LO, c_end & 63, 0)
    o_ref[...] = (out + feat.astype(F32)).astype(BF16)


def _stage_compress(kvc, bsz, seq, pe_k, w1_k, w2_k, pe_v, w1_v, w2_v):
    hl = CMP_STRIDE
    nrow = seq // hl
    G, dk, hid = NSA_KV_GROUPS, NSA_HEAD_DIM, CMP_HIDDEN
    eye = jnp.eye(2 * G, dtype=F32)

    def big_w1(lo):
        wk = w1_k.reshape(CMP_BLOCK, dk, hid)[lo:lo + hl]
        wv = w1_v.reshape(CMP_BLOCK, dk, hid)[lo:lo + hl]
        w = jnp.stack([wk, wk, wv, wv], axis=1)
        return jnp.einsum('lcdj,ce->lcdej', w, eye).reshape(hl * 2 * G * dk, 2 * G * hid)

    def big_pe(lo):
        pk, pv = pe_k[lo:lo + hl], pe_v[lo:lo + hl]
        return jnp.stack([pk, pk, pv, pv], axis=1).reshape(1, hl * 2 * G * dk)

    w2 = jnp.stack([w2_k, w2_k, w2_v, w2_v], axis=0)
    w2 = jnp.concatenate([w2, jnp.zeros((2 * G, hid, 128 - dk), w2.dtype)], axis=-1)
    slot = eye[np.array([kv * G + g for g in range(G) for kv in range(2)])].T
    w2 = jnp.einsum('cjd,ce->cjed', w2, slot).reshape(2 * G * hid, 2 * G * 128)
    f = kvc.reshape(bsz * nrow, hl * 2 * G * dk)
    width = hl * 2 * G * dk
    return pl.pallas_call(
        _compress_body, grid=(bsz,),
        in_specs=[pl.BlockSpec((nrow, width), lambda b: (b, 0)),
                  _const_spec((1, width)), _const_spec((1, width)),
                  _const_spec((width, 2 * G * hid)), _const_spec((width, 2 * G * hid)),
                  _const_spec((2 * G * hid, 2 * G * 128))],
        out_specs=pl.BlockSpec((nrow, 2 * G * 128), lambda b: (b, 0)),
        out_shape=jax.ShapeDtypeStruct((bsz * nrow, 2 * G * 128), BF16),
        compiler_params=_cparams(("parallel",)), name="nsa_compress")(
            f, big_pe(0).astype(F32), big_pe(hl).astype(F32),
            big_w1(0).astype(BF16), big_w1(hl).astype(BF16), w2.astype(BF16))


def _nt_dot(a, b):
    return lax.dot_general(a, b, (((1,), (1,)), ((), ())), preferred_element_type=F32)


def _nsa_body_old(q_ref, kcv_ref, kv_ref, g_ref, ov_ref, o_ref, acc_sc):
    i = pl.program_id(1)
    g = pl.program_id(2)
    qs = i * Q_BLOCK
    ncmp = kcv_ref.shape[0]
    rowi = lax.broadcasted_iota(jnp.int32, (Q_BLOCK, 128), 0)
    coli = lax.broadcasted_iota(jnp.int32, (Q_BLOCK, 128), 1)
    base = (rowi - coli).astype(F32)
    slopes = [jnp.where(g == 0, 2.0 ** -(r + 1), 2.0 ** -(r + 1 + NSA_REP)).astype(F32) for r in range(NSA_REP)]
    qh = [q_ref[:, r * 128:(r + 1) * 128] for r in range(NSA_REP)]

    kc = kcv_ref[:, :KV_WIDTH]
    vc = kcv_ref[:, KV_WIDTH:]
    t_c = qs + lax.broadcasted_iota(jnp.int32, (Q_BLOCK, ncmp), 0)
    c_end = lax.broadcasted_iota(jnp.int32, (Q_BLOCK, ncmp), 1) * CMP_STRIDE + (CMP_BLOCK - 1)
    dist_c = (t_c - c_end).astype(F32)
    vis_c = dist_c >= 0.0
    vis_cf = vis_c.astype(F32)
    o_cmp, p_all = [], []
    for r in range(NSA_REP):
        s = _nt_dot(qh[r], kc) - slopes[r] * dist_c
        s = jnp.where(vis_c, s, NEG)
        m = jnp.max(s, axis=-1, keepdims=True)
        p = jnp.exp(s - m) * vis_cf
        p = p / jnp.maximum(jnp.sum(p, axis=-1, keepdims=True), 1e-30)
        pb = p.astype(BF16)
        p_all.append(pb)
        o_cmp.append(jnp.dot(pb, vc, preferred_element_type=F32))

    imp = jnp.dot(jnp.concatenate(p_all, axis=1), ov_ref[...], preferred_element_type=F32)
    t_q = qs + rowi
    cur = t_q // SEL_BLOCK
    forced = (coli == 0) | (coli == cur) | (coli == cur - 1)
    score = jnp.where(coli * SEL_BLOCK <= t_q, imp + jnp.where(forced, FORCE_BONUS, 0.0), NEG)
    nsl = kv_ref.shape[0] // SEL_BLOCK
    st = score.T[:nsl]
    jrow = lax.broadcasted_iota(jnp.int32, st.shape, 0)
    rank = jnp.zeros(st.shape, F32)
    for k in range(nsl):
        rk = st[k:k + 1, :]
        tie = jnp.where(jrow > k, jnp.where(rk == st, 1.0, 0.0), 0.0)
        rank = rank + jnp.where(rk > st, 1.0, tie)
    sel_t = jnp.where(rank < float(min(SEL_TOPK, nsl)), 1.0, 0.0)
    if nsl < 128:
        sel_t = jnp.concatenate([sel_t, jnp.zeros((128 - nsl, Q_BLOCK), F32)], axis=0)
    sel = sel_t.T.astype(BF16)
    jcol = lax.broadcasted_iota(jnp.int32, (128, 128), 0)
    kcol = lax.broadcasted_iota(jnp.int32, (128, 128), 1) // SEL_BLOCK

    def attend(kt, carry, col0, window):
        k0 = pl.multiple_of(kt * 128, 128)
        kk = kv_ref[pl.ds(k0, 128), col0:col0 + KV_WIDTH]
        vv = kv_ref[pl.ds(k0, 128), col0 + KV_WIDTH:col0 + 2 * KV_WIDTH]
        dist = base + (qs - k0).astype(F32)
        if window:
            ok = (dist >= 0.0) & (dist < float(WINDOW))
        else:
            expand = jnp.where(jcol == kcol + kt * (128 // SEL_BLOCK), 1.0, 0.0).astype(BF16)
            picked = jnp.dot(sel, expand, preferred_element_type=F32)
            ok = (dist >= 0.0) & (picked > 0.5)
        out = []
        for r in range(NSA_REP):
            m_old, l_old = carry[2 * r], carry[2 * r + 1]
            s = jnp.where(ok, _nt_dot(qh[r], kk) - slopes[r] * dist, NEG)
            m_new = jnp.maximum(m_old, jnp.max(s, axis=-1, keepdims=True))
            alpha = jnp.exp(m_old - m_new)
            p = jnp.exp(s - m_new)
            acc_sc[r] = alpha * acc_sc[r] + jnp.dot(p.astype(BF16), vv, preferred_element_type=F32)
            out += [m_new, alpha * l_old + jnp.sum(p, axis=-1, keepdims=True)]
        return tuple(out)

    def run_branch(lo, col0, window):
        acc_sc[...] = jnp.zeros_like(acc_sc)
        init = tuple(jnp.full((Q_BLOCK, 1), NEG, F32) if c % 2 == 0 else jnp.zeros((Q_BLOCK, 1), F32)
                     for c in range(2 * NSA_REP))
        fin = lax.fori_loop(lo, i + 1, functools.partial(attend, col0=col0, window=window), init)
        return [acc_sc[r] / fin[2 * r + 1] for r in range(NSA_REP)]

    o_sel = run_branch(0, 0, False)
    o_win = run_branch(jnp.maximum(i - WINDOW // 128, 0), 2 * KV_WIDTH, True)

    gate = jax.nn.sigmoid(g_ref[...])
    for r in range(NSA_REP):
        o = (gate[:, 3 * r:3 * r + 1] * o_cmp[r] + gate[:, 3 * r + 1:3 * r + 2] * o_sel[r]
             + gate[:, 3 * r + 2:3 * r + 3] * o_win[r])
        for gg in range(NSA_KV_GROUPS):
            @pl.when(g == gg)
            def _():
                c0 = (gg * NSA_REP + r) * NSA_HEAD_DIM
                o_ref[:, c0:c0 + NSA_HEAD_DIM] = o[:, gg * NSA_HEAD_DIM:(gg + 1) * NSA_HEAD_DIM].astype(o_ref.dtype)


def _stage_attn_old(qp, kcv, kv4, gates, bsz, seq):
    n = bsz * seq
    nq = seq // Q_BLOCK
    ncmp = seq // CMP_STRIDE
    nsl = seq // SEL_BLOCK
    c_start = np.arange(ncmp) * CMP_STRIDE
    s_start = np.arange(nsl) * SEL_BLOCK
    ov = ((c_start[:, None] < s_start[None, :] + SEL_BLOCK) & (c_start[:, None] + CMP_BLOCK > s_start[None, :]))
    ov = np.pad(ov.astype(np.float32), ((0, 0), (0, 128 - nsl)))
    ov[ncmp - 1] = 0.0
    ov = jnp.asarray(np.tile(ov, (NSA_REP, 1)), BF16)
    return pl.pallas_call(
        _nsa_body, grid=(bsz, nq, NSA_KV_GROUPS),
        in_specs=[pl.BlockSpec((Q_BLOCK, NSA_REP * 128), lambda b, i, g: (b * nq + i, g)),
                  pl.BlockSpec((ncmp, 2 * KV_WIDTH), lambda b, i, g: (b, 0)),
                  pl.BlockSpec((seq, 4 * KV_WIDTH), lambda b, i, g: (b, 0)),
                  pl.BlockSpec((Q_BLOCK, 128), lambda b, i, g: (b * nq + i, g)),
                  _const_spec(ov.shape)],
        out_specs=pl.BlockSpec((Q_BLOCK, NSA_HEADS * NSA_HEAD_DIM), lambda b, i, g: (b * nq + i, 0)),
        out_shape=jax.ShapeDtypeStruct((n, NSA_HEADS * NSA_HEAD_DIM), BF16),
        scratch_shapes=[pltpu.VMEM((NSA_REP, Q_BLOCK, 128), F32)],
        compiler_params=_cparams(("parallel", "arbitrary", "arbitrary")), name="nsa_attention")(
            qp, kcv, kv4, gates, ov)


MASK_BIG = 2.0 ** 100
SEL_TILE = 256
WIN_TILE = 128


def _tn_dot(a, b):
    return lax.dot_general(a, b, (((0,), (0,)), ((), ())), preferred_element_type=F32)


def _nsa_body(q_ref, kcv_ref, kv_ref, g_ref, ov_ref, o_ref, acc, s_buf, p_buf, a_buf):
    i = pl.program_id(2)
    qs = i * Q_BLOCK
    ncmp = kcv_ref.shape[0]
    seq = kv_ref.shape[0]
    rows = NSA_REP * Q_BLOCK
    dk = NSA_HEAD_DIM
    qh = [q_ref[:, r * 128:(r + 1) * 128] for r in range(NSA_REP)]
    q_all = jnp.concatenate(qh, axis=0)

    def t_of(shape):
        return qs + (lax.broadcasted_iota(jnp.int32, shape, 0) & (Q_BLOCK - 1))

    kc = kcv_ref[:, :128]
    vc = kcv_ref[:, 128:]
    c_end = lax.broadcasted_iota(jnp.int32, (rows, ncmp), 1) * CMP_STRIDE + (CMP_BLOCK - 1)
    vis_c = t_of((rows, ncmp)) >= c_end
    s = jnp.where(vis_c, _nt_dot(q_all, kc), NEG)
    p = jnp.exp(s - jnp.max(s, axis=-1, keepdims=True)) * jnp.where(vis_c, 1.0, 0.0)
    p = (p / jnp.maximum(jnp.sum(p, axis=-1, keepdims=True), 1e-30)).astype(BF16)
    o_cmp = jnp.dot(p, vc, preferred_element_type=F32)

    wlen = min(WINDOW + Q_BLOCK, seq)
    w0 = pl.multiple_of(jnp.maximum(qs + Q_BLOCK - wlen, 0), Q_BLOCK)
    dist_w = t_of((rows, wlen)) - (w0 + lax.broadcasted_iota(jnp.int32, (rows, wlen), 1))
    s = _nt_dot(q_all, kv_ref[pl.ds(w0, wlen), KW_OFF:KW_OFF + 128])
    s = jnp.where((dist_w >= 0) & (dist_w < WINDOW), s, NEG)
    p_w = jnp.exp(s - jnp.max(s, axis=-1, keepdims=True)).astype(BF16)
    a_win = jnp.dot(p_w, kv_ref[pl.ds(w0, wlen), VW_OFF:VW_OFF + 128], preferred_element_type=F32)

    p_heads = jnp.concatenate([p[r * Q_BLOCK:(r + 1) * Q_BLOCK] for r in range(NSA_REP)], axis=1)
    imp = jnp.dot(p_heads, ov_ref[...], preferred_element_type=F32)
    coli = lax.broadcasted_iota(jnp.int32, (Q_BLOCK, 128), 1)
    t_q = qs + lax.broadcasted_iota(jnp.int32, (Q_BLOCK, 128), 0)
    cur = t_q >> 6
    forced = (coli == 0) | (coli == cur) | (coli == cur - 1)
    score = jnp.where(coli * SEL_BLOCK <= t_q, imp + jnp.where(forced, FORCE_BONUS, 0.0), NEG)
    nsl = seq // SEL_BLOCK
    st = score.T[:nsl]
    jrow = lax.broadcasted_iota(jnp.int32, st.shape, 0)
    rank = jnp.zeros(st.shape, F32)
    for k in range(nsl):
        rk = st[k:k + 1, :]
        tie = jnp.where(jrow > k, jnp.where(rk == st, 1.0, 0.0), 0.0)
        rank = rank + jnp.where(rk > st, 1.0, tie)
    drop_t = jnp.where(rank < float(min(SEL_TOPK, nsl)), 0.0, -MASK_BIG)
    if nsl < 128:
        drop_t = jnp.concatenate([drop_t, jnp.zeros((128 - nsl, Q_BLOCK), F32)], axis=0)
    drop = drop_t.T.astype(BF16)
    q_sel = jnp.concatenate([jnp.concatenate([qh[r], drop], axis=1) for r in range(NSA_REP)], axis=0)

    n_tiles = (qs + Q_BLOCK + SEL_TILE - 1) // SEL_TILE
    dist_s = t_of((rows, SEL_TILE)) - lax.broadcasted_iota(jnp.int32, (rows, SEL_TILE), 1)

    def scores(kt):
        k0 = pl.multiple_of(kt * SEL_TILE, SEL_TILE)
        return _nt_dot(q_sel, kv_ref[pl.ds(k0, SEL_TILE), KS_OFF:KS_OFF + 256])

    def values(kt):
        k0 = pl.multiple_of(kt * SEL_TILE, SEL_TILE)
        return kv_ref[pl.ds(k0, SEL_TILE), VS_OFF:VS_OFF + 128]

    def trip(kt, m_old):
        s_next = scores(jnp.minimum(kt + 1, n_tiles - 1))
        acc[...] = a_buf[...] * acc[...] + jnp.dot(p_buf[...], values(jnp.maximum(kt - 1, 0)),
                                                   preferred_element_type=F32)
        s = jnp.where(dist_s >= kt * SEL_TILE, s_buf[...], NEG)
        m_new = jnp.maximum(m_old, jnp.max(s, axis=-1, keepdims=True))
        p_buf[...] = jnp.exp(s - m_new).astype(BF16)
        a_buf[...] = jnp.exp(m_old - m_new)
        s_buf[...] = s_next
        return m_new

    acc[...] = jnp.zeros_like(acc)
    p_buf[...] = jnp.zeros_like(p_buf)
    a_buf[...] = jnp.ones_like(a_buf)
    s_buf[...] = scores(0)
    lax.fori_loop(0, n_tiles, trip, jnp.full((rows, 1), NEG, F32))
    a_sel = a_buf[...] * acc[...] + jnp.dot(p_buf[...], values(n_tiles - 1), preferred_element_type=F32)

    gate = jax.nn.sigmoid(g_ref[...])
    outs = []
    for r in range(NSA_REP):
        rr = slice(r * Q_BLOCK, (r + 1) * Q_BLOCK)
        o_s = a_sel[rr, :dk] / a_sel[rr, ONE_LANE:ONE_LANE + 1]
        o_w = a_win[rr, :dk] / a_win[rr, ONE_LANE:ONE_LANE + 1]
        outs.append(gate[:, 3 * r:3 * r + 1] * o_cmp[rr, :dk] + gate[:, 3 * r + 1:3 * r + 2] * o_s
                    + gate[:, 3 * r + 2:3 * r + 3] * o_w)
    o_ref[...] = jnp.concatenate(outs, axis=1).astype(o_ref.dtype)


def _nsa_body_v3(q_ref, kcv_ref, kv_ref, g_ref, ovt_ref, o_ref, acc_sel, acc_win):
    i = pl.program_id(2)
    qs = i * Q_BLOCK
    ncmp = kcv_ref.shape[0]
    cols = NSA_REP * Q_BLOCK
    dk = NSA_HEAD_DIM
    qh = [q_ref[:, r * 128:(r + 1) * 128] for r in range(NSA_REP)]
    q_win = jnp.concatenate(qh, axis=0)

    kc = kcv_ref[:, :128]
    vc = kcv_ref[:, 128:]
    t_c = qs + (lax.broadcasted_iota(jnp.int32, (ncmp, cols), 1) & (Q_BLOCK - 1))
    c_end = lax.broadcasted_iota(jnp.int32, (ncmp, cols), 0) * CMP_STRIDE + (CMP_BLOCK - 1)
    vis_c = t_c >= c_end
    s = jnp.where(vis_c, _nt_dot(kc, q_win), NEG)
    p = jnp.exp(s - jnp.max(s, axis=0, keepdims=True)) * jnp.where(vis_c, 1.0, 0.0)
    p = (p / jnp.maximum(jnp.sum(p, axis=0, keepdims=True), 1e-30)).astype(BF16)
    o_cmp = _tn_dot(vc, p)

    p_heads = jnp.concatenate([p[:, r * Q_BLOCK:(r + 1) * Q_BLOCK] for r in range(NSA_REP)], axis=0)
    imp = jnp.dot(ovt_ref[...], p_heads, preferred_element_type=F32)
    jrow = lax.broadcasted_iota(jnp.int32, (128, Q_BLOCK), 0)
    t_q = qs + lax.broadcasted_iota(jnp.int32, (128, Q_BLOCK), 1)
    cur = t_q >> 6
    forced = (jrow == 0) | (jrow == cur) | (jrow == cur - 1)
    score = jnp.where(jrow * SEL_BLOCK <= t_q, imp + jnp.where(forced, FORCE_BONUS, 0.0), NEG)
    nsl = kv_ref.shape[0] // SEL_BLOCK
    st = score[:nsl]
    jrow = jrow[:nsl]
    rank = jnp.zeros(st.shape, F32)
    for k in range(nsl):
        rk = st[k:k + 1, :]
        tie = jnp.where(jrow > k, jnp.where(rk == st, 1.0, 0.0), 0.0)
        rank = rank + jnp.where(rk > st, 1.0, tie)
    drop_t = jnp.where(rank < float(min(SEL_TOPK, nsl)), 0.0, -MASK_BIG)
    if nsl < 128:
        drop_t = jnp.concatenate([drop_t, jnp.zeros((128 - nsl, Q_BLOCK), F32)], axis=0)
    drop = drop_t.T.astype(BF16)
    q_sel = jnp.concatenate([jnp.concatenate([qh[r], drop], axis=1) for r in range(NSA_REP)], axis=0)

    half = cols // 2
    t_lane = qs + (lax.broadcasted_iota(jnp.int32, (SEL_TILE, half), 1) & (Q_BLOCK - 1))
    dist_s = t_lane - lax.broadcasted_iota(jnp.int32, (SEL_TILE, half), 0)
    dist_w = dist_s[:WIN_TILE]

    def step(acc, qq, k_off, k_w, v_off, tile, mask_of, kt, m_old):
        k0 = pl.multiple_of(kt * tile, tile)
        kk = kv_ref[pl.ds(k0, tile), k_off:k_off + k_w]
        vv = kv_ref[pl.ds(k0, tile), v_off:v_off + 128]
        ok = None if mask_of is None else mask_of(k0)
        m_out = []
        for c in range(2):
            lanes = slice(c * half, (c + 1) * half)
            s = _nt_dot(kk, qq[lanes])
            if ok is not None:
                s = jnp.where(ok, s, NEG)
            m_new = jnp.maximum(m_old[c], jnp.max(s, axis=0, keepdims=True))
            pt = jnp.exp(s - m_new).astype(BF16)
            acc[:, lanes] = jnp.exp(m_old[c] - m_new) * acc[:, lanes] + _tn_dot(vv, pt)
            m_out.append(m_new)
        return tuple(m_out)

    m0 = (jnp.full((1, half), NEG, F32),) * 2
    acc_sel[...] = jnp.zeros_like(acc_sel)
    sel_step = functools.partial(step, acc_sel, q_sel, KS_OFF, 256, VS_OFF, SEL_TILE)
    n_free = (i * Q_BLOCK) // SEL_TILE
    m_s = lax.fori_loop(0, n_free, functools.partial(sel_step, None), m0)
    sel_step(lambda k0: dist_s >= k0, n_free, m_s)
    acc_win[...] = jnp.zeros_like(acc_win)
    win_step = functools.partial(step, acc_win, q_win, KW_OFF, 128, VW_OFF, WIN_TILE,
                                 lambda k0: (dist_w >= k0) & (dist_w < k0 + WINDOW))
    lax.fori_loop(jnp.maximum(i - WINDOW // WIN_TILE, 0), i + 1, win_step, m0)

    gate = jax.nn.sigmoid(g_ref[...]).T
    outs = []
    for r in range(NSA_REP):
        lanes = slice(r * Q_BLOCK, (r + 1) * Q_BLOCK)
        a_s, a_w = acc_sel[:, lanes], acc_win[:, lanes]
        o_s = a_s[:dk] / a_s[ONE_LANE:ONE_LANE + 1]
        o_w = a_w[:dk] / a_w[ONE_LANE:ONE_LANE + 1]
        outs.append(gate[3 * r:3 * r + 1] * o_cmp[:dk, lanes] + gate[3 * r + 1:3 * r + 2] * o_s
                    + gate[3 * r + 2:3 * r + 3] * o_w)
    o_ref[...] = jnp.concatenate(outs, axis=0).T.astype(o_ref.dtype)


def _nsa_body_v2(q_ref, kcv_ref, kv_ref, g_ref, ov_ref, o_ref, acc_sel, acc_win):
    i = pl.program_id(2)
    qs = i * Q_BLOCK
    ncmp = kcv_ref.shape[0]
    rows = NSA_REP * Q_BLOCK
    qh = [q_ref[:, r * 128:(r + 1) * 128] for r in range(NSA_REP)]

    kc = kcv_ref[:, :128]
    vc = kcv_ref[:, 128:]
    t_c = qs + lax.broadcasted_iota(jnp.int32, (Q_BLOCK, ncmp), 0)
    c_end = lax.broadcasted_iota(jnp.int32, (Q_BLOCK, ncmp), 1) * CMP_STRIDE + (CMP_BLOCK - 1)
    vis_c = t_c >= c_end
    vis_cf = jnp.where(vis_c, 1.0, 0.0)
    o_cmp, p_all = [], []
    for r in range(NSA_REP):
        s = jnp.where(vis_c, _nt_dot(qh[r], kc), NEG)
        m = jnp.max(s, axis=-1, keepdims=True)
        p = jnp.exp(s - m) * vis_cf
        p = p / jnp.maximum(jnp.sum(p, axis=-1, keepdims=True), 1e-30)
        pb = p.astype(BF16)
        p_all.append(pb)
        o_cmp.append(jnp.dot(pb, vc, preferred_element_type=F32))

    rowi = lax.broadcasted_iota(jnp.int32, (Q_BLOCK, 128), 0)
    coli = lax.broadcasted_iota(jnp.int32, (Q_BLOCK, 128), 1)
    imp = jnp.dot(jnp.concatenate(p_all, axis=1), ov_ref[...], preferred_element_type=F32)
    t_q = qs + rowi
    cur = t_q >> 6
    forced = (coli == 0) | (coli == cur) | (coli == cur - 1)
    score = jnp.where(coli * SEL_BLOCK <= t_q, imp + jnp.where(forced, FORCE_BONUS, 0.0), NEG)
    nsl = kv_ref.shape[0] // SEL_BLOCK
    st = score.T[:nsl]
    jrow = lax.broadcasted_iota(jnp.int32, st.shape, 0)
    rank = jnp.zeros(st.shape, F32)
    for k in range(nsl):
        rk = st[k:k + 1, :]
        tie = jnp.where(jrow > k, jnp.where(rk == st, 1.0, 0.0), 0.0)
        rank = rank + jnp.where(rk > st, 1.0, tie)
    drop_t = jnp.where(rank < float(min(SEL_TOPK, nsl)), 0.0, -MASK_BIG)
    if nsl < 128:
        drop_t = jnp.concatenate([drop_t, jnp.zeros((128 - nsl, Q_BLOCK), F32)], axis=0)
    drop = drop_t.T.astype(BF16)
    q_sel = jnp.concatenate([jnp.concatenate([qh[r], drop], axis=1) for r in range(NSA_REP)], axis=0)
    q_win = jnp.concatenate(qh, axis=0)

    t_row = qs + (lax.broadcasted_iota(jnp.int32, (rows, SEL_TILE), 0) & (Q_BLOCK - 1))
    k_col = lax.broadcasted_iota(jnp.int32, (rows, SEL_TILE), 1)
    dist_s = t_row - k_col
    dist_w = dist_s[:, :WIN_TILE]

    def step(acc, qq, k_off, k_w, v_off, tile, mask_of, kt, m_old):
        k0 = pl.multiple_of(kt * tile, tile)
        kk = kv_ref[pl.ds(k0, tile), k_off:k_off + k_w]
        vv = kv_ref[pl.ds(k0, tile), v_off:v_off + 128]
        s = _nt_dot(qq, kk)
        if mask_of is not None:
            s = jnp.where(mask_of(k0), s, NEG)
        m_new = jnp.maximum(m_old, jnp.max(s, axis=-1, keepdims=True))
        p = jnp.exp(s - m_new).astype(BF16)
        acc[...] = jnp.exp(m_old - m_new) * acc[...] + jnp.dot(p, vv, preferred_element_type=F32)
        return m_new

    m0 = jnp.full((rows, 1), NEG, F32)
    acc_sel[...] = jnp.zeros_like(acc_sel)
    sel_step = functools.partial(step, acc_sel, q_sel, KS_OFF, 256, VS_OFF, SEL_TILE)
    n_free = (i * Q_BLOCK) // SEL_TILE
    m_s = lax.fori_loop(0, n_free, functools.partial(sel_step, None), m0)
    sel_step(lambda k0: dist_s >= k0, n_free, m_s)
    acc_win[...] = jnp.zeros_like(acc_win)
    win_step = functools.partial(step, acc_win, q_win, KW_OFF, 128, VW_OFF, WIN_TILE,
                                 lambda k0: (dist_w >= k0) & (dist_w < k0 + WINDOW))
    lax.fori_loop(jnp.maximum(i - WINDOW // WIN_TILE, 0), i + 1, win_step, m0)

    gate = jax.nn.sigmoid(g_ref[...])
    outs = []
    for r in range(NSA_REP):
        a_s = acc_sel[r * Q_BLOCK:(r + 1) * Q_BLOCK, :]
        a_w = acc_win[r * Q_BLOCK:(r + 1) * Q_BLOCK, :]
        dk = NSA_HEAD_DIM
        o_s = a_s[:, :dk] / a_s[:, ONE_LANE:ONE_LANE + 1]
        o_w = a_w[:, :dk] / a_w[:, ONE_LANE:ONE_LANE + 1]
        outs.append(gate[:, 3 * r:3 * r + 1] * o_cmp[r][:, :dk] + gate[:, 3 * r + 1:3 * r + 2] * o_s
                    + gate[:, 3 * r + 2:3 * r + 3] * o_w)
    o_ref[...] = jnp.concatenate(outs, axis=1).astype(o_ref.dtype)


def _stage_attn(qp, kcv, kv, gates, bsz, seq):
    n = bsz * seq
    nq = seq // Q_BLOCK
    ncmp = seq // CMP_STRIDE
    nsl = seq // SEL_BLOCK
    assert seq % SEL_TILE == 0
    c_start = np.arange(ncmp) * CMP_STRIDE
    s_start = np.arange(nsl) * SEL_BLOCK
    ov = ((c_start[:, None] < s_start[None, :] + SEL_BLOCK) & (c_start[:, None] + CMP_BLOCK > s_start[None, :]))
    ov = np.pad(ov.astype(np.float32), ((0, 0), (0, 128 - nsl)))
    ov[ncmp - 1] = 0.0
    ov = jnp.asarray(np.tile(ov, (NSA_REP, 1)), BF16)
    width = NSA_REP * NSA_HEAD_DIM
    rows = NSA_REP * Q_BLOCK
    return pl.pallas_call(
        _nsa_body, grid=(bsz, NSA_KV_GROUPS, nq),
        in_specs=[pl.BlockSpec((Q_BLOCK, NSA_REP * 128), lambda b, g, i: (b * nq + i, g)),
                  pl.BlockSpec((ncmp, 256), lambda b, g, i: (b, g)),
                  pl.BlockSpec((seq, KV_REC), lambda b, g, i: (b, g)),
                  pl.BlockSpec((Q_BLOCK, 128), lambda b, g, i: (b * nq + i, g)),
                  _const_spec(ov.shape)],
        out_specs=pl.BlockSpec((Q_BLOCK, width), lambda b, g, i: (b * nq + i, g)),
        out_shape=jax.ShapeDtypeStruct((n, NSA_KV_GROUPS * width), BF16),
        scratch_shapes=[pltpu.VMEM((rows, 128), F32), pltpu.VMEM((rows, SEL_TILE), F32),
                        pltpu.VMEM((rows, SEL_TILE), BF16), pltpu.VMEM((rows, 1), F32)],
        compiler_params=_cparams(("parallel", "parallel", "arbitrary")), name="nsa_attention")(
            qp, kcv, kv, gates, ov)


def _merge_body(x_ref, ya_ref, ob_ref, gm_ref, wnsa_ref, wout_ref, g_ref, b_ref, o_ref):
    yb = jnp.dot(ob_ref[...], wnsa_ref[...], preferred_element_type=F32)
    mix_in = gm_ref[:, :D_MODEL] * ya_ref[...] + gm_ref[:, D_MODEL:] * yb
    mix = jnp.dot(mix_in.astype(BF16), wout_ref[...], preferred_element_type=F32)
    o_ref[...] = _layer_norm(DN_ALPHA * x_ref[...] + mix, g_ref[...], b_ref[...])


def _stage_merge(x2, ya, ob, gm, w_up_nsa, w_out, ln_g, ln_b):
    n = x2.shape[0]
    tm = 256
    row = lambda w: pl.BlockSpec((tm, w), lambda i: (i, 0))
    return pl.pallas_call(
        _merge_body, grid=(n // tm,),
        in_specs=[row(D_MODEL), row(D_MODEL), row(ob.shape[1]), row(2 * D_MODEL),
                  _const_spec(w_up_nsa.shape), _const_spec(w_out.shape),
                  _const_spec((1, D_MODEL)), _const_spec((1, D_MODEL))],
        out_specs=row(D_MODEL), out_shape=jax.ShapeDtypeStruct((n, D_MODEL), F32),
        compiler_params=_cparams(("parallel",)), name="merge_ln")(
            x2, ya, ob, gm, w_up_nsa.astype(BF16), w_out.astype(BF16),
            ln_g.reshape(1, -1).astype(F32), ln_b.reshape(1, -1).astype(F32))


BIG_NEG = -3.0e38


def _top_rows(s, k):
    rid = lax.broadcasted_iota(jnp.int32, s.shape, 0).astype(F32)
    rank = jnp.full(s.shape, float(k), F32)
    vals, idxs = [], []
    for r in range(k):
        m = jnp.max(s, axis=0, keepdims=True)
        idx = jnp.min(jnp.where(s == m, rid, float(s.shape[0])), axis=0, keepdims=True)
        hit = rid == idx
        rank = jnp.where(hit, float(r), rank)
        s = jnp.where(hit, BIG_NEG, s)
        vals.append(m)
        idxs.append(idx)
    return jnp.concatenate(vals, axis=0), jnp.concatenate(idxs, axis=0), rank


def _peer_body(x_ref, wq_ref, sk_ref, u_ref, vt_ref, g_ref, b_ref, o_ref,
               xt_sc, acc_sc, n_sc, e0_sc, rk1_sc, e1_sc, p_sc, st_sc):
    e = pl.program_id(1)
    k = PEER_TOPK
    nk = PEER_NKEYS

    @pl.when(e == 0)
    def _():
        xt = x_ref[...].T.astype(BF16)
        xt_sc[...] = xt
        acc_sc[...] = jnp.zeros_like(acc_sc)
        qt = jnp.dot(wq_ref[...], xt, preferred_element_type=F32).astype(BF16)
        st_sc[...] = jnp.dot(sk_ref[...], qt, preferred_element_type=F32)

        @pl.loop(0, PEER_HEADS)
        def _(h):
            s0 = st_sc[pl.ds(pl.multiple_of(2 * h * nk, nk), nk), :]
            s1 = st_sc[pl.ds(pl.multiple_of((2 * h + 1) * nk, nk), nk), :]
            a, _, rk0 = _top_rows(s0, k)
            b, _, rk1 = _top_rows(s1, k)
            cand = jnp.concatenate([a[r:r + 1] + b for r in range(k)], axis=0)
            best, bidx, _ = _top_rows(cand, k)
            brank = jnp.floor(bidx * (1.0 / k))
            z = jnp.sum(jnp.exp(best - best[0:1]), axis=0, keepdims=True)
            n_i = jnp.zeros(s0.shape, F32)
            for r in range(k):
                nr = jnp.sum(jnp.where(brank == float(r), 1.0, 0.0), axis=0, keepdims=True)
                n_i = n_i + jnp.where(rk0 == float(r), nr, 0.0)
            n_sc[h] = n_i
            e0_sc[h] = jnp.exp(s0 - a[0:1]) / z
            rk1_sc[h] = rk1
            e1_sc[h] = jnp.exp(s1 - b[0:1])

    ht = jnp.dot(u_ref[...], xt_sc[...], preferred_element_type=F32)
    act = jax.nn.gelu(ht)
    rows = u_ref.shape[0] // nk
    for ii in range(rows):
        i = e * rows + ii
        w = jnp.zeros((nk, ht.shape[1]), F32)
        for h in range(PEER_HEADS):
            n_row = n_sc[h, pl.ds(i, 1), :]
            e0_row = e0_sc[h, pl.ds(i, 1), :]
            w = w + jnp.where(rk1_sc[h] < n_row, e0_row * e1_sc[h], 0.0)
        p_sc[ii * nk:(ii + 1) * nk, :] = (w * act[ii * nk:(ii + 1) * nk]).astype(BF16)
    acc_sc[...] += jnp.dot(vt_ref[...], p_sc[...], preferred_element_type=F32)

    @pl.when(e == pl.num_programs(1) - 1)
    def _():
        x = x_ref[...]
        o_ref[...] = _layer_norm(DN_ALPHA * x + acc_sc[...].T, g_ref[...], b_ref[...])


def _stage_peer(x1, w_q, subkeys, u_tab, v_tab, ln_g, ln_b, tt=512, et=512):
    n = x1.shape[0]
    tt = min(tt, n)
    H, nk, half = PEER_HEADS, PEER_NKEYS, subkeys.shape[-1]
    wq_t = w_q.T.astype(BF16)
    eye = jnp.eye(2 * H, dtype=F32)
    sk = subkeys.reshape(2 * H, nk, half)
    sk_t = jnp.einsum('ckd,ce->cked', sk, eye).reshape(2 * H * nk, 2 * H * half).astype(BF16)
    u_b = u_tab.astype(BF16)
    v_t = v_tab.T.astype(BF16)
    ne = u_tab.shape[0]
    sel_sc = pltpu.VMEM((H, nk, tt), F32)
    return pl.pallas_call(
        _peer_body, grid=(n // tt, ne // et),
        in_specs=[pl.BlockSpec((tt, D_MODEL), lambda t, e: (t, 0)),
                  _const_spec(wq_t.shape), _const_spec(sk_t.shape),
                  pl.BlockSpec((et, D_MODEL), lambda t, e: (e, 0)),
                  pl.BlockSpec((D_MODEL, et), lambda t, e: (0, e)),
                  _const_spec((1, D_MODEL)), _const_spec((1, D_MODEL))],
        out_specs=pl.BlockSpec((tt, D_MODEL), lambda t, e: (t, 0)),
        out_shape=jax.ShapeDtypeStruct((n, D_MODEL), F32),
        scratch_shapes=[pltpu.VMEM((D_MODEL, tt), BF16), pltpu.VMEM((D_MODEL, tt), F32),
                        sel_sc, sel_sc, sel_sc, sel_sc, pltpu.VMEM((et, tt), BF16),
                        pltpu.VMEM((2 * H * nk, tt), F32)],
        compiler_params=_cparams(("parallel", "arbitrary")), name="peer_ffn")(
            x1, wq_t, sk_t, u_b, v_t, ln_g.reshape(1, -1).astype(F32), ln_b.reshape(1, -1).astype(F32))


def _layer(x, w_in, b_in, ssm_a_re, ssm_a_im, ssm_log_dt, ssm_b_re, ssm_b_im, ssm_c_re, ssm_c_im, ssm_d,
           w_glu, w_up_ssm, nsa_pe_k, nsa_w1_k, nsa_w2_k, nsa_pe_v, nsa_w1_v, nsa_w2_v, w_up_nsa,
           w_out, ln1_g, ln1_b, peer_w_q, peer_subkeys, peer_u, peer_v, ln2_g, ln2_b):
    bsz, seq, _ = x.shape
    x2 = x.reshape(bsz * seq, D_MODEL)
    u, qp, kvc, kv4, gates, gm = _stage_proj(x2, w_in, b_in, seq)
    ya = _stage_ssm(u, bsz, seq, ssm_a_re, ssm_a_im, ssm_log_dt, ssm_b_re, ssm_b_im, ssm_c_re, ssm_c_im,
                    ssm_d, w_glu, w_up_ssm)
    kcv = _stage_compress(kvc, bsz, seq, nsa_pe_k, nsa_w1_k, nsa_w2_k, nsa_pe_v, nsa_w1_v, nsa_w2_v)
    ob = _stage_attn(qp, kcv, kv4, gates, bsz, seq)
    x1 = _stage_merge(x2, ya, ob, gm, w_up_nsa, w_out, ln1_g, ln1_b)
    out = _stage_peer(x1, peer_w_q, peer_subkeys, peer_u, peer_v, ln2_g, ln2_b)
    return out.reshape(bsz, seq, D_MODEL)


def kernel(x, w_in, b_in, ssm_a_re, ssm_a_im, ssm_log_dt, ssm_b_re, ssm_b_im, ssm_c_re, ssm_c_im, ssm_d, w_glu,
           w_up_ssm, nsa_pe_k, nsa_w1_k, nsa_w2_k, nsa_pe_v, nsa_w1_v, nsa_w2_v, w_up_nsa, w_out, ln1_g, ln1_b,
           peer_w_q, peer_subkeys, peer_u, peer_v, ln2_g, ln2_b):
    params = (w_in, b_in, ssm_a_re, ssm_a_im, ssm_log_dt, ssm_b_re, ssm_b_im, ssm_c_re, ssm_c_im, ssm_d, w_glu,
              w_up_ssm, nsa_pe_k, nsa_w1_k, nsa_w2_k, nsa_pe_v, nsa_w1_v, nsa_w2_v, w_up_nsa, w_out, ln1_g, ln1_b,
              peer_w_q, peer_subkeys, peer_u, peer_v, ln2_g, ln2_b)
    for layer in range(w_in.shape[0]):
        x = _layer(x, *[p[layer] for p in params])
    return x
```

```python
import functools
import math

import jax
import jax.numpy as jnp
import numpy as np
from jax import lax
from jax.experimental import pallas as pl
from jax.experimental.pallas import tpu as pltpu

F32 = jnp.float32
BF16 = jnp.bfloat16

D_MODEL = 1024
SSM_WIDTH = 512
SSM_GROUP = 16
SSM_GROUPS = 32
SSM_STATE = 64
SSM_CHUNK = 8
LANE_GROUPS = 8
N_LANE_TILES = SSM_WIDTH // 128
NSA_HEADS = 8
NSA_KV_GROUPS = 2
NSA_REP = NSA_HEADS // NSA_KV_GROUPS
NSA_HEAD_DIM = 64
KV_WIDTH = NSA_KV_GROUPS * NSA_HEAD_DIM
CMP_BLOCK = 32
CMP_STRIDE = 16
CMP_HIDDEN = 128
SEL_BLOCK = 64
SEL_TOPK = 16
WINDOW = 512
Q_BLOCK = 128
FORCE_BONUS = 1.0e4
PEER_HEADS = 8
PEER_NKEYS = 128
PEER_EXPERTS = PEER_NKEYS * PEER_NKEYS
PEER_TOPK = 16
DN_ALPHA = 2.0 ** 0.25
LN_EPS = 1e-5
NEG = -1e30
VMEM_LIMIT = 56 * 1024 * 1024


def _cparams(sem):
    return pltpu.CompilerParams(dimension_semantics=sem, vmem_limit_bytes=VMEM_LIMIT)


def _const_spec(shape):
    nd = len(shape)
    return pl.BlockSpec(shape, lambda *_: (0,) * nd, pipeline_mode=pl.Buffered(1))


def _layer_norm(z, g, b):
    mu = jnp.mean(z, axis=-1, keepdims=True)
    var = jnp.mean(jnp.square(z - mu), axis=-1, keepdims=True)
    return (z - mu) * lax.rsqrt(var + LN_EPS) * g + b


KV_REC = 640
KS_OFF, VS_OFF, KW_OFF, VW_OFF = 0, 256, 384, 512
POS_HI, POS_LO, ONE_LANE, BLOCK_LANE0 = 64, 65, 64, 128
assert SEL_BLOCK == 64


def _position_features(pos, lane):
    hi, lo = pos >> 6, pos & 63
    f = jnp.where(lane == KS_OFF + POS_HI, hi, 0) + jnp.where(lane == KW_OFF + POS_HI, hi, 0)
    f = f + jnp.where(lane == KS_OFF + POS_LO, lo, 0) + jnp.where(lane == KW_OFF + POS_LO, lo, 0)
    f = f + jnp.where(lane == KS_OFF + BLOCK_LANE0 + hi, 1, 0)
    f = f + jnp.where(lane == VS_OFF + ONE_LANE, 1, 0) + jnp.where(lane == VW_OFF + ONE_LANE, 1, 0)
    return f.astype(F32)


def _proj_body(seq, x_ref, wu, bu, wq, bq, wc, bc, wk, bk, wg, bg, wm, bm,
               u_o, q_o, c_o, k_o, g_o, m_o):
    xb = x_ref[...].astype(BF16)
    tm = x_ref.shape[0]

    def lin(w, b):
        return jnp.dot(xb, w[...], preferred_element_type=F32) + b[...]

    u_o[...] = lin(wu, bu)
    q_o[...] = (lin(wq, bq) * (NSA_HEAD_DIM ** -0.5)).astype(BF16)
    c_o[...] = lin(wc, bc)
    pos0 = (pl.program_id(0) % (seq // tm)) * tm
    shape = (tm, NSA_KV_GROUPS * KV_REC)
    lane = lax.broadcasted_iota(jnp.int32, shape, 1)
    lane = jnp.where(lane >= KV_REC, lane - KV_REC, lane)
    feat = _position_features(pos0 + lax.broadcasted_iota(jnp.int32, shape, 0), lane)
    k_o[...] = (lin(wk, bk) + feat).astype(BF16)
    g_o[...] = lin(wg, bg)
    m_o[...] = jax.nn.sigmoid(lin(wm, bm))


def _stage_proj(x2, w_in, b_in, seq):
    n = x2.shape[0]
    tm = 256
    o0 = SSM_WIDTH
    o1 = o0 + NSA_HEADS * NSA_HEAD_DIM
    o2 = o1 + 6 * KV_WIDTH
    o3 = o2 + 3 * NSA_HEADS
    w_u, b_u = w_in[:, :o0], b_in[:o0]
    w_q, b_q = w_in[:, o0:o1], b_in[o0:o1]
    w_kv, b_kv = w_in[:, o1:o2], b_in[o1:o2]
    w_g, b_g = w_in[:, o2:o3], b_in[o2:o3]
    w_m, b_m = w_in[:, o3:], b_in[o3:]

    def pad_q(a, aux):
        a = a.reshape(a.shape[:-1] + (NSA_HEADS, NSA_HEAD_DIM))
        aux = jnp.broadcast_to(aux, a.shape[:-2] + aux.shape)
        return jnp.concatenate([a, aux], axis=-1).reshape(a.shape[:-2] + (NSA_HEADS * 128,))

    slope = 2.0 ** -(np.arange(NSA_HEADS, dtype=np.float32) + 1.0) * NSA_HEAD_DIM ** 0.5
    q_aux = np.zeros((NSA_HEADS, 128 - NSA_HEAD_DIM), np.float32)
    q_aux[:, POS_HI - NSA_HEAD_DIM] = slope * 64.0
    q_aux[:, POS_LO - NSA_HEAD_DIM] = slope

    def kv_records(a):
        parts = a.reshape(a.shape[:-1] + (6, NSA_KV_GROUPS, NSA_HEAD_DIM))
        z = lambda w: jnp.zeros(a.shape[:-1] + (w,), a.dtype)
        recs = []
        for g in range(NSA_KV_GROUPS):
            k_s, v_s, k_w, v_w = (parts[..., c, g, :] for c in (2, 3, 4, 5))
            recs += [k_s, z(256 - NSA_HEAD_DIM), v_s, z(128 - NSA_HEAD_DIM), k_w, z(128 - NSA_HEAD_DIM),
                     v_w, z(128 - NSA_HEAD_DIM)]
        return jnp.concatenate(recs, axis=-1)

    def pad_g(a):
        a = a.reshape(a.shape[:-1] + (NSA_KV_GROUPS, 3 * NSA_REP))
        a = jnp.concatenate([a, jnp.zeros(a.shape[:-1] + (128 - 3 * NSA_REP,), a.dtype)], axis=-1)
        return a.reshape(a.shape[:-2] + (NSA_KV_GROUPS * 128,))

    ws = [w_u, pad_q(w_q, jnp.zeros_like(q_aux)), w_kv[:, :2 * KV_WIDTH], kv_records(w_kv), pad_g(w_g), w_m]
    bs = [b_u, pad_q(b_q, jnp.asarray(q_aux)), b_kv[:2 * KV_WIDTH], kv_records(b_kv), pad_g(b_g), b_m]
    odt = [F32, BF16, F32, BF16, F32, F32]
    args, in_specs = [x2], [pl.BlockSpec((tm, D_MODEL), lambda i: (i, 0))]
    for w, b in zip(ws, bs):
        args += [w.astype(BF16), b.reshape(1, -1).astype(F32)]
        in_specs += [_const_spec(w.shape), _const_spec((1, w.shape[1]))]
    out_shape = [jax.ShapeDtypeStruct((n, w.shape[1]), dt) for w, dt in zip(ws, odt)]
    out_specs = [pl.BlockSpec((tm, w.shape[1]), lambda i: (i, 0)) for w in ws]
    return pl.pallas_call(
        functools.partial(_proj_body, seq), grid=(n // tm,), in_specs=in_specs, out_specs=out_specs, out_shape=out_shape,
        compiler_params=_cparams(("parallel",)), name="in_proj")(*args)


def _ssm_weights(a_re, a_im, log_dt, b_re, b_im, c_re, c_im):
    L, G, P, H, A, J = SSM_CHUNK, SSM_GROUPS, SSM_STATE, SSM_GROUP, LANE_GROUPS, N_LANE_TILES
    lam = lax.complex(a_re.astype(F32), a_im.astype(F32))
    dt = jnp.exp(log_dt.astype(F32))[:, None]
    lam_bar = jnp.exp(lam * dt)
    b_bar = ((lam_bar - 1.0) / lam)[:, :, None] * lax.complex(b_re.astype(F32), b_im.astype(F32))
    c = lax.complex(c_re.astype(F32), c_im.astype(F32))
    k = jnp.arange(L + 1, dtype=F32)
    pw = jnp.exp((lam * dt)[None] * k[:, None, None])
    eye = jnp.eye(A, dtype=F32)
    kern = jnp.real(jnp.einsum('ghp,kgp,gpj->kghj', c, pw[:L], b_bar))
    s_i, t_i = jnp.arange(L)[:, None], jnp.arange(L)[None, :]
    tau = jnp.clip(t_i - s_i, 0, L - 1)
    causal = (t_i >= s_i).astype(F32)
    wfull = kern[tau] * causal[:, :, None, None, None]
    wfull = wfull.reshape(L, L, J, A, H, H)
    w_intra = jnp.einsum('stjahi,ab->jsaitbh', wfull, eye).reshape(J, L * 128, L * 128)
    q = pw[:L][::-1][:, :, :, None] * b_bar[None]
    q = q.reshape(L, J, A, P, H)
    m2 = jnp.stack([jnp.real(q), jnp.imag(q)], axis=0)
    m2 = jnp.einsum('rsjapi,ab->jsairbp', m2, eye).reshape(J, L * 128, 2 * A * P)
    cl = c[None] * pw[1:][:, :, None, :]
    cl = cl.reshape(L, J, A, H, P)
    m1 = jnp.stack([jnp.real(cl), -jnp.imag(cl)], axis=0)
    m1 = jnp.einsum('rtjahp,ab->jraptbh', m1, eye).reshape(J, 2 * A * P, L * 128)
    w2 = jnp.concatenate([w_intra, m1], axis=1)
    lam8 = pw[L].reshape(G * P)
    return m2.astype(BF16), w2.astype(BF16), jnp.real(lam8), jnp.imag(lam8)


def _chunk_inputs(u_ref, j):
    return jnp.concatenate(
        [u_ref[:, s * SSM_WIDTH + j * 128: s * SSM_WIDTH + (j + 1) * 128] for s in range(SSM_CHUNK)], axis=1)


def _ssm_state_body(u_ref, m2_ref, zre_ref, zim_ref):
    half = LANE_GROUPS * SSM_STATE
    for j in range(N_LANE_TILES):
        z = jnp.dot(_chunk_inputs(u_ref, j).astype(BF16), m2_ref[j], preferred_element_type=F32)
        zre_ref[:, j * half:(j + 1) * half] = z[:, :half]
        zim_ref[:, j * half:(j + 1) * half] = z[:, half:]


def _ssm_scan_body(zre_ref, zim_ref, lre_ref, lim_ref, xre_ref, xim_ref, sre, sim):
    @pl.when(pl.program_id(0) == 0)
    def _():
        sre[...] = jnp.zeros_like(sre)
        sim[...] = jnp.zeros_like(sim)

    lr, li = lre_ref[...], lim_ref[...]

    def step(c, carry):
        xr, xi = carry
        xre_ref[c] = xr
        xim_ref[c] = xi
        return lr * xr - li * xi + zre_ref[c], lr * xi + li * xr + zim_ref[c]

    xr, xi = lax.fori_loop(0, zre_ref.shape[0], step, (sre[...], sim[...]))
    sre[...] = xr
    sim[...] = xi


def _ssm_out_body(u_ref, xre_ref, xim_ref, w2_ref, d_ref, wglu_ref, wup_ref, o_ref, y_sc):
    half = LANE_GROUPS * SSM_STATE
    for j in range(N_LANE_TILES):
        lhs = jnp.concatenate(
            [_chunk_inputs(u_ref, j), xre_ref[:, j * half:(j + 1) * half], xim_ref[:, j * half:(j + 1) * half]],
            axis=1).astype(BF16)
        yj = jnp.dot(lhs, w2_ref[j], preferred_element_type=F32)
        for t in range(SSM_CHUNK):
            y_sc[:, t * SSM_WIDTH + j * 128: t * SSM_WIDTH + (j + 1) * 128] = yj[:, t * 128:(t + 1) * 128]
    for t in range(SSM_CHUNK):
        cols = slice(t * SSM_WIDTH, (t + 1) * SSM_WIDTH)
        y = jax.nn.gelu(y_sc[:, cols] + d_ref[...] * u_ref[:, cols])
        gl = jnp.dot(y.astype(BF16), wglu_ref[...], preferred_element_type=F32)
        v = gl[:, :SSM_WIDTH] * jax.nn.sigmoid(gl[:, SSM_WIDTH:])
        o_ref[:, t * D_MODEL:(t + 1) * D_MODEL] = jnp.dot(v.astype(BF16), wup_ref[...], preferred_element_type=F32)


def _stage_ssm(u, bsz, seq, a_re, a_im, log_dt, b_re, b_im, c_re, c_im, d_skip, w_glu, w_up):
    n = bsz * seq
    L = SSM_CHUNK
    nck = seq // L
    gp = SSM_GROUPS * SSM_STATE
    m2, w2, l8re, l8im = _ssm_weights(a_re, a_im, log_dt, b_re, b_im, c_re, c_im)
    u2 = u.reshape(n // L, L * SSM_WIDTH)
    rs = min(nck, 512)
    zre, zim = pl.pallas_call(
        _ssm_state_body, grid=(bsz, nck // rs),
        in_specs=[pl.BlockSpec((rs, L * SSM_WIDTH), lambda b, i: (b * (nck // rs) + i, 0)),
                  _const_spec(m2.shape)],
        out_specs=[pl.BlockSpec((rs, gp), lambda b, i: (i, b))] * 2,
        out_shape=[jax.ShapeDtypeStruct((nck, bsz * gp), F32)] * 2,
        compiler_params=_cparams(("parallel", "parallel")), name="ssm_chunk_state")(u2, m2)
    lanes = bsz * gp // 8
    rep = lanes and (8 * lanes) // gp
    assert 8 * lanes == rep * gp and gp % lanes == 0
    lre = jnp.tile(l8re, rep).reshape(8, lanes)
    lim = jnp.tile(l8im, rep).reshape(8, lanes)
    cs = min(nck, 64)
    blk = pl.BlockSpec((cs, 8, lanes), lambda i: (i, 0, 0))
    xre, xim = pl.pallas_call(
        _ssm_scan_body, grid=(nck // cs,),
        in_specs=[blk, blk, _const_spec((8, lanes)), _const_spec((8, lanes))],
        out_specs=[blk, blk],
        out_shape=[jax.ShapeDtypeStruct((nck, 8, lanes), F32)] * 2,
        scratch_shapes=[pltpu.VMEM((8, lanes), F32)] * 2,
        compiler_params=_cparams(("arbitrary",)), name="ssm_carry_scan")(
            zre.reshape(nck, 8, lanes), zim.reshape(nck, 8, lanes), lre, lim)
    xre, xim = xre.reshape(nck, bsz * gp), xim.reshape(nck, bsz * gp)
    ro = min(nck, 128)
    d8 = d_skip.reshape(1, SSM_WIDTH).astype(F32)
    ya = pl.pallas_call(
        _ssm_out_body, grid=(bsz, nck // ro),
        in_specs=[pl.BlockSpec((ro, L * SSM_WIDTH), lambda b, i: (b * (nck // ro) + i, 0)),
                  pl.BlockSpec((ro, gp), lambda b, i: (i, b)),
                  pl.BlockSpec((ro, gp), lambda b, i: (i, b)),
                  _const_spec(w2.shape), _const_spec((1, SSM_WIDTH)),
                  _const_spec(w_glu.shape), _const_spec(w_up.shape)],
        out_specs=pl.BlockSpec((ro, L * D_MODEL), lambda b, i: (b * (nck // ro) + i, 0)),
        out_shape=jax.ShapeDtypeStruct((n // L, L * D_MODEL), F32),
        scratch_shapes=[pltpu.VMEM((ro, L * SSM_WIDTH), F32)],
        compiler_params=_cparams(("parallel", "parallel")), name="ssm_out")(
            u2, xre, xim, w2, d8, w_glu.astype(BF16), w_up.astype(BF16))
    return ya.reshape(n, D_MODEL)


def _compress_body(f_ref, pea_ref, peb_ref, w1a_ref, w1b_ref, w2_ref, o_ref):
    f = f_ref[...]
    a = jnp.dot((f + pea_ref[...]).astype(BF16), w1a_ref[...], preferred_element_type=F32)
    b = jnp.dot((f + peb_ref[...]).astype(BF16), w1b_ref[...], preferred_element_type=F32)
    pre = a + pltpu.roll(b, b.shape[0] - 1, 0)
    hid = jax.nn.gelu(pre)
    out = jnp.dot(hid.astype(BF16), w2_ref[...], preferred_element_type=F32)
    lane = lax.broadcasted_iota(jnp.int32, out.shape, 1) & 255
    c_end = lax.broadcasted_iota(jnp.int32, out.shape, 0) * CMP_STRIDE + (CMP_BLOCK - 1)
    feat = jnp.where(lane == POS_HI, c_end >> 6, 0) + jnp.where(lane == POS_LO, c_end & 63, 0)
    o_ref[...] = (out + feat.astype(F32)).astype(BF16)


def _stage_compress(kvc, bsz, seq, pe_k, w1_k, w2_k, pe_v, w1_v, w2_v):
    hl = CMP_STRIDE
    nrow = seq // hl
    G, dk, hid = NSA_KV_GROUPS, NSA_HEAD_DIM, CMP_HIDDEN
    eye = jnp.eye(2 * G, dtype=F32)

    def big_w1(lo):
        wk = w1_k.reshape(CMP_BLOCK, dk, hid)[lo:lo + hl]
        wv = w1_v.reshape(CMP_BLOCK, dk, hid)[lo:lo + hl]
        w = jnp.stack([wk, wk, wv, wv], axis=1)
        return jnp.einsum('lcdj,ce->lcdej', w, eye).reshape(hl * 2 * G * dk, 2 * G * hid)

    def big_pe(lo):
        pk, pv = pe_k[lo:lo + hl], pe_v[lo:lo + hl]
        return jnp.stack([pk, pk, pv, pv], axis=1).reshape(1, hl * 2 * G * dk)

    w2 = jnp.stack([w2_k, w2_k, w2_v, w2_v], axis=0)
    w2 = jnp.concatenate([w2, jnp.zeros((2 * G, hid, 128 - dk), w2.dtype)], axis=-1)
    slot = eye[np.array([kv * G + g for g in range(G) for kv in range(2)])].T
    w2 = jnp.einsum('cjd,ce->cjed', w2, slot).reshape(2 * G * hid, 2 * G * 128)
    f = kvc.reshape(bsz * nrow, hl * 2 * G * dk)
    width = hl * 2 * G * dk
    return pl.pallas_call(
        _compress_body, grid=(bsz,),
        in_specs=[pl.BlockSpec((nrow, width), lambda b: (b, 0)),
                  _const_spec((1, width)), _const_spec((1, width)),
                  _const_spec((width, 2 * G * hid)), _const_spec((width, 2 * G * hid)),
                  _const_spec((2 * G * hid, 2 * G * 128))],
        out_specs=pl.BlockSpec((nrow, 2 * G * 128), lambda b: (b, 0)),
        out_shape=jax.ShapeDtypeStruct((bsz * nrow, 2 * G * 128), BF16),
        compiler_params=_cparams(("parallel",)), name="nsa_compress")(
            f, big_pe(0).astype(F32), big_pe(hl).astype(F32),
            big_w1(0).astype(BF16), big_w1(hl).astype(BF16), w2.astype(BF16))


def _nt_dot(a, b):
    return lax.dot_general(a, b, (((1,), (1,)), ((), ())), preferred_element_type=F32)


def _nsa_body_old(q_ref, kcv_ref, kv_ref, g_ref, ov_ref, o_ref, acc_sc):
    i = pl.program_id(1)
    g = pl.program_id(2)
    qs = i * Q_BLOCK
    ncmp = kcv_ref.shape[0]
    rowi = lax.broadcasted_iota(jnp.int32, (Q_BLOCK, 128), 0)
    coli = lax.broadcasted_iota(jnp.int32, (Q_BLOCK, 128), 1)
    base = (rowi - coli).astype(F32)
    slopes = [jnp.where(g == 0, 2.0 ** -(r + 1), 2.0 ** -(r + 1 + NSA_REP)).astype(F32) for r in range(NSA_REP)]
    qh = [q_ref[:, r * 128:(r + 1) * 128] for r in range(NSA_REP)]

    kc = kcv_ref[:, :KV_WIDTH]
    vc = kcv_ref[:, KV_WIDTH:]
    t_c = qs + lax.broadcasted_iota(jnp.int32, (Q_BLOCK, ncmp), 0)
    c_end = lax.broadcasted_iota(jnp.int32, (Q_BLOCK, ncmp), 1) * CMP_STRIDE + (CMP_BLOCK - 1)
    dist_c = (t_c - c_end).astype(F32)
    vis_c = dist_c >= 0.0
    vis_cf = vis_c.astype(F32)
    o_cmp, p_all = [], []
    for r in range(NSA_REP):
        s = _nt_dot(qh[r], kc) - slopes[r] * dist_c
        s = jnp.where(vis_c, s, NEG)
        m = jnp.max(s, axis=-1, keepdims=True)
        p = jnp.exp(s - m) * vis_cf
        p = p / jnp.maximum(jnp.sum(p, axis=-1, keepdims=True), 1e-30)
        pb = p.astype(BF16)
        p_all.append(pb)
        o_cmp.append(jnp.dot(pb, vc, preferred_element_type=F32))

    imp = jnp.dot(jnp.concatenate(p_all, axis=1), ov_ref[...], preferred_element_type=F32)
    t_q = qs + rowi
    cur = t_q // SEL_BLOCK
    forced = (coli == 0) | (coli == cur) | (coli == cur - 1)
    score = jnp.where(coli * SEL_BLOCK <= t_q, imp + jnp.where(forced, FORCE_BONUS, 0.0), NEG)
    nsl = kv_ref.shape[0] // SEL_BLOCK
    st = score.T[:nsl]
    jrow = lax.broadcasted_iota(jnp.int32, st.shape, 0)
    rank = jnp.zeros(st.shape, F32)
    for k in range(nsl):
        rk = st[k:k + 1, :]
        tie = jnp.where(jrow > k, jnp.where(rk == st, 1.0, 0.0), 0.0)
        rank = rank + jnp.where(rk > st, 1.0, tie)
    sel_t = jnp.where(rank < float(min(SEL_TOPK, nsl)), 1.0, 0.0)
    if nsl < 128:
        sel_t = jnp.concatenate([sel_t, jnp.zeros((128 - nsl, Q_BLOCK), F32)], axis=0)
    sel = sel_t.T.astype(BF16)
    jcol = lax.broadcasted_iota(jnp.int32, (128, 128), 0)
    kcol = lax.broadcasted_iota(jnp.int32, (128, 128), 1) // SEL_BLOCK

    def attend(kt, carry, col0, window):
        k0 = pl.multiple_of(kt * 128, 128)
        kk = kv_ref[pl.ds(k0, 128), col0:col0 + KV_WIDTH]
        vv = kv_ref[pl.ds(k0, 128), col0 + KV_WIDTH:col0 + 2 * KV_WIDTH]
        dist = base + (qs - k0).astype(F32)
        if window:
            ok = (dist >= 0.0) & (dist < float(WINDOW))
        else:
            expand = jnp.where(jcol == kcol + kt * (128 // SEL_BLOCK), 1.0, 0.0).astype(BF16)
            picked = jnp.dot(sel, expand, preferred_element_type=F32)
            ok = (dist >= 0.0) & (picked > 0.5)
        out = []
        for r in range(NSA_REP):
            m_old, l_old = carry[2 * r], carry[2 * r + 1]
            s = jnp.where(ok, _nt_dot(qh[r], kk) - slopes[r] * dist, NEG)
            m_new = jnp.maximum(m_old, jnp.max(s, axis=-1, keepdims=True))
            alpha = jnp.exp(m_old - m_new)
            p = jnp.exp(s - m_new)
            acc_sc[r] = alpha * acc_sc[r] + jnp.dot(p.astype(BF16), vv, preferred_element_type=F32)
            out += [m_new, alpha * l_old + jnp.sum(p, axis=-1, keepdims=True)]
        return tuple(out)

    def run_branch(lo, col0, window):
        acc_sc[...] = jnp.zeros_like(acc_sc)
        init = tuple(jnp.full((Q_BLOCK, 1), NEG, F32) if c % 2 == 0 else jnp.zeros((Q_BLOCK, 1), F32)
                     for c in range(2 * NSA_REP))
        fin = lax.fori_loop(lo, i + 1, functools.partial(attend, col0=col0, window=window), init)
        return [acc_sc[r] / fin[2 * r + 1] for r in range(NSA_REP)]

    o_sel = run_branch(0, 0, False)
    o_win = run_branch(jnp.maximum(i - WINDOW // 128, 0), 2 * KV_WIDTH, True)

    gate = jax.nn.sigmoid(g_ref[...])
    for r in range(NSA_REP):
        o = (gate[:, 3 * r:3 * r + 1] * o_cmp[r] + gate[:, 3 * r + 1:3 * r + 2] * o_sel[r]
             + gate[:, 3 * r + 2:3 * r + 3] * o_win[r])
        for gg in range(NSA_KV_GROUPS):
            @pl.when(g == gg)
            def _():
                c0 = (gg * NSA_REP + r) * NSA_HEAD_DIM
                o_ref[:, c0:c0 + NSA_HEAD_DIM] = o[:, gg * NSA_HEAD_DIM:(gg + 1) * NSA_HEAD_DIM].astype(o_ref.dtype)


def _stage_attn_old(qp, kcv, kv4, gates, bsz, seq):
    n = bsz * seq
    nq = seq // Q_BLOCK
    ncmp = seq // CMP_STRIDE
    nsl = seq // SEL_BLOCK
    c_start = np.arange(ncmp) * CMP_STRIDE
    s_start = np.arange(nsl) * SEL_BLOCK
    ov = ((c_start[:, None] < s_start[None, :] + SEL_BLOCK) & (c_start[:, None] + CMP_BLOCK > s_start[None, :]))
    ov = np.pad(ov.astype(np.float32), ((0, 0), (0, 128 - nsl)))
    ov[ncmp - 1] = 0.0
    ov = jnp.asarray(np.tile(ov, (NSA_REP, 1)), BF16)
    return pl.pallas_call(
        _nsa_body, grid=(bsz, nq, NSA_KV_GROUPS),
        in_specs=[pl.BlockSpec((Q_BLOCK, NSA_REP * 128), lambda b, i, g: (b * nq + i, g)),
                  pl.BlockSpec((ncmp, 2 * KV_WIDTH), lambda b, i, g: (b, 0)),
                  pl.BlockSpec((seq, 4 * KV_WIDTH), lambda b, i, g: (b, 0)),
                  pl.BlockSpec((Q_BLOCK, 128), lambda b, i, g: (b * nq + i, g)),
                  _const_spec(ov.shape)],
        out_specs=pl.BlockSpec((Q_BLOCK, NSA_HEADS * NSA_HEAD_DIM), lambda b, i, g: (b * nq + i, 0)),
        out_shape=jax.ShapeDtypeStruct((n, NSA_HEADS * NSA_HEAD_DIM), BF16),
        scratch_shapes=[pltpu.VMEM((NSA_REP, Q_BLOCK, 128), F32)],
        compiler_params=_cparams(("parallel", "arbitrary", "arbitrary")), name="nsa_attention")(
            qp, kcv, kv4, gates, ov)


MASK_BIG = 2.0 ** 100
SEL_TILE = 256
WIN_TILE = 128


def _tn_dot(a, b):
    return lax.dot_general(a, b, (((0,), (0,)), ((), ())), preferred_element_type=F32)


def _nsa_body(q_ref, kcv_ref, kv_ref, g_ref, ov_ref, o_ref, acc, s_buf, p_buf, a_buf):
    i = pl.program_id(2)
    qs = i * Q_BLOCK
    ncmp = kcv_ref.shape[0]
    seq = kv_ref.shape[0]
    rows = NSA_REP * Q_BLOCK
    dk = NSA_HEAD_DIM
    qh = [q_ref[:, r * 128:(r + 1) * 128] for r in range(NSA_REP)]
    q_all = jnp.concatenate(qh, axis=0)

    def t_of(shape):
        return qs + (lax.broadcasted_iota(jnp.int32, shape, 0) & (Q_BLOCK - 1))

    kc = kcv_ref[:, :128]
    vc = kcv_ref[:, 128:]
    c_end = lax.broadcasted_iota(jnp.int32, (rows, ncmp), 1) * CMP_STRIDE + (CMP_BLOCK - 1)
    vis_c = t_of((rows, ncmp)) >= c_end
    s = jnp.where(vis_c, _nt_dot(q_all, kc), NEG)
    p = jnp.exp(s - jnp.max(s, axis=-1, keepdims=True)) * jnp.where(vis_c, 1.0, 0.0)
    p = (p / jnp.maximum(jnp.sum(p, axis=-1, keepdims=True), 1e-30)).astype(BF16)
    o_cmp = jnp.dot(p, vc, preferred_element_type=F32)

    wlen = min(WINDOW + Q_BLOCK, seq)
    w0 = pl.multiple_of(jnp.maximum(qs + Q_BLOCK - wlen, 0), Q_BLOCK)
    dist_w = t_of((rows, wlen)) - (w0 + lax.broadcasted_iota(jnp.int32, (rows, wlen), 1))
    s = _nt_dot(q_all, kv_ref[pl.ds(w0, wlen), KW_OFF:KW_OFF + 128])
    s = jnp.where((dist_w >= 0) & (dist_w < WINDOW), s, NEG)
    p_w = jnp.exp(s - jnp.max(s, axis=-1, keepdims=True)).astype(BF16)
    a_win = jnp.dot(p_w, kv_ref[pl.ds(w0, wlen), VW_OFF:VW_OFF + 128], preferred_element_type=F32)

    p_heads = jnp.concatenate([p[r * Q_BLOCK:(r + 1) * Q_BLOCK] for r in range(NSA_REP)], axis=1)
    imp = jnp.dot(p_heads, ov_ref[...], preferred_element_type=F32)
    coli = lax.broadcasted_iota(jnp.int32, (Q_BLOCK, 128), 1)
    t_q = qs + lax.broadcasted_iota(jnp.int32, (Q_BLOCK, 128), 0)
    cur = t_q >> 6
    forced = (coli == 0) | (coli == cur) | (coli == cur - 1)
    score = jnp.where(coli * SEL_BLOCK <= t_q, imp + jnp.where(forced, FORCE_BONUS, 0.0), NEG)
    nsl = seq // SEL_BLOCK
    st = score.T[:nsl]
    jrow = lax.broadcasted_iota(jnp.int32, st.shape, 0)
    rank = jnp.zeros(st.shape, F32)
    for k in range(nsl):
        rk = st[k:k + 1, :]
        tie = jnp.where(jrow > k, jnp.where(rk == st, 1.0, 0.0), 0.0)
        rank = rank + jnp.where(rk > st, 1.0, tie)
    drop_t = jnp.where(rank < float(min(SEL_TOPK, nsl)), 0.0, -MASK_BIG)
    if nsl < 128:
        drop_t = jnp.concatenate([drop_t, jnp.zeros((128 - nsl, Q_BLOCK), F32)], axis=0)
    drop = drop_t.T.astype(BF16)
    q_sel = jnp.concatenate([jnp.concatenate([qh[r], drop], axis=1) for r in range(NSA_REP)], axis=0)

    n_tiles = (qs + Q_BLOCK + SEL_TILE - 1) // SEL_TILE
    dist_s = t_of((rows, SEL_TILE)) - lax.broadcasted_iota(jnp.int32, (rows, SEL_TILE), 1)

    def scores(kt):
        k0 = pl.multiple_of(kt * SEL_TILE, SEL_TILE)
        return _nt_dot(q_sel, kv_ref[pl.ds(k0, SEL_TILE), KS_OFF:KS_OFF + 256])

    def values(kt):
        k0 = pl.multiple_of(kt * SEL_TILE, SEL_TILE)
        return kv_ref[pl.ds(k0, SEL_TILE), VS_OFF:VS_OFF + 128]

    def trip(kt, m_old):
        s_next = scores(jnp.minimum(kt + 1, n_tiles - 1))
        acc[...] = a_buf[...] * acc[...] + jnp.dot(p_buf[...], values(jnp.maximum(kt - 1, 0)),
                                                   preferred_element_type=F32)
        s = jnp.where(dist_s >= kt * SEL_TILE, s_buf[...], NEG)
        m_new = jnp.maximum(m_old, jnp.max(s, axis=-1, keepdims=True))
        p_buf[...] = jnp.exp(s - m_new).astype(BF16)
        a_buf[...] = jnp.exp(m_old - m_new)
        s_buf[...] = s_next
        return m_new

    acc[...] = jnp.zeros_like(acc)
    p_buf[...] = jnp.zeros_like(p_buf)
    a_buf[...] = jnp.ones_like(a_buf)
    s_buf[...] = scores(0)
    lax.fori_loop(0, n_tiles, trip, jnp.full((rows, 1), NEG, F32))
    a_sel = a_buf[...] * acc[...] + jnp.dot(p_buf[...], values(n_tiles - 1), preferred_element_type=F32)

    gate = jax.nn.sigmoid(g_ref[...])
    outs = []
    for r in range(NSA_REP):
        rr = slice(r * Q_BLOCK, (r + 1) * Q_BLOCK)
        o_s = a_sel[rr, :dk] / a_sel[rr, ONE_LANE:ONE_LANE + 1]
        o_w = a_win[rr, :dk] / a_win[rr, ONE_LANE:ONE_LANE + 1]
        outs.append(gate[:, 3 * r:3 * r + 1] * o_cmp[rr, :dk] + gate[:, 3 * r + 1:3 * r + 2] * o_s
                    + gate[:, 3 * r + 2:3 * r + 3] * o_w)
    o_ref[...] = jnp.concatenate(outs, axis=1).astype(o_ref.dtype)


def _nsa_body_v3(q_ref, kcv_ref, kv_ref, g_ref, ovt_ref, o_ref, acc_sel, acc_win):
    i = pl.program_id(2)
    qs = i * Q_BLOCK
    ncmp = kcv_ref.shape[0]
    cols = NSA_REP * Q_BLOCK
    dk = NSA_HEAD_DIM
    qh = [q_ref[:, r * 128:(r + 1) * 128] for r in range(NSA_REP)]
    q_win = jnp.concatenate(qh, axis=0)

    kc = kcv_ref[:, :128]
    vc = kcv_ref[:, 128:]
    t_c = qs + (lax.broadcasted_iota(jnp.int32, (ncmp, cols), 1) & (Q_BLOCK - 1))
    c_end = lax.broadcasted_iota(jnp.int32, (ncmp, cols), 0) * CMP_STRIDE + (CMP_BLOCK - 1)
    vis_c = t_c >= c_end
    s = jnp.where(vis_c, _nt_dot(kc, q_win), NEG)
    p = jnp.exp(s - jnp.max(s, axis=0, keepdims=True)) * jnp.where(vis_c, 1.0, 0.0)
    p = (p / jnp.maximum(jnp.sum(p, axis=0, keepdims=True), 1e-30)).astype(BF16)
    o_cmp = _tn_dot(vc, p)

    p_heads = jnp.concatenate([p[:, r * Q_BLOCK:(r + 1) * Q_BLOCK] for r in range(NSA_REP)], axis=0)
    imp = jnp.dot(ovt_ref[...], p_heads, preferred_element_type=F32)
    jrow = lax.broadcasted_iota(jnp.int32, (128, Q_BLOCK), 0)
    t_q = qs + lax.broadcasted_iota(jnp.int32, (128, Q_BLOCK), 1)
    cur = t_q >> 6
    forced = (jrow == 0) | (jrow == cur) | (jrow == cur - 1)
    score = jnp.where(jrow * SEL_BLOCK <= t_q, imp + jnp.where(forced, FORCE_BONUS, 0.0), NEG)
    nsl = kv_ref.shape[0] // SEL_BLOCK
    st = score[:nsl]
    jrow = jrow[:nsl]
    rank = jnp.zeros(st.shape, F32)
    for k in range(nsl):
        rk = st[k:k + 1, :]
        tie = jnp.where(jrow > k, jnp.where(rk == st, 1.0, 0.0), 0.0)
        rank = rank + jnp.where(rk > st, 1.0, tie)
    drop_t = jnp.where(rank < float(min(SEL_TOPK, nsl)), 0.0, -MASK_BIG)
    if nsl < 128:
        drop_t = jnp.concatenate([drop_t, jnp.zeros((128 - nsl, Q_BLOCK), F32)], axis=0)
    drop = drop_t.T.astype(BF16)
    q_sel = jnp.concatenate([jnp.concatenate([qh[r], drop], axis=1) for r in range(NSA_REP)], axis=0)

    half = cols // 2
    t_lane = qs + (lax.broadcasted_iota(jnp.int32, (SEL_TILE, half), 1) & (Q_BLOCK - 1))
    dist_s = t_lane - lax.broadcasted_iota(jnp.int32, (SEL_TILE, half), 0)
    dist_w = dist_s[:WIN_TILE]

    def step(acc, qq, k_off, k_w, v_off, tile, mask_of, kt, m_old):
        k0 = pl.multiple_of(kt * tile, tile)
        kk = kv_ref[pl.ds(k0, tile), k_off:k_off + k_w]
        vv = kv_ref[pl.ds(k0, tile), v_off:v_off + 128]
        ok = None if mask_of is None else mask_of(k0)
        m_out = []
        for c in range(2):
            lanes = slice(c * half, (c + 1) * half)
            s = _nt_dot(kk, qq[lanes])
            if ok is not None:
                s = jnp.where(ok, s, NEG)
            m_new = jnp.maximum(m_old[c], jnp.max(s, axis=0, keepdims=True))
            pt = jnp.exp(s - m_new).astype(BF16)
            acc[:, lanes] = jnp.exp(m_old[c] - m_new) * acc[:, lanes] + _tn_dot(vv, pt)
            m_out.append(m_new)
        return tuple(m_out)

    m0 = (jnp.full((1, half), NEG, F32),) * 2
    acc_sel[...] = jnp.zeros_like(acc_sel)
    sel_step = functools.partial(step, acc_sel, q_sel, KS_OFF, 256, VS_OFF, SEL_TILE)
    n_free = (i * Q_BLOCK) // SEL_TILE
    m_s = lax.fori_loop(0, n_free, functools.partial(sel_step, None), m0)
    sel_step(lambda k0: dist_s >= k0, n_free, m_s)
    acc_win[...] = jnp.zeros_like(acc_win)
    win_step = functools.partial(step, acc_win, q_win, KW_OFF, 128, VW_OFF, WIN_TILE,
                                 lambda k0: (dist_w >= k0) & (dist_w < k0 + WINDOW))
    lax.fori_loop(jnp.maximum(i - WINDOW // WIN_TILE, 0), i + 1, win_step, m0)

    gate = jax.nn.sigmoid(g_ref[...]).T
    outs = []
    for r in range(NSA_REP):
        lanes = slice(r * Q_BLOCK, (r + 1) * Q_BLOCK)
        a_s, a_w = acc_sel[:, lanes], acc_win[:, lanes]
        o_s = a_s[:dk] / a_s[ONE_LANE:ONE_LANE + 1]
        o_w = a_w[:dk] / a_w[ONE_LANE:ONE_LANE + 1]
        outs.append(gate[3 * r:3 * r + 1] * o_cmp[:dk, lanes] + gate[3 * r + 1:3 * r + 2] * o_s
                    + gate[3 * r + 2:3 * r + 3] * o_w)
    o_ref[...] = jnp.concatenate(outs, axis=0).T.astype(o_ref.dtype)


def _nsa_body_v2(q_ref, kcv_ref, kv_ref, g_ref, ov_ref, o_ref, acc_sel, acc_win):
    i = pl.program_id(2)
    qs = i * Q_BLOCK
    ncmp = kcv_ref.shape[0]
    rows = NSA_REP * Q_BLOCK
    qh = [q_ref[:, r * 128:(r + 1) * 128] for r in range(NSA_REP)]

    kc = kcv_ref[:, :128]
    vc = kcv_ref[:, 128:]
    t_c = qs + lax.broadcasted_iota(jnp.int32, (Q_BLOCK, ncmp), 0)
    c_end = lax.broadcasted_iota(jnp.int32, (Q_BLOCK, ncmp), 1) * CMP_STRIDE + (CMP_BLOCK - 1)
    vis_c = t_c >= c_end
    vis_cf = jnp.where(vis_c, 1.0, 0.0)
    o_cmp, p_all = [], []
    for r in range(NSA_REP):
        s = jnp.where(vis_c, _nt_dot(qh[r], kc), NEG)
        m = jnp.max(s, axis=-1, keepdims=True)
        p = jnp.exp(s - m) * vis_cf
        p = p / jnp.maximum(jnp.sum(p, axis=-1, keepdims=True), 1e-30)
        pb = p.astype(BF16)
        p_all.append(pb)
        o_cmp.append(jnp.dot(pb, vc, preferred_element_type=F32))

    rowi = lax.broadcasted_iota(jnp.int32, (Q_BLOCK, 128), 0)
    coli = lax.broadcasted_iota(jnp.int32, (Q_BLOCK, 128), 1)
    imp = jnp.dot(jnp.concatenate(p_all, axis=1), ov_ref[...], preferred_element_type=F32)
    t_q = qs + rowi
    cur = t_q >> 6
    forced = (coli == 0) | (coli == cur) | (coli == cur - 1)
    score = jnp.where(coli * SEL_BLOCK <= t_q, imp + jnp.where(forced, FORCE_BONUS, 0.0), NEG)
    nsl = kv_ref.shape[0] // SEL_BLOCK
    st = score.T[:nsl]
    jrow = lax.broadcasted_iota(jnp.int32, st.shape, 0)
    rank = jnp.zeros(st.shape, F32)
    for k in range(nsl):
        rk = st[k:k + 1, :]
        tie = jnp.where(jrow > k, jnp.where(rk == st, 1.0, 0.0), 0.0)
        rank = rank + jnp.where(rk > st, 1.0, tie)
    drop_t = jnp.where(rank < float(min(SEL_TOPK, nsl)), 0.0, -MASK_BIG)
    if nsl < 128:
        drop_t = jnp.concatenate([drop_t, jnp.zeros((128 - nsl, Q_BLOCK), F32)], axis=0)
    drop = drop_t.T.astype(BF16)
    q_sel = jnp.concatenate([jnp.concatenate([qh[r], drop], axis=1) for r in range(NSA_REP)], axis=0)
    q_win = jnp.concatenate(qh, axis=0)

    t_row = qs + (lax.broadcasted_iota(jnp.int32, (rows, SEL_TILE), 0) & (Q_BLOCK - 1))
    k_col = lax.broadcasted_iota(jnp.int32, (rows, SEL_TILE), 1)
    dist_s = t_row - k_col
    dist_w = dist_s[:, :WIN_TILE]

    def step(acc, qq, k_off, k_w, v_off, tile, mask_of, kt, m_old):
        k0 = pl.multiple_of(kt * tile, tile)
        kk = kv_ref[pl.ds(k0, tile), k_off:k_off + k_w]
        vv = kv_ref[pl.ds(k0, tile), v_off:v_off + 128]
        s = _nt_dot(qq, kk)
        if mask_of is not None:
            s = jnp.where(mask_of(k0), s, NEG)
        m_new = jnp.maximum(m_old, jnp.max(s, axis=-1, keepdims=True))
        p = jnp.exp(s - m_new).astype(BF16)
        acc[...] = jnp.exp(m_old - m_new) * acc[...] + jnp.dot(p, vv, preferred_element_type=F32)
        return m_new

    m0 = jnp.full((rows, 1), NEG, F32)
    acc_sel[...] = jnp.zeros_like(acc_sel)
    sel_step = functools.partial(step, acc_sel, q_sel, KS_OFF, 256, VS_OFF, SEL_TILE)
    n_free = (i * Q_BLOCK) // SEL_TILE
    m_s = lax.fori_loop(0, n_free, functools.partial(sel_step, None), m0)
    sel_step(lambda k0: dist_s >= k0, n_free, m_s)
    acc_win[...] = jnp.zeros_like(acc_win)
    win_step = functools.partial(step, acc_win, q_win, KW_OFF, 128, VW_OFF, WIN_TILE,
                                 lambda k0: (dist_w >= k0) & (dist_w < k0 + WINDOW))
    lax.fori_loop(jnp.maximum(i - WINDOW // WIN_TILE, 0), i + 1, win_step, m0)

    gate = jax.nn.sigmoid(g_ref[...])
    outs = []
    for r in range(NSA_REP):
        a_s = acc_sel[r * Q_BLOCK:(r + 1) * Q_BLOCK, :]
        a_w = acc_win[r * Q_BLOCK:(r + 1) * Q_BLOCK, :]
        dk = NSA_HEAD_DIM
        o_s = a_s[:, :dk] / a_s[:, ONE_LANE:ONE_LANE + 1]
        o_w = a_w[:, :dk] / a_w[:, ONE_LANE:ONE_LANE + 1]
        outs.append(gate[:, 3 * r:3 * r + 1] * o_cmp[r][:, :dk] + gate[:, 3 * r + 1:3 * r + 2] * o_s
                    + gate[:, 3 * r + 2:3 * r + 3] * o_w)
    o_ref[...] = jnp.concatenate(outs, axis=1).astype(o_ref.dtype)


def _stage_attn(qp, kcv, kv, gates, bsz, seq):
    n = bsz * seq
    nq = seq // Q_BLOCK
    ncmp = seq // CMP_STRIDE
    nsl = seq // SEL_BLOCK
    assert seq % SEL_TILE == 0
    c_start = np.arange(ncmp) * CMP_STRIDE
    s_start = np.arange(nsl) * SEL_BLOCK
    ov = ((c_start[:, None] < s_start[None, :] + SEL_BLOCK) & (c_start[:, None] + CMP_BLOCK > s_start[None, :]))
    ov = np.pad(ov.astype(np.float32), ((0, 0), (0, 128 - nsl)))
    ov[ncmp - 1] = 0.0
    ov = jnp.asarray(np.tile(ov, (NSA_REP, 1)), BF16)
    width = NSA_REP * NSA_HEAD_DIM
    rows = NSA_REP * Q_BLOCK
    return pl.pallas_call(
        _nsa_body, grid=(bsz, NSA_KV_GROUPS, nq),
        in_specs=[pl.BlockSpec((Q_BLOCK, NSA_REP * 128), lambda b, g, i: (b * nq + i, g)),
                  pl.BlockSpec((ncmp, 256), lambda b, g, i: (b, g)),
                  pl.BlockSpec((seq, KV_REC), lambda b, g, i: (b, g)),
                  pl.BlockSpec((Q_BLOCK, 128), lambda b, g, i: (b * nq + i, g)),
                  _const_spec(ov.shape)],
        out_specs=pl.BlockSpec((Q_BLOCK, width), lambda b, g, i: (b * nq + i, g)),
        out_shape=jax.ShapeDtypeStruct((n, NSA_KV_GROUPS * width), BF16),
        scratch_shapes=[pltpu.VMEM((rows, 128), F32), pltpu.VMEM((rows, SEL_TILE), F32),
                        pltpu.VMEM((rows, SEL_TILE), BF16), pltpu.VMEM((rows, 1), F32)],
        compiler_params=_cparams(("parallel", "parallel", "arbitrary")), name="nsa_attention")(
            qp, kcv, kv, gates, ov)


def _merge_body(x_ref, ya_ref, ob_ref, gm_ref, wnsa_ref, wout_ref, g_ref, b_ref, o_ref):
    yb = jnp.dot(ob_ref[...], wnsa_ref[...], preferred_element_type=F32)
    mix_in = gm_ref[:, :D_MODEL] * ya_ref[...] + gm_ref[:, D_MODEL:] * yb
    mix = jnp.dot(mix_in.astype(BF16), wout_ref[...], preferred_element_type=F32)
    o_ref[...] = _layer_norm(DN_ALPHA * x_ref[...] + mix, g_ref[...], b_ref[...])


def _stage_merge(x2, ya, ob, gm, w_up_nsa, w_out, ln_g, ln_b):
    n = x2.shape[0]
    tm = 256
    row = lambda w: pl.BlockSpec((tm, w), lambda i: (i, 0))
    return pl.pallas_call(
        _merge_body, grid=(n // tm,),
        in_specs=[row(D_MODEL), row(D_MODEL), row(ob.shape[1]), row(2 * D_MODEL),
                  _const_spec(w_up_nsa.shape), _const_spec(w_out.shape),
                  _const_spec((1, D_MODEL)), _const_spec((1, D_MODEL))],
        out_specs=row(D_MODEL), out_shape=jax.ShapeDtypeStruct((n, D_MODEL), F32),
        compiler_params=_cparams(("parallel",)), name="merge_ln")(
            x2, ya, ob, gm, w_up_nsa.astype(BF16), w_out.astype(BF16),
            ln_g.reshape(1, -1).astype(F32), ln_b.reshape(1, -1).astype(F32))


BIG_NEG = -3.0e38


def _top_rows(s, k):
    rid = lax.broadcasted_iota(jnp.int32, s.shape, 0).astype(F32)
    rank = jnp.full(s.shape, float(k), F32)
    vals, idxs = [], []
    for r in range(k):
        m = jnp.max(s, axis=0, keepdims=True)
        idx = jnp.min(jnp.where(s == m, rid, float(s.shape[0])), axis=0, keepdims=True)
        hit = rid == idx
        rank = jnp.where(hit, float(r), rank)
        s = jnp.where(hit, BIG_NEG, s)
        vals.append(m)
        idxs.append(idx)
    return jnp.concatenate(vals, axis=0), jnp.concatenate(idxs, axis=0), rank


def _peer_body_old(x_ref, wq_ref, sk_ref, u_ref, vt_ref, g_ref, b_ref, o_ref,
               xt_sc, acc_sc, n_sc, e0_sc, rk1_sc, e1_sc, p_sc, st_sc):
    e = pl.program_id(1)
    k = PEER_TOPK
    nk = PEER_NKEYS

    @pl.when(e == 0)
    def _():
        xt = x_ref[...].T.astype(BF16)
        xt_sc[...] = xt
        acc_sc[...] = jnp.zeros_like(acc_sc)
        qt = jnp.dot(wq_ref[...], xt, preferred_element_type=F32).astype(BF16)
        st_sc[...] = jnp.dot(sk_ref[...], qt, preferred_element_type=F32)

        @pl.loop(0, PEER_HEADS)
        def _(h):
            s0 = st_sc[pl.ds(pl.multiple_of(2 * h * nk, nk), nk), :]
            s1 = st_sc[pl.ds(pl.multiple_of((2 * h + 1) * nk, nk), nk), :]
            a, _, rk0 = _top_rows(s0, k)
            b, _, rk1 = _top_rows(s1, k)
            cand = jnp.concatenate([a[r:r + 1] + b for r in range(k)], axis=0)
            best, bidx, _ = _top_rows(cand, k)
            brank = jnp.floor(bidx * (1.0 / k))
            z = jnp.sum(jnp.exp(best - best[0:1]), axis=0, keepdims=True)
            n_i = jnp.zeros(s0.shape, F32)
            for r in range(k):
                nr = jnp.sum(jnp.where(brank == float(r), 1.0, 0.0), axis=0, keepdims=True)
                n_i = n_i + jnp.where(rk0 == float(r), nr, 0.0)
            n_sc[h] = n_i
            e0_sc[h] = jnp.exp(s0 - a[0:1]) / z
            rk1_sc[h] = rk1
            e1_sc[h] = jnp.exp(s1 - b[0:1])

    ht = jnp.dot(u_ref[...], xt_sc[...], preferred_element_type=F32)
    act = jax.nn.gelu(ht)
    rows = u_ref.shape[0] // nk
    for ii in range(rows):
        i = e * rows + ii
        w = jnp.zeros((nk, ht.shape[1]), F32)
        for h in range(PEER_HEADS):
            n_row = n_sc[h, pl.ds(i, 1), :]
            e0_row = e0_sc[h, pl.ds(i, 1), :]
            w = w + jnp.where(rk1_sc[h] < n_row, e0_row * e1_sc[h], 0.0)
        p_sc[ii * nk:(ii + 1) * nk, :] = (w * act[ii * nk:(ii + 1) * nk]).astype(BF16)
    acc_sc[...] += jnp.dot(vt_ref[...], p_sc[...], preferred_element_type=F32)

    @pl.when(e == pl.num_programs(1) - 1)
    def _():
        x = x_ref[...]
        o_ref[...] = _layer_norm(DN_ALPHA * x + acc_sc[...].T, g_ref[...], b_ref[...])


def _stage_peer_old(x1, w_q, subkeys, u_tab, v_tab, ln_g, ln_b, tt=512, et=512):
    n = x1.shape[0]
    tt = min(tt, n)
    H, nk, half = PEER_HEADS, PEER_NKEYS, subkeys.shape[-1]
    wq_t = w_q.T.astype(BF16)
    eye = jnp.eye(2 * H, dtype=F32)
    sk = subkeys.reshape(2 * H, nk, half)
    sk_t = jnp.einsum('ckd,ce->cked', sk, eye).reshape(2 * H * nk, 2 * H * half).astype(BF16)
    u_b = u_tab.astype(BF16)
    v_t = v_tab.T.astype(BF16)
    ne = u_tab.shape[0]
    sel_sc = pltpu.VMEM((H, nk, tt), F32)
    return pl.pallas_call(
        _peer_body, grid=(n // tt, ne // et),
        in_specs=[pl.BlockSpec((tt, D_MODEL), lambda t, e: (t, 0)),
                  _const_spec(wq_t.shape), _const_spec(sk_t.shape),
                  pl.BlockSpec((et, D_MODEL), lambda t, e: (e, 0)),
                  pl.BlockSpec((D_MODEL, et), lambda t, e: (0, e)),
                  _const_spec((1, D_MODEL)), _const_spec((1, D_MODEL))],
        out_specs=pl.BlockSpec((tt, D_MODEL), lambda t, e: (t, 0)),
        out_shape=jax.ShapeDtypeStruct((n, D_MODEL), F32),
        scratch_shapes=[pltpu.VMEM((D_MODEL, tt), BF16), pltpu.VMEM((D_MODEL, tt), F32),
                        sel_sc, sel_sc, sel_sc, sel_sc, pltpu.VMEM((et, tt), BF16),
                        pltpu.VMEM((2 * H * nk, tt), F32)],
        compiler_params=_cparams(("parallel", "arbitrary")), name="peer_ffn")(
            x1, wq_t, sk_t, u_b, v_t, ln_g.reshape(1, -1).astype(F32), ln_b.reshape(1, -1).astype(F32))


def _max_rounds(s, k, want_round=True):
    rnd = jnp.full(s.shape, float(k), F32) if want_round else None
    vals = []
    for r in range(k):
        m = jnp.max(s, axis=0, keepdims=True)
        eq = s == m
        if want_round:
            rnd = jnp.where(eq, float(r), rnd)
        s = jnp.where(eq, BIG_NEG, s)
        vals.append(m)
    return jnp.concatenate(vals, axis=0), rnd


def _pair_candidates(a, b):
    c8 = lax.broadcasted_iota(jnp.int32, (8, a.shape[1]), 0)
    parts = [a[0:1] + b[0:8], a[0:1] + b[8:16]]
    for r in range(1, 8):
        keep = PEER_TOPK // (r + 1)
        cand = a[r:r + 1] + b[0:8]
        parts.append(cand if keep >= 8 else jnp.where(c8 < keep, cand, BIG_NEG))
    parts.append(a[8:16] + b[0:1])
    return jnp.concatenate(parts, axis=0)


def _picks_per_rank(picked):
    rows = [jnp.sum(picked[0:16], axis=0, keepdims=True)]
    rows += [jnp.sum(picked[8 * (r + 1):8 * (r + 2)], axis=0, keepdims=True) for r in range(1, 8)]
    return jnp.concatenate(rows + [picked[72:80]], axis=0)


def _spread_by_rank(rank, per_rank):
    out = jnp.zeros(rank.shape, F32)
    for r in range(PEER_TOPK):
        out = out + jnp.where(rank == float(r), per_rank[r:r + 1], 0.0)
    return out


def _route_fast(s0, s1):
    k = PEER_TOPK
    a, rk0 = _max_rounds(s0, k)
    b, rk1 = _max_rounds(s1, k)
    cand = _pair_candidates(a, b)
    best, _ = _max_rounds(cand, k, want_round=False)
    hit = cand >= best[k - 1:k]
    picked = jnp.where(hit, 1.0, 0.0)
    z = jnp.sum(jnp.where(hit, jnp.exp(cand - best[0:1]), 0.0), axis=0, keepdims=True)
    nrank = _picks_per_rank(picked)
    count = lambda v: jnp.sum(v, axis=0, keepdims=True)
    bad = (jnp.abs(count(jnp.where(rk0 < float(k), 1.0, 0.0)) - k) + jnp.abs(count(jnp.where(rk1 < float(k), 1.0, 0.0)) - k)
           + jnp.abs(count(nrank) - k))
    return _spread_by_rank(rk0, nrank), jnp.exp(s0 - a[0:1]) / z, rk1, jnp.exp(s1 - b[0:1]), bad


def _route_exact(s0, s1):
    k = PEER_TOPK
    a, _, rk0 = _top_rows(s0, k)
    b, _, rk1 = _top_rows(s1, k)
    cand = jnp.concatenate([a[r:r + 1] + b for r in range(k)], axis=0)
    best, bidx, _ = _top_rows(cand, k)
    brank = jnp.floor(bidx * (1.0 / k))
    z = jnp.sum(jnp.exp(best - best[0:1]), axis=0, keepdims=True)
    nrank = jnp.concatenate([jnp.sum(jnp.where(brank == float(r), 1.0, 0.0), axis=0, keepdims=True)
                             for r in range(k)], axis=0)
    return _spread_by_rank(rk0, nrank), jnp.exp(s0 - a[0:1]) / z, rk1, jnp.exp(s1 - b[0:1])


def _peer_body(x_ref, wq_ref, sk_ref, u_ref, vt_ref, g_ref, b_ref, o_ref,
               xt_sc, acc_sc, n_sc, e0_sc, rk1_sc, e1_sc, p_sc, st_sc):
    e = pl.program_id(1)
    nk = PEER_NKEYS

    @pl.when(e == 0)
    def _():
        xt = x_ref[...].T.astype(BF16)
        xt_sc[...] = xt
        acc_sc[...] = jnp.zeros_like(acc_sc)
        qt = jnp.dot(wq_ref[...], xt, preferred_element_type=F32).astype(BF16)
        st_sc[...] = jnp.dot(sk_ref[...], qt, preferred_element_type=F32)

        @pl.loop(0, PEER_HEADS)
        def _(h):
            s0 = st_sc[pl.ds(pl.multiple_of(2 * h * nk, nk), nk), :]
            s1 = st_sc[pl.ds(pl.multiple_of((2 * h + 1) * nk, nk), nk), :]

            def put(n_i, e0, rk1, e1):
                n_sc[h] = n_i
                e0_sc[h] = e0
                rk1_sc[h] = rk1.astype(BF16)
                e1_sc[h] = e1.astype(BF16)

            n_i, e0, rk1, e1, bad = _route_fast(s0, s1)
            put(n_i, e0, rk1, e1)

            @pl.when(jnp.max(bad) > 0.0)
            def _():
                put(*_route_exact(s0, s1))

    ht = jnp.dot(u_ref[...], xt_sc[...], preferred_element_type=F32)
    act = jax.nn.gelu(ht).astype(BF16)
    rows = u_ref.shape[0] // nk
    for ii in range(rows):
        i = e * rows + ii
        w = jnp.zeros((nk, ht.shape[1]), BF16)
        for h in range(PEER_HEADS):
            n_row = n_sc[h, pl.ds(i, 1), :].astype(BF16)
            e0_row = e0_sc[h, pl.ds(i, 1), :].astype(BF16)
            w = w + jnp.where(rk1_sc[h] < n_row, e0_row * e1_sc[h], 0)
        p_sc[ii * nk:(ii + 1) * nk, :] = w * act[ii * nk:(ii + 1) * nk]
    acc_sc[...] += jnp.dot(vt_ref[...], p_sc[...], preferred_element_type=F32)

    @pl.when(e == pl.num_programs(1) - 1)
    def _():
        o_ref[...] = _layer_norm(DN_ALPHA * x_ref[...] + acc_sc[...].T, g_ref[...], b_ref[...])


def _stage_peer(x1, w_q, subkeys, u_tab, v_tab, ln_g, ln_b, tt=512, et=1024):
    n = x1.shape[0]
    tt = min(tt, n)
    H, nk, half = PEER_HEADS, PEER_NKEYS, subkeys.shape[-1]
    wq_t = w_q.T.astype(BF16)
    eye = jnp.eye(2 * H, dtype=F32)
    sk = subkeys.reshape(2 * H, nk, half)
    sk_t = jnp.einsum('ckd,ce->cked', sk, eye).reshape(2 * H * nk, 2 * H * half).astype(BF16)
    u_b = u_tab.astype(BF16)
    v_t = v_tab.T.astype(BF16)
    ne = u_tab.shape[0]
    return pl.pallas_call(
        _peer_body, grid=(n // tt, ne // et),
        in_specs=[pl.BlockSpec((tt, D_MODEL), lambda t, e: (t, 0)),
                  _const_spec(wq_t.shape), _const_spec(sk_t.shape),
                  pl.BlockSpec((et, D_MODEL), lambda t, e: (e, 0)),
                  pl.BlockSpec((D_MODEL, et), lambda t, e: (0, e)),
                  _const_spec((1, D_MODEL)), _const_spec((1, D_MODEL))],
        out_specs=pl.BlockSpec((tt, D_MODEL), lambda t, e: (t, 0)),
        out_shape=jax.ShapeDtypeStruct((n, D_MODEL), F32),
        scratch_shapes=[pltpu.VMEM((D_MODEL, tt), BF16), pltpu.VMEM((D_MODEL, tt), F32),
                        pltpu.VMEM((H, nk, tt), F32), pltpu.VMEM((H, nk, tt), F32),
                        pltpu.VMEM((H, nk, tt), BF16), pltpu.VMEM((H, nk, tt), BF16),
                        pltpu.VMEM((et, tt), BF16), pltpu.VMEM((2 * H * nk, tt), F32)],
        compiler_params=_cparams(("parallel", "arbitrary")), name="peer_ffn")(
            x1, wq_t, sk_t, u_b, v_t, ln_g.reshape(1, -1).astype(F32), ln_b.reshape(1, -1).astype(F32))


def _layer(x, w_in, b_in, ssm_a_re, ssm_a_im, ssm_log_dt, ssm_b_re, ssm_b_im, ssm_c_re, ssm_c_im, ssm_d,
           w_glu, w_up_ssm, nsa_pe_k, nsa_w1_k, nsa_w2_k, nsa_pe_v, nsa_w1_v, nsa_w2_v, w_up_nsa,
           w_out, ln1_g, ln1_b, peer_w_q, peer_subkeys, peer_u, peer_v, ln2_g, ln2_b):
    bsz, seq, _ = x.shape
    x2 = x.reshape(bsz * seq, D_MODEL)
    u, qp, kvc, kv4, gates, gm = _stage_proj(x2, w_in, b_in, seq)
    ya = _stage_ssm(u, bsz, seq, ssm_a_re, ssm_a_im, ssm_log_dt, ssm_b_re, ssm_b_im, ssm_c_re, ssm_c_im,
                    ssm_d, w_glu, w_up_ssm)
    kcv = _stage_compress(kvc, bsz, seq, nsa_pe_k, nsa_w1_k, nsa_w2_k, nsa_pe_v, nsa_w1_v, nsa_w2_v)
    ob = _stage_attn(qp, kcv, kv4, gates, bsz, seq)
    x1 = _stage_merge(x2, ya, ob, gm, w_up_nsa, w_out, ln1_g, ln1_b)
    out = _stage_peer(x1, peer_w_q, peer_subkeys, peer_u, peer_v, ln2_g, ln2_b)
    return out.reshape(bsz, seq, D_MODEL)


def kernel(x, w_in, b_in, ssm_a_re, ssm_a_im, ssm_log_dt, ssm_b_re, ssm_b_im, ssm_c_re, ssm_c_im, ssm_d, w_glu,
           w_up_ssm, nsa_pe_k, nsa_w1_k, nsa_w2_k, nsa_pe_v, nsa_w1_v, nsa_w2_v, w_up_nsa, w_out, ln1_g, ln1_b,
           peer_w_q, peer_subkeys, peer_u, peer_v, ln2_g, ln2_b):
    params = (w_in, b_in, ssm_a_re, ssm_a_im, ssm_log_dt, ssm_b_re, ssm_b_im, ssm_c_re, ssm_c_im, ssm_d, w_glu,
              w_up_ssm, nsa_pe_k, nsa_w1_k, nsa_w2_k, nsa_pe_v, nsa_w1_v, nsa_w2_v, w_up_nsa, w_out, ln1_g, ln1_b,
              peer_w_q, peer_subkeys, peer_u, peer_v, ln2_g, ln2_b)
    for layer in range(w_in.shape[0]):
        x = _layer(x, *[p[layer] for p in params])
    return x
```

```python
import functools
import math

import jax
import jax.numpy as jnp
import numpy as np
from jax import lax
from jax.experimental import pallas as pl
from jax.experimental.pallas import tpu as pltpu

F32 = jnp.float32
BF16 = jnp.bfloat16

D_MODEL = 1024
SSM_WIDTH = 512
SSM_GROUP = 16
SSM_GROUPS = 32
SSM_STATE = 64
SSM_CHUNK = 8
LANE_GROUPS = 8
N_LANE_TILES = SSM_WIDTH // 128
NSA_HEADS = 8
NSA_KV_GROUPS = 2
NSA_REP = NSA_HEADS // NSA_KV_GROUPS
NSA_HEAD_DIM = 64
KV_WIDTH = NSA_KV_GROUPS * NSA_HEAD_DIM
CMP_BLOCK = 32
CMP_STRIDE = 16
CMP_HIDDEN = 128
SEL_BLOCK = 64
SEL_TOPK = 16
WINDOW = 512
Q_BLOCK = 128
FORCE_BONUS = 1.0e4
PEER_HEADS = 8
PEER_NKEYS = 128
PEER_EXPERTS = PEER_NKEYS * PEER_NKEYS
PEER_TOPK = 16
DN_ALPHA = 2.0 ** 0.25
LN_EPS = 1e-5
NEG = -1e30
VMEM_LIMIT = 56 * 1024 * 1024


def _cparams(sem):
    return pltpu.CompilerParams(dimension_semantics=sem, vmem_limit_bytes=VMEM_LIMIT)


def _const_spec(shape):
    nd = len(shape)
    return pl.BlockSpec(shape, lambda *_: (0,) * nd, pipeline_mode=pl.Buffered(1))


def _layer_norm(z, g, b):
    mu = jnp.mean(z, axis=-1, keepdims=True)
    var = jnp.mean(jnp.square(z - mu), axis=-1, keepdims=True)
    return (z - mu) * lax.rsqrt(var + LN_EPS) * g + b


def _nt_dot(a, b):
    return lax.dot_general(a, b, (((1,), (1,)), ((), ())), preferred_element_type=F32)


def _tn_dot(a, b):
    return lax.dot_general(a, b, (((0,), (0,)), ((), ())), preferred_element_type=F32)


KV_REC = 640
KS_OFF, VS_OFF, KW_OFF, VW_OFF = 0, 256, 384, 512
POS_HI, POS_LO, ONE_LANE, BLOCK_LANE0 = 64, 65, 64, 128
assert SEL_BLOCK == 64


def _position_features(pos, lane):
    hi, lo = pos >> 6, pos & 63
    f = jnp.where(lane == KS_OFF + POS_HI, hi, 0) + jnp.where(lane == KW_OFF + POS_HI, hi, 0)
    f = f + jnp.where(lane == KS_OFF + POS_LO, lo, 0) + jnp.where(lane == KW_OFF + POS_LO, lo, 0)
    f = f + jnp.where(lane == KS_OFF + BLOCK_LANE0 + hi, 1, 0)
    f = f + jnp.where(lane == VS_OFF + ONE_LANE, 1, 0) + jnp.where(lane == VW_OFF + ONE_LANE, 1, 0)
    return f.astype(F32)


def _proj_body(seq, x_ref, wu, bu, wq, bq, wc, bc, wk, bk, wg, bg, wm, bm,
               u_o, q_o, c_o, k_o, g_o, m_o):
    xb = x_ref[...].astype(BF16)
    tm = x_ref.shape[0]

    def lin(w, b):
        return jnp.dot(xb, w[...], preferred_element_type=F32) + b[...]

    u_o[...] = lin(wu, bu)
    q_o[...] = (lin(wq, bq) * (NSA_HEAD_DIM ** -0.5)).astype(BF16)
    c_o[...] = lin(wc, bc)
    pos0 = (pl.program_id(0) % (seq // tm)) * tm
    shape = (tm, NSA_KV_GROUPS * KV_REC)
    lane = lax.broadcasted_iota(jnp.int32, shape, 1)
    lane = jnp.where(lane >= KV_REC, lane - KV_REC, lane)
    feat = _position_features(pos0 + lax.broadcasted_iota(jnp.int32, shape, 0), lane)
    k_o[...] = (lin(wk, bk) + feat).astype(BF16)
    g_o[...] = lin(wg, bg)
    m_o[...] = jax.nn.sigmoid(lin(wm, bm))


def _stage_proj(x2, w_in, b_in, seq):
    n = x2.shape[0]
    tm = 256
    o0 = SSM_WIDTH
    o1 = o0 + NSA_HEADS * NSA_HEAD_DIM
    o2 = o1 + 6 * KV_WIDTH
    o3 = o2 + 3 * NSA_HEADS
    w_u, b_u = w_in[:, :o0], b_in[:o0]
    w_q, b_q = w_in[:, o0:o1], b_in[o0:o1]
    w_kv, b_kv = w_in[:, o1:o2], b_in[o1:o2]
    w_g, b_g = w_in[:, o2:o3], b_in[o2:o3]
    w_m, b_m = w_in[:, o3:], b_in[o3:]

    def pad_q(a, aux):
        a = a.reshape(a.shape[:-1] + (NSA_HEADS, NSA_HEAD_DIM))
        aux = jnp.broadcast_to(aux, a.shape[:-2] + aux.shape)
        return jnp.concatenate([a, aux], axis=-1).reshape(a.shape[:-2] + (NSA_HEADS * 128,))

    slope = 2.0 ** -(np.arange(NSA_HEADS, dtype=np.float32) + 1.0) * NSA_HEAD_DIM ** 0.5
    q_aux = np.zeros((NSA_HEADS, 128 - NSA_HEAD_DIM), np.float32)
    q_aux[:, POS_HI - NSA_HEAD_DIM] = slope * 64.0
    q_aux[:, POS_LO - NSA_HEAD_DIM] = slope

    def kv_records(a):
        parts = a.reshape(a.shape[:-1] + (6, NSA_KV_GROUPS, NSA_HEAD_DIM))
        z = lambda w: jnp.zeros(a.shape[:-1] + (w,), a.dtype)
        recs = []
        for g in range(NSA_KV_GROUPS):
            k_s, v_s, k_w, v_w = (parts[..., c, g, :] for c in (2, 3, 4, 5))
            recs += [k_s, z(256 - NSA_HEAD_DIM), v_s, z(128 - NSA_HEAD_DIM), k_w, z(128 - NSA_HEAD_DIM),
                     v_w, z(128 - NSA_HEAD_DIM)]
        return jnp.concatenate(recs, axis=-1)

    def pad_g(a):
        a = a.reshape(a.shape[:-1] + (NSA_KV_GROUPS, 3 * NSA_REP))
        a = jnp.concatenate([a, jnp.zeros(a.shape[:-1] + (128 - 3 * NSA_REP,), a.dtype)], axis=-1)
        return a.reshape(a.shape[:-2] + (NSA_KV_GROUPS * 128,))

    ws = [w_u, pad_q(w_q, jnp.zeros_like(q_aux)), w_kv[:, :2 * KV_WIDTH], kv_records(w_kv), pad_g(w_g), w_m]
    bs = [b_u, pad_q(b_q, jnp.asarray(q_aux)), b_kv[:2 * KV_WIDTH], kv_records(b_kv), pad_g(b_g), b_m]
    odt = [F32, BF16, F32, BF16, F32, F32]
    args, in_specs = [x2], [pl.BlockSpec((tm, D_MODEL), lambda i: (i, 0))]
    for w, b in zip(ws, bs):
        args += [w.astype(BF16), b.reshape(1, -1).astype(F32)]
        in_specs += [_const_spec(w.shape), _const_spec((1, w.shape[1]))]
    out_shape = [jax.ShapeDtypeStruct((n, w.shape[1]), dt) for w, dt in zip(ws, odt)]
    out_specs = [pl.BlockSpec((tm, w.shape[1]), lambda i: (i, 0)) for w in ws]
    return pl.pallas_call(
        functools.partial(_proj_body, seq), grid=(n // tm,), in_specs=in_specs, out_specs=out_specs, out_shape=out_shape,
        compiler_params=_cparams(("parallel",)), name="in_proj")(*args)


def _ssm_weights(a_re, a_im, log_dt, b_re, b_im, c_re, c_im):
    L, G, P, H, A, J = SSM_CHUNK, SSM_GROUPS, SSM_STATE, SSM_GROUP, LANE_GROUPS, N_LANE_TILES
    lam = lax.complex(a_re.astype(F32), a_im.astype(F32))
    dt = jnp.exp(log_dt.astype(F32))[:, None]
    lam_bar = jnp.exp(lam * dt)
    b_bar = ((lam_bar - 1.0) / lam)[:, :, None] * lax.complex(b_re.astype(F32), b_im.astype(F32))
    c = lax.complex(c_re.astype(F32), c_im.astype(F32))
    k = jnp.arange(L + 1, dtype=F32)
    pw = jnp.exp((lam * dt)[None] * k[:, None, None])
    eye = jnp.eye(A, dtype=F32)
    kern = jnp.real(jnp.einsum('ghp,kgp,gpj->kghj', c, pw[:L], b_bar))
    s_i, t_i = jnp.arange(L)[:, None], jnp.arange(L)[None, :]
    tau = jnp.clip(t_i - s_i, 0, L - 1)
    causal = (t_i >= s_i).astype(F32)
    wfull = kern[tau] * causal[:, :, None, None, None]
    wfull = wfull.reshape(L, L, J, A, H, H)
    w_intra = jnp.einsum('stjahi,ab->jsaitbh', wfull, eye).reshape(J, L * 128, L * 128)
    q = pw[:L][::-1][:, :, :, None] * b_bar[None]
    q = q.reshape(L, J, A, P, H)
    m2 = jnp.stack([jnp.real(q), jnp.imag(q)], axis=0)
    m2 = jnp.einsum('rsjapi,ab->jsairbp', m2, eye).reshape(J, L * 128, 2 * A * P)
    cl = c[None] * pw[1:][:, :, None, :]
    cl = cl.reshape(L, J, A, H, P)
    m1 = jnp.stack([jnp.real(cl), -jnp.imag(cl)], axis=0)
    m1 = jnp.einsum('rtjahp,ab->jraptbh', m1, eye).reshape(J, 2 * A * P, L * 128)
    w2 = jnp.concatenate([w_intra, m1], axis=1)
    lam8 = pw[L].reshape(G * P)
    return m2.astype(BF16), w2.astype(BF16), jnp.real(lam8), jnp.imag(lam8)


def _split_lane_tiles(src_ref, dst_sc):
    for j in range(dst_sc.shape[0]):
        dst_sc[j] = src_ref[:, j * 128:(j + 1) * 128]


def _chunk_inputs(u_sc, j, rows):
    return jnp.concatenate([u_sc[j, pl.ds(s, rows, stride=SSM_CHUNK), :] for s in range(SSM_CHUNK)], axis=1)


def _ssm_state_body(u_ref, m2_ref, zre_ref, zim_ref, u_sc):
    half = LANE_GROUPS * SSM_STATE
    _split_lane_tiles(u_ref, u_sc)
    for j in range(N_LANE_TILES):
        z = jnp.dot(_chunk_inputs(u_sc, j, zre_ref.shape[0]).astype(BF16), m2_ref[j], preferred_element_type=F32)
        zre_ref[:, j * half:(j + 1) * half] = z[:, :half]
        zim_ref[:, j * half:(j + 1) * half] = z[:, half:]


def _ssm_scan_body(zre_ref, zim_ref, lre_ref, lim_ref, xre_ref, xim_ref):
    lr, li = lre_ref[...], lim_ref[...]

    def step(c, carry):
        xr, xi = carry
        xre_ref[pl.ds(c, 1), :] = xr
        xim_ref[pl.ds(c, 1), :] = xi
        return lr * xr - li * xi + zre_ref[pl.ds(c, 1), :], lr * xi + li * xr + zim_ref[pl.ds(c, 1), :]

    zero = jnp.zeros(lr.shape, F32)
    lax.fori_loop(0, zre_ref.shape[0], step, (zero, zero))


def _ssm_out_body(u_ref, xre_ref, xim_ref, w2_ref, d_ref, wglu_ref, wup_ref, o_ref, u_sc, y_sc):
    half = LANE_GROUPS * SSM_STATE
    rows = xre_ref.shape[0]
    _split_lane_tiles(u_ref, u_sc)
    for j in range(N_LANE_TILES):
        lhs = jnp.concatenate(
            [_chunk_inputs(u_sc, j, rows), xre_ref[:, j * half:(j + 1) * half], xim_ref[:, j * half:(j + 1) * half]],
            axis=1).astype(BF16)
        yj = jnp.dot(lhs, w2_ref[j], preferred_element_type=F32)
        for t in range(SSM_CHUNK):
            y_sc[j, pl.ds(t, rows, stride=SSM_CHUNK), :] = yj[:, t * 128:(t + 1) * 128]
    y = jnp.concatenate([y_sc[j] for j in range(N_LANE_TILES)], axis=1) + d_ref[...] * u_ref[...]
    gl = jnp.dot(jax.nn.gelu(y).astype(BF16), wglu_ref[...], preferred_element_type=F32)
    v = gl[:, :SSM_WIDTH] * jax.nn.sigmoid(gl[:, SSM_WIDTH:])
    o_ref[...] = jnp.dot(v.astype(BF16), wup_ref[...], preferred_element_type=F32)


def _stage_ssm(u, bsz, seq, a_re, a_im, log_dt, b_re, b_im, c_re, c_im, d_skip, w_glu, w_up):
    n = bsz * seq
    L = SSM_CHUNK
    nck = seq // L
    gp = SSM_GROUPS * SSM_STATE
    m2, w2, l8re, l8im = _ssm_weights(a_re, a_im, log_dt, b_re, b_im, c_re, c_im)
    rs = min(nck, 512)
    zre, zim = pl.pallas_call(
        _ssm_state_body, grid=(bsz, nck // rs),
        in_specs=[pl.BlockSpec((rs * L, SSM_WIDTH), lambda b, i: (b * (nck // rs) + i, 0)),
                  _const_spec(m2.shape)],
        out_specs=[pl.BlockSpec((rs, gp), lambda b, i: (i, b))] * 2,
        out_shape=[jax.ShapeDtypeStruct((nck, bsz * gp), F32)] * 2,
        scratch_shapes=[pltpu.VMEM((N_LANE_TILES, rs * L, 128), F32)],
        compiler_params=_cparams(("parallel", "parallel")), name="ssm_chunk_state")(u, m2)
    cw = 1024
    ncol = bsz * gp // cw
    lre = jnp.tile(l8re, bsz).reshape(1, bsz * gp)
    lim = jnp.tile(l8im, bsz).reshape(1, bsz * gp)
    col = pl.BlockSpec((nck, cw), lambda i: (0, i))
    lam = pl.BlockSpec((1, cw), lambda i: (0, i))
    xre, xim = pl.pallas_call(
        _ssm_scan_body, grid=(ncol,), in_specs=[col, col, lam, lam], out_specs=[col, col],
        out_shape=[jax.ShapeDtypeStruct((nck, bsz * gp), F32)] * 2,
        compiler_params=_cparams(("parallel",)), name="ssm_carry_scan")(zre, zim, lre, lim)
    ro = min(nck, 128)
    d8 = d_skip.reshape(1, SSM_WIDTH).astype(F32)
    return pl.pallas_call(
        _ssm_out_body, grid=(bsz, nck // ro),
        in_specs=[pl.BlockSpec((ro * L, SSM_WIDTH), lambda b, i: (b * (nck // ro) + i, 0)),
                  pl.BlockSpec((ro, gp), lambda b, i: (i, b)),
                  pl.BlockSpec((ro, gp), lambda b, i: (i, b)),
                  _const_spec(w2.shape), _const_spec((1, SSM_WIDTH)),
                  _const_spec(w_glu.shape), _const_spec(w_up.shape)],
        out_specs=pl.BlockSpec((ro * L, D_MODEL), lambda b, i: (b * (nck // ro) + i, 0)),
        out_shape=jax.ShapeDtypeStruct((n, D_MODEL), F32),
        scratch_shapes=[pltpu.VMEM((N_LANE_TILES, ro * L, 128), F32)] * 2,
        compiler_params=_cparams(("parallel", "parallel")), name="ssm_out")(
            u, xre, xim, w2, d8, w_glu.astype(BF16), w_up.astype(BF16))


def _compress_body(kv_ref, pea_ref, peb_ref, w1a_ref, w1b_ref, w2_ref, o_ref, kv_sc):
    nrow = o_ref.shape[0]
    _split_lane_tiles(kv_ref, kv_sc)
    f = jnp.concatenate([kv_sc[j, pl.ds(l, nrow, stride=CMP_STRIDE), :]
                         for l in range(CMP_STRIDE) for j in range(kv_sc.shape[0])], axis=1)
    a = jnp.dot((f + pea_ref[...]).astype(BF16), w1a_ref[...], preferred_element_type=F32)
    b = jnp.dot((f + peb_ref[...]).astype(BF16), w1b_ref[...], preferred_element_type=F32)
    pre = a + pltpu.roll(b, b.shape[0] - 1, 0)
    hid = jax.nn.gelu(pre)
    out = jnp.dot(hid.astype(BF16), w2_ref[...], preferred_element_type=F32)
    lane = lax.broadcasted_iota(jnp.int32, out.shape, 1) & 255
    c_end = lax.broadcasted_iota(jnp.int32, out.shape, 0) * CMP_STRIDE + (CMP_BLOCK - 1)
    feat = jnp.where(lane == POS_HI, c_end >> 6, 0) + jnp.where(lane == POS_LO, c_end & 63, 0)
    o_ref[...] = (out + feat.astype(F32)).astype(BF16)


def _stage_compress(kvc, bsz, seq, pe_k, w1_k, w2_k, pe_v, w1_v, w2_v):
    hl = CMP_STRIDE
    nrow = seq // hl
    G, dk, hid = NSA_KV_GROUPS, NSA_HEAD_DIM, CMP_HIDDEN
    eye = jnp.eye(2 * G, dtype=F32)

    def big_w1(lo):
        wk = w1_k.reshape(CMP_BLOCK, dk, hid)[lo:lo + hl]
        wv = w1_v.reshape(CMP_BLOCK, dk, hid)[lo:lo + hl]
        w = jnp.stack([wk, wk, wv, wv], axis=1)
        return jnp.einsum('lcdj,ce->lcdej', w, eye).reshape(hl * 2 * G * dk, 2 * G * hid)

    def big_pe(lo):
        pk, pv = pe_k[lo:lo + hl], pe_v[lo:lo + hl]
        return jnp.stack([pk, pk, pv, pv], axis=1).reshape(1, hl * 2 * G * dk)

    w2 = jnp.stack([w2_k, w2_k, w2_v, w2_v], axis=0)
    w2 = jnp.concatenate([w2, jnp.zeros((2 * G, hid, 128 - dk), w2.dtype)], axis=-1)
    slot = eye[np.array([kv * G + g for g in range(G) for kv in range(2)])].T
    w2 = jnp.einsum('cjd,ce->cjed', w2, slot).reshape(2 * G * hid, 2 * G * 128)
    width = hl * 2 * G * dk
    return pl.pallas_call(
        _compress_body, grid=(bsz,),
        in_specs=[pl.BlockSpec((seq, 2 * G * dk), lambda b: (b, 0)),
                  _const_spec((1, width)), _const_spec((1, width)),
                  _const_spec((width, 2 * G * hid)), _const_spec((width, 2 * G * hid)),
                  _const_spec((2 * G * hid, 2 * G * 128))],
        out_specs=pl.BlockSpec((nrow, 2 * G * 128), lambda b: (b, 0)),
        out_shape=jax.ShapeDtypeStruct((bsz * nrow, 2 * G * 128), BF16),
        scratch_shapes=[pltpu.VMEM((2 * G * dk // 128, seq, 128), F32)],
        compiler_params=_cparams(("parallel",)), name="nsa_compress")(
            kvc, big_pe(0).astype(F32), big_pe(hl).astype(F32),
            big_w1(0).astype(BF16), big_w1(hl).astype(BF16), w2.astype(BF16))


MASK_BIG = 2.0 ** 100
SEL_TILE = 256


def _nsa_body(q_ref, kcv_ref, kv_ref, g_ref, ov_ref, rel_ref, relc_ref, o_ref, acc, s_buf, p_buf, a_buf):
    i = pl.program_id(2)
    qs = i * Q_BLOCK
    seq = kv_ref.shape[0]
    rows = NSA_REP * Q_BLOCK
    dk = NSA_HEAD_DIM
    qh = [q_ref[:, r * 128:(r + 1) * 128] for r in range(NSA_REP)]
    q_all = jnp.concatenate(qh, axis=0)

    kc = kcv_ref[:, :128]
    vc = kcv_ref[:, 128:]
    vis_c = relc_ref[...] >= -qs
    s = jnp.where(vis_c, _nt_dot(q_all, kc), NEG)
    p = jnp.exp(s - jnp.max(s, axis=-1, keepdims=True)) * jnp.where(vis_c, 1.0, 0.0)
    p = (p / jnp.maximum(jnp.sum(p, axis=-1, keepdims=True), 1e-30)).astype(BF16)
    o_cmp = jnp.dot(p, vc, preferred_element_type=F32)

    wlen = min(WINDOW + Q_BLOCK, seq)
    w0 = pl.multiple_of(jnp.maximum(qs + Q_BLOCK - wlen, 0), Q_BLOCK)
    rel_w = rel_ref[...]
    s = _nt_dot(q_all, kv_ref[pl.ds(w0, wlen), KW_OFF:KW_OFF + 128])
    s = jnp.where((rel_w >= w0 - qs) & (rel_w < w0 - qs + WINDOW), s, NEG)
    p_w = jnp.exp(s - jnp.max(s, axis=-1, keepdims=True)).astype(BF16)
    a_win = jnp.dot(p_w, kv_ref[pl.ds(w0, wlen), VW_OFF:VW_OFF + 128], preferred_element_type=F32)

    p_heads = jnp.concatenate([p[r * Q_BLOCK:(r + 1) * Q_BLOCK] for r in range(NSA_REP)], axis=1)
    imp = jnp.dot(p_heads, ov_ref[...], preferred_element_type=F32)
    coli = lax.broadcasted_iota(jnp.int32, (Q_BLOCK, 128), 1)
    t_q = qs + lax.broadcasted_iota(jnp.int32, (Q_BLOCK, 128), 0)
    cur = t_q >> 6
    forced = (coli == 0) | (coli == cur) | (coli == cur - 1)
    score = jnp.where(coli * SEL_BLOCK <= t_q, imp + jnp.where(forced, FORCE_BONUS, 0.0), NEG)
    nsl = seq // SEL_BLOCK
    st = score.T[:nsl]
    groups = [st[8 * v:8 * v + 8] for v in range(nsl // 8)]
    jsub = lax.broadcasted_iota(jnp.int32, (8, Q_BLOCK), 0)
    rank = [jnp.zeros((8, Q_BLOCK), F32) for _ in groups]
    for k in range(nsl):
        rk = st[k:k + 1, :]
        for v, sv in enumerate(groups):
            if v > k // 8:
                one = jnp.where(rk >= sv, 1.0, 0.0)
            elif v < k // 8:
                one = jnp.where(rk > sv, 1.0, 0.0)
            else:
                one = jnp.where(rk > sv, 1.0, jnp.where(jsub > k % 8, jnp.where(rk == sv, 1.0, 0.0), 0.0))
            rank[v] = rank[v] + one
    drop_t = jnp.where(jnp.concatenate(rank, axis=0) < float(min(SEL_TOPK, nsl)), 0.0, -MASK_BIG)
    if nsl < 128:
        drop_t = jnp.concatenate([drop_t, jnp.zeros((128 - nsl, Q_BLOCK), F32)], axis=0)
    drop = drop_t.T.astype(BF16)
    q_sel = jnp.concatenate([jnp.concatenate([qh[r], drop], axis=1) for r in range(NSA_REP)], axis=0)

    n_tiles = (qs + Q_BLOCK + SEL_TILE - 1) // SEL_TILE

    def scores(kt):
        k0 = pl.multiple_of(kt * SEL_TILE, SEL_TILE)
        return _nt_dot(q_sel, kv_ref[pl.ds(k0, SEL_TILE), KS_OFF:KS_OFF + 256])

    def values(kt):
        k0 = pl.multiple_of(kt * SEL_TILE, SEL_TILE)
        return kv_ref[pl.ds(k0, SEL_TILE), VS_OFF:VS_OFF + 128]

    def trip(diagonal, kt, m_old):
        if not diagonal:
            s_next = scores(kt + 1)
        acc[...] = a_buf[...] * acc[...] + jnp.dot(p_buf[...], values(jnp.maximum(kt - 1, 0)),
                                                   preferred_element_type=F32)
        s = s_buf[...]
        if diagonal:
            s = jnp.where(rel_ref[:, :SEL_TILE] >= kt * SEL_TILE - qs, s, NEG)
        m_new = jnp.maximum(m_old, jnp.max(s, axis=-1, keepdims=True))
        p_buf[...] = jnp.exp(s - m_new).astype(BF16)
        a_buf[...] = jnp.exp(m_old - m_new)
        if not diagonal:
            s_buf[...] = s_next
        return m_new

    acc[...] = jnp.zeros_like(acc)
    p_buf[...] = jnp.zeros_like(p_buf)
    a_buf[...] = jnp.ones_like(a_buf)
    s_buf[...] = scores(0)
    m_s = lax.fori_loop(0, n_tiles - 1, functools.partial(trip, False), jnp.full((rows, 1), NEG, F32))
    trip(True, n_tiles - 1, m_s)
    a_sel = a_buf[...] * acc[...] + jnp.dot(p_buf[...], values(n_tiles - 1), preferred_element_type=F32)

    gate = jax.nn.sigmoid(g_ref[...])
    outs = []
    for r in range(NSA_REP):
        rr = slice(r * Q_BLOCK, (r + 1) * Q_BLOCK)
        o_s = a_sel[rr, :dk] / a_sel[rr, ONE_LANE:ONE_LANE + 1]
        o_w = a_win[rr, :dk] / a_win[rr, ONE_LANE:ONE_LANE + 1]
        outs.append(gate[:, 3 * r:3 * r + 1] * o_cmp[rr, :dk] + gate[:, 3 * r + 1:3 * r + 2] * o_s
                    + gate[:, 3 * r + 2:3 * r + 3] * o_w)
    o_ref[...] = jnp.concatenate(outs, axis=1).astype(o_ref.dtype)


def _stage_attn(qp, kcv, kv, gates, bsz, seq):
    n = bsz * seq
    nq = seq // Q_BLOCK
    ncmp = seq // CMP_STRIDE
    nsl = seq // SEL_BLOCK
    assert seq % SEL_TILE == 0
    c_start = np.arange(ncmp) * CMP_STRIDE
    s_start = np.arange(nsl) * SEL_BLOCK
    ov = ((c_start[:, None] < s_start[None, :] + SEL_BLOCK) & (c_start[:, None] + CMP_BLOCK > s_start[None, :]))
    ov = np.pad(ov.astype(np.float32), ((0, 0), (0, 128 - nsl)))
    ov[ncmp - 1] = 0.0
    ov = jnp.asarray(np.tile(ov, (NSA_REP, 1)), BF16)
    width = NSA_REP * NSA_HEAD_DIM
    rows = NSA_REP * Q_BLOCK
    wlen = min(WINDOW + Q_BLOCK, seq)
    assert wlen >= SEL_TILE
    offs = jnp.arange(rows, dtype=jnp.int32)[:, None] % Q_BLOCK
    rel = offs - jnp.arange(wlen, dtype=jnp.int32)[None, :]
    relc = offs - (jnp.arange(ncmp, dtype=jnp.int32)[None, :] * CMP_STRIDE + (CMP_BLOCK - 1))
    return pl.pallas_call(
        _nsa_body, grid=(bsz, NSA_KV_GROUPS, nq),
        in_specs=[pl.BlockSpec((Q_BLOCK, NSA_REP * 128), lambda b, g, i: (b * nq + i, g)),
                  pl.BlockSpec((ncmp, 256), lambda b, g, i: (b, g)),
                  pl.BlockSpec((seq, KV_REC), lambda b, g, i: (b, g)),
                  pl.BlockSpec((Q_BLOCK, 128), lambda b, g, i: (b * nq + i, g)),
                  _const_spec(ov.shape), _const_spec(rel.shape), _const_spec(relc.shape)],
        out_specs=pl.BlockSpec((Q_BLOCK, width), lambda b, g, i: (b * nq + i, g)),
        out_shape=jax.ShapeDtypeStruct((n, NSA_KV_GROUPS * width), BF16),
        scratch_shapes=[pltpu.VMEM((rows, 128), F32), pltpu.VMEM((rows, SEL_TILE), F32),
                        pltpu.VMEM((rows, SEL_TILE), BF16), pltpu.VMEM((rows, 1), F32)],
        compiler_params=_cparams(("parallel", "parallel", "arbitrary")), name="nsa_attention")(
            qp, kcv, kv, gates, ov, rel, relc)


def _merge_body(x_ref, ya_ref, ob_ref, gm_ref, wnsa_ref, wout_ref, g_ref, b_ref, o_ref):
    yb = jnp.dot(ob_ref[...], wnsa_ref[...], preferred_element_type=F32)
    mix_in = gm_ref[:, :D_MODEL] * ya_ref[...] + gm_ref[:, D_MODEL:] * yb
    mix = jnp.dot(mix_in.astype(BF16), wout_ref[...], preferred_element_type=F32)
    o_ref[...] = _layer_norm(DN_ALPHA * x_ref[...] + mix, g_ref[...], b_ref[...])


def _stage_merge(x2, ya, ob, gm, w_up_nsa, w_out, ln_g, ln_b):
    n = x2.shape[0]
    tm = 256
    row = lambda w: pl.BlockSpec((tm, w), lambda i: (i, 0))
    return pl.pallas_call(
        _merge_body, grid=(n // tm,),
        in_specs=[row(D_MODEL), row(D_MODEL), row(ob.shape[1]), row(2 * D_MODEL),
                  _const_spec(w_up_nsa.shape), _const_spec(w_out.shape),
                  _const_spec((1, D_MODEL)), _const_spec((1, D_MODEL))],
        out_specs=row(D_MODEL), out_shape=jax.ShapeDtypeStruct((n, D_MODEL), F32),
        compiler_params=_cparams(("parallel",)), name="merge_ln")(
            x2, ya, ob, gm, w_up_nsa.astype(BF16), w_out.astype(BF16),
            ln_g.reshape(1, -1).astype(F32), ln_b.reshape(1, -1).astype(F32))


BIG_NEG = -3.0e38


def _top_rows(s, k):
    rid = lax.broadcasted_iota(jnp.int32, s.shape, 0).astype(F32)
    rank = jnp.full(s.shape, float(k), F32)
    vals, idxs = [], []
    for r in range(k):
        m = jnp.max(s, axis=0, keepdims=True)
        idx = jnp.min(jnp.where(s == m, rid, float(s.shape[0])), axis=0, keepdims=True)
        hit = rid == idx
        rank = jnp.where(hit, float(r), rank)
        s = jnp.where(hit, BIG_NEG, s)
        vals.append(m)
        idxs.append(idx)
    return jnp.concatenate(vals, axis=0), jnp.concatenate(idxs, axis=0), rank


def _max_rounds(s, k, want_round=True):
    rnd = jnp.full(s.shape, float(k), F32) if want_round else None
    vals = []
    for r in range(k):
        m = jnp.max(s, axis=0, keepdims=True)
        eq = s == m
        if want_round:
            rnd = jnp.where(eq, float(r), rnd)
        s = jnp.where(eq, BIG_NEG, s)
        vals.append(m)
    return jnp.concatenate(vals, axis=0), rnd


def _pair_candidates(a, b):
    c8 = lax.broadcasted_iota(jnp.int32, (8, a.shape[1]), 0)
    parts = [a[0:1] + b[0:8], a[0:1] + b[8:16]]
    for r in range(1, 8):
        keep = PEER_TOPK // (r + 1)
        cand = a[r:r + 1] + b[0:8]
        parts.append(cand if keep >= 8 else jnp.where(c8 < keep, cand, BIG_NEG))
    parts.append(a[8:16] + b[0:1])
    return jnp.concatenate(parts, axis=0)


def _picks_per_rank(picked):
    rows = [jnp.sum(picked[0:16], axis=0, keepdims=True)]
    rows += [jnp.sum(picked[8 * (r + 1):8 * (r + 2)], axis=0, keepdims=True) for r in range(1, 8)]
    return jnp.concatenate(rows + [picked[72:80]], axis=0)


def _spread_by_rank(rank, per_rank):
    out = jnp.zeros(rank.shape, F32)
    for r in range(PEER_TOPK):
        out = out + jnp.where(rank == float(r), per_rank[r:r + 1], 0.0)
    return out


def _route_fast(s0, s1):
    k = PEER_TOPK
    a, rk0 = _max_rounds(s0, k)
    b, rk1 = _max_rounds(s1, k)
    cand = _pair_candidates(a, b)
    best, _ = _max_rounds(cand, k, want_round=False)
    hit = cand >= best[k - 1:k]
    picked = jnp.where(hit, 1.0, 0.0)
    z = jnp.sum(jnp.where(hit, jnp.exp(cand - best[0:1]), 0.0), axis=0, keepdims=True)
    nrank = _picks_per_rank(picked)
    count = lambda v: jnp.sum(v, axis=0, keepdims=True)
    bad = (jnp.abs(count(jnp.where(rk0 < float(k), 1.0, 0.0)) - k) + jnp.abs(count(jnp.where(rk1 < float(k), 1.0, 0.0)) - k)
           + jnp.abs(count(nrank) - k))
    return _spread_by_rank(rk0, nrank), jnp.exp(s0 - a[0:1]) / z, rk1, jnp.exp(s1 - b[0:1]), bad


def _route_exact(s0, s1):
    k = PEER_TOPK
    a, _, rk0 = _top_rows(s0, k)
    b, _, rk1 = _top_rows(s1, k)
    cand = jnp.concatenate([a[r:r + 1] + b for r in range(k)], axis=0)
    best, bidx, _ = _top_rows(cand, k)
    brank = jnp.floor(bidx * (1.0 / k))
    z = jnp.sum(jnp.exp(best - best[0:1]), axis=0, keepdims=True)
    nrank = jnp.concatenate([jnp.sum(jnp.where(brank == float(r), 1.0, 0.0), axis=0, keepdims=True)
                             for r in range(k)], axis=0)
    return _spread_by_rank(rk0, nrank), jnp.exp(s0 - a[0:1]) / z, rk1, jnp.exp(s1 - b[0:1])


def _gelu_tanh(x):
    c = math.sqrt(2.0 / math.pi)
    return x / (1.0 + jnp.exp(x * (x * x * (-2.0 * c * 0.044715) - 2.0 * c)))


def _peer_body(x_ref, wq_ref, sk_ref, u_ref, v_ref, g_ref, b_ref, o_ref,
               xt_sc, acc_sc, n_sc, e0_sc, rk1_sc, e1_sc, p_sc, st_sc):
    e = pl.program_id(1)
    nk = PEER_NKEYS

    @pl.when(e == 0)
    def _():
        xt = x_ref[...].T.astype(BF16)
        xt_sc[...] = xt
        acc_sc[...] = jnp.zeros_like(acc_sc)
        qt = jnp.dot(wq_ref[...], xt, preferred_element_type=F32).astype(BF16)
        st_sc[...] = jnp.dot(sk_ref[...], qt, preferred_element_type=F32)

        @pl.loop(0, PEER_HEADS)
        def _(h):
            s0 = st_sc[pl.ds(pl.multiple_of(2 * h * nk, nk), nk), :]
            s1 = st_sc[pl.ds(pl.multiple_of((2 * h + 1) * nk, nk), nk), :]

            def put(n_i, e0, rk1, e1):
                n_sc[h] = n_i
                e0_sc[h] = e0
                rk1_sc[h] = rk1.astype(BF16)
                e1_sc[h] = e1.astype(BF16)

            n_i, e0, rk1, e1, bad = _route_fast(s0, s1)
            put(n_i, e0, rk1, e1)

            @pl.when(jnp.max(bad) > 0.0)
            def _():
                put(*_route_exact(s0, s1))

    rows = u_ref.shape[0] // nk
    ht = jnp.dot(u_ref[...], xt_sc[...], preferred_element_type=F32)
    act = _gelu_tanh(ht).astype(BF16)
    for ii in range(rows):
        w = jnp.zeros((nk, xt_sc.shape[1]), BF16)
        for h in range(PEER_HEADS):
            n_row = n_sc[h, pl.ds(e * rows + ii, 1), :].astype(BF16)
            e0_row = e0_sc[h, pl.ds(e * rows + ii, 1), :].astype(BF16)
            w = w + jnp.where(rk1_sc[h] < n_row, e0_row * e1_sc[h], 0)
        p_sc[ii * nk:(ii + 1) * nk, :] = w * act[ii * nk:(ii + 1) * nk]
    acc_sc[...] += _tn_dot(v_ref[...], p_sc[...])

    @pl.when(e == pl.num_programs(1) - 1)
    def _():
        o_ref[...] = _layer_norm(DN_ALPHA * x_ref[...] + acc_sc[...].T, g_ref[...], b_ref[...])


def _stage_peer(x1, w_q, subkeys, u_tab, v_tab, ln_g, ln_b, tt=512, et=1024):
    n = x1.shape[0]
    tt = min(tt, n)
    H, nk, half = PEER_HEADS, PEER_NKEYS, subkeys.shape[-1]
    wq_t = w_q.T.astype(BF16)
    eye = jnp.eye(2 * H, dtype=F32)
    sk = subkeys.reshape(2 * H, nk, half)
    sk_t = jnp.einsum('ckd,ce->cked', sk, eye).reshape(2 * H * nk, 2 * H * half).astype(BF16)
    ne = u_tab.shape[0]
    return pl.pallas_call(
        _peer_body, grid=(n // tt, ne // et),
        in_specs=[pl.BlockSpec((tt, D_MODEL), lambda t, e: (t, 0)),
                  _const_spec(wq_t.shape), _const_spec(sk_t.shape),
                  pl.BlockSpec((et, D_MODEL), lambda t, e: (e, 0)),
                  pl.BlockSpec((et, D_MODEL), lambda t, e: (e, 0)),
                  _const_spec((1, D_MODEL)), _const_spec((1, D_MODEL))],
        out_specs=pl.BlockSpec((tt, D_MODEL), lambda t, e: (t, 0)),
        out_shape=jax.ShapeDtypeStruct((n, D_MODEL), F32),
        scratch_shapes=[pltpu.VMEM((D_MODEL, tt), BF16), pltpu.VMEM((D_MODEL, tt), F32),
                        pltpu.VMEM((H, nk, tt), F32), pltpu.VMEM((H, nk, tt), F32),
                        pltpu.VMEM((H, nk, tt), BF16), pltpu.VMEM((H, nk, tt), BF16),
                        pltpu.VMEM((et, tt), BF16), pltpu.VMEM((2 * H * nk, tt), F32)],
        compiler_params=_cparams(("parallel", "arbitrary")), name="peer_ffn")(
            x1, wq_t, sk_t, u_tab.astype(BF16), v_tab.astype(BF16),
            ln_g.reshape(1, -1).astype(F32), ln_b.reshape(1, -1).astype(F32))


def _layer(x, w_in, b_in, ssm_a_re, ssm_a_im, ssm_log_dt, ssm_b_re, ssm_b_im, ssm_c_re, ssm_c_im, ssm_d,
           w_glu, w_up_ssm, nsa_pe_k, nsa_w1_k, nsa_w2_k, nsa_pe_v, nsa_w1_v, nsa_w2_v, w_up_nsa,
           w_out, ln1_g, ln1_b, peer_w_q, peer_subkeys, peer_u, peer_v, ln2_g, ln2_b):
    bsz, seq, _ = x.shape
    x2 = x.reshape(bsz * seq, D_MODEL)
    u, qp, kvc, kv, gates, gm = _stage_proj(x2, w_in, b_in, seq)
    ya = _stage_ssm(u, bsz, seq, ssm_a_re, ssm_a_im, ssm_log_dt, ssm_b_re, ssm_b_im, ssm_c_re, ssm_c_im,
                    ssm_d, w_glu, w_up_ssm)
    kcv = _stage_compress(kvc, bsz, seq, nsa_pe_k, nsa_w1_k, nsa_w2_k, nsa_pe_v, nsa_w1_v, nsa_w2_v)
    ob = _stage_attn(qp, kcv, kv, gates, bsz, seq)
    x1 = _stage_merge(x2, ya, ob, gm, w_up_nsa, w_out, ln1_g, ln1_b)
    out = _stage_peer(x1, peer_w_q, peer_subkeys, peer_u, peer_v, ln2_g, ln2_b)
    return out.reshape(bsz, seq, D_MODEL)


def kernel(x, w_in, b_in, ssm_a_re, ssm_a_im, ssm_log_dt, ssm_b_re, ssm_b_im, ssm_c_re, ssm_c_im, ssm_d, w_glu,
           w_up_ssm, nsa_pe_k, nsa_w1_k, nsa_w2_k, nsa_pe_v, nsa_w1_v, nsa_w2_v, w_up_nsa, w_out, ln1_g, ln1_b,
           peer_w_q, peer_subkeys, peer_u, peer_v, ln2_g, ln2_b):
    params = (w_in, b_in, ssm_a_re, ssm_a_im, ssm_log_dt, ssm_b_re, ssm_b_im, ssm_c_re, ssm_c_im, ssm_d, w_glu,
              w_up_ssm, nsa_pe_k, nsa_w1_k, nsa_w2_k, nsa_pe_v, nsa_w1_v, nsa_w2_v, w_up_nsa, w_out, ln1_g, ln1_b,
              peer_w_q, peer_subkeys, peer_u, peer_v, ln2_g, ln2_b)
    for layer in range(w_in.shape[0]):
        x = _layer(x, *[p[layer] for p in params])
    return x
```

```python
import functools
import math

import jax
import jax.numpy as jnp
import numpy as np
from jax import lax
from jax.experimental import pallas as pl
from jax.experimental.pallas import tpu as pltpu

F32 = jnp.float32
BF16 = jnp.bfloat16

D_MODEL = 1024
SSM_WIDTH = 512
SSM_GROUP = 16
SSM_GROUPS = 32
SSM_STATE = 64
SSM_CHUNK = 8
LANE_GROUPS = 8
N_LANE_TILES = SSM_WIDTH // 128
NSA_HEADS = 8
NSA_KV_GROUPS = 2
NSA_REP = NSA_HEADS // NSA_KV_GROUPS
NSA_HEAD_DIM = 64
KV_WIDTH = NSA_KV_GROUPS * NSA_HEAD_DIM
CMP_BLOCK = 32
CMP_STRIDE = 16
CMP_HIDDEN = 128
SEL_BLOCK = 64
SEL_TOPK = 16
WINDOW = 512
Q_BLOCK = 128
FORCE_BONUS = 1.0e4
PEER_HEADS = 8
PEER_NKEYS = 128
PEER_EXPERTS = PEER_NKEYS * PEER_NKEYS
PEER_TOPK = 16
DN_ALPHA = 2.0 ** 0.25
LN_EPS = 1e-5
NEG = -1e30
VMEM_LIMIT = 56 * 1024 * 1024


def _cparams(sem):
    return pltpu.CompilerParams(dimension_semantics=sem, vmem_limit_bytes=VMEM_LIMIT)


def _const_spec(shape):
    nd = len(shape)
    return pl.BlockSpec(shape, lambda *_: (0,) * nd, pipeline_mode=pl.Buffered(1))


def _layer_norm(z, g, b):
    mu = jnp.mean(z, axis=-1, keepdims=True)
    var = jnp.mean(jnp.square(z - mu), axis=-1, keepdims=True)
    return (z - mu) * lax.rsqrt(var + LN_EPS) * g + b


def _nt_dot(a, b):
    return lax.dot_general(a, b, (((1,), (1,)), ((), ())), preferred_element_type=F32)


def _tn_dot(a, b):
    return lax.dot_general(a, b, (((0,), (0,)), ((), ())), preferred_element_type=F32)


KV_REC = 640
KS_OFF, VS_OFF, KW_OFF, VW_OFF = 0, 256, 384, 512
POS_HI, POS_LO, ONE_LANE, BLOCK_LANE0 = 64, 65, 64, 128
assert SEL_BLOCK == 64


def _position_features(pos, lane):
    hi, lo = pos >> 6, pos & 63
    f = jnp.where(lane == KS_OFF + POS_HI, hi, 0) + jnp.where(lane == KW_OFF + POS_HI, hi, 0)
    f = f + jnp.where(lane == KS_OFF + POS_LO, lo, 0) + jnp.where(lane == KW_OFF + POS_LO, lo, 0)
    f = f + jnp.where(lane == KS_OFF + BLOCK_LANE0 + hi, 1, 0)
    f = f + jnp.where(lane == VS_OFF + ONE_LANE, 1, 0) + jnp.where(lane == VW_OFF + ONE_LANE, 1, 0)
    return f.astype(F32)


def _proj_body(seq, x_ref, wu, bu, wq, bq, wc, bc, wk, bk, wg, bg, wm, bm,
               u_o, q_o, c_o, k_o, g_o, m_o):
    xb = x_ref[...].astype(BF16)
    tm = x_ref.shape[0]

    def lin(w, b):
        return jnp.dot(xb, w[...], preferred_element_type=F32) + b[...]

    u_o[...] = lin(wu, bu)
    q_o[...] = (lin(wq, bq) * (NSA_HEAD_DIM ** -0.5)).astype(BF16)
    c_o[...] = lin(wc, bc)
    pos0 = (pl.program_id(0) % (seq // tm)) * tm
    shape = (tm, NSA_KV_GROUPS * KV_REC)
    lane = lax.broadcasted_iota(jnp.int32, shape, 1)
    lane = jnp.where(lane >= KV_REC, lane - KV_REC, lane)
    feat = _position_features(pos0 + lax.broadcasted_iota(jnp.int32, shape, 0), lane)
    k_o[...] = (lin(wk, bk) + feat).astype(BF16)
    g_o[...] = lin(wg, bg)
    m_o[...] = jax.nn.sigmoid(lin(wm, bm)).astype(m_o.dtype)


def _stage_proj(x2, w_in, b_in, seq):
    n = x2.shape[0]
    tm = 256
    o0 = SSM_WIDTH
    o1 = o0 + NSA_HEADS * NSA_HEAD_DIM
    o2 = o1 + 6 * KV_WIDTH
    o3 = o2 + 3 * NSA_HEADS
    w_u, b_u = w_in[:, :o0], b_in[:o0]
    w_q, b_q = w_in[:, o0:o1], b_in[o0:o1]
    w_kv, b_kv = w_in[:, o1:o2], b_in[o1:o2]
    w_g, b_g = w_in[:, o2:o3], b_in[o2:o3]
    w_m, b_m = w_in[:, o3:], b_in[o3:]

    def pad_q(a, aux):
        a = a.reshape(a.shape[:-1] + (NSA_HEADS, NSA_HEAD_DIM))
        aux = jnp.broadcast_to(aux, a.shape[:-2] + aux.shape)
        return jnp.concatenate([a, aux], axis=-1).reshape(a.shape[:-2] + (NSA_HEADS * 128,))

    slope = 2.0 ** -(np.arange(NSA_HEADS, dtype=np.float32) + 1.0) * NSA_HEAD_DIM ** 0.5
    q_aux = np.zeros((NSA_HEADS, 128 - NSA_HEAD_DIM), np.float32)
    q_aux[:, POS_HI - NSA_HEAD_DIM] = slope * 64.0
    q_aux[:, POS_LO - NSA_HEAD_DIM] = slope

    def kv_records(a):
        parts = a.reshape(a.shape[:-1] + (6, NSA_KV_GROUPS, NSA_HEAD_DIM))
        z = lambda w: jnp.zeros(a.shape[:-1] + (w,), a.dtype)
        recs = []
        for g in range(NSA_KV_GROUPS):
            k_s, v_s, k_w, v_w = (parts[..., c, g, :] for c in (2, 3, 4, 5))
            recs += [k_s, z(256 - NSA_HEAD_DIM), v_s, z(128 - NSA_HEAD_DIM), k_w, z(128 - NSA_HEAD_DIM),
                     v_w, z(128 - NSA_HEAD_DIM)]
        return jnp.concatenate(recs, axis=-1)

    def pad_g(a):
        a = a.reshape(a.shape[:-1] + (NSA_KV_GROUPS, 3 * NSA_REP))
        a = jnp.concatenate([a, jnp.zeros(a.shape[:-1] + (128 - 3 * NSA_REP,), a.dtype)], axis=-1)
        return a.reshape(a.shape[:-2] + (NSA_KV_GROUPS * 128,))

    ws = [w_u, pad_q(w_q, jnp.zeros_like(q_aux)), w_kv[:, :2 * KV_WIDTH], kv_records(w_kv), pad_g(w_g), w_m]
    bs = [b_u, pad_q(b_q, jnp.asarray(q_aux)), b_kv[:2 * KV_WIDTH], kv_records(b_kv), pad_g(b_g), b_m]
    odt = [F32, BF16, F32, BF16, F32, BF16]
    args, in_specs = [x2], [pl.BlockSpec((tm, D_MODEL), lambda i: (i, 0))]
    for w, b in zip(ws, bs):
        args += [w.astype(BF16), b.reshape(1, -1).astype(F32)]
        in_specs += [_const_spec(w.shape), _const_spec((1, w.shape[1]))]
    out_shape = [jax.ShapeDtypeStruct((n, w.shape[1]), dt) for w, dt in zip(ws, odt)]
    out_specs = [pl.BlockSpec((tm, w.shape[1]), lambda i: (i, 0)) for w in ws]
    return pl.pallas_call(
        functools.partial(_proj_body, seq), grid=(n // tm,), in_specs=in_specs, out_specs=out_specs, out_shape=out_shape,
        compiler_params=_cparams(("parallel",)), name="in_proj")(*args)


def _ssm_weights(a_re, a_im, log_dt, b_re, b_im, c_re, c_im):
    L, G, P, H, A, J = SSM_CHUNK, SSM_GROUPS, SSM_STATE, SSM_GROUP, LANE_GROUPS, N_LANE_TILES
    lam = lax.complex(a_re.astype(F32), a_im.astype(F32))
    dt = jnp.exp(log_dt.astype(F32))[:, None]
    lam_bar = jnp.exp(lam * dt)
    b_bar = ((lam_bar - 1.0) / lam)[:, :, None] * lax.complex(b_re.astype(F32), b_im.astype(F32))
    c = lax.complex(c_re.astype(F32), c_im.astype(F32))
    k = jnp.arange(L + 1, dtype=F32)
    pw = jnp.exp((lam * dt)[None] * k[:, None, None])
    def spread(x, n_inner):
        src = np.arange(x.shape[2])
        dst = np.arange(x.shape[2] * A)
        sel = ((src[:, None] // n_inner == dst[None, :] // (A * n_inner))
               & (src[:, None] % n_inner == dst[None, :] % n_inner))
        return jnp.einsum('jrk,kc->jrc', x, jnp.asarray(sel, F32)), (dst % (A * n_inner)) // n_inner

    def group_mask(row_group, col_group):
        return jnp.asarray(row_group[:, None] == col_group[None, :], F32)

    in_rows = (np.arange(L * 128) % 128) // H
    st_rows = (np.arange(2 * A * P) % (A * P)) // P
    kern = jnp.real(jnp.einsum('ghp,kgp,gpj->kghj', c, pw[:L], b_bar))
    s_i, t_i = jnp.arange(L)[:, None], jnp.arange(L)[None, :]
    tau = jnp.clip(t_i - s_i, 0, L - 1)
    causal = (t_i >= s_i).astype(F32)
    k_st = (kern[tau] * causal[:, :, None, None, None]).reshape(L, L, J, A, H, H)
    k_st = k_st.transpose(2, 0, 3, 5, 1, 4).reshape(J, L * 128, L * H)
    w_intra, cg = spread(k_st, H)
    w_intra = w_intra * group_mask(in_rows, cg)
    q = (pw[:L][::-1][:, :, :, None] * b_bar[None]).reshape(L, J, A, P, H)
    q = jnp.stack([jnp.real(q), jnp.imag(q)], axis=0)
    q = q.transpose(2, 1, 3, 5, 0, 4).reshape(J, L * 128, 2 * P)
    m2, cg = spread(q, P)
    m2 = m2 * group_mask(in_rows, cg)
    cl = (c[None] * pw[1:][:, :, None, :]).reshape(L, J, A, H, P)
    cl = jnp.stack([jnp.real(cl), -jnp.imag(cl)], axis=0)
    cl = cl.transpose(2, 0, 3, 5, 1, 4).reshape(J, 2 * A * P, L * H)
    m1, cg = spread(cl, H)
    m1 = m1 * group_mask(st_rows, cg)
    w2 = jnp.concatenate([w_intra, m1], axis=1)
    lam8 = pw[L].reshape(G * P)
    return m2.astype(BF16), w2.astype(BF16), jnp.real(lam8), jnp.imag(lam8)


def _split_lane_tiles(src_ref, dst_sc):
    for j in range(dst_sc.shape[0]):
        dst_sc[j] = src_ref[:, j * 128:(j + 1) * 128]


def _chunk_inputs(u_sc, j, rows):
    return jnp.concatenate([u_sc[j, pl.ds(s, rows, stride=SSM_CHUNK), :] for s in range(SSM_CHUNK)], axis=1)


def _ssm_state_body(u_ref, m2_ref, zre_ref, zim_ref, u_sc):
    half = LANE_GROUPS * SSM_STATE
    _split_lane_tiles(u_ref, u_sc)
    for j in range(N_LANE_TILES):
        z = jnp.dot(_chunk_inputs(u_sc, j, zre_ref.shape[0]).astype(BF16), m2_ref[j], preferred_element_type=F32)
        zre_ref[:, j * half:(j + 1) * half] = z[:, :half]
        zim_ref[:, j * half:(j + 1) * half] = z[:, half:]


def _ssm_scan_body(zre_ref, zim_ref, lre_ref, lim_ref, xre_ref, xim_ref):
    lr, li = lre_ref[...], lim_ref[...]

    def step(c, carry):
        xr, xi = carry
        xre_ref[pl.ds(c, 1), :] = xr
        xim_ref[pl.ds(c, 1), :] = xi
        return lr * xr - li * xi + zre_ref[pl.ds(c, 1), :], lr * xi + li * xr + zim_ref[pl.ds(c, 1), :]

    zero = jnp.zeros(lr.shape, F32)
    lax.fori_loop(0, zre_ref.shape[0], step, (zero, zero))


def _ssm_out_body(u_ref, xre_ref, xim_ref, w2_ref, d_ref, wglu_ref, wup_ref, o_ref, u_sc, y_sc):
    half = LANE_GROUPS * SSM_STATE
    rows = xre_ref.shape[0]
    _split_lane_tiles(u_ref, u_sc)
    for j in range(N_LANE_TILES):
        lhs = jnp.concatenate(
            [_chunk_inputs(u_sc, j, rows), xre_ref[:, j * half:(j + 1) * half], xim_ref[:, j * half:(j + 1) * half]],
            axis=1).astype(BF16)
        yj = jnp.dot(lhs, w2_ref[j], preferred_element_type=F32)
        for t in range(SSM_CHUNK):
            y_sc[j, pl.ds(t, rows, stride=SSM_CHUNK), :] = yj[:, t * 128:(t + 1) * 128]
    y = jnp.concatenate([y_sc[j] for j in range(N_LANE_TILES)], axis=1) + d_ref[...] * u_ref[...]
    gl = jnp.dot(jax.nn.gelu(y).astype(BF16), wglu_ref[...], preferred_element_type=F32)
    v = gl[:, :SSM_WIDTH] * jax.nn.sigmoid(gl[:, SSM_WIDTH:])
    o_ref[...] = jnp.dot(v.astype(BF16), wup_ref[...], preferred_element_type=F32).astype(o_ref.dtype)


def _stage_ssm(u, bsz, seq, a_re, a_im, log_dt, b_re, b_im, c_re, c_im, d_skip, w_glu, w_up):
    n = bsz * seq
    L = SSM_CHUNK
    nck = seq // L
    gp = SSM_GROUPS * SSM_STATE
    m2, w2, l8re, l8im = _ssm_weights(a_re, a_im, log_dt, b_re, b_im, c_re, c_im)
    rs = min(nck, 512)
    zre, zim = pl.pallas_call(
        _ssm_state_body, grid=(bsz, nck // rs),
        in_specs=[pl.BlockSpec((rs * L, SSM_WIDTH), lambda b, i: (b * (nck // rs) + i, 0)),
                  _const_spec(m2.shape)],
        out_specs=[pl.BlockSpec((rs, gp), lambda b, i: (i, b))] * 2,
        out_shape=[jax.ShapeDtypeStruct((nck, bsz * gp), F32)] * 2,
        scratch_shapes=[pltpu.VMEM((N_LANE_TILES, rs * L, 128), F32)],
        compiler_params=_cparams(("parallel", "parallel")), name="ssm_chunk_state")(u, m2)
    cw = 1024
    ncol = bsz * gp // cw
    lre = jnp.tile(l8re, bsz).reshape(1, bsz * gp)
    lim = jnp.tile(l8im, bsz).reshape(1, bsz * gp)
    col = pl.BlockSpec((nck, cw), lambda i: (0, i))
    lam = pl.BlockSpec((1, cw), lambda i: (0, i))
    xre, xim = pl.pallas_call(
        _ssm_scan_body, grid=(ncol,), in_specs=[col, col, lam, lam], out_specs=[col, col],
        out_shape=[jax.ShapeDtypeStruct((nck, bsz * gp), F32)] * 2,
        compiler_params=_cparams(("parallel",)), name="ssm_carry_scan")(zre, zim, lre, lim)
    ro = min(nck, 128)
    d8 = d_skip.reshape(1, SSM_WIDTH).astype(F32)
    return pl.pallas_call(
        _ssm_out_body, grid=(bsz, nck // ro),
        in_specs=[pl.BlockSpec((ro * L, SSM_WIDTH), lambda b, i: (b * (nck // ro) + i, 0)),
                  pl.BlockSpec((ro, gp), lambda b, i: (i, b)),
                  pl.BlockSpec((ro, gp), lambda b, i: (i, b)),
                  _const_spec(w2.shape), _const_spec((1, SSM_WIDTH)),
                  _const_spec(w_glu.shape), _const_spec(w_up.shape)],
        out_specs=pl.BlockSpec((ro * L, D_MODEL), lambda b, i: (b * (nck // ro) + i, 0)),
        out_shape=jax.ShapeDtypeStruct((n, D_MODEL), BF16),
        scratch_shapes=[pltpu.VMEM((N_LANE_TILES, ro * L, 128), F32)] * 2,
        compiler_params=_cparams(("parallel", "parallel")), name="ssm_out")(
            u, xre, xim, w2, d8, w_glu.astype(BF16), w_up.astype(BF16))


def _compress_body(kv_ref, pea_ref, peb_ref, w1a_ref, w1b_ref, w2_ref, o_ref, kv_sc):
    nrow = o_ref.shape[0]
    _split_lane_tiles(kv_ref, kv_sc)
    f = jnp.concatenate([kv_sc[j, pl.ds(l, nrow, stride=CMP_STRIDE), :]
                         for l in range(CMP_STRIDE) for j in range(kv_sc.shape[0])], axis=1)
    a = jnp.dot((f + pea_ref[...]).astype(BF16), w1a_ref[...], preferred_element_type=F32)
    b = jnp.dot((f + peb_ref[...]).astype(BF16), w1b_ref[...], preferred_element_type=F32)
    pre = a + pltpu.roll(b, b.shape[0] - 1, 0)
    hid = jax.nn.gelu(pre)
    out = jnp.dot(hid.astype(BF16), w2_ref[...], preferred_element_type=F32)
    lane = lax.broadcasted_iota(jnp.int32, out.shape, 1) & 255
    c_end = lax.broadcasted_iota(jnp.int32, out.shape, 0) * CMP_STRIDE + (CMP_BLOCK - 1)
    feat = jnp.where(lane == POS_HI, c_end >> 6, 0) + jnp.where(lane == POS_LO, c_end & 63, 0)
    o_ref[...] = (out + feat.astype(F32)).astype(BF16)


def _stage_compress(kvc, bsz, seq, pe_k, w1_k, w2_k, pe_v, w1_v, w2_v):
    hl = CMP_STRIDE
    nrow = seq // hl
    G, dk, hid = NSA_KV_GROUPS, NSA_HEAD_DIM, CMP_HIDDEN
    eye = jnp.eye(2 * G, dtype=F32)

    def big_w1(lo):
        wk = w1_k.reshape(CMP_BLOCK, dk, hid)[lo:lo + hl]
        wv = w1_v.reshape(CMP_BLOCK, dk, hid)[lo:lo + hl]
        w = jnp.stack([wk, wk, wv, wv], axis=1)
        return jnp.einsum('lcdj,ce->lcdej', w, eye).reshape(hl * 2 * G * dk, 2 * G * hid)

    def big_pe(lo):
        pk, pv = pe_k[lo:lo + hl], pe_v[lo:lo + hl]
        return jnp.stack([pk, pk, pv, pv], axis=1).reshape(1, hl * 2 * G * dk)

    w2 = jnp.stack([w2_k, w2_k, w2_v, w2_v], axis=0)
    w2 = jnp.concatenate([w2, jnp.zeros((2 * G, hid, 128 - dk), w2.dtype)], axis=-1)
    slot = eye[np.array([kv * G + g for g in range(G) for kv in range(2)])].T
    w2 = jnp.einsum('cjd,ce->cjed', w2, slot).reshape(2 * G * hid, 2 * G * 128)
    width = hl * 2 * G * dk
    return pl.pallas_call(
        _compress_body, grid=(bsz,),
        in_specs=[pl.BlockSpec((seq, 2 * G * dk), lambda b: (b, 0)),
                  _const_spec((1, width)), _const_spec((1, width)),
                  _const_spec((width, 2 * G * hid)), _const_spec((width, 2 * G * hid)),
                  _const_spec((2 * G * hid, 2 * G * 128))],
        out_specs=pl.BlockSpec((nrow, 2 * G * 128), lambda b: (b, 0)),
        out_shape=jax.ShapeDtypeStruct((bsz * nrow, 2 * G * 128), BF16),
        scratch_shapes=[pltpu.VMEM((2 * G * dk // 128, seq, 128), F32)],
        compiler_params=_cparams(("parallel",)), name="nsa_compress")(
            kvc, big_pe(0).astype(F32), big_pe(hl).astype(F32),
            big_w1(0).astype(BF16), big_w1(hl).astype(BF16), w2.astype(BF16))


MASK_BIG = 2.0 ** 100
SEL_TILE = 256


def _nsa_body(q_ref, kcv_ref, kv_ref, g_ref, ov_ref, rel_ref, relc_ref, o_ref, acc, s_buf, p_buf, a_buf):
    i = pl.program_id(2)
    qs = i * Q_BLOCK
    seq = kv_ref.shape[0]
    rows = NSA_REP * Q_BLOCK
    dk = NSA_HEAD_DIM
    qh = [q_ref[:, r * 128:(r + 1) * 128] for r in range(NSA_REP)]
    q_all = jnp.concatenate(qh, axis=0)

    kc = kcv_ref[:, :128]
    vc = kcv_ref[:, 128:]
    vis_c = relc_ref[...] >= -qs
    s = jnp.where(vis_c, _nt_dot(q_all, kc), NEG)
    p = jnp.exp(s - jnp.max(s, axis=-1, keepdims=True)) * jnp.where(vis_c, 1.0, 0.0)
    p = (p / jnp.maximum(jnp.sum(p, axis=-1, keepdims=True), 1e-30)).astype(BF16)
    o_cmp = jnp.dot(p, vc, preferred_element_type=F32)

    wlen = min(WINDOW + Q_BLOCK, seq)
    w0 = pl.multiple_of(jnp.maximum(qs + Q_BLOCK - wlen, 0), Q_BLOCK)
    rel_w = rel_ref[...]
    s = _nt_dot(q_all, kv_ref[pl.ds(w0, wlen), KW_OFF:KW_OFF + 128])
    s = jnp.where((rel_w >= w0 - qs) & (rel_w < w0 - qs + WINDOW), s, NEG)
    p_w = jnp.exp(s - jnp.max(s, axis=-1, keepdims=True)).astype(BF16)
    a_win = jnp.dot(p_w, kv_ref[pl.ds(w0, wlen), VW_OFF:VW_OFF + 128], preferred_element_type=F32)

    p_heads = jnp.concatenate([p[r * Q_BLOCK:(r + 1) * Q_BLOCK] for r in range(NSA_REP)], axis=1)
    imp = jnp.dot(p_heads, ov_ref[...], preferred_element_type=F32)
    coli = lax.broadcasted_iota(jnp.int32, (Q_BLOCK, 128), 1)
    t_q = qs + lax.broadcasted_iota(jnp.int32, (Q_BLOCK, 128), 0)
    cur = t_q >> 6
    forced = (coli == 0) | (coli == cur) | (coli == cur - 1)
    score = jnp.where(coli * SEL_BLOCK <= t_q, imp + jnp.where(forced, FORCE_BONUS, 0.0), NEG)
    nsl = seq // SEL_BLOCK
    st = score.T[:nsl]
    groups = [st[8 * v:8 * v + 8] for v in range(nsl // 8)]
    jsub = lax.broadcasted_iota(jnp.int32, (8, Q_BLOCK), 0)
    rank = [jnp.zeros((8, Q_BLOCK), F32) for _ in groups]
    for k in range(nsl):
        rk = st[k:k + 1, :]
        for v, sv in enumerate(groups):
            if v > k // 8:
                one = jnp.where(rk >= sv, 1.0, 0.0)
            elif v < k // 8:
                one = jnp.where(rk > sv, 1.0, 0.0)
            else:
                one = jnp.where(rk > sv, 1.0, jnp.where(jsub > k % 8, jnp.where(rk == sv, 1.0, 0.0), 0.0))
            rank[v] = rank[v] + one
    drop_t = jnp.where(jnp.concatenate(rank, axis=0) < float(min(SEL_TOPK, nsl)), 0.0, -MASK_BIG)
    if nsl < 128:
        drop_t = jnp.concatenate([drop_t, jnp.zeros((128 - nsl, Q_BLOCK), F32)], axis=0)
    drop = drop_t.T.astype(BF16)
    q_sel = jnp.concatenate([jnp.concatenate([qh[r], drop], axis=1) for r in range(NSA_REP)], axis=0)

    n_tiles = (qs + Q_BLOCK + SEL_TILE - 1) // SEL_TILE

    def scores(kt):
        k0 = pl.multiple_of(kt * SEL_TILE, SEL_TILE)
        return _nt_dot(q_sel, kv_ref[pl.ds(k0, SEL_TILE), KS_OFF:KS_OFF + 256])

    def values(kt):
        k0 = pl.multiple_of(kt * SEL_TILE, SEL_TILE)
        return kv_ref[pl.ds(k0, SEL_TILE), VS_OFF:VS_OFF + 128]

    def trip(diagonal, kt, m_old):
        if not diagonal:
            s_next = scores(kt + 1)
        acc[...] = a_buf[...] * acc[...] + jnp.dot(p_buf[...], values(jnp.maximum(kt - 1, 0)),
                                                   preferred_element_type=F32)
        s = s_buf[...]
        if diagonal:
            s = jnp.where(rel_ref[:, :SEL_TILE] >= kt * SEL_TILE - qs, s, NEG)
        m_new = jnp.maximum(m_old, jnp.max(s, axis=-1, keepdims=True))
        p_buf[...] = jnp.exp(s - m_new).astype(BF16)
        a_buf[...] = jnp.exp(m_old - m_new)
        if not diagonal:
            s_buf[...] = s_next
        return m_new

    acc[...] = jnp.zeros_like(acc)
    p_buf[...] = jnp.zeros_like(p_buf)
    a_buf[...] = jnp.ones_like(a_buf)
    s_buf[...] = scores(0)
    m_s = lax.fori_loop(0, n_tiles - 1, functools.partial(trip, False), jnp.full((rows, 1), NEG, F32))
    trip(True, n_tiles - 1, m_s)
    a_sel = a_buf[...] * acc[...] + jnp.dot(p_buf[...], values(n_tiles - 1), preferred_element_type=F32)

    gate = jax.nn.sigmoid(g_ref[...])
    outs = []
    for r in range(NSA_REP):
        rr = slice(r * Q_BLOCK, (r + 1) * Q_BLOCK)
        o_s = a_sel[rr, :dk] / a_sel[rr, ONE_LANE:ONE_LANE + 1]
        o_w = a_win[rr, :dk] / a_win[rr, ONE_LANE:ONE_LANE + 1]
        outs.append(gate[:, 3 * r:3 * r + 1] * o_cmp[rr, :dk] + gate[:, 3 * r + 1:3 * r + 2] * o_s
                    + gate[:, 3 * r + 2:3 * r + 3] * o_w)
    o_ref[...] = jnp.concatenate(outs, axis=1).astype(o_ref.dtype)


def _stage_attn(qp, kcv, kv, gates, bsz, seq):
    n = bsz * seq
    nq = seq // Q_BLOCK
    ncmp = seq // CMP_STRIDE
    nsl = seq // SEL_BLOCK
    assert seq % SEL_TILE == 0
    c_start = np.arange(ncmp) * CMP_STRIDE
    s_start = np.arange(nsl) * SEL_BLOCK
    ov = ((c_start[:, None] < s_start[None, :] + SEL_BLOCK) & (c_start[:, None] + CMP_BLOCK > s_start[None, :]))
    ov = np.pad(ov.astype(np.float32), ((0, 0), (0, 128 - nsl)))
    ov[ncmp - 1] = 0.0
    ov = jnp.asarray(np.tile(ov, (NSA_REP, 1)), BF16)
    width = NSA_REP * NSA_HEAD_DIM
    rows = NSA_REP * Q_BLOCK
    wlen = min(WINDOW + Q_BLOCK, seq)
    assert wlen >= SEL_TILE
    offs = jnp.arange(rows, dtype=jnp.int32)[:, None] % Q_BLOCK
    rel = offs - jnp.arange(wlen, dtype=jnp.int32)[None, :]
    relc = offs - (jnp.arange(ncmp, dtype=jnp.int32)[None, :] * CMP_STRIDE + (CMP_BLOCK - 1))
    return pl.pallas_call(
        _nsa_body, grid=(bsz, NSA_KV_GROUPS, nq),
        in_specs=[pl.BlockSpec((Q_BLOCK, NSA_REP * 128), lambda b, g, i: (b * nq + i, g)),
                  pl.BlockSpec((ncmp, 256), lambda b, g, i: (b, g)),
                  pl.BlockSpec((seq, KV_REC), lambda b, g, i: (b, g)),
                  pl.BlockSpec((Q_BLOCK, 128), lambda b, g, i: (b * nq + i, g)),
                  _const_spec(ov.shape), _const_spec(rel.shape), _const_spec(relc.shape)],
        out_specs=pl.BlockSpec((Q_BLOCK, width), lambda b, g, i: (b * nq + i, g)),
        out_shape=jax.ShapeDtypeStruct((n, NSA_KV_GROUPS * width), BF16),
        scratch_shapes=[pltpu.VMEM((rows, 128), F32), pltpu.VMEM((rows, SEL_TILE), F32),
                        pltpu.VMEM((rows, SEL_TILE), BF16), pltpu.VMEM((rows, 1), F32)],
        compiler_params=_cparams(("parallel", "parallel", "arbitrary")), name="nsa_attention")(
            qp, kcv, kv, gates, ov, rel, relc)


def _merge_body(x_ref, ya_ref, ob_ref, gm_ref, wnsa_ref, wout_ref, g_ref, b_ref, o_ref):
    yb = jnp.dot(ob_ref[...], wnsa_ref[...], preferred_element_type=F32)
    gm = gm_ref[...].astype(F32)
    mix_in = gm[:, :D_MODEL] * ya_ref[...].astype(F32) + gm[:, D_MODEL:] * yb
    mix = jnp.dot(mix_in.astype(BF16), wout_ref[...], preferred_element_type=F32)
    o_ref[...] = _layer_norm(DN_ALPHA * x_ref[...] + mix, g_ref[...], b_ref[...])


def _stage_merge(x2, ya, ob, gm, w_up_nsa, w_out, ln_g, ln_b):
    n = x2.shape[0]
    tm = 256
    row = lambda w: pl.BlockSpec((tm, w), lambda i: (i, 0))
    return pl.pallas_call(
        _merge_body, grid=(n // tm,),
        in_specs=[row(D_MODEL), row(D_MODEL), row(ob.shape[1]), row(2 * D_MODEL),
                  _const_spec(w_up_nsa.shape), _const_spec(w_out.shape),
                  _const_spec((1, D_MODEL)), _const_spec((1, D_MODEL))],
        out_specs=row(D_MODEL), out_shape=jax.ShapeDtypeStruct((n, D_MODEL), F32),
        compiler_params=_cparams(("parallel",)), name="merge_ln")(
            x2, ya, ob, gm, w_up_nsa.astype(BF16), w_out.astype(BF16),
            ln_g.reshape(1, -1).astype(F32), ln_b.reshape(1, -1).astype(F32))


BIG_NEG = -3.0e38


def _top_rows(s, k):
    rid = lax.broadcasted_iota(jnp.int32, s.shape, 0).astype(F32)
    rank = jnp.full(s.shape, float(k), F32)
    vals, idxs = [], []
    for r in range(k):
        m = jnp.max(s, axis=0, keepdims=True)
        idx = jnp.min(jnp.where(s == m, rid, float(s.shape[0])), axis=0, keepdims=True)
        hit = rid == idx
        rank = jnp.where(hit, float(r), rank)
        s = jnp.where(hit, BIG_NEG, s)
        vals.append(m)
        idxs.append(idx)
    return jnp.concatenate(vals, axis=0), jnp.concatenate(idxs, axis=0), rank


def _max_rounds(s, k, want_round=True):
    rnd = jnp.full(s.shape, float(k), F32) if want_round else None
    vals = []
    for r in range(k):
        m = jnp.max(s, axis=0, keepdims=True)
        eq = s == m
        if want_round:
            rnd = jnp.where(eq, float(r), rnd)
        s = jnp.where(eq, BIG_NEG, s)
        vals.append(m)
    return jnp.concatenate(vals, axis=0), rnd


def _pair_candidates(a, b):
    c8 = lax.broadcasted_iota(jnp.int32, (8, a.shape[1]), 0)
    parts = [a[0:1] + b[0:8], a[0:1] + b[8:16]]
    for r in range(1, 8):
        keep = PEER_TOPK // (r + 1)
        cand = a[r:r + 1] + b[0:8]
        parts.append(cand if keep >= 8 else jnp.where(c8 < keep, cand, BIG_NEG))
    parts.append(a[8:16] + b[0:1])
    return jnp.concatenate(parts, axis=0)


def _picks_per_rank(picked):
    rows = [jnp.sum(picked[0:16], axis=0, keepdims=True)]
    rows += [jnp.sum(picked[8 * (r + 1):8 * (r + 2)], axis=0, keepdims=True) for r in range(1, 8)]
    return jnp.concatenate(rows + [picked[72:80]], axis=0)


def _spread_by_rank(rank, per_rank):
    out = jnp.zeros(rank.shape, F32)
    for r in range(PEER_TOPK):
        out = out + jnp.where(rank == float(r), per_rank[r:r + 1], 0.0)
    return out


def _route_fast(s0, s1):
    k = PEER_TOPK
    a, rk0 = _max_rounds(s0, k)
    b, rk1 = _max_rounds(s1, k)
    cand = _pair_candidates(a, b)
    best, _ = _max_rounds(cand, k, want_round=False)
    hit = cand >= best[k - 1:k]
    picked = jnp.where(hit, 1.0, 0.0)
    z = jnp.sum(jnp.where(hit, jnp.exp(cand - best[0:1]), 0.0), axis=0, keepdims=True)
    nrank = _picks_per_rank(picked)
    count = lambda v: jnp.sum(v, axis=0, keepdims=True)
    bad = (jnp.abs(count(jnp.where(rk0 < float(k), 1.0, 0.0)) - k) + jnp.abs(count(jnp.where(rk1 < float(k), 1.0, 0.0)) - k)
           + jnp.abs(count(nrank) - k))
    return _spread_by_rank(rk0, nrank), jnp.exp(s0 - a[0:1]) / z, rk1, jnp.exp(s1 - b[0:1]), bad


def _route_exact(s0, s1):
    k = PEER_TOPK
    a, _, rk0 = _top_rows(s0, k)
    b, _, rk1 = _top_rows(s1, k)
    cand = jnp.concatenate([a[r:r + 1] + b for r in range(k)], axis=0)
    best, bidx, _ = _top_rows(cand, k)
    brank = jnp.floor(bidx * (1.0 / k))
    z = jnp.sum(jnp.exp(best - best[0:1]), axis=0, keepdims=True)
    nrank = jnp.concatenate([jnp.sum(jnp.where(brank == float(r), 1.0, 0.0), axis=0, keepdims=True)
                             for r in range(k)], axis=0)
    return _spread_by_rank(rk0, nrank), jnp.exp(s0 - a[0:1]) / z, rk1, jnp.exp(s1 - b[0:1])


def _gelu_tanh(x):
    c = math.sqrt(2.0 / math.pi)
    return x / (1.0 + jnp.exp(x * (x * x * (-2.0 * c * 0.044715) - 2.0 * c)))


def _peer_body(x_ref, wq_ref, sk_ref, u_ref, v_ref, g_ref, b_ref, o_ref,
               xt_sc, acc_sc, n_sc, e0_sc, rk1_sc, e1_sc, p_sc, st_sc):
    e = pl.program_id(1)
    nk = PEER_NKEYS

    @pl.when(e == 0)
    def _():
        xt = x_ref[...].T.astype(BF16)
        xt_sc[...] = xt
        acc_sc[...] = jnp.zeros_like(acc_sc)
        qt = jnp.dot(wq_ref[...], xt, preferred_element_type=F32).astype(BF16)
        st_sc[...] = jnp.dot(sk_ref[...], qt, preferred_element_type=F32)

        @pl.loop(0, PEER_HEADS)
        def _(h):
            s0 = st_sc[pl.ds(pl.multiple_of(2 * h * nk, nk), nk), :]
            s1 = st_sc[pl.ds(pl.multiple_of((2 * h + 1) * nk, nk), nk), :]

            def put(n_i, e0, rk1, e1):
                n_sc[h] = n_i
                e0_sc[h] = e0
                rk1_sc[h] = rk1.astype(BF16)
                e1_sc[h] = e1.astype(BF16)

            n_i, e0, rk1, e1, bad = _route_fast(s0, s1)
            put(n_i, e0, rk1, e1)

            @pl.when(jnp.max(bad) > 0.0)
            def _():
                put(*_route_exact(s0, s1))

    rows = u_ref.shape[0] // nk
    ht = jnp.dot(u_ref[...], xt_sc[...], preferred_element_type=F32)
    act = _gelu_tanh(ht).astype(BF16)
    for ii in range(rows):
        w = jnp.zeros((nk, xt_sc.shape[1]), BF16)
        for h in range(PEER_HEADS):
            n_row = n_sc[h, pl.ds(e * rows + ii, 1), :].astype(BF16)
            e0_row = e0_sc[h, pl.ds(e * rows + ii, 1), :].astype(BF16)
            w = w + jnp.where(rk1_sc[h] < n_row, e0_row * e1_sc[h], 0)
        p_sc[ii * nk:(ii + 1) * nk, :] = w * act[ii * nk:(ii + 1) * nk]
    acc_sc[...] += _tn_dot(v_ref[...], p_sc[...])

    @pl.when(e == pl.num_programs(1) - 1)
    def _():
        o_ref[...] = _layer_norm(DN_ALPHA * x_ref[...] + acc_sc[...].T, g_ref[...], b_ref[...])


def _stage_peer(x1, w_q, subkeys, u_tab, v_tab, ln_g, ln_b, tt=512, et=1024):
    n = x1.shape[0]
    tt = min(tt, n)
    H, nk, half = PEER_HEADS, PEER_NKEYS, subkeys.shape[-1]
    wq_t = w_q.T.astype(BF16)
    eye = jnp.eye(2 * H, dtype=F32)
    sk = subkeys.reshape(2 * H, nk, half)
    sk_t = jnp.einsum('ckd,ce->cked', sk, eye).reshape(2 * H * nk, 2 * H * half).astype(BF16)
    ne = u_tab.shape[0]
    return pl.pallas_call(
        _peer_body, grid=(n // tt, ne // et),
        in_specs=[pl.BlockSpec((tt, D_MODEL), lambda t, e: (t, 0)),
                  _const_spec(wq_t.shape), _const_spec(sk_t.shape),
                  pl.BlockSpec((et, D_MODEL), lambda t, e: (e, 0)),
                  pl.BlockSpec((et, D_MODEL), lambda t, e: (e, 0)),
                  _const_spec((1, D_MODEL)), _const_spec((1, D_MODEL))],
        out_specs=pl.BlockSpec((tt, D_MODEL), lambda t, e: (t, 0)),
        out_shape=jax.ShapeDtypeStruct((n, D_MODEL), F32),
        scratch_shapes=[pltpu.VMEM((D_MODEL, tt), BF16), pltpu.VMEM((D_MODEL, tt), F32),
                        pltpu.VMEM((H, nk, tt), F32), pltpu.VMEM((H, nk, tt), F32),
                        pltpu.VMEM((H, nk, tt), BF16), pltpu.VMEM((H, nk, tt), BF16),
                        pltpu.VMEM((et, tt), BF16), pltpu.VMEM((2 * H * nk, tt), F32)],
        compiler_params=_cparams(("parallel", "arbitrary")), name="peer_ffn")(
            x1, wq_t, sk_t, u_tab.astype(BF16), v_tab.astype(BF16),
            ln_g.reshape(1, -1).astype(F32), ln_b.reshape(1, -1).astype(F32))


def _layer(x, w_in, b_in, ssm_a_re, ssm_a_im, ssm_log_dt, ssm_b_re, ssm_b_im, ssm_c_re, ssm_c_im, ssm_d,
           w_glu, w_up_ssm, nsa_pe_k, nsa_w1_k, nsa_w2_k, nsa_pe_v, nsa_w1_v, nsa_w2_v, w_up_nsa,
           w_out, ln1_g, ln1_b, peer_w_q, peer_subkeys, peer_u, peer_v, ln2_g, ln2_b):
    bsz, seq, _ = x.shape
    x2 = x.reshape(bsz * seq, D_MODEL)
    u, qp, kvc, kv, gates, gm = _stage_proj(x2, w_in, b_in, seq)
    ya = _stage_ssm(u, bsz, seq, ssm_a_re, ssm_a_im, ssm_log_dt, ssm_b_re, ssm_b_im, ssm_c_re, ssm_c_im,
                    ssm_d, w_glu, w_up_ssm)
    kcv = _stage_compress(kvc, bsz, seq, nsa_pe_k, nsa_w1_k, nsa_w2_k, nsa_pe_v, nsa_w1_v, nsa_w2_v)
    ob = _stage_attn(qp, kcv, kv, gates, bsz, seq)
    x1 = _stage_merge(x2, ya, ob, gm, w_up_nsa, w_out, ln1_g, ln1_b)
    out = _stage_peer(x1, peer_w_q, peer_subkeys, peer_u, peer_v, ln2_g, ln2_b)
    return out.reshape(bsz, seq, D_MODEL)


def kernel(x, w_in, b_in, ssm_a_re, ssm_a_im, ssm_log_dt, ssm_b_re, ssm_b_im, ssm_c_re, ssm_c_im, ssm_d, w_glu,
           w_up_ssm, nsa_pe_k, nsa_w1_k, nsa_w2_k, nsa_pe_v, nsa_w1_v, nsa_w2_v, w_up_nsa, w_out, ln1_g, ln1_b,
           peer_w_q, peer_subkeys, peer_u, peer_v, ln2_g, ln2_b):
    params = (w_in, b_in, ssm_a_re, ssm_a_im, ssm_log_dt, ssm_b_re, ssm_b_im, ssm_c_re, ssm_c_im, ssm_d, w_glu,
              w_up_ssm, nsa_pe_k, nsa_w1_k, nsa_w2_k, nsa_pe_v, nsa_w1_v, nsa_w2_v, w_up_nsa, w_out, ln1_g, ln1_b,
              peer_w_q, peer_subkeys, peer_u, peer_v, ln2_g, ln2_b)
    for layer in range(w_in.shape[0]):
        x = _layer(x, *[p[layer] for p in params])
    return x
```

```python
import functools
import math

import jax
import jax.numpy as jnp
import numpy as np
from jax import lax
from jax.experimental import pallas as pl
from jax.experimental.pallas import tpu as pltpu

F32 = jnp.float32
BF16 = jnp.bfloat16

D_MODEL = 1024
SSM_WIDTH = 512
SSM_GROUP = 16
SSM_GROUPS = 32
SSM_STATE = 64
SSM_CHUNK = 8
LANE_GROUPS = 8
N_LANE_TILES = SSM_WIDTH // 128
NSA_HEADS = 8
NSA_KV_GROUPS = 2
NSA_REP = NSA_HEADS // NSA_KV_GROUPS
NSA_HEAD_DIM = 64
KV_WIDTH = NSA_KV_GROUPS * NSA_HEAD_DIM
CMP_BLOCK = 32
CMP_STRIDE = 16
CMP_HIDDEN = 128
SEL_BLOCK = 64
SEL_TOPK = 16
WINDOW = 512
Q_BLOCK = 256
FORCE_BONUS = 1.0e4
PEER_HEADS = 8
PEER_NKEYS = 128
PEER_EXPERTS = PEER_NKEYS * PEER_NKEYS
PEER_TOPK = 16
DN_ALPHA = 2.0 ** 0.25
LN_EPS = 1e-5
NEG = -1e30
VMEM_LIMIT = 56 * 1024 * 1024


def _cparams(sem):
    return pltpu.CompilerParams(dimension_semantics=sem, vmem_limit_bytes=VMEM_LIMIT)


def _const_spec(shape):
    nd = len(shape)
    return pl.BlockSpec(shape, lambda *_: (0,) * nd, pipeline_mode=pl.Buffered(1))


def _layer_norm(z, g, b):
    mu = jnp.mean(z, axis=-1, keepdims=True)
    var = jnp.mean(jnp.square(z - mu), axis=-1, keepdims=True)
    return (z - mu) * lax.rsqrt(var + LN_EPS) * g + b


def _nt_dot(a, b):
    return lax.dot_general(a, b, (((1,), (1,)), ((), ())), preferred_element_type=F32)


def _tn_dot(a, b):
    return lax.dot_general(a, b, (((0,), (0,)), ((), ())), preferred_element_type=F32)


KV_REC = 640
KS_OFF, VS_OFF, KW_OFF, VW_OFF = 0, 256, 384, 512
POS_HI, POS_LO, ONE_LANE, BLOCK_LANE0 = 64, 65, 64, 128
assert SEL_BLOCK == 64


def _position_features(pos, lane):
    hi, lo = pos >> 6, pos & 63
    f = jnp.where(lane == KS_OFF + POS_HI, hi, 0) + jnp.where(lane == KW_OFF + POS_HI, hi, 0)
    f = f + jnp.where(lane == KS_OFF + POS_LO, lo, 0) + jnp.where(lane == KW_OFF + POS_LO, lo, 0)
    f = f + jnp.where(lane == KS_OFF + BLOCK_LANE0 + hi, 1, 0)
    f = f + jnp.where(lane == VS_OFF + ONE_LANE, 1, 0) + jnp.where(lane == VW_OFF + ONE_LANE, 1, 0)
    return f.astype(F32)


def _proj_body(seq, x_ref, wu, bu, wq, bq, wc, bc, wk, bk, wg, bg, wm, bm,
               u_o, q_o, c_o, k_o, g_o, m_o):
    xb = x_ref[...].astype(BF16)
    tm = x_ref.shape[0]

    def lin(w, b):
        return jnp.dot(xb, w[...], preferred_element_type=F32) + b[...]

    u_o[...] = lin(wu, bu)
    q_o[...] = (lin(wq, bq) * (NSA_HEAD_DIM ** -0.5)).astype(BF16)
    c_o[...] = lin(wc, bc)
    pos0 = (pl.program_id(0) % (seq // tm)) * tm
    shape = (tm, NSA_KV_GROUPS * KV_REC)
    lane = lax.broadcasted_iota(jnp.int32, shape, 1)
    lane = jnp.where(lane >= KV_REC, lane - KV_REC, lane)
    feat = _position_features(pos0 + lax.broadcasted_iota(jnp.int32, shape, 0), lane)
    k_o[...] = (lin(wk, bk) + feat).astype(BF16)
    g_o[...] = lin(wg, bg)
    m_o[...] = jax.nn.sigmoid(lin(wm, bm)).astype(m_o.dtype)


def _stage_proj(x2, w_in, b_in, seq):
    n = x2.shape[0]
    tm = 256
    o0 = SSM_WIDTH
    o1 = o0 + NSA_HEADS * NSA_HEAD_DIM
    o2 = o1 + 6 * KV_WIDTH
    o3 = o2 + 3 * NSA_HEADS
    w_u, b_u = w_in[:, :o0], b_in[:o0]
    w_q, b_q = w_in[:, o0:o1], b_in[o0:o1]
    w_kv, b_kv = w_in[:, o1:o2], b_in[o1:o2]
    w_g, b_g = w_in[:, o2:o3], b_in[o2:o3]
    w_m, b_m = w_in[:, o3:], b_in[o3:]

    def pad_q(a, aux):
        a = a.reshape(a.shape[:-1] + (NSA_HEADS, NSA_HEAD_DIM))
        aux = jnp.broadcast_to(aux, a.shape[:-2] + aux.shape)
        return jnp.concatenate([a, aux], axis=-1).reshape(a.shape[:-2] + (NSA_HEADS * 128,))

    slope = 2.0 ** -(np.arange(NSA_HEADS, dtype=np.float32) + 1.0) * NSA_HEAD_DIM ** 0.5
    q_aux = np.zeros((NSA_HEADS, 128 - NSA_HEAD_DIM), np.float32)
    q_aux[:, POS_HI - NSA_HEAD_DIM] = slope * 64.0
    q_aux[:, POS_LO - NSA_HEAD_DIM] = slope

    def kv_records(a):
        parts = a.reshape(a.shape[:-1] + (6, NSA_KV_GROUPS, NSA_HEAD_DIM))
        z = lambda w: jnp.zeros(a.shape[:-1] + (w,), a.dtype)
        recs = []
        for g in range(NSA_KV_GROUPS):
            k_s, v_s, k_w, v_w = (parts[..., c, g, :] for c in (2, 3, 4, 5))
            recs += [k_s, z(256 - NSA_HEAD_DIM), v_s, z(128 - NSA_HEAD_DIM), k_w, z(128 - NSA_HEAD_DIM),
                     v_w, z(128 - NSA_HEAD_DIM)]
        return jnp.concatenate(recs, axis=-1)

    def pad_g(a):
        a = a.reshape(a.shape[:-1] + (NSA_KV_GROUPS, 3 * NSA_REP))
        a = jnp.concatenate([a, jnp.zeros(a.shape[:-1] + (128 - 3 * NSA_REP,), a.dtype)], axis=-1)
        return a.reshape(a.shape[:-2] + (NSA_KV_GROUPS * 128,))

    ws = [w_u, pad_q(w_q, jnp.zeros_like(q_aux)), w_kv[:, :2 * KV_WIDTH], kv_records(w_kv), pad_g(w_g), w_m]
    bs = [b_u, pad_q(b_q, jnp.asarray(q_aux)), b_kv[:2 * KV_WIDTH], kv_records(b_kv), pad_g(b_g), b_m]
    odt = [F32, BF16, F32, BF16, F32, BF16]
    args, in_specs = [x2], [pl.BlockSpec((tm, D_MODEL), lambda i: (i, 0))]
    for w, b in zip(ws, bs):
        args += [w.astype(BF16), b.reshape(1, -1).astype(F32)]
        in_specs += [_const_spec(w.shape), _const_spec((1, w.shape[1]))]
    out_shape = [jax.ShapeDtypeStruct((n, w.shape[1]), dt) for w, dt in zip(ws, odt)]
    out_specs = [pl.BlockSpec((tm, w.shape[1]), lambda i: (i, 0)) for w in ws]
    return pl.pallas_call(
        functools.partial(_proj_body, seq), grid=(n // tm,), in_specs=in_specs, out_specs=out_specs, out_shape=out_shape,
        compiler_params=_cparams(("parallel",)), name="in_proj")(*args)


def _ssm_weights(a_re, a_im, log_dt, b_re, b_im, c_re, c_im):
    L, G, P, H, A, J = SSM_CHUNK, SSM_GROUPS, SSM_STATE, SSM_GROUP, LANE_GROUPS, N_LANE_TILES
    lam = lax.complex(a_re.astype(F32), a_im.astype(F32))
    dt = jnp.exp(log_dt.astype(F32))[:, None]
    lam_bar = jnp.exp(lam * dt)
    b_bar = ((lam_bar - 1.0) / lam)[:, :, None] * lax.complex(b_re.astype(F32), b_im.astype(F32))
    c = lax.complex(c_re.astype(F32), c_im.astype(F32))
    k = jnp.arange(L + 1, dtype=F32)
    pw = jnp.exp((lam * dt)[None] * k[:, None, None])
    def spread(x, n_inner):
        src = np.arange(x.shape[2])
        dst = np.arange(x.shape[2] * A)
        sel = ((src[:, None] // n_inner == dst[None, :] // (A * n_inner))
               & (src[:, None] % n_inner == dst[None, :] % n_inner))
        return jnp.einsum('jrk,kc->jrc', x, jnp.asarray(sel, F32)), (dst % (A * n_inner)) // n_inner

    def group_mask(row_group, col_group):
        return jnp.asarray(row_group[:, None] == col_group[None, :], F32)

    in_rows = (np.arange(L * 128) % 128) // H
    st_rows = (np.arange(2 * A * P) % (A * P)) // P
    kern = jnp.real(jnp.einsum('ghp,kgp,gpj->kghj', c, pw[:L], b_bar))
    s_i, t_i = jnp.arange(L)[:, None], jnp.arange(L)[None, :]
    tau = jnp.clip(t_i - s_i, 0, L - 1)
    causal = (t_i >= s_i).astype(F32)
    k_st = (kern[tau] * causal[:, :, None, None, None]).reshape(L, L, J, A, H, H)
    k_st = k_st.transpose(2, 0, 3, 5, 1, 4).reshape(J, L * 128, L * H)
    w_intra, cg = spread(k_st, H)
    w_intra = w_intra * group_mask(in_rows, cg)
    q = (pw[:L][::-1][:, :, :, None] * b_bar[None]).reshape(L, J, A, P, H)
    q = jnp.stack([jnp.real(q), jnp.imag(q)], axis=0)
    q = q.transpose(2, 1, 3, 5, 0, 4).reshape(J, L * 128, 2 * P)
    m2, cg = spread(q, P)
    m2 = m2 * group_mask(in_rows, cg)
    cl = (c[None] * pw[1:][:, :, None, :]).reshape(L, J, A, H, P)
    cl = jnp.stack([jnp.real(cl), -jnp.imag(cl)], axis=0)
    cl = cl.transpose(2, 0, 3, 5, 1, 4).reshape(J, 2 * A * P, L * H)
    m1, cg = spread(cl, H)
    m1 = m1 * group_mask(st_rows, cg)
    w2 = jnp.concatenate([w_intra, m1], axis=1)
    lam8 = pw[L].reshape(G * P)
    return m2.astype(BF16), w2.astype(BF16), jnp.real(lam8), jnp.imag(lam8)


def _split_lane_tiles(src_ref, dst_sc):
    for j in range(dst_sc.shape[0]):
        dst_sc[j] = src_ref[:, j * 128:(j + 1) * 128]


def _chunk_inputs(u_sc, j, rows):
    return jnp.concatenate([u_sc[j, pl.ds(s, rows, stride=SSM_CHUNK), :] for s in range(SSM_CHUNK)], axis=1)


def _ssm_state_body(u_ref, m2_ref, zre_ref, zim_ref, u_sc):
    half = LANE_GROUPS * SSM_STATE
    _split_lane_tiles(u_ref, u_sc)
    for j in range(N_LANE_TILES):
        z = jnp.dot(_chunk_inputs(u_sc, j, zre_ref.shape[0]).astype(BF16), m2_ref[j], preferred_element_type=F32)
        zre_ref[:, j * half:(j + 1) * half] = z[:, :half]
        zim_ref[:, j * half:(j + 1) * half] = z[:, half:]


def _ssm_scan_body(zre_ref, zim_ref, lre_ref, lim_ref, xre_ref, xim_ref):
    lr, li = lre_ref[...], lim_ref[...]

    def step(c, carry):
        xr, xi = carry
        xre_ref[pl.ds(c, 1), :] = xr
        xim_ref[pl.ds(c, 1), :] = xi
        return lr * xr - li * xi + zre_ref[pl.ds(c, 1), :], lr * xi + li * xr + zim_ref[pl.ds(c, 1), :]

    zero = jnp.zeros(lr.shape, F32)
    lax.fori_loop(0, zre_ref.shape[0], step, (zero, zero))


def _ssm_out_body(u_ref, xre_ref, xim_ref, w2_ref, d_ref, wglu_ref, wup_ref, o_ref, u_sc, y_sc):
    half = LANE_GROUPS * SSM_STATE
    rows = xre_ref.shape[0]
    _split_lane_tiles(u_ref, u_sc)
    for j in range(N_LANE_TILES):
        lhs = jnp.concatenate(
            [_chunk_inputs(u_sc, j, rows), xre_ref[:, j * half:(j + 1) * half], xim_ref[:, j * half:(j + 1) * half]],
            axis=1).astype(BF16)
        yj = jnp.dot(lhs, w2_ref[j], preferred_element_type=F32)
        for t in range(SSM_CHUNK):
            y_sc[j, pl.ds(t, rows, stride=SSM_CHUNK), :] = yj[:, t * 128:(t + 1) * 128]
    y = jnp.concatenate([y_sc[j] for j in range(N_LANE_TILES)], axis=1) + d_ref[...] * u_ref[...]
    gl = jnp.dot(jax.nn.gelu(y).astype(BF16), wglu_ref[...], preferred_element_type=F32)
    v = gl[:, :SSM_WIDTH] * jax.nn.sigmoid(gl[:, SSM_WIDTH:])
    o_ref[...] = jnp.dot(v.astype(BF16), wup_ref[...], preferred_element_type=F32).astype(o_ref.dtype)


def _stage_ssm(u, bsz, seq, a_re, a_im, log_dt, b_re, b_im, c_re, c_im, d_skip, w_glu, w_up):
    n = bsz * seq
    L = SSM_CHUNK
    nck = seq // L
    gp = SSM_GROUPS * SSM_STATE
    m2, w2, l8re, l8im = _ssm_weights(a_re, a_im, log_dt, b_re, b_im, c_re, c_im)
    rs = min(nck, 512)
    zre, zim = pl.pallas_call(
        _ssm_state_body, grid=(bsz, nck // rs),
        in_specs=[pl.BlockSpec((rs * L, SSM_WIDTH), lambda b, i: (b * (nck // rs) + i, 0)),
                  _const_spec(m2.shape)],
        out_specs=[pl.BlockSpec((rs, gp), lambda b, i: (i, b))] * 2,
        out_shape=[jax.ShapeDtypeStruct((nck, bsz * gp), F32)] * 2,
        scratch_shapes=[pltpu.VMEM((N_LANE_TILES, rs * L, 128), F32)],
        compiler_params=_cparams(("parallel", "parallel")), name="ssm_chunk_state")(u, m2)
    cw = 1024
    ncol = bsz * gp // cw
    lre = jnp.tile(l8re, bsz).reshape(1, bsz * gp)
    lim = jnp.tile(l8im, bsz).reshape(1, bsz * gp)
    col = pl.BlockSpec((nck, cw), lambda i: (0, i))
    lam = pl.BlockSpec((1, cw), lambda i: (0, i))
    xre, xim = pl.pallas_call(
        _ssm_scan_body, grid=(ncol,), in_specs=[col, col, lam, lam], out_specs=[col, col],
        out_shape=[jax.ShapeDtypeStruct((nck, bsz * gp), F32)] * 2,
        compiler_params=_cparams(("parallel",)), name="ssm_carry_scan")(zre, zim, lre, lim)
    ro = min(nck, 128)
    d8 = d_skip.reshape(1, SSM_WIDTH).astype(F32)
    return pl.pallas_call(
        _ssm_out_body, grid=(bsz, nck // ro),
        in_specs=[pl.BlockSpec((ro * L, SSM_WIDTH), lambda b, i: (b * (nck // ro) + i, 0)),
                  pl.BlockSpec((ro, gp), lambda b, i: (i, b)),
                  pl.BlockSpec((ro, gp), lambda b, i: (i, b)),
                  _const_spec(w2.shape), _const_spec((1, SSM_WIDTH)),
                  _const_spec(w_glu.shape), _const_spec(w_up.shape)],
        out_specs=pl.BlockSpec((ro * L, D_MODEL), lambda b, i: (b * (nck // ro) + i, 0)),
        out_shape=jax.ShapeDtypeStruct((n, D_MODEL), BF16),
        scratch_shapes=[pltpu.VMEM((N_LANE_TILES, ro * L, 128), F32)] * 2,
        compiler_params=_cparams(("parallel", "parallel")), name="ssm_out")(
            u, xre, xim, w2, d8, w_glu.astype(BF16), w_up.astype(BF16))


def _compress_body(kv_ref, pea_ref, peb_ref, w1a_ref, w1b_ref, w2_ref, o_ref, kv_sc):
    nrow = o_ref.shape[0]
    _split_lane_tiles(kv_ref, kv_sc)
    f = jnp.concatenate([kv_sc[j, pl.ds(l, nrow, stride=CMP_STRIDE), :]
                         for l in range(CMP_STRIDE) for j in range(kv_sc.shape[0])], axis=1)
    a = jnp.dot((f + pea_ref[...]).astype(BF16), w1a_ref[...], preferred_element_type=F32)
    b = jnp.dot((f + peb_ref[...]).astype(BF16), w1b_ref[...], preferred_element_type=F32)
    pre = a + pltpu.roll(b, b.shape[0] - 1, 0)
    hid = jax.nn.gelu(pre)
    out = jnp.dot(hid.astype(BF16), w2_ref[...], preferred_element_type=F32)
    lane = lax.broadcasted_iota(jnp.int32, out.shape, 1) & 255
    c_end = lax.broadcasted_iota(jnp.int32, out.shape, 0) * CMP_STRIDE + (CMP_BLOCK - 1)
    feat = jnp.where(lane == POS_HI, c_end >> 6, 0) + jnp.where(lane == POS_LO, c_end & 63, 0)
    o_ref[...] = (out + feat.astype(F32)).astype(BF16)


def _stage_compress(kvc, bsz, seq, pe_k, w1_k, w2_k, pe_v, w1_v, w2_v):
    hl = CMP_STRIDE
    nrow = seq // hl
    G, dk, hid = NSA_KV_GROUPS, NSA_HEAD_DIM, CMP_HIDDEN
    eye = jnp.eye(2 * G, dtype=F32)

    def big_w1(lo):
        wk = w1_k.reshape(CMP_BLOCK, dk, hid)[lo:lo + hl]
        wv = w1_v.reshape(CMP_BLOCK, dk, hid)[lo:lo + hl]
        w = jnp.stack([wk, wk, wv, wv], axis=1)
        return jnp.einsum('lcdj,ce->lcdej', w, eye).reshape(hl * 2 * G * dk, 2 * G * hid)

    def big_pe(lo):
        pk, pv = pe_k[lo:lo + hl], pe_v[lo:lo + hl]
        return jnp.stack([pk, pk, pv, pv], axis=1).reshape(1, hl * 2 * G * dk)

    w2 = jnp.stack([w2_k, w2_k, w2_v, w2_v], axis=0)
    w2 = jnp.concatenate([w2, jnp.zeros((2 * G, hid, 128 - dk), w2.dtype)], axis=-1)
    slot = eye[np.array([kv * G + g for g in range(G) for kv in range(2)])].T
    w2 = jnp.einsum('cjd,ce->cjed', w2, slot).reshape(2 * G * hid, 2 * G * 128)
    width = hl * 2 * G * dk
    return pl.pallas_call(
        _compress_body, grid=(bsz,),
        in_specs=[pl.BlockSpec((seq, 2 * G * dk), lambda b: (b, 0)),
                  _const_spec((1, width)), _const_spec((1, width)),
                  _const_spec((width, 2 * G * hid)), _const_spec((width, 2 * G * hid)),
                  _const_spec((2 * G * hid, 2 * G * 128))],
        out_specs=pl.BlockSpec((nrow, 2 * G * 128), lambda b: (b, 0)),
        out_shape=jax.ShapeDtypeStruct((bsz * nrow, 2 * G * 128), BF16),
        scratch_shapes=[pltpu.VMEM((2 * G * dk // 128, seq, 128), F32)],
        compiler_params=_cparams(("parallel",)), name="nsa_compress")(
            kvc, big_pe(0).astype(F32), big_pe(hl).astype(F32),
            big_w1(0).astype(BF16), big_w1(hl).astype(BF16), w2.astype(BF16))


MASK_BIG = 2.0 ** 100
SEL_TILE = 256


def _nsa_body(q_ref, kcv_ref, kv_ref, g_ref, ov_ref, rel_ref, relc_ref, o_ref, acc, s_buf, p_buf, a_buf):
    i = pl.program_id(2)
    qs = i * Q_BLOCK
    seq = kv_ref.shape[0]
    rows = NSA_REP * Q_BLOCK
    dk = NSA_HEAD_DIM
    qh = [q_ref[:, r * 128:(r + 1) * 128] for r in range(NSA_REP)]
    q_all = jnp.concatenate(qh, axis=0)

    kc = kcv_ref[:, :128]
    vc = kcv_ref[:, 128:]
    vis_c = relc_ref[...] >= -qs
    s = jnp.where(vis_c, _nt_dot(q_all, kc), NEG)
    p = jnp.exp(s - jnp.max(s, axis=-1, keepdims=True)) * jnp.where(vis_c, 1.0, 0.0)
    p = (p / jnp.maximum(jnp.sum(p, axis=-1, keepdims=True), 1e-30)).astype(BF16)
    o_cmp = jnp.dot(p, vc, preferred_element_type=F32)

    wlen = min(WINDOW + Q_BLOCK, seq)
    w0 = pl.multiple_of(jnp.maximum(qs + Q_BLOCK - wlen, 0), Q_BLOCK)
    rel_w = rel_ref[...]
    s = _nt_dot(q_all, kv_ref[pl.ds(w0, wlen), KW_OFF:KW_OFF + 128])
    s = jnp.where((rel_w >= w0 - qs) & (rel_w < w0 - qs + WINDOW), s, NEG)
    p_w = jnp.exp(s - jnp.max(s, axis=-1, keepdims=True)).astype(BF16)
    a_win = jnp.dot(p_w, kv_ref[pl.ds(w0, wlen), VW_OFF:VW_OFF + 128], preferred_element_type=F32)

    p_heads = jnp.concatenate([p[r * Q_BLOCK:(r + 1) * Q_BLOCK] for r in range(NSA_REP)], axis=1)
    imp = jnp.dot(p_heads, ov_ref[...], preferred_element_type=F32)
    coli = lax.broadcasted_iota(jnp.int32, (Q_BLOCK, 128), 1)
    t_q = qs + lax.broadcasted_iota(jnp.int32, (Q_BLOCK, 128), 0)
    cur = t_q >> 6
    forced = (coli == 0) | (coli == cur) | (coli == cur - 1)
    score = jnp.where(coli * SEL_BLOCK <= t_q, imp + jnp.where(forced, FORCE_BONUS, 0.0), NEG)
    nsl = seq // SEL_BLOCK
    st = score.T[:nsl]
    groups = [st[8 * v:8 * v + 8] for v in range(nsl // 8)]
    jsub = lax.broadcasted_iota(jnp.int32, (8, Q_BLOCK), 0)
    rank = [jnp.zeros((8, Q_BLOCK), F32) for _ in groups]
    for k in range(nsl):
        rk = st[k:k + 1, :]
        for v, sv in enumerate(groups):
            if v > k // 8:
                one = jnp.where(rk >= sv, 1.0, 0.0)
            elif v < k // 8:
                one = jnp.where(rk > sv, 1.0, 0.0)
            else:
                one = jnp.where(rk > sv, 1.0, jnp.where(jsub > k % 8, jnp.where(rk == sv, 1.0, 0.0), 0.0))
            rank[v] = rank[v] + one
    drop_t = jnp.where(jnp.concatenate(rank, axis=0) < float(min(SEL_TOPK, nsl)), 0.0, -MASK_BIG)
    if nsl < 128:
        drop_t = jnp.concatenate([drop_t, jnp.zeros((128 - nsl, Q_BLOCK), F32)], axis=0)
    drop = drop_t.T.astype(BF16)
    q_sel = jnp.concatenate([jnp.concatenate([qh[r], drop], axis=1) for r in range(NSA_REP)], axis=0)

    n_tiles = (qs + Q_BLOCK + SEL_TILE - 1) // SEL_TILE

    def scores(kt):
        k0 = pl.multiple_of(kt * SEL_TILE, SEL_TILE)
        return _nt_dot(q_sel, kv_ref[pl.ds(k0, SEL_TILE), KS_OFF:KS_OFF + 256])

    def values(kt):
        k0 = pl.multiple_of(kt * SEL_TILE, SEL_TILE)
        return kv_ref[pl.ds(k0, SEL_TILE), VS_OFF:VS_OFF + 128]

    def trip(diagonal, kt, m_old):
        if not diagonal:
            s_next = scores(kt + 1)
        acc[...] = a_buf[...] * acc[...] + jnp.dot(p_buf[...], values(jnp.maximum(kt - 1, 0)),
                                                   preferred_element_type=F32)
        s = s_buf[...]
        if diagonal:
            s = jnp.where(rel_ref[:, :SEL_TILE] >= kt * SEL_TILE - qs, s, NEG)
        m_new = jnp.maximum(m_old, jnp.max(s, axis=-1, keepdims=True))
        p_buf[...] = jnp.exp(s - m_new).astype(BF16)
        a_buf[...] = jnp.exp(m_old - m_new)
        if not diagonal:
            s_buf[...] = s_next
        return m_new

    acc[...] = jnp.zeros_like(acc)
    p_buf[...] = jnp.zeros_like(p_buf)
    a_buf[...] = jnp.ones_like(a_buf)
    s_buf[...] = scores(0)
    m_s = lax.fori_loop(0, n_tiles - 1, functools.partial(trip, False), jnp.full((rows, 1), NEG, F32))
    trip(True, n_tiles - 1, m_s)
    a_sel = a_buf[...] * acc[...] + jnp.dot(p_buf[...], values(n_tiles - 1), preferred_element_type=F32)

    gate = jax.nn.sigmoid(g_ref[...])
    outs = []
    for r in range(NSA_REP):
        rr = slice(r * Q_BLOCK, (r + 1) * Q_BLOCK)
        o_s = a_sel[rr, :dk] / a_sel[rr, ONE_LANE:ONE_LANE + 1]
        o_w = a_win[rr, :dk] / a_win[rr, ONE_LANE:ONE_LANE + 1]
        outs.append(gate[:, 3 * r:3 * r + 1] * o_cmp[rr, :dk] + gate[:, 3 * r + 1:3 * r + 2] * o_s
                    + gate[:, 3 * r + 2:3 * r + 3] * o_w)
    o_ref[...] = jnp.concatenate(outs, axis=1).astype(o_ref.dtype)


def _stage_attn(qp, kcv, kv, gates, bsz, seq):
    n = bsz * seq
    nq = seq // Q_BLOCK
    ncmp = seq // CMP_STRIDE
    nsl = seq // SEL_BLOCK
    assert seq % SEL_TILE == 0
    c_start = np.arange(ncmp) * CMP_STRIDE
    s_start = np.arange(nsl) * SEL_BLOCK
    ov = ((c_start[:, None] < s_start[None, :] + SEL_BLOCK) & (c_start[:, None] + CMP_BLOCK > s_start[None, :]))
    ov = np.pad(ov.astype(np.float32), ((0, 0), (0, 128 - nsl)))
    ov[ncmp - 1] = 0.0
    ov = jnp.asarray(np.tile(ov, (NSA_REP, 1)), BF16)
    width = NSA_REP * NSA_HEAD_DIM
    rows = NSA_REP * Q_BLOCK
    wlen = min(WINDOW + Q_BLOCK, seq)
    assert wlen >= SEL_TILE
    offs = jnp.arange(rows, dtype=jnp.int32)[:, None] % Q_BLOCK
    rel = offs - jnp.arange(wlen, dtype=jnp.int32)[None, :]
    relc = offs - (jnp.arange(ncmp, dtype=jnp.int32)[None, :] * CMP_STRIDE + (CMP_BLOCK - 1))
    return pl.pallas_call(
        _nsa_body, grid=(bsz, NSA_KV_GROUPS, nq),
        in_specs=[pl.BlockSpec((Q_BLOCK, NSA_REP * 128), lambda b, g, i: (b * nq + i, g)),
                  pl.BlockSpec((ncmp, 256), lambda b, g, i: (b, g)),
                  pl.BlockSpec((seq, KV_REC), lambda b, g, i: (b, g)),
                  pl.BlockSpec((Q_BLOCK, 128), lambda b, g, i: (b * nq + i, g)),
                  _const_spec(ov.shape), _const_spec(rel.shape), _const_spec(relc.shape)],
        out_specs=pl.BlockSpec((Q_BLOCK, width), lambda b, g, i: (b * nq + i, g)),
        out_shape=jax.ShapeDtypeStruct((n, NSA_KV_GROUPS * width), BF16),
        scratch_shapes=[pltpu.VMEM((rows, 128), F32), pltpu.VMEM((rows, SEL_TILE), F32),
                        pltpu.VMEM((rows, SEL_TILE), BF16), pltpu.VMEM((rows, 1), F32)],
        compiler_params=_cparams(("parallel", "parallel", "arbitrary")), name="nsa_attention")(
            qp, kcv, kv, gates, ov, rel, relc)


def _merge_body(x_ref, ya_ref, ob_ref, gm_ref, wnsa_ref, wout_ref, g_ref, b_ref, o_ref):
    yb = jnp.dot(ob_ref[...], wnsa_ref[...], preferred_element_type=F32)
    gm = gm_ref[...].astype(F32)
    mix_in = gm[:, :D_MODEL] * ya_ref[...].astype(F32) + gm[:, D_MODEL:] * yb
    mix = jnp.dot(mix_in.astype(BF16), wout_ref[...], preferred_element_type=F32)
    o_ref[...] = _layer_norm(DN_ALPHA * x_ref[...] + mix, g_ref[...], b_ref[...])


def _stage_merge(x2, ya, ob, gm, w_up_nsa, w_out, ln_g, ln_b):
    n = x2.shape[0]
    tm = 256
    row = lambda w: pl.BlockSpec((tm, w), lambda i: (i, 0))
    return pl.pallas_call(
        _merge_body, grid=(n // tm,),
        in_specs=[row(D_MODEL), row(D_MODEL), row(ob.shape[1]), row(2 * D_MODEL),
                  _const_spec(w_up_nsa.shape), _const_spec(w_out.shape),
                  _const_spec((1, D_MODEL)), _const_spec((1, D_MODEL))],
        out_specs=row(D_MODEL), out_shape=jax.ShapeDtypeStruct((n, D_MODEL), F32),
        compiler_params=_cparams(("parallel",)), name="merge_ln")(
            x2, ya, ob, gm, w_up_nsa.astype(BF16), w_out.astype(BF16),
            ln_g.reshape(1, -1).astype(F32), ln_b.reshape(1, -1).astype(F32))


BIG_NEG = -3.0e38


def _top_rows(s, k):
    rid = lax.broadcasted_iota(jnp.int32, s.shape, 0).astype(F32)
    rank = jnp.full(s.shape, float(k), F32)
    vals, idxs = [], []
    for r in range(k):
        m = jnp.max(s, axis=0, keepdims=True)
        idx = jnp.min(jnp.where(s == m, rid, float(s.shape[0])), axis=0, keepdims=True)
        hit = rid == idx
        rank = jnp.where(hit, float(r), rank)
        s = jnp.where(hit, BIG_NEG, s)
        vals.append(m)
        idxs.append(idx)
    return jnp.concatenate(vals, axis=0), jnp.concatenate(idxs, axis=0), rank


def _max_rounds(s, k, want_round=True):
    rnd = jnp.full(s.shape, float(k), F32) if want_round else None
    vals = []
    for r in range(k):
        m = jnp.max(s, axis=0, keepdims=True)
        eq = s == m
        if want_round:
            rnd = jnp.where(eq, float(r), rnd)
        s = jnp.where(eq, BIG_NEG, s)
        vals.append(m)
    return jnp.concatenate(vals, axis=0), rnd


def _pair_candidates(a, b):
    c8 = lax.broadcasted_iota(jnp.int32, (8, a.shape[1]), 0)
    parts = [a[0:1] + b[0:8], a[0:1] + b[8:16]]
    for r in range(1, 8):
        keep = PEER_TOPK // (r + 1)
        cand = a[r:r + 1] + b[0:8]
        parts.append(cand if keep >= 8 else jnp.where(c8 < keep, cand, BIG_NEG))
    parts.append(a[8:16] + b[0:1])
    return jnp.concatenate(parts, axis=0)


def _picks_per_rank(picked):
    rows = [jnp.sum(picked[0:16], axis=0, keepdims=True)]
    rows += [jnp.sum(picked[8 * (r + 1):8 * (r + 2)], axis=0, keepdims=True) for r in range(1, 8)]
    return jnp.concatenate(rows + [picked[72:80]], axis=0)


def _spread_by_rank(rank, per_rank):
    out = jnp.zeros(rank.shape, F32)
    for r in range(PEER_TOPK):
        out = out + jnp.where(rank == float(r), per_rank[r:r + 1], 0.0)
    return out


def _route_fast(s0, s1):
    k = PEER_TOPK
    a, rk0 = _max_rounds(s0, k)
    b, rk1 = _max_rounds(s1, k)
    cand = _pair_candidates(a, b)
    best, _ = _max_rounds(cand, k, want_round=False)
    hit = cand >= best[k - 1:k]
    picked = jnp.where(hit, 1.0, 0.0)
    z = jnp.sum(jnp.where(hit, jnp.exp(cand - best[0:1]), 0.0), axis=0, keepdims=True)
    nrank = _picks_per_rank(picked)
    count = lambda v: jnp.sum(v, axis=0, keepdims=True)
    bad = (jnp.abs(count(jnp.where(rk0 < float(k), 1.0, 0.0)) - k) + jnp.abs(count(jnp.where(rk1 < float(k), 1.0, 0.0)) - k)
           + jnp.abs(count(nrank) - k))
    return _spread_by_rank(rk0, nrank), jnp.exp(s0 - a[0:1]) / z, rk1, jnp.exp(s1 - b[0:1]), bad


def _route_exact(s0, s1):
    k = PEER_TOPK
    a, _, rk0 = _top_rows(s0, k)
    b, _, rk1 = _top_rows(s1, k)
    cand = jnp.concatenate([a[r:r + 1] + b for r in range(k)], axis=0)
    best, bidx, _ = _top_rows(cand, k)
    brank = jnp.floor(bidx * (1.0 / k))
    z = jnp.sum(jnp.exp(best - best[0:1]), axis=0, keepdims=True)
    nrank = jnp.concatenate([jnp.sum(jnp.where(brank == float(r), 1.0, 0.0), axis=0, keepdims=True)
                             for r in range(k)], axis=0)
    return _spread_by_rank(rk0, nrank), jnp.exp(s0 - a[0:1]) / z, rk1, jnp.exp(s1 - b[0:1])


def _gelu_tanh(x):
    c = math.sqrt(2.0 / math.pi)
    return x / (1.0 + jnp.exp(x * (x * x * (-2.0 * c * 0.044715) - 2.0 * c)))


def _peer_body(x_ref, wq_ref, sk_ref, u_ref, v_ref, g_ref, b_ref, o_ref,
               xt_sc, acc_sc, n_sc, e0_sc, rk1_sc, e1_sc, p_sc, st_sc):
    e = pl.program_id(1)
    nk = PEER_NKEYS

    @pl.when(e == 0)
    def _():
        xt = x_ref[...].T.astype(BF16)
        xt_sc[...] = xt
        acc_sc[...] = jnp.zeros_like(acc_sc)
        qt = jnp.dot(wq_ref[...], xt, preferred_element_type=F32).astype(BF16)
        st_sc[...] = jnp.dot(sk_ref[...], qt, preferred_element_type=F32)

        @pl.loop(0, PEER_HEADS)
        def _(h):
            s0 = st_sc[pl.ds(pl.multiple_of(2 * h * nk, nk), nk), :]
            s1 = st_sc[pl.ds(pl.multiple_of((2 * h + 1) * nk, nk), nk), :]

            def put(n_i, e0, rk1, e1):
                n_sc[h] = n_i
                e0_sc[h] = e0
                rk1_sc[h] = rk1.astype(BF16)
                e1_sc[h] = e1.astype(BF16)

            n_i, e0, rk1, e1, bad = _route_fast(s0, s1)
            put(n_i, e0, rk1, e1)

            @pl.when(jnp.max(bad) > 0.0)
            def _():
                put(*_route_exact(s0, s1))

    rows = u_ref.shape[0] // nk
    ht = jnp.dot(u_ref[...], xt_sc[...], preferred_element_type=F32)
    act = _gelu_tanh(ht).astype(BF16)
    for ii in range(rows):
        w = jnp.zeros((nk, xt_sc.shape[1]), BF16)
        for h in range(PEER_HEADS):
            n_row = n_sc[h, pl.ds(e * rows + ii, 1), :].astype(BF16)
            e0_row = e0_sc[h, pl.ds(e * rows + ii, 1), :].astype(BF16)
            w = w + jnp.where(rk1_sc[h] < n_row, e0_row * e1_sc[h], 0)
        p_sc[ii * nk:(ii + 1) * nk, :] = w * act[ii * nk:(ii + 1) * nk]
    acc_sc[...] += _tn_dot(v_ref[...], p_sc[...])

    @pl.when(e == pl.num_programs(1) - 1)
    def _():
        o_ref[...] = _layer_norm(DN_ALPHA * x_ref[...] + acc_sc[...].T, g_ref[...], b_ref[...])


def _stage_peer(x1, w_q, subkeys, u_tab, v_tab, ln_g, ln_b, tt=512, et=1024):
    n = x1.shape[0]
    tt = min(tt, n)
    H, nk, half = PEER_HEADS, PEER_NKEYS, subkeys.shape[-1]
    wq_t = w_q.T.astype(BF16)
    eye = jnp.eye(2 * H, dtype=F32)
    sk = subkeys.reshape(2 * H, nk, half)
    sk_t = jnp.einsum('ckd,ce->cked', sk, eye).reshape(2 * H * nk, 2 * H * half).astype(BF16)
    ne = u_tab.shape[0]
    return pl.pallas_call(
        _peer_body, grid=(n // tt, ne // et),
        in_specs=[pl.BlockSpec((tt, D_MODEL), lambda t, e: (t, 0)),
                  _const_spec(wq_t.shape), _const_spec(sk_t.shape),
                  pl.BlockSpec((et, D_MODEL), lambda t, e: (e, 0)),
                  pl.BlockSpec((et, D_MODEL), lambda t, e: (e, 0)),
                  _const_spec((1, D_MODEL)), _const_spec((1, D_MODEL))],
        out_specs=pl.BlockSpec((tt, D_MODEL), lambda t, e: (t, 0)),
        out_shape=jax.ShapeDtypeStruct((n, D_MODEL), F32),
        scratch_shapes=[pltpu.VMEM((D_MODEL, tt), BF16), pltpu.VMEM((D_MODEL, tt), F32),
                        pltpu.VMEM((H, nk, tt), F32), pltpu.VMEM((H, nk, tt), F32),
                        pltpu.VMEM((H, nk, tt), BF16), pltpu.VMEM((H, nk, tt), BF16),
                        pltpu.VMEM((et, tt), BF16), pltpu.VMEM((2 * H * nk, tt), F32)],
        compiler_params=_cparams(("parallel", "arbitrary")), name="peer_ffn")(
            x1, wq_t, sk_t, u_tab.astype(BF16), v_tab.astype(BF16),
            ln_g.reshape(1, -1).astype(F32), ln_b.reshape(1, -1).astype(F32))


def _layer(x, w_in, b_in, ssm_a_re, ssm_a_im, ssm_log_dt, ssm_b_re, ssm_b_im, ssm_c_re, ssm_c_im, ssm_d,
           w_glu, w_up_ssm, nsa_pe_k, nsa_w1_k, nsa_w2_k, nsa_pe_v, nsa_w1_v, nsa_w2_v, w_up_nsa,
           w_out, ln1_g, ln1_b, peer_w_q, peer_subkeys, peer_u, peer_v, ln2_g, ln2_b):
    bsz, seq, _ = x.shape
    x2 = x.reshape(bsz * seq, D_MODEL)
    u, qp, kvc, kv, gates, gm = _stage_proj(x2, w_in, b_in, seq)
    ya = _stage_ssm(u, bsz, seq, ssm_a_re, ssm_a_im, ssm_log_dt, ssm_b_re, ssm_b_im, ssm_c_re, ssm_c_im,
                    ssm_d, w_glu, w_up_ssm)
    kcv = _stage_compress(kvc, bsz, seq, nsa_pe_k, nsa_w1_k, nsa_w2_k, nsa_pe_v, nsa_w1_v, nsa_w2_v)
    ob = _stage_attn(qp, kcv, kv, gates, bsz, seq)
    x1 = _stage_merge(x2, ya, ob, gm, w_up_nsa, w_out, ln1_g, ln1_b)
    out = _stage_peer(x1, peer_w_q, peer_subkeys, peer_u, peer_v, ln2_g, ln2_b)
    return out.reshape(bsz, seq, D_MODEL)


def kernel(x, w_in, b_in, ssm_a_re, ssm_a_im, ssm_log_dt, ssm_b_re, ssm_b_im, ssm_c_re, ssm_c_im, ssm_d, w_glu,
           w_up_ssm, nsa_pe_k, nsa_w1_k, nsa_w2_k, nsa_pe_v, nsa_w1_v, nsa_w2_v, w_up_nsa, w_out, ln1_g, ln1_b,
           peer_w_q, peer_subkeys, peer_u, peer_v, ln2_g, ln2_b):
    params = (w_in, b_in, ssm_a_re, ssm_a_im, ssm_log_dt, ssm_b_re, ssm_b_im, ssm_c_re, ssm_c_im, ssm_d, w_glu,
              w_up_ssm, nsa_pe_k, nsa_w1_k, nsa_w2_k, nsa_pe_v, nsa_w1_v, nsa_w2_v, w_up_nsa, w_out, ln1_g, ln1_b,
              peer_w_q, peer_subkeys, peer_u, peer_v, ln2_g, ln2_b)
    for layer in range(w_in.shape[0]):
        x = _layer(x, *[p[layer] for p in params])
    return x
```

```python
import functools
import math

import jax
import jax.numpy as jnp
import numpy as np
from jax import lax
from jax.experimental import pallas as pl
from jax.experimental.pallas import tpu as pltpu

F32 = jnp.float32
BF16 = jnp.bfloat16

D_MODEL = 1024
SSM_WIDTH = 512
SSM_GROUP = 16
SSM_GROUPS = 32
SSM_STATE = 64
SSM_CHUNK = 8
LANE_GROUPS = 8
N_LANE_TILES = SSM_WIDTH // 128
NSA_HEADS = 8
NSA_KV_GROUPS = 2
NSA_REP = NSA_HEADS // NSA_KV_GROUPS
NSA_HEAD_DIM = 64
KV_WIDTH = NSA_KV_GROUPS * NSA_HEAD_DIM
CMP_BLOCK = 32
CMP_STRIDE = 16
CMP_HIDDEN = 128
SEL_BLOCK = 64
SEL_TOPK = 16
WINDOW = 512
Q_BLOCK = 256
FORCE_BONUS = 1.0e4
PEER_HEADS = 8
PEER_NKEYS = 128
PEER_EXPERTS = PEER_NKEYS * PEER_NKEYS
PEER_TOPK = 16
DN_ALPHA = 2.0 ** 0.25
LN_EPS = 1e-5
NEG = -1e30
VMEM_LIMIT = 56 * 1024 * 1024


def _cparams(sem):
    return pltpu.CompilerParams(dimension_semantics=sem, vmem_limit_bytes=VMEM_LIMIT)


def _const_spec(shape):
    nd = len(shape)
    return pl.BlockSpec(shape, lambda *_: (0,) * nd, pipeline_mode=pl.Buffered(1))


def _layer_norm(z, g, b):
    mu = jnp.mean(z, axis=-1, keepdims=True)
    var = jnp.mean(jnp.square(z - mu), axis=-1, keepdims=True)
    return (z - mu) * lax.rsqrt(var + LN_EPS) * g + b


def _nt_dot(a, b):
    return lax.dot_general(a, b, (((1,), (1,)), ((), ())), preferred_element_type=F32)


def _tn_dot(a, b):
    return lax.dot_general(a, b, (((0,), (0,)), ((), ())), preferred_element_type=F32)


KV_REC = 640
KS_OFF, VS_OFF, KW_OFF, VW_OFF = 0, 256, 384, 512
POS_HI, POS_LO, ONE_LANE, BLOCK_LANE0 = 64, 65, 64, 128
assert SEL_BLOCK == 64


def _position_features(pos, lane):
    hi, lo = pos >> 6, pos & 63
    f = jnp.where(lane == KS_OFF + POS_HI, hi, 0) + jnp.where(lane == KW_OFF + POS_HI, hi, 0)
    f = f + jnp.where(lane == KS_OFF + POS_LO, lo, 0) + jnp.where(lane == KW_OFF + POS_LO, lo, 0)
    f = f + jnp.where(lane == KS_OFF + BLOCK_LANE0 + hi, 1, 0)
    f = f + jnp.where(lane == VS_OFF + ONE_LANE, 1, 0) + jnp.where(lane == VW_OFF + ONE_LANE, 1, 0)
    return f.astype(F32)


def _proj_body(seq, x_ref, wu, bu, wq, bq, wc, bc, wk, bk, wg, bg, wm, bm,
               u_o, q_o, c_o, k_o, g_o, m_o):
    xb = x_ref[...].astype(BF16)
    tm = x_ref.shape[0]

    def lin(w, b):
        return jnp.dot(xb, w[...], preferred_element_type=F32) + b[...]

    u_o[...] = lin(wu, bu)
    q_o[...] = (lin(wq, bq) * (NSA_HEAD_DIM ** -0.5)).astype(BF16)
    c_o[...] = lin(wc, bc)
    pos0 = (pl.program_id(0) % (seq // tm)) * tm
    shape = (tm, NSA_KV_GROUPS * KV_REC)
    lane = lax.broadcasted_iota(jnp.int32, shape, 1)
    lane = jnp.where(lane >= KV_REC, lane - KV_REC, lane)
    feat = _position_features(pos0 + lax.broadcasted_iota(jnp.int32, shape, 0), lane)
    k_o[...] = (lin(wk, bk) + feat).astype(BF16)
    g_o[...] = lin(wg, bg)
    m_o[...] = jax.nn.sigmoid(lin(wm, bm)).astype(m_o.dtype)


def _stage_proj(x2, w_in, b_in, seq):
    n = x2.shape[0]
    tm = 256
    o0 = SSM_WIDTH
    o1 = o0 + NSA_HEADS * NSA_HEAD_DIM
    o2 = o1 + 6 * KV_WIDTH
    o3 = o2 + 3 * NSA_HEADS
    w_u, b_u = w_in[:, :o0], b_in[:o0]
    w_q, b_q = w_in[:, o0:o1], b_in[o0:o1]
    w_kv, b_kv = w_in[:, o1:o2], b_in[o1:o2]
    w_g, b_g = w_in[:, o2:o3], b_in[o2:o3]
    w_m, b_m = w_in[:, o3:], b_in[o3:]

    def pad_q(a, aux):
        a = a.reshape(a.shape[:-1] + (NSA_HEADS, NSA_HEAD_DIM))
        aux = jnp.broadcast_to(aux, a.shape[:-2] + aux.shape)
        return jnp.concatenate([a, aux], axis=-1).reshape(a.shape[:-2] + (NSA_HEADS * 128,))

    slope = 2.0 ** -(np.arange(NSA_HEADS, dtype=np.float32) + 1.0) * NSA_HEAD_DIM ** 0.5
    q_aux = np.zeros((NSA_HEADS, 128 - NSA_HEAD_DIM), np.float32)
    q_aux[:, POS_HI - NSA_HEAD_DIM] = slope * 64.0
    q_aux[:, POS_LO - NSA_HEAD_DIM] = slope

    def kv_records(a):
        parts = a.reshape(a.shape[:-1] + (6, NSA_KV_GROUPS, NSA_HEAD_DIM))
        z = lambda w: jnp.zeros(a.shape[:-1] + (w,), a.dtype)
        recs = []
        for g in range(NSA_KV_GROUPS):
            k_s, v_s, k_w, v_w = (parts[..., c, g, :] for c in (2, 3, 4, 5))
            recs += [k_s, z(256 - NSA_HEAD_DIM), v_s, z(128 - NSA_HEAD_DIM), k_w, z(128 - NSA_HEAD_DIM),
                     v_w, z(128 - NSA_HEAD_DIM)]
        return jnp.concatenate(recs, axis=-1)

    def pad_g(a):
        a = a.reshape(a.shape[:-1] + (NSA_KV_GROUPS, 3 * NSA_REP))
        a = jnp.concatenate([a, jnp.zeros(a.shape[:-1] + (128 - 3 * NSA_REP,), a.dtype)], axis=-1)
        return a.reshape(a.shape[:-2] + (NSA_KV_GROUPS * 128,))

    ws = [w_u, pad_q(w_q, jnp.zeros_like(q_aux)), w_kv[:, :2 * KV_WIDTH], kv_records(w_kv), pad_g(w_g), w_m]
    bs = [b_u, pad_q(b_q, jnp.asarray(q_aux)), b_kv[:2 * KV_WIDTH], kv_records(b_kv), pad_g(b_g), b_m]
    odt = [F32, BF16, F32, BF16, F32, BF16]
    args, in_specs = [x2], [pl.BlockSpec((tm, D_MODEL), lambda i: (i, 0))]
    for w, b in zip(ws, bs):
        args += [w.astype(BF16), b.reshape(1, -1).astype(F32)]
        in_specs += [_const_spec(w.shape), _const_spec((1, w.shape[1]))]
    out_shape = [jax.ShapeDtypeStruct((n, w.shape[1]), dt) for w, dt in zip(ws, odt)]
    out_specs = [pl.BlockSpec((tm, w.shape[1]), lambda i: (i, 0)) for w in ws]
    return pl.pallas_call(
        functools.partial(_proj_body, seq), grid=(n // tm,), in_specs=in_specs, out_specs=out_specs, out_shape=out_shape,
        compiler_params=_cparams(("parallel",)), name="in_proj")(*args)


def _ssm_weights(a_re, a_im, log_dt, b_re, b_im, c_re, c_im):
    L, G, P, H, A, J = SSM_CHUNK, SSM_GROUPS, SSM_STATE, SSM_GROUP, LANE_GROUPS, N_LANE_TILES
    lam = lax.complex(a_re.astype(F32), a_im.astype(F32))
    dt = jnp.exp(log_dt.astype(F32))[:, None]
    lam_bar = jnp.exp(lam * dt)
    b_bar = ((lam_bar - 1.0) / lam)[:, :, None] * lax.complex(b_re.astype(F32), b_im.astype(F32))
    c = lax.complex(c_re.astype(F32), c_im.astype(F32))
    k = jnp.arange(L + 1, dtype=F32)
    pw = jnp.exp((lam * dt)[None] * k[:, None, None])
    def spread(x, n_inner):
        src = np.arange(x.shape[2])
        dst = np.arange(x.shape[2] * A)
        sel = ((src[:, None] // n_inner == dst[None, :] // (A * n_inner))
               & (src[:, None] % n_inner == dst[None, :] % n_inner))
        return jnp.einsum('jrk,kc->jrc', x, jnp.asarray(sel, F32)), (dst % (A * n_inner)) // n_inner

    def group_mask(row_group, col_group):
        return jnp.asarray(row_group[:, None] == col_group[None, :], F32)

    in_rows = (np.arange(L * 128) % 128) // H
    st_rows = (np.arange(2 * A * P) % (A * P)) // P
    kern = jnp.real(jnp.einsum('ghp,kgp,gpj->kghj', c, pw[:L], b_bar))
    s_i, t_i = jnp.arange(L)[:, None], jnp.arange(L)[None, :]
    tau = jnp.clip(t_i - s_i, 0, L - 1)
    causal = (t_i >= s_i).astype(F32)
    k_st = (kern[tau] * causal[:, :, None, None, None]).reshape(L, L, J, A, H, H)
    k_st = k_st.transpose(2, 0, 3, 5, 1, 4).reshape(J, L * 128, L * H)
    w_intra, cg = spread(k_st, H)
    w_intra = w_intra * group_mask(in_rows, cg)
    q = (pw[:L][::-1][:, :, :, None] * b_bar[None]).reshape(L, J, A, P, H)
    q = jnp.stack([jnp.real(q), jnp.imag(q)], axis=0)
    q = q.transpose(2, 1, 3, 5, 0, 4).reshape(J, L * 128, 2 * P)
    m2, cg = spread(q, P)
    m2 = m2 * group_mask(in_rows, cg)
    cl = (c[None] * pw[1:][:, :, None, :]).reshape(L, J, A, H, P)
    cl = jnp.stack([jnp.real(cl), -jnp.imag(cl)], axis=0)
    cl = cl.transpose(2, 0, 3, 5, 1, 4).reshape(J, 2 * A * P, L * H)
    m1, cg = spread(cl, H)
    m1 = m1 * group_mask(st_rows, cg)
    w2 = jnp.concatenate([w_intra, m1], axis=1)
    lam8 = pw[L].reshape(G * P)
    return m2.astype(BF16), w2.astype(BF16), jnp.real(lam8), jnp.imag(lam8)


def _split_lane_tiles(src_ref, dst_sc):
    for j in range(dst_sc.shape[0]):
        dst_sc[j] = src_ref[:, j * 128:(j + 1) * 128]


def _chunk_inputs(u_sc, j, rows):
    return jnp.concatenate([u_sc[j, pl.ds(s, rows, stride=SSM_CHUNK), :] for s in range(SSM_CHUNK)], axis=1)


def _ssm_state_body(u_ref, m2_ref, zre_ref, zim_ref, u_sc):
    half = LANE_GROUPS * SSM_STATE
    _split_lane_tiles(u_ref, u_sc)
    for j in range(N_LANE_TILES):
        z = jnp.dot(_chunk_inputs(u_sc, j, zre_ref.shape[0]).astype(BF16), m2_ref[j], preferred_element_type=F32)
        zre_ref[:, j * half:(j + 1) * half] = z[:, :half]
        zim_ref[:, j * half:(j + 1) * half] = z[:, half:]


def _ssm_scan_body(zre_ref, zim_ref, lre_ref, lim_ref, xre_ref, xim_ref):
    lr, li = lre_ref[...], lim_ref[...]

    def step(c, carry):
        xr, xi = carry
        xre_ref[pl.ds(c, 1), :] = xr
        xim_ref[pl.ds(c, 1), :] = xi
        return lr * xr - li * xi + zre_ref[pl.ds(c, 1), :], lr * xi + li * xr + zim_ref[pl.ds(c, 1), :]

    zero = jnp.zeros(lr.shape, F32)
    lax.fori_loop(0, zre_ref.shape[0], step, (zero, zero))


def _ssm_out_body(u_ref, xre_ref, xim_ref, w2_ref, d_ref, wglu_ref, wup_ref, o_ref, u_sc, y_sc):
    half = LANE_GROUPS * SSM_STATE
    rows = xre_ref.shape[0]
    _split_lane_tiles(u_ref, u_sc)
    for j in range(N_LANE_TILES):
        lhs = jnp.concatenate(
            [_chunk_inputs(u_sc, j, rows), xre_ref[:, j * half:(j + 1) * half], xim_ref[:, j * half:(j + 1) * half]],
            axis=1).astype(BF16)
        yj = jnp.dot(lhs, w2_ref[j], preferred_element_type=F32)
        for t in range(SSM_CHUNK):
            y_sc[j, pl.ds(t, rows, stride=SSM_CHUNK), :] = yj[:, t * 128:(t + 1) * 128]
    y = jnp.concatenate([y_sc[j] for j in range(N_LANE_TILES)], axis=1) + d_ref[...] * u_ref[...]
    gl = jnp.dot(jax.nn.gelu(y).astype(BF16), wglu_ref[...], preferred_element_type=F32)
    v = gl[:, :SSM_WIDTH] * jax.nn.sigmoid(gl[:, SSM_WIDTH:])
    o_ref[...] = jnp.dot(v.astype(BF16), wup_ref[...], preferred_element_type=F32).astype(o_ref.dtype)


def _stage_ssm(u, bsz, seq, a_re, a_im, log_dt, b_re, b_im, c_re, c_im, d_skip, w_glu, w_up):
    n = bsz * seq
    L = SSM_CHUNK
    nck = seq // L
    gp = SSM_GROUPS * SSM_STATE
    m2, w2, l8re, l8im = _ssm_weights(a_re, a_im, log_dt, b_re, b_im, c_re, c_im)
    rs = min(nck, 512)
    zre, zim = pl.pallas_call(
        _ssm_state_body, grid=(bsz, nck // rs),
        in_specs=[pl.BlockSpec((rs * L, SSM_WIDTH), lambda b, i: (b * (nck // rs) + i, 0)),
                  _const_spec(m2.shape)],
        out_specs=[pl.BlockSpec((rs, gp), lambda b, i: (i, b))] * 2,
        out_shape=[jax.ShapeDtypeStruct((nck, bsz * gp), F32)] * 2,
        scratch_shapes=[pltpu.VMEM((N_LANE_TILES, rs * L, 128), F32)],
        compiler_params=_cparams(("parallel", "parallel")), name="ssm_chunk_state")(u, m2)
    cw = 1024
    ncol = bsz * gp // cw
    lre = jnp.tile(l8re, bsz).reshape(1, bsz * gp)
    lim = jnp.tile(l8im, bsz).reshape(1, bsz * gp)
    col = pl.BlockSpec((nck, cw), lambda i: (0, i))
    lam = pl.BlockSpec((1, cw), lambda i: (0, i))
    xre, xim = pl.pallas_call(
        _ssm_scan_body, grid=(ncol,), in_specs=[col, col, lam, lam], out_specs=[col, col],
        out_shape=[jax.ShapeDtypeStruct((nck, bsz * gp), F32)] * 2,
        compiler_params=_cparams(("parallel",)), name="ssm_carry_scan")(zre, zim, lre, lim)
    ro = min(nck, 128)
    d8 = d_skip.reshape(1, SSM_WIDTH).astype(F32)
    return pl.pallas_call(
        _ssm_out_body, grid=(bsz, nck // ro),
        in_specs=[pl.BlockSpec((ro * L, SSM_WIDTH), lambda b, i: (b * (nck // ro) + i, 0)),
                  pl.BlockSpec((ro, gp), lambda b, i: (i, b)),
                  pl.BlockSpec((ro, gp), lambda b, i: (i, b)),
                  _const_spec(w2.shape), _const_spec((1, SSM_WIDTH)),
                  _const_spec(w_glu.shape), _const_spec(w_up.shape)],
        out_specs=pl.BlockSpec((ro * L, D_MODEL), lambda b, i: (b * (nck // ro) + i, 0)),
        out_shape=jax.ShapeDtypeStruct((n, D_MODEL), BF16),
        scratch_shapes=[pltpu.VMEM((N_LANE_TILES, ro * L, 128), F32)] * 2,
        compiler_params=_cparams(("parallel", "parallel")), name="ssm_out")(
            u, xre, xim, w2, d8, w_glu.astype(BF16), w_up.astype(BF16))


def _compress_body(kv_ref, pea_ref, peb_ref, w1a_ref, w1b_ref, w2_ref, o_ref, kv_sc):
    nrow = o_ref.shape[0]
    _split_lane_tiles(kv_ref, kv_sc)
    f = jnp.concatenate([kv_sc[j, pl.ds(l, nrow, stride=CMP_STRIDE), :]
                         for l in range(CMP_STRIDE) for j in range(kv_sc.shape[0])], axis=1)
    a = jnp.dot((f + pea_ref[...]).astype(BF16), w1a_ref[...], preferred_element_type=F32)
    b = jnp.dot((f + peb_ref[...]).astype(BF16), w1b_ref[...], preferred_element_type=F32)
    pre = a + pltpu.roll(b, b.shape[0] - 1, 0)
    hid = jax.nn.gelu(pre)
    out = jnp.dot(hid.astype(BF16), w2_ref[...], preferred_element_type=F32)
    lane = lax.broadcasted_iota(jnp.int32, out.shape, 1) & 255
    c_end = lax.broadcasted_iota(jnp.int32, out.shape, 0) * CMP_STRIDE + (CMP_BLOCK - 1)
    feat = jnp.where(lane == POS_HI, c_end >> 6, 0) + jnp.where(lane == POS_LO, c_end & 63, 0)
    o_ref[...] = (out + feat.astype(F32)).astype(BF16)


def _stage_compress(kvc, bsz, seq, pe_k, w1_k, w2_k, pe_v, w1_v, w2_v):
    hl = CMP_STRIDE
    nrow = seq // hl
    G, dk, hid = NSA_KV_GROUPS, NSA_HEAD_DIM, CMP_HIDDEN
    eye = jnp.eye(2 * G, dtype=F32)

    def big_w1(lo):
        wk = w1_k.reshape(CMP_BLOCK, dk, hid)[lo:lo + hl]
        wv = w1_v.reshape(CMP_BLOCK, dk, hid)[lo:lo + hl]
        w = jnp.stack([wk, wk, wv, wv], axis=1)
        return jnp.einsum('lcdj,ce->lcdej', w, eye).reshape(hl * 2 * G * dk, 2 * G * hid)

    def big_pe(lo):
        pk, pv = pe_k[lo:lo + hl], pe_v[lo:lo + hl]
        return jnp.stack([pk, pk, pv, pv], axis=1).reshape(1, hl * 2 * G * dk)

    w2 = jnp.stack([w2_k, w2_k, w2_v, w2_v], axis=0)
    w2 = jnp.concatenate([w2, jnp.zeros((2 * G, hid, 128 - dk), w2.dtype)], axis=-1)
    slot = eye[np.array([kv * G + g for g in range(G) for kv in range(2)])].T
    w2 = jnp.einsum('cjd,ce->cjed', w2, slot).reshape(2 * G * hid, 2 * G * 128)
    width = hl * 2 * G * dk
    return pl.pallas_call(
        _compress_body, grid=(bsz,),
        in_specs=[pl.BlockSpec((seq, 2 * G * dk), lambda b: (b, 0)),
                  _const_spec((1, width)), _const_spec((1, width)),
                  _const_spec((width, 2 * G * hid)), _const_spec((width, 2 * G * hid)),
                  _const_spec((2 * G * hid, 2 * G * 128))],
        out_specs=pl.BlockSpec((nrow, 2 * G * 128), lambda b: (b, 0)),
        out_shape=jax.ShapeDtypeStruct((bsz * nrow, 2 * G * 128), BF16),
        scratch_shapes=[pltpu.VMEM((2 * G * dk // 128, seq, 128), F32)],
        compiler_params=_cparams(("parallel",)), name="nsa_compress")(
            kvc, big_pe(0).astype(F32), big_pe(hl).astype(F32),
            big_w1(0).astype(BF16), big_w1(hl).astype(BF16), w2.astype(BF16))


MASK_BIG = 2.0 ** 100
SEL_TILE = 512


def _nsa_body(q_ref, kcv_ref, kv_ref, g_ref, ov_ref, rel_ref, relc_ref, o_ref, acc, s_buf, p_buf, a_buf):
    i = pl.program_id(2)
    qs = i * Q_BLOCK
    seq = kv_ref.shape[0]
    rows = NSA_REP * Q_BLOCK
    dk = NSA_HEAD_DIM
    qh = [q_ref[:, r * 128:(r + 1) * 128] for r in range(NSA_REP)]
    q_all = jnp.concatenate(qh, axis=0)

    kc = kcv_ref[:, :128]
    vc = kcv_ref[:, 128:]
    vis_c = relc_ref[...] >= -qs
    s = jnp.where(vis_c, _nt_dot(q_all, kc), NEG)
    p = jnp.exp(s - jnp.max(s, axis=-1, keepdims=True)) * jnp.where(vis_c, 1.0, 0.0)
    p = (p / jnp.maximum(jnp.sum(p, axis=-1, keepdims=True), 1e-30)).astype(BF16)
    o_cmp = jnp.dot(p, vc, preferred_element_type=F32)

    wlen = min(WINDOW + Q_BLOCK, seq)
    w0 = pl.multiple_of(jnp.maximum(qs + Q_BLOCK - wlen, 0), Q_BLOCK)
    rel_w = rel_ref[...]
    s = _nt_dot(q_all, kv_ref[pl.ds(w0, wlen), KW_OFF:KW_OFF + 128])
    s = jnp.where((rel_w >= w0 - qs) & (rel_w < w0 - qs + WINDOW), s, NEG)
    p_w = jnp.exp(s - jnp.max(s, axis=-1, keepdims=True)).astype(BF16)
    a_win = jnp.dot(p_w, kv_ref[pl.ds(w0, wlen), VW_OFF:VW_OFF + 128], preferred_element_type=F32)

    p_heads = jnp.concatenate([p[r * Q_BLOCK:(r + 1) * Q_BLOCK] for r in range(NSA_REP)], axis=1)
    imp = jnp.dot(p_heads, ov_ref[...], preferred_element_type=F32)
    coli = lax.broadcasted_iota(jnp.int32, (Q_BLOCK, 128), 1)
    t_q = qs + lax.broadcasted_iota(jnp.int32, (Q_BLOCK, 128), 0)
    cur = t_q >> 6
    forced = (coli == 0) | (coli == cur) | (coli == cur - 1)
    score = jnp.where(coli * SEL_BLOCK <= t_q, imp + jnp.where(forced, FORCE_BONUS, 0.0), NEG)
    nsl = seq // SEL_BLOCK
    st = score.T[:nsl]
    groups = [st[8 * v:8 * v + 8] for v in range(nsl // 8)]
    jsub = lax.broadcasted_iota(jnp.int32, (8, Q_BLOCK), 0)
    rank = [jnp.zeros((8, Q_BLOCK), F32) for _ in groups]
    for k in range(nsl):
        rk = st[k:k + 1, :]
        for v, sv in enumerate(groups):
            if v > k // 8:
                one = jnp.where(rk >= sv, 1.0, 0.0)
            elif v < k // 8:
                one = jnp.where(rk > sv, 1.0, 0.0)
            else:
                one = jnp.where(rk > sv, 1.0, jnp.where(jsub > k % 8, jnp.where(rk == sv, 1.0, 0.0), 0.0))
            rank[v] = rank[v] + one
    drop_t = jnp.where(jnp.concatenate(rank, axis=0) < float(min(SEL_TOPK, nsl)), 0.0, -MASK_BIG)
    if nsl < 128:
        drop_t = jnp.concatenate([drop_t, jnp.zeros((128 - nsl, Q_BLOCK), F32)], axis=0)
    drop = drop_t.T.astype(BF16)
    q_sel = jnp.concatenate([jnp.concatenate([qh[r], drop], axis=1) for r in range(NSA_REP)], axis=0)

    n_tiles = (qs + Q_BLOCK + SEL_TILE - 1) // SEL_TILE

    def scores(kt):
        k0 = pl.multiple_of(kt * SEL_TILE, SEL_TILE)
        return _nt_dot(q_sel, kv_ref[pl.ds(k0, SEL_TILE), KS_OFF:KS_OFF + 256])

    def values(kt):
        k0 = pl.multiple_of(kt * SEL_TILE, SEL_TILE)
        return kv_ref[pl.ds(k0, SEL_TILE), VS_OFF:VS_OFF + 128]

    def trip(diagonal, kt, m_old):
        if not diagonal:
            s_next = scores(kt + 1)
        acc[...] = a_buf[...] * acc[...] + jnp.dot(p_buf[...], values(jnp.maximum(kt - 1, 0)),
                                                   preferred_element_type=F32)
        s = s_buf[...]
        if diagonal:
            s = jnp.where(rel_ref[:, :SEL_TILE] >= kt * SEL_TILE - qs, s, NEG)
        m_new = jnp.maximum(m_old, jnp.max(s, axis=-1, keepdims=True))
        p_buf[...] = jnp.exp(s - m_new).astype(BF16)
        a_buf[...] = jnp.exp(m_old - m_new)
        if not diagonal:
            s_buf[...] = s_next
        return m_new

    acc[...] = jnp.zeros_like(acc)
    p_buf[...] = jnp.zeros_like(p_buf)
    a_buf[...] = jnp.ones_like(a_buf)
    s_buf[...] = scores(0)
    m_s = lax.fori_loop(0, n_tiles - 1, functools.partial(trip, False), jnp.full((rows, 1), NEG, F32))
    trip(True, n_tiles - 1, m_s)
    a_sel = a_buf[...] * acc[...] + jnp.dot(p_buf[...], values(n_tiles - 1), preferred_element_type=F32)

    gate = jax.nn.sigmoid(g_ref[...])
    outs = []
    for r in range(NSA_REP):
        rr = slice(r * Q_BLOCK, (r + 1) * Q_BLOCK)
        o_s = a_sel[rr, :dk] / a_sel[rr, ONE_LANE:ONE_LANE + 1]
        o_w = a_win[rr, :dk] / a_win[rr, ONE_LANE:ONE_LANE + 1]
        outs.append(gate[:, 3 * r:3 * r + 1] * o_cmp[rr, :dk] + gate[:, 3 * r + 1:3 * r + 2] * o_s
                    + gate[:, 3 * r + 2:3 * r + 3] * o_w)
    o_ref[...] = jnp.concatenate(outs, axis=1).astype(o_ref.dtype)


def _stage_attn(qp, kcv, kv, gates, bsz, seq):
    n = bsz * seq
    nq = seq // Q_BLOCK
    ncmp = seq // CMP_STRIDE
    nsl = seq // SEL_BLOCK
    assert seq % SEL_TILE == 0
    c_start = np.arange(ncmp) * CMP_STRIDE
    s_start = np.arange(nsl) * SEL_BLOCK
    ov = ((c_start[:, None] < s_start[None, :] + SEL_BLOCK) & (c_start[:, None] + CMP_BLOCK > s_start[None, :]))
    ov = np.pad(ov.astype(np.float32), ((0, 0), (0, 128 - nsl)))
    ov[ncmp - 1] = 0.0
    ov = jnp.asarray(np.tile(ov, (NSA_REP, 1)), BF16)
    width = NSA_REP * NSA_HEAD_DIM
    rows = NSA_REP * Q_BLOCK
    wlen = min(WINDOW + Q_BLOCK, seq)
    assert wlen >= SEL_TILE
    offs = jnp.arange(rows, dtype=jnp.int32)[:, None] % Q_BLOCK
    rel = offs - jnp.arange(wlen, dtype=jnp.int32)[None, :]
    relc = offs - (jnp.arange(ncmp, dtype=jnp.int32)[None, :] * CMP_STRIDE + (CMP_BLOCK - 1))
    return pl.pallas_call(
        _nsa_body, grid=(bsz, NSA_KV_GROUPS, nq),
        in_specs=[pl.BlockSpec((Q_BLOCK, NSA_REP * 128), lambda b, g, i: (b * nq + i, g)),
                  pl.BlockSpec((ncmp, 256), lambda b, g, i: (b, g)),
                  pl.BlockSpec((seq, KV_REC), lambda b, g, i: (b, g)),
                  pl.BlockSpec((Q_BLOCK, 128), lambda b, g, i: (b * nq + i, g)),
                  _const_spec(ov.shape), _const_spec(rel.shape), _const_spec(relc.shape)],
        out_specs=pl.BlockSpec((Q_BLOCK, width), lambda b, g, i: (b * nq + i, g)),
        out_shape=jax.ShapeDtypeStruct((n, NSA_KV_GROUPS * width), BF16),
        scratch_shapes=[pltpu.VMEM((rows, 128), F32), pltpu.VMEM((rows, SEL_TILE), F32),
                        pltpu.VMEM((rows, SEL_TILE), BF16), pltpu.VMEM((rows, 1), F32)],
        compiler_params=_cparams(("parallel", "parallel", "arbitrary")), name="nsa_attention")(
            qp, kcv, kv, gates, ov, rel, relc)


def _merge_body(x_ref, ya_ref, ob_ref, gm_ref, wnsa_ref, wout_ref, g_ref, b_ref, o_ref):
    yb = jnp.dot(ob_ref[...], wnsa_ref[...], preferred_element_type=F32)
    gm = gm_ref[...].astype(F32)
    mix_in = gm[:, :D_MODEL] * ya_ref[...].astype(F32) + gm[:, D_MODEL:] * yb
    mix = jnp.dot(mix_in.astype(BF16), wout_ref[...], preferred_element_type=F32)
    o_ref[...] = _layer_norm(DN_ALPHA * x_ref[...] + mix, g_ref[...], b_ref[...])


def _stage_merge(x2, ya, ob, gm, w_up_nsa, w_out, ln_g, ln_b):
    n = x2.shape[0]
    tm = 256
    row = lambda w: pl.BlockSpec((tm, w), lambda i: (i, 0))
    return pl.pallas_call(
        _merge_body, grid=(n // tm,),
        in_specs=[row(D_MODEL), row(D_MODEL), row(ob.shape[1]), row(2 * D_MODEL),
                  _const_spec(w_up_nsa.shape), _const_spec(w_out.shape),
                  _const_spec((1, D_MODEL)), _const_spec((1, D_MODEL))],
        out_specs=row(D_MODEL), out_shape=jax.ShapeDtypeStruct((n, D_MODEL), F32),
        compiler_params=_cparams(("parallel",)), name="merge_ln")(
            x2, ya, ob, gm, w_up_nsa.astype(BF16), w_out.astype(BF16),
            ln_g.reshape(1, -1).astype(F32), ln_b.reshape(1, -1).astype(F32))


BIG_NEG = -3.0e38


def _top_rows(s, k):
    rid = lax.broadcasted_iota(jnp.int32, s.shape, 0).astype(F32)
    rank = jnp.full(s.shape, float(k), F32)
    vals, idxs = [], []
    for r in range(k):
        m = jnp.max(s, axis=0, keepdims=True)
        idx = jnp.min(jnp.where(s == m, rid, float(s.shape[0])), axis=0, keepdims=True)
        hit = rid == idx
        rank = jnp.where(hit, float(r), rank)
        s = jnp.where(hit, BIG_NEG, s)
        vals.append(m)
        idxs.append(idx)
    return jnp.concatenate(vals, axis=0), jnp.concatenate(idxs, axis=0), rank


def _max_rounds(s, k, want_round=True):
    rnd = jnp.full(s.shape, float(k), F32) if want_round else None
    vals = []
    for r in range(k):
        m = jnp.max(s, axis=0, keepdims=True)
        eq = s == m
        if want_round:
            rnd = jnp.where(eq, float(r), rnd)
        s = jnp.where(eq, BIG_NEG, s)
        vals.append(m)
    return jnp.concatenate(vals, axis=0), rnd


def _pair_candidates(a, b):
    c8 = lax.broadcasted_iota(jnp.int32, (8, a.shape[1]), 0)
    parts = [a[0:1] + b[0:8], a[0:1] + b[8:16]]
    for r in range(1, 8):
        keep = PEER_TOPK // (r + 1)
        cand = a[r:r + 1] + b[0:8]
        parts.append(cand if keep >= 8 else jnp.where(c8 < keep, cand, BIG_NEG))
    parts.append(a[8:16] + b[0:1])
    return jnp.concatenate(parts, axis=0)


def _picks_per_rank(picked):
    rows = [jnp.sum(picked[0:16], axis=0, keepdims=True)]
    rows += [jnp.sum(picked[8 * (r + 1):8 * (r + 2)], axis=0, keepdims=True) for r in range(1, 8)]
    return jnp.concatenate(rows + [picked[72:80]], axis=0)


def _spread_by_rank(rank, per_rank):
    out = jnp.zeros(rank.shape, F32)
    for r in range(PEER_TOPK):
        out = out + jnp.where(rank == float(r), per_rank[r:r + 1], 0.0)
    return out


def _sort16_network():
    pairs, p = [], 1
    while p < 16:
        k = p
        while k >= 1:
            for j in range(k % p, 16 - k, 2 * k):
                for i in range(min(k, 16 - j - k)):
                    if (i + j) // (2 * p) == (i + j + k) // (2 * p):
                        pairs.append((i + j, i + j + k))
            k //= 2
        p *= 2
    return pairs


_SORT16 = _sort16_network()


def _sorted_top16(s):
    def exchange(v, i, j):
        v[i], v[j] = jnp.maximum(v[i], v[j]), jnp.minimum(v[i], v[j])

    v = [s[8 * k:8 * k + 8] for k in range(16)]
    for i, j in _SORT16:
        exchange(v, i, j)
    for shift in (4, 2, 1):
        v = [jnp.maximum(v[k], pltpu.roll(v[15 - k], shift, 0)) for k in range(16)]
        for d in (8, 4, 2, 1):
            for i in range(16):
                if i & d == 0:
                    exchange(v, i, i + d)
    return jnp.concatenate([x[0:1] for x in v], axis=0)


def _count_rows(vals, s, strict):
    rows = [jnp.broadcast_to(vals[r:r + 1], (8, s.shape[1])) for r in range(vals.shape[0])]
    assert len(rows) == 16
    outs = []
    for c in range(0, s.shape[0], 8):
        sc = s[c:c + 8]
        test = (lambda row: row > sc) if strict else (lambda row: row <= sc)
        c1 = test(rows[7])
        c2 = test(jnp.where(c1, rows[11], rows[3]))
        c3 = test(jnp.where(c1, jnp.where(c2, rows[13], rows[9]), jnp.where(c2, rows[5], rows[1])))
        hi = jnp.where(c2, jnp.where(c3, rows[14], rows[12]), jnp.where(c3, rows[10], rows[8]))
        lo = jnp.where(c2, jnp.where(c3, rows[6], rows[4]), jnp.where(c3, rows[2], rows[0]))
        c4 = test(jnp.where(c1, hi, lo))
        acc = (jnp.where(c1, 8.0, 0.0) + jnp.where(c2, 4.0, 0.0) + jnp.where(c3, 2.0, 0.0) + jnp.where(c4, 1.0, 0.0)
               + jnp.where(test(rows[15]), 1.0, 0.0))
        outs.append(acc)
    return jnp.concatenate(outs, axis=0)


def _route_fast(s0, s1):
    k = PEER_TOPK
    count = lambda v: jnp.sum(v, axis=0, keepdims=True)
    a, b = _sorted_top16(s0), _sorted_top16(s1)
    cand = _pair_candidates(a, b)
    best, _ = _max_rounds(cand, k, want_round=False)
    hit = cand >= best[k - 1:k]
    z = jnp.sum(jnp.where(hit, jnp.exp(cand - best[0:1]), 0.0), axis=0, keepdims=True)
    nrank = _picks_per_rank(jnp.where(hit, 1.0, 0.0))
    earns = jnp.concatenate([jnp.min(jnp.where(nrank >= float(v), a, -BIG_NEG), axis=0, keepdims=True)
                             for v in range(1, k + 1)], axis=0)
    n_i = _count_rows(earns, s0, strict=False)
    ties = lambda t, s: (count(jnp.where(t[:-1] == t[1:], 1.0, 0.0))
                         + jnp.abs(count(jnp.where(s >= t[k - 1:k], 1.0, 0.0)) - k))
    bad = ties(a, s0) + ties(b, s1) + jnp.abs(count(nrank) - k)
    return n_i, jnp.exp(s0 - a[0:1]) / z, _count_rows(b, s1, strict=True), jnp.exp(s1 - b[0:1]), bad


def _route_exact(s0, s1):
    k = PEER_TOPK
    a, _, rk0 = _top_rows(s0, k)
    b, _, rk1 = _top_rows(s1, k)
    cand = jnp.concatenate([a[r:r + 1] + b for r in range(k)], axis=0)
    best, bidx, _ = _top_rows(cand, k)
    brank = jnp.floor(bidx * (1.0 / k))
    z = jnp.sum(jnp.exp(best - best[0:1]), axis=0, keepdims=True)
    nrank = jnp.concatenate([jnp.sum(jnp.where(brank == float(r), 1.0, 0.0), axis=0, keepdims=True)
                             for r in range(k)], axis=0)
    return _spread_by_rank(rk0, nrank), jnp.exp(s0 - a[0:1]) / z, rk1, jnp.exp(s1 - b[0:1])


def _gelu_tanh(x):
    c = math.sqrt(2.0 / math.pi)
    return x / (1.0 + jnp.exp(x * (x * x * (-2.0 * c * 0.044715) - 2.0 * c)))


def _peer_body(x_ref, wq_ref, sk_ref, u_ref, v_ref, g_ref, b_ref, o_ref,
               xt_sc, acc_sc, n_sc, e0_sc, rk1_sc, e1_sc, p_sc, st_sc):
    e = pl.program_id(1)
    nk = PEER_NKEYS

    @pl.when(e == 0)
    def _():
        xt = x_ref[...].T.astype(BF16)
        xt_sc[...] = xt
        acc_sc[...] = jnp.zeros_like(acc_sc)
        qt = jnp.dot(wq_ref[...], xt, preferred_element_type=F32).astype(BF16)
        st_sc[...] = jnp.dot(sk_ref[...], qt, preferred_element_type=F32)

        @pl.loop(0, PEER_HEADS)
        def _(h):
            s0 = st_sc[pl.ds(pl.multiple_of(2 * h * nk, nk), nk), :]
            s1 = st_sc[pl.ds(pl.multiple_of((2 * h + 1) * nk, nk), nk), :]

            def put(n_i, e0, rk1, e1):
                n_sc[h] = n_i
                e0_sc[h] = e0
                rk1_sc[h] = rk1.astype(BF16)
                e1_sc[h] = e1.astype(BF16)

            n_i, e0, rk1, e1, bad = _route_fast(s0, s1)
            put(n_i, e0, rk1, e1)

            @pl.when(jnp.max(bad) > 0.0)
            def _():
                put(*_route_exact(s0, s1))

    rows = u_ref.shape[0] // nk
    ht = jnp.dot(u_ref[...], xt_sc[...], preferred_element_type=F32)
    act = _gelu_tanh(ht).astype(BF16)
    for ii in range(rows):
        w = jnp.zeros((nk, xt_sc.shape[1]), BF16)
        for h in range(PEER_HEADS):
            n_row = n_sc[h, pl.ds(e * rows + ii, 1), :].astype(BF16)
            e0_row = e0_sc[h, pl.ds(e * rows + ii, 1), :].astype(BF16)
            w = w + jnp.where(rk1_sc[h] < n_row, e0_row * e1_sc[h], 0)
        p_sc[ii * nk:(ii + 1) * nk, :] = w * act[ii * nk:(ii + 1) * nk]
    acc_sc[...] += _tn_dot(v_ref[...], p_sc[...])

    @pl.when(e == pl.num_programs(1) - 1)
    def _():
        o_ref[...] = _layer_norm(DN_ALPHA * x_ref[...] + acc_sc[...].T, g_ref[...], b_ref[...])


def _stage_peer(x1, w_q, subkeys, u_tab, v_tab, ln_g, ln_b, tt=512, et=1024):
    n = x1.shape[0]
    tt = min(tt, n)
    H, nk, half = PEER_HEADS, PEER_NKEYS, subkeys.shape[-1]
    wq_t = w_q.T.astype(BF16)
    eye = jnp.eye(2 * H, dtype=F32)
    sk = subkeys.reshape(2 * H, nk, half)
    sk_t = jnp.einsum('ckd,ce->cked', sk, eye).reshape(2 * H * nk, 2 * H * half).astype(BF16)
    ne = u_tab.shape[0]
    return pl.pallas_call(
        _peer_body, grid=(n // tt, ne // et),
        in_specs=[pl.BlockSpec((tt, D_MODEL), lambda t, e: (t, 0)),
                  _const_spec(wq_t.shape), _const_spec(sk_t.shape),
                  pl.BlockSpec((et, D_MODEL), lambda t, e: (e, 0)),
                  pl.BlockSpec((et, D_MODEL), lambda t, e: (e, 0)),
                  _const_spec((1, D_MODEL)), _const_spec((1, D_MODEL))],
        out_specs=pl.BlockSpec((tt, D_MODEL), lambda t, e: (t, 0)),
        out_shape=jax.ShapeDtypeStruct((n, D_MODEL), F32),
        scratch_shapes=[pltpu.VMEM((D_MODEL, tt), BF16), pltpu.VMEM((D_MODEL, tt), F32),
                        pltpu.VMEM((H, nk, tt), F32), pltpu.VMEM((H, nk, tt), F32),
                        pltpu.VMEM((H, nk, tt), BF16), pltpu.VMEM((H, nk, tt), BF16),
                        pltpu.VMEM((et, tt), BF16), pltpu.VMEM((2 * H * nk, tt), F32)],
        compiler_params=_cparams(("parallel", "arbitrary")), name="peer_ffn")(
            x1, wq_t, sk_t, u_tab.astype(BF16), v_tab.astype(BF16),
            ln_g.reshape(1, -1).astype(F32), ln_b.reshape(1, -1).astype(F32))


def _layer(x, w_in, b_in, ssm_a_re, ssm_a_im, ssm_log_dt, ssm_b_re, ssm_b_im, ssm_c_re, ssm_c_im, ssm_d,
           w_glu, w_up_ssm, nsa_pe_k, nsa_w1_k, nsa_w2_k, nsa_pe_v, nsa_w1_v, nsa_w2_v, w_up_nsa,
           w_out, ln1_g, ln1_b, peer_w_q, peer_subkeys, peer_u, peer_v, ln2_g, ln2_b):
    bsz, seq, _ = x.shape
    x2 = x.reshape(bsz * seq, D_MODEL)
    u, qp, kvc, kv, gates, gm = _stage_proj(x2, w_in, b_in, seq)
    ya = _stage_ssm(u, bsz, seq, ssm_a_re, ssm_a_im, ssm_log_dt, ssm_b_re, ssm_b_im, ssm_c_re, ssm_c_im,
                    ssm_d, w_glu, w_up_ssm)
    kcv = _stage_compress(kvc, bsz, seq, nsa_pe_k, nsa_w1_k, nsa_w2_k, nsa_pe_v, nsa_w1_v, nsa_w2_v)
    ob = _stage_attn(qp, kcv, kv, gates, bsz, seq)
    x1 = _stage_merge(x2, ya, ob, gm, w_up_nsa, w_out, ln1_g, ln1_b)
    out = _stage_peer(x1, peer_w_q, peer_subkeys, peer_u, peer_v, ln2_g, ln2_b)
    return out.reshape(bsz, seq, D_MODEL)


def kernel(x, w_in, b_in, ssm_a_re, ssm_a_im, ssm_log_dt, ssm_b_re, ssm_b_im, ssm_c_re, ssm_c_im, ssm_d, w_glu,
           w_up_ssm, nsa_pe_k, nsa_w1_k, nsa_w2_k, nsa_pe_v, nsa_w1_v, nsa_w2_v, w_up_nsa, w_out, ln1_g, ln1_b,
           peer_w_q, peer_subkeys, peer_u, peer_v, ln2_g, ln2_b):
    params = (w_in, b_in, ssm_a_re, ssm_a_im, ssm_log_dt, ssm_b_re, ssm_b_im, ssm_c_re, ssm_c_im, ssm_d, w_glu,
              w_up_ssm, nsa_pe_k, nsa_w1_k, nsa_w2_k, nsa_pe_v, nsa_w1_v, nsa_w2_v, w_up_nsa, w_out, ln1_g, ln1_b,
              peer_w_q, peer_subkeys, peer_u, peer_v, ln2_g, ln2_b)
    for layer in range(w_in.shape[0]):
        x = _layer(x, *[p[layer] for p in params])
    return x
```

```python
import functools
import math

import jax
import jax.numpy as jnp
import numpy as np
from jax import lax
from jax.experimental import pallas as pl
from jax.experimental.pallas import tpu as pltpu

F32 = jnp.float32
BF16 = jnp.bfloat16

D_MODEL = 1024
SSM_WIDTH = 512
SSM_GROUP = 16
SSM_GROUPS = 32
SSM_STATE = 64
SSM_CHUNK = 8
LANE_GROUPS = 8
N_LANE_TILES = SSM_WIDTH // 128
NSA_HEADS = 8
NSA_KV_GROUPS = 2
NSA_REP = NSA_HEADS // NSA_KV_GROUPS
NSA_HEAD_DIM = 64
KV_WIDTH = NSA_KV_GROUPS * NSA_HEAD_DIM
CMP_BLOCK = 32
CMP_STRIDE = 16
CMP_HIDDEN = 128
SEL_BLOCK = 64
SEL_TOPK = 16
WINDOW = 512
Q_BLOCK = 256
FORCE_BONUS = 1.0e4
PEER_HEADS = 8
PEER_NKEYS = 128
PEER_EXPERTS = PEER_NKEYS * PEER_NKEYS
PEER_TOPK = 16
DN_ALPHA = 2.0 ** 0.25
LN_EPS = 1e-5
NEG = -1e30
VMEM_LIMIT = 56 * 1024 * 1024


def _cparams(sem):
    return pltpu.CompilerParams(dimension_semantics=sem, vmem_limit_bytes=VMEM_LIMIT)


def _const_spec(shape):
    nd = len(shape)
    return pl.BlockSpec(shape, lambda *_: (0,) * nd, pipeline_mode=pl.Buffered(1))


def _layer_norm(z, g, b):
    mu = jnp.mean(z, axis=-1, keepdims=True)
    var = jnp.mean(jnp.square(z - mu), axis=-1, keepdims=True)
    return (z - mu) * lax.rsqrt(var + LN_EPS) * g + b


def _nt_dot(a, b):
    return lax.dot_general(a, b, (((1,), (1,)), ((), ())), preferred_element_type=F32)


def _tn_dot(a, b):
    return lax.dot_general(a, b, (((0,), (0,)), ((), ())), preferred_element_type=F32)


KV_REC = 640
KS_OFF, VS_OFF, KW_OFF, VW_OFF = 0, 256, 384, 512
POS_HI, POS_LO, ONE_LANE, BLOCK_LANE0 = 64, 65, 64, 128
assert SEL_BLOCK == 64


def _position_features(pos, lane):
    hi, lo = pos >> 6, pos & 63
    f = jnp.where(lane == KS_OFF + POS_HI, hi, 0) + jnp.where(lane == KW_OFF + POS_HI, hi, 0)
    f = f + jnp.where(lane == KS_OFF + POS_LO, lo, 0) + jnp.where(lane == KW_OFF + POS_LO, lo, 0)
    f = f + jnp.where(lane == KS_OFF + BLOCK_LANE0 + hi, 1, 0)
    f = f + jnp.where(lane == VS_OFF + ONE_LANE, 1, 0) + jnp.where(lane == VW_OFF + ONE_LANE, 1, 0)
    return f.astype(F32)


def _proj_body(seq, x_ref, wu, bu, wq, bq, wc, bc, wk, bk, wg, bg, wm, bm,
               u_o, q_o, c_o, k_o, g_o, m_o):
    xb = x_ref[...].astype(BF16)
    tm = x_ref.shape[0]

    def lin(w, b):
        return jnp.dot(xb, w[...], preferred_element_type=F32) + b[...]

    u_o[...] = lin(wu, bu)
    q_o[...] = (lin(wq, bq) * (NSA_HEAD_DIM ** -0.5)).astype(BF16)
    c_o[...] = lin(wc, bc)
    pos0 = (pl.program_id(0) % (seq // tm)) * tm
    shape = (tm, NSA_KV_GROUPS * KV_REC)
    lane = lax.broadcasted_iota(jnp.int32, shape, 1)
    lane = jnp.where(lane >= KV_REC, lane - KV_REC, lane)
    feat = _position_features(pos0 + lax.broadcasted_iota(jnp.int32, shape, 0), lane)
    k_o[...] = (lin(wk, bk) + feat).astype(BF16)
    g_o[...] = lin(wg, bg)
    m_o[...] = jax.nn.sigmoid(lin(wm, bm)).astype(m_o.dtype)


def _stage_proj(x2, w_in, b_in, seq):
    n = x2.shape[0]
    tm = 256
    o0 = SSM_WIDTH
    o1 = o0 + NSA_HEADS * NSA_HEAD_DIM
    o2 = o1 + 6 * KV_WIDTH
    o3 = o2 + 3 * NSA_HEADS
    w_u, b_u = w_in[:, :o0], b_in[:o0]
    w_q, b_q = w_in[:, o0:o1], b_in[o0:o1]
    w_kv, b_kv = w_in[:, o1:o2], b_in[o1:o2]
    w_g, b_g = w_in[:, o2:o3], b_in[o2:o3]
    w_m, b_m = w_in[:, o3:], b_in[o3:]

    def pad_q(a, aux):
        a = a.reshape(a.shape[:-1] + (NSA_HEADS, NSA_HEAD_DIM))
        aux = jnp.broadcast_to(aux, a.shape[:-2] + aux.shape)
        return jnp.concatenate([a, aux], axis=-1).reshape(a.shape[:-2] + (NSA_HEADS * 128,))

    slope = 2.0 ** -(np.arange(NSA_HEADS, dtype=np.float32) + 1.0) * NSA_HEAD_DIM ** 0.5
    q_aux = np.zeros((NSA_HEADS, 128 - NSA_HEAD_DIM), np.float32)
    q_aux[:, POS_HI - NSA_HEAD_DIM] = slope * 64.0
    q_aux[:, POS_LO - NSA_HEAD_DIM] = slope

    def kv_records(a):
        parts = a.reshape(a.shape[:-1] + (6, NSA_KV_GROUPS, NSA_HEAD_DIM))
        z = lambda w: jnp.zeros(a.shape[:-1] + (w,), a.dtype)
        recs = []
        for g in range(NSA_KV_GROUPS):
            k_s, v_s, k_w, v_w = (parts[..., c, g, :] for c in (2, 3, 4, 5))
            recs += [k_s, z(256 - NSA_HEAD_DIM), v_s, z(128 - NSA_HEAD_DIM), k_w, z(128 - NSA_HEAD_DIM),
                     v_w, z(128 - NSA_HEAD_DIM)]
        return jnp.concatenate(recs, axis=-1)

    def pad_g(a):
        a = a.reshape(a.shape[:-1] + (NSA_KV_GROUPS, 3 * NSA_REP))
        a = jnp.concatenate([a, jnp.zeros(a.shape[:-1] + (128 - 3 * NSA_REP,), a.dtype)], axis=-1)
        return a.reshape(a.shape[:-2] + (NSA_KV_GROUPS * 128,))

    ws = [w_u, pad_q(w_q, jnp.zeros_like(q_aux)), w_kv[:, :2 * KV_WIDTH], kv_records(w_kv), pad_g(w_g), w_m]
    bs = [b_u, pad_q(b_q, jnp.asarray(q_aux)), b_kv[:2 * KV_WIDTH], kv_records(b_kv), pad_g(b_g), b_m]
    odt = [F32, BF16, F32, BF16, F32, BF16]
    args, in_specs = [x2], [pl.BlockSpec((tm, D_MODEL), lambda i: (i, 0))]
    for w, b in zip(ws, bs):
        args += [w.astype(BF16), b.reshape(1, -1).astype(F32)]
        in_specs += [_const_spec(w.shape), _const_spec((1, w.shape[1]))]
    out_shape = [jax.ShapeDtypeStruct((n, w.shape[1]), dt) for w, dt in zip(ws, odt)]
    out_specs = [pl.BlockSpec((tm, w.shape[1]), lambda i: (i, 0)) for w in ws]
    return pl.pallas_call(
        functools.partial(_proj_body, seq), grid=(n // tm,), in_specs=in_specs, out_specs=out_specs, out_shape=out_shape,
        compiler_params=_cparams(("parallel",)), name="in_proj")(*args)


def _ssm_weights(a_re, a_im, log_dt, b_re, b_im, c_re, c_im):
    L, G, P, H, A, J = SSM_CHUNK, SSM_GROUPS, SSM_STATE, SSM_GROUP, LANE_GROUPS, N_LANE_TILES
    lam = lax.complex(a_re.astype(F32), a_im.astype(F32))
    dt = jnp.exp(log_dt.astype(F32))[:, None]
    lam_bar = jnp.exp(lam * dt)
    b_bar = ((lam_bar - 1.0) / lam)[:, :, None] * lax.complex(b_re.astype(F32), b_im.astype(F32))
    c = lax.complex(c_re.astype(F32), c_im.astype(F32))
    k = jnp.arange(L + 1, dtype=F32)
    pw = jnp.exp((lam * dt)[None] * k[:, None, None])
    def spread(x, n_inner):
        src = np.arange(x.shape[2])
        dst = np.arange(x.shape[2] * A)
        sel = ((src[:, None] // n_inner == dst[None, :] // (A * n_inner))
               & (src[:, None] % n_inner == dst[None, :] % n_inner))
        return jnp.einsum('jrk,kc->jrc', x, jnp.asarray(sel, F32)), (dst % (A * n_inner)) // n_inner

    def group_mask(row_group, col_group):
        return jnp.asarray(row_group[:, None] == col_group[None, :], F32)

    in_rows = (np.arange(L * 128) % 128) // H
    st_rows = (np.arange(2 * A * P) % (A * P)) // P
    kern = jnp.real(jnp.einsum('ghp,kgp,gpj->kghj', c, pw[:L], b_bar))
    s_i, t_i = jnp.arange(L)[:, None], jnp.arange(L)[None, :]
    tau = jnp.clip(t_i - s_i, 0, L - 1)
    causal = (t_i >= s_i).astype(F32)
    k_st = (kern[tau] * causal[:, :, None, None, None]).reshape(L, L, J, A, H, H)
    k_st = k_st.transpose(2, 0, 3, 5, 1, 4).reshape(J, L * 128, L * H)
    w_intra, cg = spread(k_st, H)
    w_intra = w_intra * group_mask(in_rows, cg)
    q = (pw[:L][::-1][:, :, :, None] * b_bar[None]).reshape(L, J, A, P, H)
    q = jnp.stack([jnp.real(q), jnp.imag(q)], axis=0)
    q = q.transpose(2, 1, 3, 5, 0, 4).reshape(J, L * 128, 2 * P)
    m2, cg = spread(q, P)
    m2 = m2 * group_mask(in_rows, cg)
    cl = (c[None] * pw[1:][:, :, None, :]).reshape(L, J, A, H, P)
    cl = jnp.stack([jnp.real(cl), -jnp.imag(cl)], axis=0)
    cl = cl.transpose(2, 0, 3, 5, 1, 4).reshape(J, 2 * A * P, L * H)
    m1, cg = spread(cl, H)
    m1 = m1 * group_mask(st_rows, cg)
    w2 = jnp.concatenate([w_intra, m1], axis=1)
    lam8 = pw[L].reshape(G * P)
    return m2.astype(BF16), w2.astype(BF16), jnp.real(lam8), jnp.imag(lam8)


def _split_lane_tiles(src_ref, dst_sc):
    for j in range(dst_sc.shape[0]):
        dst_sc[j] = src_ref[:, j * 128:(j + 1) * 128]


def _chunk_inputs(u_sc, j, rows):
    return jnp.concatenate([u_sc[j, pl.ds(s, rows, stride=SSM_CHUNK), :] for s in range(SSM_CHUNK)], axis=1)


def _ssm_state_body(u_ref, m2_ref, zre_ref, zim_ref, u_sc):
    half = LANE_GROUPS * SSM_STATE
    _split_lane_tiles(u_ref, u_sc)
    for j in range(N_LANE_TILES):
        z = jnp.dot(_chunk_inputs(u_sc, j, zre_ref.shape[0]).astype(BF16), m2_ref[j], preferred_element_type=F32)
        zre_ref[:, j * half:(j + 1) * half] = z[:, :half]
        zim_ref[:, j * half:(j + 1) * half] = z[:, half:]


def _ssm_scan_body(zre_ref, zim_ref, lre_ref, lim_ref, xre_ref, xim_ref):
    lr, li = lre_ref[...], lim_ref[...]

    def step(c, carry):
        xr, xi = carry
        xre_ref[pl.ds(c, 1), :] = xr
        xim_ref[pl.ds(c, 1), :] = xi
        return lr * xr - li * xi + zre_ref[pl.ds(c, 1), :], lr * xi + li * xr + zim_ref[pl.ds(c, 1), :]

    zero = jnp.zeros(lr.shape, F32)
    lax.fori_loop(0, zre_ref.shape[0], step, (zero, zero))


def _ssm_out_body(u_ref, xre_ref, xim_ref, w2_ref, d_ref, wglu_ref, wup_ref, o_ref, u_sc, y_sc):
    half = LANE_GROUPS * SSM_STATE
    rows = xre_ref.shape[0]
    _split_lane_tiles(u_ref, u_sc)
    for j in range(N_LANE_TILES):
        lhs = jnp.concatenate(
            [_chunk_inputs(u_sc, j, rows), xre_ref[:, j * half:(j + 1) * half], xim_ref[:, j * half:(j + 1) * half]],
            axis=1).astype(BF16)
        yj = jnp.dot(lhs, w2_ref[j], preferred_element_type=F32)
        for t in range(SSM_CHUNK):
            y_sc[j, pl.ds(t, rows, stride=SSM_CHUNK), :] = yj[:, t * 128:(t + 1) * 128]
    y = jnp.concatenate([y_sc[j] for j in range(N_LANE_TILES)], axis=1) + d_ref[...] * u_ref[...]
    gl = jnp.dot(jax.nn.gelu(y).astype(BF16), wglu_ref[...], preferred_element_type=F32)
    v = gl[:, :SSM_WIDTH] * jax.nn.sigmoid(gl[:, SSM_WIDTH:])
    o_ref[...] = jnp.dot(v.astype(BF16), wup_ref[...], preferred_element_type=F32).astype(o_ref.dtype)


def _stage_ssm(u, bsz, seq, a_re, a_im, log_dt, b_re, b_im, c_re, c_im, d_skip, w_glu, w_up):
    n = bsz * seq
    L = SSM_CHUNK
    nck = seq // L
    gp = SSM_GROUPS * SSM_STATE
    m2, w2, l8re, l8im = _ssm_weights(a_re, a_im, log_dt, b_re, b_im, c_re, c_im)
    rs = min(nck, 512)
    zre, zim = pl.pallas_call(
        _ssm_state_body, grid=(bsz, nck // rs),
        in_specs=[pl.BlockSpec((rs * L, SSM_WIDTH), lambda b, i: (b * (nck // rs) + i, 0)),
                  _const_spec(m2.shape)],
        out_specs=[pl.BlockSpec((rs, gp), lambda b, i: (i, b))] * 2,
        out_shape=[jax.ShapeDtypeStruct((nck, bsz * gp), F32)] * 2,
        scratch_shapes=[pltpu.VMEM((N_LANE_TILES, rs * L, 128), F32)],
        compiler_params=_cparams(("parallel", "parallel")), name="ssm_chunk_state")(u, m2)
    cw = 1024
    ncol = bsz * gp // cw
    lre = jnp.tile(l8re, bsz).reshape(1, bsz * gp)
    lim = jnp.tile(l8im, bsz).reshape(1, bsz * gp)
    col = pl.BlockSpec((nck, cw), lambda i: (0, i))
    lam = pl.BlockSpec((1, cw), lambda i: (0, i))
    xre, xim = pl.pallas_call(
        _ssm_scan_body, grid=(ncol,), in_specs=[col, col, lam, lam], out_specs=[col, col],
        out_shape=[jax.ShapeDtypeStruct((nck, bsz * gp), F32)] * 2,
        compiler_params=_cparams(("parallel",)), name="ssm_carry_scan")(zre, zim, lre, lim)
    ro = min(nck, 128)
    d8 = d_skip.reshape(1, SSM_WIDTH).astype(F32)
    return pl.pallas_call(
        _ssm_out_body, grid=(bsz, nck // ro),
        in_specs=[pl.BlockSpec((ro * L, SSM_WIDTH), lambda b, i: (b * (nck // ro) + i, 0)),
                  pl.BlockSpec((ro, gp), lambda b, i: (i, b)),
                  pl.BlockSpec((ro, gp), lambda b, i: (i, b)),
                  _const_spec(w2.shape), _const_spec((1, SSM_WIDTH)),
                  _const_spec(w_glu.shape), _const_spec(w_up.shape)],
        out_specs=pl.BlockSpec((ro * L, D_MODEL), lambda b, i: (b * (nck // ro) + i, 0)),
        out_shape=jax.ShapeDtypeStruct((n, D_MODEL), BF16),
        scratch_shapes=[pltpu.VMEM((N_LANE_TILES, ro * L, 128), F32)] * 2,
        compiler_params=_cparams(("parallel", "parallel")), name="ssm_out")(
            u, xre, xim, w2, d8, w_glu.astype(BF16), w_up.astype(BF16))


def _compress_body(kv_ref, pea_ref, peb_ref, w1a_ref, w1b_ref, w2_ref, o_ref, kv_sc):
    nrow = o_ref.shape[0]
    _split_lane_tiles(kv_ref, kv_sc)
    f = jnp.concatenate([kv_sc[j, pl.ds(l, nrow, stride=CMP_STRIDE), :]
                         for l in range(CMP_STRIDE) for j in range(kv_sc.shape[0])], axis=1)
    a = jnp.dot((f + pea_ref[...]).astype(BF16), w1a_ref[...], preferred_element_type=F32)
    b = jnp.dot((f + peb_ref[...]).astype(BF16), w1b_ref[...], preferred_element_type=F32)
    pre = a + pltpu.roll(b, b.shape[0] - 1, 0)
    hid = jax.nn.gelu(pre)
    out = jnp.dot(hid.astype(BF16), w2_ref[...], preferred_element_type=F32)
    lane = lax.broadcasted_iota(jnp.int32, out.shape, 1) & 255
    c_end = lax.broadcasted_iota(jnp.int32, out.shape, 0) * CMP_STRIDE + (CMP_BLOCK - 1)
    feat = jnp.where(lane == POS_HI, c_end >> 6, 0) + jnp.where(lane == POS_LO, c_end & 63, 0)
    o_ref[...] = (out + feat.astype(F32)).astype(BF16)


def _stage_compress(kvc, bsz, seq, pe_k, w1_k, w2_k, pe_v, w1_v, w2_v):
    hl = CMP_STRIDE
    nrow = seq // hl
    G, dk, hid = NSA_KV_GROUPS, NSA_HEAD_DIM, CMP_HIDDEN
    eye = jnp.eye(2 * G, dtype=F32)

    def big_w1(lo):
        wk = w1_k.reshape(CMP_BLOCK, dk, hid)[lo:lo + hl]
        wv = w1_v.reshape(CMP_BLOCK, dk, hid)[lo:lo + hl]
        w = jnp.stack([wk, wk, wv, wv], axis=1)
        return jnp.einsum('lcdj,ce->lcdej', w, eye).reshape(hl * 2 * G * dk, 2 * G * hid)

    def big_pe(lo):
        pk, pv = pe_k[lo:lo + hl], pe_v[lo:lo + hl]
        return jnp.stack([pk, pk, pv, pv], axis=1).reshape(1, hl * 2 * G * dk)

    w2 = jnp.stack([w2_k, w2_k, w2_v, w2_v], axis=0)
    w2 = jnp.concatenate([w2, jnp.zeros((2 * G, hid, 128 - dk), w2.dtype)], axis=-1)
    slot = eye[np.array([kv * G + g for g in range(G) for kv in range(2)])].T
    w2 = jnp.einsum('cjd,ce->cjed', w2, slot).reshape(2 * G * hid, 2 * G * 128)
    width = hl * 2 * G * dk
    return pl.pallas_call(
        _compress_body, grid=(bsz,),
        in_specs=[pl.BlockSpec((seq, 2 * G * dk), lambda b: (b, 0)),
                  _const_spec((1, width)), _const_spec((1, width)),
                  _const_spec((width, 2 * G * hid)), _const_spec((width, 2 * G * hid)),
                  _const_spec((2 * G * hid, 2 * G * 128))],
        out_specs=pl.BlockSpec((nrow, 2 * G * 128), lambda b: (b, 0)),
        out_shape=jax.ShapeDtypeStruct((bsz * nrow, 2 * G * 128), BF16),
        scratch_shapes=[pltpu.VMEM((2 * G * dk // 128, seq, 128), F32)],
        compiler_params=_cparams(("parallel",)), name="nsa_compress")(
            kvc, big_pe(0).astype(F32), big_pe(hl).astype(F32),
            big_w1(0).astype(BF16), big_w1(hl).astype(BF16), w2.astype(BF16))


MASK_BIG = 2.0 ** 100
SEL_TILE = 512


def _nsa_body(q_ref, kcv_ref, kv_ref, g_ref, ov_ref, rel_ref, relc_ref, o_ref, acc, s_buf, p_buf, a_buf):
    i = pl.program_id(2)
    qs = i * Q_BLOCK
    seq = kv_ref.shape[0]
    rows = NSA_REP * Q_BLOCK
    dk = NSA_HEAD_DIM
    qh = [q_ref[:, r * 128:(r + 1) * 128] for r in range(NSA_REP)]
    q_all = jnp.concatenate(qh, axis=0)

    kc = kcv_ref[:, :128]
    vc = kcv_ref[:, 128:]
    vis_c = relc_ref[...] >= -qs
    s = jnp.where(vis_c, _nt_dot(q_all, kc), NEG)
    p = jnp.exp(s - jnp.max(s, axis=-1, keepdims=True)) * jnp.where(vis_c, 1.0, 0.0)
    p = (p / jnp.maximum(jnp.sum(p, axis=-1, keepdims=True), 1e-30)).astype(BF16)
    o_cmp = jnp.dot(p, vc, preferred_element_type=F32)

    wlen = min(WINDOW + Q_BLOCK, seq)
    w0 = pl.multiple_of(jnp.maximum(qs + Q_BLOCK - wlen, 0), Q_BLOCK)
    rel_w = rel_ref[...]
    s = _nt_dot(q_all, kv_ref[pl.ds(w0, wlen), KW_OFF:KW_OFF + 128])
    s = jnp.where((rel_w >= w0 - qs) & (rel_w < w0 - qs + WINDOW), s, NEG)
    p_w = jnp.exp(s - jnp.max(s, axis=-1, keepdims=True)).astype(BF16)
    a_win = jnp.dot(p_w, kv_ref[pl.ds(w0, wlen), VW_OFF:VW_OFF + 128], preferred_element_type=F32)

    p_heads = jnp.concatenate([p[r * Q_BLOCK:(r + 1) * Q_BLOCK] for r in range(NSA_REP)], axis=1)
    imp = jnp.dot(p_heads, ov_ref[...], preferred_element_type=F32)
    coli = lax.broadcasted_iota(jnp.int32, (Q_BLOCK, 128), 1)
    t_q = qs + lax.broadcasted_iota(jnp.int32, (Q_BLOCK, 128), 0)
    cur = t_q >> 6
    forced = (coli == 0) | (coli == cur) | (coli == cur - 1)
    score = jnp.where(coli * SEL_BLOCK <= t_q, imp + jnp.where(forced, FORCE_BONUS, 0.0), NEG)
    nsl = seq // SEL_BLOCK
    st = score.T[:nsl]
    groups = [st[8 * v:8 * v + 8] for v in range(nsl // 8)]
    jsub = lax.broadcasted_iota(jnp.int32, (8, Q_BLOCK), 0)
    rank = [jnp.zeros((8, Q_BLOCK), F32) for _ in groups]
    for k in range(nsl):
        rk = st[k:k + 1, :]
        for v, sv in enumerate(groups):
            if v > k // 8:
                one = jnp.where(rk >= sv, 1.0, 0.0)
            elif v < k // 8:
                one = jnp.where(rk > sv, 1.0, 0.0)
            else:
                one = jnp.where(rk > sv, 1.0, jnp.where(jsub > k % 8, jnp.where(rk == sv, 1.0, 0.0), 0.0))
            rank[v] = rank[v] + one
    drop_t = jnp.where(jnp.concatenate(rank, axis=0) < float(min(SEL_TOPK, nsl)), 0.0, -MASK_BIG)
    if nsl < 128:
        drop_t = jnp.concatenate([drop_t, jnp.zeros((128 - nsl, Q_BLOCK), F32)], axis=0)
    drop = drop_t.T.astype(BF16)
    q_sel = jnp.concatenate([jnp.concatenate([qh[r], drop], axis=1) for r in range(NSA_REP)], axis=0)

    n_tiles = (qs + Q_BLOCK + SEL_TILE - 1) // SEL_TILE

    def scores(kt):
        k0 = pl.multiple_of(kt * SEL_TILE, SEL_TILE)
        return _nt_dot(q_sel, kv_ref[pl.ds(k0, SEL_TILE), KS_OFF:KS_OFF + 256])

    def values(kt):
        k0 = pl.multiple_of(kt * SEL_TILE, SEL_TILE)
        return kv_ref[pl.ds(k0, SEL_TILE), VS_OFF:VS_OFF + 128]

    def trip(diagonal, kt, m_old):
        if not diagonal:
            s_next = scores(kt + 1)
        acc[...] = a_buf[...] * acc[...] + jnp.dot(p_buf[...], values(jnp.maximum(kt - 1, 0)),
                                                   preferred_element_type=F32)
        s = s_buf[...]
        if diagonal:
            s = jnp.where(rel_ref[:, :SEL_TILE] >= kt * SEL_TILE - qs, s, NEG)
        m_new = jnp.maximum(m_old, jnp.max(s, axis=-1, keepdims=True))
        p_buf[...] = jnp.exp(s - m_new).astype(BF16)
        a_buf[...] = jnp.exp(m_old - m_new)
        if not diagonal:
            s_buf[...] = s_next
        return m_new

    acc[...] = jnp.zeros_like(acc)
    p_buf[...] = jnp.zeros_like(p_buf)
    a_buf[...] = jnp.ones_like(a_buf)
    s_buf[...] = scores(0)
    m_s = lax.fori_loop(0, n_tiles - 1, functools.partial(trip, False), jnp.full((rows, 1), NEG, F32))
    trip(True, n_tiles - 1, m_s)
    a_sel = a_buf[...] * acc[...] + jnp.dot(p_buf[...], values(n_tiles - 1), preferred_element_type=F32)

    gate = jax.nn.sigmoid(g_ref[...])
    outs = []
    for r in range(NSA_REP):
        rr = slice(r * Q_BLOCK, (r + 1) * Q_BLOCK)
        o_s = a_sel[rr, :dk] / a_sel[rr, ONE_LANE:ONE_LANE + 1]
        o_w = a_win[rr, :dk] / a_win[rr, ONE_LANE:ONE_LANE + 1]
        outs.append(gate[:, 3 * r:3 * r + 1] * o_cmp[rr, :dk] + gate[:, 3 * r + 1:3 * r + 2] * o_s
                    + gate[:, 3 * r + 2:3 * r + 3] * o_w)
    o_ref[...] = jnp.concatenate(outs, axis=1).astype(o_ref.dtype)


def _stage_attn(qp, kcv, kv, gates, bsz, seq):
    n = bsz * seq
    nq = seq // Q_BLOCK
    ncmp = seq // CMP_STRIDE
    nsl = seq // SEL_BLOCK
    assert seq % SEL_TILE == 0
    c_start = np.arange(ncmp) * CMP_STRIDE
    s_start = np.arange(nsl) * SEL_BLOCK
    ov = ((c_start[:, None] < s_start[None, :] + SEL_BLOCK) & (c_start[:, None] + CMP_BLOCK > s_start[None, :]))
    ov = np.pad(ov.astype(np.float32), ((0, 0), (0, 128 - nsl)))
    ov[ncmp - 1] = 0.0
    ov = jnp.asarray(np.tile(ov, (NSA_REP, 1)), BF16)
    width = NSA_REP * NSA_HEAD_DIM
    rows = NSA_REP * Q_BLOCK
    wlen = min(WINDOW + Q_BLOCK, seq)
    assert wlen >= SEL_TILE
    offs = jnp.arange(rows, dtype=jnp.int32)[:, None] % Q_BLOCK
    rel = offs - jnp.arange(wlen, dtype=jnp.int32)[None, :]
    relc = offs - (jnp.arange(ncmp, dtype=jnp.int32)[None, :] * CMP_STRIDE + (CMP_BLOCK - 1))
    return pl.pallas_call(
        _nsa_body, grid=(bsz, NSA_KV_GROUPS, nq),
        in_specs=[pl.BlockSpec((Q_BLOCK, NSA_REP * 128), lambda b, g, i: (b * nq + i, g)),
                  pl.BlockSpec((ncmp, 256), lambda b, g, i: (b, g)),
                  pl.BlockSpec((seq, KV_REC), lambda b, g, i: (b, g)),
                  pl.BlockSpec((Q_BLOCK, 128), lambda b, g, i: (b * nq + i, g)),
                  _const_spec(ov.shape), _const_spec(rel.shape), _const_spec(relc.shape)],
        out_specs=pl.BlockSpec((Q_BLOCK, width), lambda b, g, i: (b * nq + i, g)),
        out_shape=jax.ShapeDtypeStruct((n, NSA_KV_GROUPS * width), BF16),
        scratch_shapes=[pltpu.VMEM((rows, 128), F32), pltpu.VMEM((rows, SEL_TILE), F32),
                        pltpu.VMEM((rows, SEL_TILE), BF16), pltpu.VMEM((rows, 1), F32)],
        compiler_params=_cparams(("parallel", "parallel", "arbitrary")), name="nsa_attention")(
            qp, kcv, kv, gates, ov, rel, relc)


def _merge_body(x_ref, ya_ref, ob_ref, gm_ref, wnsa_ref, wout_ref, g_ref, b_ref, o_ref):
    yb = jnp.dot(ob_ref[...], wnsa_ref[...], preferred_element_type=F32)
    gm = gm_ref[...].astype(F32)
    mix_in = gm[:, :D_MODEL] * ya_ref[...].astype(F32) + gm[:, D_MODEL:] * yb
    mix = jnp.dot(mix_in.astype(BF16), wout_ref[...], preferred_element_type=F32)
    o_ref[...] = _layer_norm(DN_ALPHA * x_ref[...] + mix, g_ref[...], b_ref[...])


def _stage_merge(x2, ya, ob, gm, w_up_nsa, w_out, ln_g, ln_b):
    n = x2.shape[0]
    tm = 512
    row = lambda w: pl.BlockSpec((tm, w), lambda i: (i, 0))
    return pl.pallas_call(
        _merge_body, grid=(n // tm,),
        in_specs=[row(D_MODEL), row(D_MODEL), row(ob.shape[1]), row(2 * D_MODEL),
                  _const_spec(w_up_nsa.shape), _const_spec(w_out.shape),
                  _const_spec((1, D_MODEL)), _const_spec((1, D_MODEL))],
        out_specs=row(D_MODEL), out_shape=jax.ShapeDtypeStruct((n, D_MODEL), F32),
        compiler_params=_cparams(("parallel",)), name="merge_ln")(
            x2, ya, ob, gm, w_up_nsa.astype(BF16), w_out.astype(BF16),
            ln_g.reshape(1, -1).astype(F32), ln_b.reshape(1, -1).astype(F32))


BIG_NEG = -3.0e38


def _top_rows(s, k):
    rid = lax.broadcasted_iota(jnp.int32, s.shape, 0).astype(F32)
    rank = jnp.full(s.shape, float(k), F32)
    vals, idxs = [], []
    for r in range(k):
        m = jnp.max(s, axis=0, keepdims=True)
        idx = jnp.min(jnp.where(s == m, rid, float(s.shape[0])), axis=0, keepdims=True)
        hit = rid == idx
        rank = jnp.where(hit, float(r), rank)
        s = jnp.where(hit, BIG_NEG, s)
        vals.append(m)
        idxs.append(idx)
    return jnp.concatenate(vals, axis=0), jnp.concatenate(idxs, axis=0), rank


def _max_rounds(s, k, want_round=True):
    rnd = jnp.full(s.shape, float(k), F32) if want_round else None
    vals = []
    for r in range(k):
        m = jnp.max(s, axis=0, keepdims=True)
        eq = s == m
        if want_round:
            rnd = jnp.where(eq, float(r), rnd)
        s = jnp.where(eq, BIG_NEG, s)
        vals.append(m)
    return jnp.concatenate(vals, axis=0), rnd


def _pair_candidates(a, b):
    c8 = lax.broadcasted_iota(jnp.int32, (8, a.shape[1]), 0)
    parts = [a[0:1] + b[0:8], a[0:1] + b[8:16]]
    for r in range(1, 8):
        keep = PEER_TOPK // (r + 1)
        cand = a[r:r + 1] + b[0:8]
        parts.append(cand if keep >= 8 else jnp.where(c8 < keep, cand, BIG_NEG))
    parts.append(a[8:16] + b[0:1])
    return jnp.concatenate(parts, axis=0)


def _picks_per_rank(picked):
    rows = [jnp.sum(picked[0:16], axis=0, keepdims=True)]
    rows += [jnp.sum(picked[8 * (r + 1):8 * (r + 2)], axis=0, keepdims=True) for r in range(1, 8)]
    return jnp.concatenate(rows + [picked[72:80]], axis=0)


def _spread_by_rank(rank, per_rank):
    out = jnp.zeros(rank.shape, F32)
    for r in range(PEER_TOPK):
        out = out + jnp.where(rank == float(r), per_rank[r:r + 1], 0.0)
    return out


def _sort16_network():
    pairs, p = [], 1
    while p < 16:
        k = p
        while k >= 1:
            for j in range(k % p, 16 - k, 2 * k):
                for i in range(min(k, 16 - j - k)):
                    if (i + j) // (2 * p) == (i + j + k) // (2 * p):
                        pairs.append((i + j, i + j + k))
            k //= 2
        p *= 2
    return pairs


_SORT16 = _sort16_network()


def _sorted_top16(s):
    def exchange(v, i, j):
        v[i], v[j] = jnp.maximum(v[i], v[j]), jnp.minimum(v[i], v[j])

    v = [s[8 * k:8 * k + 8] for k in range(16)]
    for i, j in _SORT16:
        exchange(v, i, j)
    for shift in (4, 2, 1):
        v = [jnp.maximum(v[k], pltpu.roll(v[15 - k], shift, 0)) for k in range(16)]
        for d in (8, 4, 2, 1):
            for i in range(16):
                if i & d == 0:
                    exchange(v, i, i + d)
    return jnp.concatenate([x[0:1] for x in v], axis=0)


def _count_rows(vals, s, strict):
    rows = [jnp.broadcast_to(vals[r:r + 1], (8, s.shape[1])) for r in range(vals.shape[0])]
    assert len(rows) == 16
    outs = []
    for c in range(0, s.shape[0], 8):
        sc = s[c:c + 8]
        test = (lambda row: row > sc) if strict else (lambda row: row <= sc)
        c1 = test(rows[7])
        c2 = test(jnp.where(c1, rows[11], rows[3]))
        c3 = test(jnp.where(c1, jnp.where(c2, rows[13], rows[9]), jnp.where(c2, rows[5], rows[1])))
        hi = jnp.where(c2, jnp.where(c3, rows[14], rows[12]), jnp.where(c3, rows[10], rows[8]))
        lo = jnp.where(c2, jnp.where(c3, rows[6], rows[4]), jnp.where(c3, rows[2], rows[0]))
        c4 = test(jnp.where(c1, hi, lo))
        acc = (jnp.where(c1, 8.0, 0.0) + jnp.where(c2, 4.0, 0.0) + jnp.where(c3, 2.0, 0.0) + jnp.where(c4, 1.0, 0.0)
               + jnp.where(test(rows[15]), 1.0, 0.0))
        outs.append(acc)
    return jnp.concatenate(outs, axis=0)


def _route_fast(s0, s1):
    k = PEER_TOPK
    count = lambda v: jnp.sum(v, axis=0, keepdims=True)
    a, b = _sorted_top16(s0), _sorted_top16(s1)
    cand = _pair_candidates(a, b)
    best, _ = _max_rounds(cand, k, want_round=False)
    hit = cand >= best[k - 1:k]
    z = jnp.sum(jnp.where(hit, jnp.exp(cand - best[0:1]), 0.0), axis=0, keepdims=True)
    nrank = _picks_per_rank(jnp.where(hit, 1.0, 0.0))
    earns = jnp.concatenate([jnp.min(jnp.where(nrank >= float(v), a, -BIG_NEG), axis=0, keepdims=True)
                             for v in range(1, k + 1)], axis=0)
    n_i = _count_rows(earns, s0, strict=False)
    ties = lambda t, s: (count(jnp.where(t[:-1] == t[1:], 1.0, 0.0))
                         + jnp.abs(count(jnp.where(s >= t[k - 1:k], 1.0, 0.0)) - k))
    bad = ties(a, s0) + ties(b, s1) + jnp.abs(count(nrank) - k)
    return n_i, jnp.exp(s0 - a[0:1]) / z, _count_rows(b, s1, strict=True), jnp.exp(s1 - b[0:1]), bad


def _route_exact(s0, s1):
    k = PEER_TOPK
    a, _, rk0 = _top_rows(s0, k)
    b, _, rk1 = _top_rows(s1, k)
    cand = jnp.concatenate([a[r:r + 1] + b for r in range(k)], axis=0)
    best, bidx, _ = _top_rows(cand, k)
    brank = jnp.floor(bidx * (1.0 / k))
    z = jnp.sum(jnp.exp(best - best[0:1]), axis=0, keepdims=True)
    nrank = jnp.concatenate([jnp.sum(jnp.where(brank == float(r), 1.0, 0.0), axis=0, keepdims=True)
                             for r in range(k)], axis=0)
    return _spread_by_rank(rk0, nrank), jnp.exp(s0 - a[0:1]) / z, rk1, jnp.exp(s1 - b[0:1])


def _gelu_tanh(x):
    c = math.sqrt(2.0 / math.pi)
    return x / (1.0 + jnp.exp(x * (x * x * (-2.0 * c * 0.044715) - 2.0 * c)))


def _peer_body(x_ref, wq_ref, sk_ref, u_ref, v_ref, g_ref, b_ref, o_ref,
               xt_sc, acc_sc, n_sc, e0_sc, rk1_sc, e1_sc, p_sc, st_sc):
    e = pl.program_id(1)
    nk = PEER_NKEYS

    @pl.when(e == 0)
    def _():
        xt = x_ref[...].T.astype(BF16)
        xt_sc[...] = xt
        acc_sc[...] = jnp.zeros_like(acc_sc)
        qt = jnp.dot(wq_ref[...], xt, preferred_element_type=F32).astype(BF16)
        st_sc[...] = jnp.dot(sk_ref[...], qt, preferred_element_type=F32)

        @pl.loop(0, PEER_HEADS)
        def _(h):
            s0 = st_sc[pl.ds(pl.multiple_of(2 * h * nk, nk), nk), :]
            s1 = st_sc[pl.ds(pl.multiple_of((2 * h + 1) * nk, nk), nk), :]

            def put(n_i, e0, rk1, e1):
                n_sc[h] = n_i
                e0_sc[h] = e0
                rk1_sc[h] = rk1.astype(BF16)
                e1_sc[h] = e1.astype(BF16)

            n_i, e0, rk1, e1, bad = _route_fast(s0, s1)
            put(n_i, e0, rk1, e1)

            @pl.when(jnp.max(bad) > 0.0)
            def _():
                put(*_route_exact(s0, s1))

    rows = u_ref.shape[0] // nk
    ht = jnp.dot(u_ref[...], xt_sc[...], preferred_element_type=F32)
    act = _gelu_tanh(ht).astype(BF16)
    for ii in range(rows):
        w = jnp.zeros((nk, xt_sc.shape[1]), BF16)
        for h in range(PEER_HEADS):
            n_row = n_sc[h, pl.ds(e * rows + ii, 1), :].astype(BF16)
            e0_row = e0_sc[h, pl.ds(e * rows + ii, 1), :].astype(BF16)
            w = w + jnp.where(rk1_sc[h] < n_row, e0_row * e1_sc[h], 0)
        p_sc[ii * nk:(ii + 1) * nk, :] = w * act[ii * nk:(ii + 1) * nk]
    acc_sc[...] += _tn_dot(v_ref[...], p_sc[...])

    @pl.when(e == pl.num_programs(1) - 1)
    def _():
        o_ref[...] = _layer_norm(DN_ALPHA * x_ref[...] + acc_sc[...].T, g_ref[...], b_ref[...])


def _stage_peer(x1, w_q, subkeys, u_tab, v_tab, ln_g, ln_b, tt=512, et=1024):
    n = x1.shape[0]
    tt = min(tt, n)
    H, nk, half = PEER_HEADS, PEER_NKEYS, subkeys.shape[-1]
    wq_t = w_q.T.astype(BF16)
    eye = jnp.eye(2 * H, dtype=F32)
    sk = subkeys.reshape(2 * H, nk, half)
    sk_t = jnp.einsum('ckd,ce->cked', sk, eye).reshape(2 * H * nk, 2 * H * half).astype(BF16)
    ne = u_tab.shape[0]
    return pl.pallas_call(
        _peer_body, grid=(n // tt, ne // et),
        in_specs=[pl.BlockSpec((tt, D_MODEL), lambda t, e: (t, 0)),
                  _const_spec(wq_t.shape), _const_spec(sk_t.shape),
                  pl.BlockSpec((et, D_MODEL), lambda t, e: (e, 0)),
                  pl.BlockSpec((et, D_MODEL), lambda t, e: (e, 0)),
                  _const_spec((1, D_MODEL)), _const_spec((1, D_MODEL))],
        out_specs=pl.BlockSpec((tt, D_MODEL), lambda t, e: (t, 0)),
        out_shape=jax.ShapeDtypeStruct((n, D_MODEL), F32),
        scratch_shapes=[pltpu.VMEM((D_MODEL, tt), BF16), pltpu.VMEM((D_MODEL, tt), F32),
                        pltpu.VMEM((H, nk, tt), F32), pltpu.VMEM((H, nk, tt), F32),
                        pltpu.VMEM((H, nk, tt), BF16), pltpu.VMEM((H, nk, tt), BF16),
                        pltpu.VMEM((et, tt), BF16), pltpu.VMEM((2 * H * nk, tt), F32)],
        compiler_params=_cparams(("parallel", "arbitrary")), name="peer_ffn")(
            x1, wq_t, sk_t, u_tab.astype(BF16), v_tab.astype(BF16),
            ln_g.reshape(1, -1).astype(F32), ln_b.reshape(1, -1).astype(F32))


def _layer(x, w_in, b_in, ssm_a_re, ssm_a_im, ssm_log_dt, ssm_b_re, ssm_b_im, ssm_c_re, ssm_c_im, ssm_d,
           w_glu, w_up_ssm, nsa_pe_k, nsa_w1_k, nsa_w2_k, nsa_pe_v, nsa_w1_v, nsa_w2_v, w_up_nsa,
           w_out, ln1_g, ln1_b, peer_w_q, peer_subkeys, peer_u, peer_v, ln2_g, ln2_b):
    bsz, seq, _ = x.shape
    x2 = x.reshape(bsz * seq, D_MODEL)
    u, qp, kvc, kv, gates, gm = _stage_proj(x2, w_in, b_in, seq)
    ya = _stage_ssm(u, bsz, seq, ssm_a_re, ssm_a_im, ssm_log_dt, ssm_b_re, ssm_b_im, ssm_c_re, ssm_c_im,
                    ssm_d, w_glu, w_up_ssm)
    kcv = _stage_compress(kvc, bsz, seq, nsa_pe_k, nsa_w1_k, nsa_w2_k, nsa_pe_v, nsa_w1_v, nsa_w2_v)
    ob = _stage_attn(qp, kcv, kv, gates, bsz, seq)
    x1 = _stage_merge(x2, ya, ob, gm, w_up_nsa, w_out, ln1_g, ln1_b)
    out = _stage_peer(x1, peer_w_q, peer_subkeys, peer_u, peer_v, ln2_g, ln2_b)
    return out.reshape(bsz, seq, D_MODEL)


def kernel(x, w_in, b_in, ssm_a_re, ssm_a_im, ssm_log_dt, ssm_b_re, ssm_b_im, ssm_c_re, ssm_c_im, ssm_d, w_glu,
           w_up_ssm, nsa_pe_k, nsa_w1_k, nsa_w2_k, nsa_pe_v, nsa_w1_v, nsa_w2_v, w_up_nsa, w_out, ln1_g, ln1_b,
           peer_w_q, peer_subkeys, peer_u, peer_v, ln2_g, ln2_b):
    params = (w_in, b_in, ssm_a_re, ssm_a_im, ssm_log_dt, ssm_b_re, ssm_b_im, ssm_c_re, ssm_c_im, ssm_d, w_glu,
              w_up_ssm, nsa_pe_k, nsa_w1_k, nsa_w2_k, nsa_pe_v, nsa_w1_v, nsa_w2_v, w_up_nsa, w_out, ln1_g, ln1_b,
              peer_w_q, peer_subkeys, peer_u, peer_v, ln2_g, ln2_b)
    for layer in range(w_in.shape[0]):
        x = _layer(x, *[p[layer] for p in params])
    return x
```

```python
import functools
import math

import jax
import jax.numpy as jnp
import numpy as np
from jax import lax
from jax.experimental import pallas as pl
from jax.experimental.pallas import tpu as pltpu

F32 = jnp.float32
BF16 = jnp.bfloat16

D_MODEL = 1024
SSM_WIDTH = 512
SSM_GROUP = 16
SSM_GROUPS = 32
SSM_STATE = 64
SSM_CHUNK = 8
LANE_GROUPS = 8
N_LANE_TILES = SSM_WIDTH // 128
NSA_HEADS = 8
NSA_KV_GROUPS = 2
NSA_REP = NSA_HEADS // NSA_KV_GROUPS
NSA_HEAD_DIM = 64
KV_WIDTH = NSA_KV_GROUPS * NSA_HEAD_DIM
CMP_BLOCK = 32
CMP_STRIDE = 16
CMP_HIDDEN = 128
SEL_BLOCK = 64
SEL_TOPK = 16
WINDOW = 512
Q_BLOCK = 256
FORCE_BONUS = 1.0e4
PEER_HEADS = 8
PEER_NKEYS = 128
PEER_EXPERTS = PEER_NKEYS * PEER_NKEYS
PEER_TOPK = 16
DN_ALPHA = 2.0 ** 0.25
LN_EPS = 1e-5
NEG = -1e30
VMEM_LIMIT = 56 * 1024 * 1024


def _cparams(sem):
    return pltpu.CompilerParams(dimension_semantics=sem, vmem_limit_bytes=VMEM_LIMIT)


def _const_spec(shape):
    nd = len(shape)
    return pl.BlockSpec(shape, lambda *_: (0,) * nd, pipeline_mode=pl.Buffered(1))


def _layer_norm(z, g, b):
    mu = jnp.mean(z, axis=-1, keepdims=True)
    var = jnp.mean(jnp.square(z - mu), axis=-1, keepdims=True)
    return (z - mu) * lax.rsqrt(var + LN_EPS) * g + b


def _nt_dot(a, b):
    return lax.dot_general(a, b, (((1,), (1,)), ((), ())), preferred_element_type=F32)


def _tn_dot(a, b):
    return lax.dot_general(a, b, (((0,), (0,)), ((), ())), preferred_element_type=F32)


KV_REC = 640
KS_OFF, VS_OFF, KW_OFF, VW_OFF = 0, 256, 384, 512
POS_HI, POS_LO, ONE_LANE, BLOCK_LANE0 = 64, 65, 64, 128
assert SEL_BLOCK == 64


def _position_features(pos, lane):
    hi, lo = pos >> 6, pos & 63
    f = jnp.where(lane == KS_OFF + POS_HI, hi, 0) + jnp.where(lane == KW_OFF + POS_HI, hi, 0)
    f = f + jnp.where(lane == KS_OFF + POS_LO, lo, 0) + jnp.where(lane == KW_OFF + POS_LO, lo, 0)
    f = f + jnp.where(lane == KS_OFF + BLOCK_LANE0 + hi, 1, 0)
    f = f + jnp.where(lane == VS_OFF + ONE_LANE, 1, 0) + jnp.where(lane == VW_OFF + ONE_LANE, 1, 0)
    return f.astype(F32)


def _proj_body(seq, x_ref, wu, bu, wq, bq, wc, bc, wk, bk, wg, bg, wm, bm, aux_ref,
               u_o, q_o, c_o, k_o, g_o, m_o):
    xb = x_ref[...].astype(BF16)
    tm = x_ref.shape[0]
    dk = NSA_HEAD_DIM

    def lin(w, b):
        return jnp.dot(xb, w[...], preferred_element_type=F32) + b[...]

    def halves(a, tile):
        t = a[:, tile * 128:(tile + 1) * 128]
        return t, pltpu.roll(t, dk, 1)

    low = lax.broadcasted_iota(jnp.int32, (tm, 128), 1) < dk
    u_o[...] = lin(wu, bu)
    q = lin(wq, bq) * (dk ** -0.5)
    for tile in range(NSA_HEADS // 2):
        for half, piece in enumerate(halves(q, tile)):
            h = 2 * tile + half
            q_o[:, h * 128:(h + 1) * 128] = jnp.where(low, piece, aux_ref[h:h + 1, :]).astype(BF16)
    c_o[...] = lin(wc, bc)
    kv = lin(wk, bk)
    parts = [halves(kv, tile) for tile in range(4)]
    zero = jnp.zeros((tm, 128), F32)
    recs = []
    for g in range(NSA_KV_GROUPS):
        k_s, v_s, k_w, v_w = (jnp.where(low, parts[tile][g], 0.0) for tile in range(4))
        recs += [k_s, zero, v_s, k_w, v_w]
    pos0 = (pl.program_id(0) % (seq // tm)) * tm
    shape = (tm, NSA_KV_GROUPS * KV_REC)
    lane = lax.broadcasted_iota(jnp.int32, shape, 1)
    lane = jnp.where(lane >= KV_REC, lane - KV_REC, lane)
    feat = _position_features(pos0 + lax.broadcasted_iota(jnp.int32, shape, 0), lane)
    k_o[...] = (jnp.concatenate(recs, axis=1) + feat).astype(BF16)
    g_o[...] = lin(wg, bg)
    m_o[...] = jax.nn.sigmoid(lin(wm, bm)).astype(m_o.dtype)


def _stage_proj(x2, w_in, b_in, seq):
    n = x2.shape[0]
    tm = 256
    o0 = SSM_WIDTH
    o1 = o0 + NSA_HEADS * NSA_HEAD_DIM
    o2 = o1 + 6 * KV_WIDTH
    o3 = o2 + 3 * NSA_HEADS
    w_u, b_u = w_in[:, :o0], b_in[:o0]
    w_q, b_q = w_in[:, o0:o1], b_in[o0:o1]
    w_kv, b_kv = w_in[:, o1:o2], b_in[o1:o2]
    w_g, b_g = w_in[:, o2:o3], b_in[o2:o3]
    w_m, b_m = w_in[:, o3:], b_in[o3:]

    slope = 2.0 ** -(np.arange(NSA_HEADS, dtype=np.float32) + 1.0)
    q_aux = np.zeros((NSA_HEADS, 128), np.float32)
    q_aux[:, POS_HI] = slope * 64.0
    q_aux[:, POS_LO] = slope

    def pad_g(a):
        a = a.reshape(a.shape[:-1] + (NSA_KV_GROUPS, 3 * NSA_REP))
        a = jnp.concatenate([a, jnp.zeros(a.shape[:-1] + (128 - 3 * NSA_REP,), a.dtype)], axis=-1)
        return a.reshape(a.shape[:-2] + (NSA_KV_GROUPS * 128,))

    ws = [w_u, w_q, w_kv[:, :2 * KV_WIDTH], w_kv[:, 2 * KV_WIDTH:], pad_g(w_g), w_m]
    bs = [b_u, b_q, b_kv[:2 * KV_WIDTH], b_kv[2 * KV_WIDTH:], pad_g(b_g), b_m]
    odt = [F32, BF16, F32, BF16, F32, BF16]
    widths = [SSM_WIDTH, NSA_HEADS * 128, 2 * KV_WIDTH, NSA_KV_GROUPS * KV_REC, NSA_KV_GROUPS * 128, 2 * D_MODEL]
    args, in_specs = [x2], [pl.BlockSpec((tm, D_MODEL), lambda i: (i, 0))]
    for w, b in zip(ws, bs):
        args += [w.astype(BF16), b.reshape(1, -1).astype(F32)]
        in_specs += [_const_spec(w.shape), _const_spec((1, w.shape[1]))]
    args.append(jnp.asarray(q_aux))
    in_specs.append(_const_spec(q_aux.shape))
    out_shape = [jax.ShapeDtypeStruct((n, w), dt) for w, dt in zip(widths, odt)]
    out_specs = [pl.BlockSpec((tm, w), lambda i: (i, 0)) for w in widths]
    return pl.pallas_call(
        functools.partial(_proj_body, seq), grid=(n // tm,), in_specs=in_specs, out_specs=out_specs, out_shape=out_shape,
        compiler_params=_cparams(("parallel",)), name="in_proj")(*args)


def _ssm_weights(a_re, a_im, log_dt, b_re, b_im, c_re, c_im):
    L, G, P, H, A, J = SSM_CHUNK, SSM_GROUPS, SSM_STATE, SSM_GROUP, LANE_GROUPS, N_LANE_TILES
    lam = lax.complex(a_re.astype(F32), a_im.astype(F32))
    dt = jnp.exp(log_dt.astype(F32))[:, None]
    lam_bar = jnp.exp(lam * dt)
    b_bar = ((lam_bar - 1.0) / lam)[:, :, None] * lax.complex(b_re.astype(F32), b_im.astype(F32))
    c = lax.complex(c_re.astype(F32), c_im.astype(F32))
    k = jnp.arange(L + 1, dtype=F32)
    pw = jnp.exp((lam * dt)[None] * k[:, None, None])
    def spread(x, n_inner):
        src = np.arange(x.shape[2])
        dst = np.arange(x.shape[2] * A)
        sel = ((src[:, None] // n_inner == dst[None, :] // (A * n_inner))
               & (src[:, None] % n_inner == dst[None, :] % n_inner))
        return jnp.einsum('jrk,kc->jrc', x, jnp.asarray(sel, F32)), (dst % (A * n_inner)) // n_inner

    def group_mask(row_group, col_group):
        return jnp.asarray(row_group[:, None] == col_group[None, :], F32)

    in_rows = (np.arange(L * 128) % 128) // H
    st_rows = (np.arange(2 * A * P) % (A * P)) // P
    kern = jnp.real(jnp.einsum('ghp,kgp,gpj->kghj', c, pw[:L], b_bar))
    s_i, t_i = jnp.arange(L)[:, None], jnp.arange(L)[None, :]
    tau = jnp.clip(t_i - s_i, 0, L - 1)
    causal = (t_i >= s_i).astype(F32)
    k_st = (kern[tau] * causal[:, :, None, None, None]).reshape(L, L, J, A, H, H)
    k_st = k_st.transpose(2, 0, 3, 5, 1, 4).reshape(J, L * 128, L * H)
    w_intra, cg = spread(k_st, H)
    w_intra = w_intra * group_mask(in_rows, cg)
    q = (pw[:L][::-1][:, :, :, None] * b_bar[None]).reshape(L, J, A, P, H)
    q = jnp.stack([jnp.real(q), jnp.imag(q)], axis=0)
    q = q.transpose(2, 1, 3, 5, 0, 4).reshape(J, L * 128, 2 * P)
    m2, cg = spread(q, P)
    m2 = m2 * group_mask(in_rows, cg)
    cl = (c[None] * pw[1:][:, :, None, :]).reshape(L, J, A, H, P)
    cl = jnp.stack([jnp.real(cl), -jnp.imag(cl)], axis=0)
    cl = cl.transpose(2, 0, 3, 5, 1, 4).reshape(J, 2 * A * P, L * H)
    m1, cg = spread(cl, H)
    m1 = m1 * group_mask(st_rows, cg)
    w2 = jnp.concatenate([w_intra, m1], axis=1)
    lam8 = pw[L].reshape(G * P)
    return m2.astype(BF16), w2.astype(BF16), jnp.real(lam8), jnp.imag(lam8)


def _split_lane_tiles(src_ref, dst_sc):
    for j in range(dst_sc.shape[0]):
        dst_sc[j] = src_ref[:, j * 128:(j + 1) * 128]


def _chunk_inputs(u_sc, j, rows):
    return jnp.concatenate([u_sc[j, pl.ds(s, rows, stride=SSM_CHUNK), :] for s in range(SSM_CHUNK)], axis=1)


def _ssm_state_body(u_ref, m2_ref, zre_ref, zim_ref, u_sc):
    half = LANE_GROUPS * SSM_STATE
    _split_lane_tiles(u_ref, u_sc)
    for j in range(N_LANE_TILES):
        z = jnp.dot(_chunk_inputs(u_sc, j, zre_ref.shape[0]).astype(BF16), m2_ref[j], preferred_element_type=F32)
        zre_ref[:, j * half:(j + 1) * half] = z[:, :half]
        zim_ref[:, j * half:(j + 1) * half] = z[:, half:]


def _ssm_scan_body(zre_ref, zim_ref, lre_ref, lim_ref, xre_ref, xim_ref):
    lr, li = lre_ref[...], lim_ref[...]

    def step(c, carry):
        xr, xi = carry
        xre_ref[pl.ds(c, 1), :] = xr
        xim_ref[pl.ds(c, 1), :] = xi
        return lr * xr - li * xi + zre_ref[pl.ds(c, 1), :], lr * xi + li * xr + zim_ref[pl.ds(c, 1), :]

    zero = jnp.zeros(lr.shape, F32)
    lax.fori_loop(0, zre_ref.shape[0], step, (zero, zero))


def _ssm_out_body(u_ref, xre_ref, xim_ref, w2_ref, d_ref, wglu_ref, wup_ref, o_ref, u_sc, y_sc):
    half = LANE_GROUPS * SSM_STATE
    rows = xre_ref.shape[0]
    _split_lane_tiles(u_ref, u_sc)
    for j in range(N_LANE_TILES):
        lhs = jnp.concatenate(
            [_chunk_inputs(u_sc, j, rows), xre_ref[:, j * half:(j + 1) * half], xim_ref[:, j * half:(j + 1) * half]],
            axis=1).astype(BF16)
        yj = jnp.dot(lhs, w2_ref[j], preferred_element_type=F32)
        for t in range(SSM_CHUNK):
            y_sc[j, pl.ds(t, rows, stride=SSM_CHUNK), :] = yj[:, t * 128:(t + 1) * 128]
    y = jnp.concatenate([y_sc[j] for j in range(N_LANE_TILES)], axis=1) + d_ref[...] * u_ref[...]
    gl = jnp.dot(jax.nn.gelu(y).astype(BF16), wglu_ref[...], preferred_element_type=F32)
    v = gl[:, :SSM_WIDTH] * jax.nn.sigmoid(gl[:, SSM_WIDTH:])
    o_ref[...] = jnp.dot(v.astype(BF16), wup_ref[...], preferred_element_type=F32).astype(o_ref.dtype)


def _stage_ssm(u, bsz, seq, a_re, a_im, log_dt, b_re, b_im, c_re, c_im, d_skip, w_glu, w_up):
    n = bsz * seq
    L = SSM_CHUNK
    nck = seq // L
    gp = SSM_GROUPS * SSM_STATE
    m2, w2, l8re, l8im = _ssm_weights(a_re, a_im, log_dt, b_re, b_im, c_re, c_im)
    rs = min(nck, 512)
    zre, zim = pl.pallas_call(
        _ssm_state_body, grid=(bsz, nck // rs),
        in_specs=[pl.BlockSpec((rs * L, SSM_WIDTH), lambda b, i: (b * (nck // rs) + i, 0)),
                  _const_spec(m2.shape)],
        out_specs=[pl.BlockSpec((rs, gp), lambda b, i: (i, b))] * 2,
        out_shape=[jax.ShapeDtypeStruct((nck, bsz * gp), F32)] * 2,
        scratch_shapes=[pltpu.VMEM((N_LANE_TILES, rs * L, 128), F32)],
        compiler_params=_cparams(("parallel", "parallel")), name="ssm_chunk_state")(u, m2)
    cw = 1024
    ncol = bsz * gp // cw
    lre = jnp.tile(l8re, bsz).reshape(1, bsz * gp)
    lim = jnp.tile(l8im, bsz).reshape(1, bsz * gp)
    col = pl.BlockSpec((nck, cw), lambda i: (0, i))
    lam = pl.BlockSpec((1, cw), lambda i: (0, i))
    xre, xim = pl.pallas_call(
        _ssm_scan_body, grid=(ncol,), in_specs=[col, col, lam, lam], out_specs=[col, col],
        out_shape=[jax.ShapeDtypeStruct((nck, bsz * gp), F32)] * 2,
        compiler_params=_cparams(("parallel",)), name="ssm_carry_scan")(zre, zim, lre, lim)
    ro = min(nck, 128)
    d8 = d_skip.reshape(1, SSM_WIDTH).astype(F32)
    return pl.pallas_call(
        _ssm_out_body, grid=(bsz, nck // ro),
        in_specs=[pl.BlockSpec((ro * L, SSM_WIDTH), lambda b, i: (b * (nck // ro) + i, 0)),
                  pl.BlockSpec((ro, gp), lambda b, i: (i, b)),
                  pl.BlockSpec((ro, gp), lambda b, i: (i, b)),
                  _const_spec(w2.shape), _const_spec((1, SSM_WIDTH)),
                  _const_spec(w_glu.shape), _const_spec(w_up.shape)],
        out_specs=pl.BlockSpec((ro * L, D_MODEL), lambda b, i: (b * (nck // ro) + i, 0)),
        out_shape=jax.ShapeDtypeStruct((n, D_MODEL), BF16),
        scratch_shapes=[pltpu.VMEM((N_LANE_TILES, ro * L, 128), F32)] * 2,
        compiler_params=_cparams(("parallel", "parallel")), name="ssm_out")(
            u, xre, xim, w2, d8, w_glu.astype(BF16), w_up.astype(BF16))


def _compress_body(kv_ref, pea_ref, peb_ref, w1a_ref, w1b_ref, w2_ref, o_ref, kv_sc):
    nrow = o_ref.shape[0]
    _split_lane_tiles(kv_ref, kv_sc)
    f = jnp.concatenate([kv_sc[j, pl.ds(l, nrow, stride=CMP_STRIDE), :]
                         for l in range(CMP_STRIDE) for j in range(kv_sc.shape[0])], axis=1)
    a = jnp.dot((f + pea_ref[...]).astype(BF16), w1a_ref[...], preferred_element_type=F32)
    b = jnp.dot((f + peb_ref[...]).astype(BF16), w1b_ref[...], preferred_element_type=F32)
    pre = a + pltpu.roll(b, b.shape[0] - 1, 0)
    hid = jax.nn.gelu(pre)
    out = jnp.dot(hid.astype(BF16), w2_ref[...], preferred_element_type=F32)
    lane = lax.broadcasted_iota(jnp.int32, out.shape, 1) & 255
    c_end = lax.broadcasted_iota(jnp.int32, out.shape, 0) * CMP_STRIDE + (CMP_BLOCK - 1)
    feat = jnp.where(lane == POS_HI, c_end >> 6, 0) + jnp.where(lane == POS_LO, c_end & 63, 0)
    o_ref[...] = (out + feat.astype(F32)).astype(BF16)


def _stage_compress(kvc, bsz, seq, pe_k, w1_k, w2_k, pe_v, w1_v, w2_v):
    hl = CMP_STRIDE
    nrow = seq // hl
    G, dk, hid = NSA_KV_GROUPS, NSA_HEAD_DIM, CMP_HIDDEN
    eye = jnp.eye(2 * G, dtype=F32)

    def big_w1(lo):
        wk = w1_k.reshape(CMP_BLOCK, dk, hid)[lo:lo + hl]
        wv = w1_v.reshape(CMP_BLOCK, dk, hid)[lo:lo + hl]
        w = jnp.stack([wk, wk, wv, wv], axis=1)
        return jnp.einsum('lcdj,ce->lcdej', w, eye).reshape(hl * 2 * G * dk, 2 * G * hid)

    def big_pe(lo):
        pk, pv = pe_k[lo:lo + hl], pe_v[lo:lo + hl]
        return jnp.stack([pk, pk, pv, pv], axis=1).reshape(1, hl * 2 * G * dk)

    w2 = jnp.stack([w2_k, w2_k, w2_v, w2_v], axis=0)
    w2 = jnp.concatenate([w2, jnp.zeros((2 * G, hid, 128 - dk), w2.dtype)], axis=-1)
    slot = eye[np.array([kv * G + g for g in range(G) for kv in range(2)])].T
    w2 = jnp.einsum('cjd,ce->cjed', w2, slot).reshape(2 * G * hid, 2 * G * 128)
    width = hl * 2 * G * dk
    return pl.pallas_call(
        _compress_body, grid=(bsz,),
        in_specs=[pl.BlockSpec((seq, 2 * G * dk), lambda b: (b, 0)),
                  _const_spec((1, width)), _const_spec((1, width)),
                  _const_spec((width, 2 * G * hid)), _const_spec((width, 2 * G * hid)),
                  _const_spec((2 * G * hid, 2 * G * 128))],
        out_specs=pl.BlockSpec((nrow, 2 * G * 128), lambda b: (b, 0)),
        out_shape=jax.ShapeDtypeStruct((bsz * nrow, 2 * G * 128), BF16),
        scratch_shapes=[pltpu.VMEM((2 * G * dk // 128, seq, 128), F32)],
        compiler_params=_cparams(("parallel",)), name="nsa_compress")(
            kvc, big_pe(0).astype(F32), big_pe(hl).astype(F32),
            big_w1(0).astype(BF16), big_w1(hl).astype(BF16), w2.astype(BF16))


MASK_BIG = 2.0 ** 100
SEL_TILE = 512


def _nsa_body(q_ref, kcv_ref, kv_ref, g_ref, ov_ref, rel_ref, relc_ref, o_ref, acc, s_buf, p_buf, a_buf):
    i = pl.program_id(2)
    qs = i * Q_BLOCK
    seq = kv_ref.shape[0]
    rows = NSA_REP * Q_BLOCK
    dk = NSA_HEAD_DIM
    qh = [q_ref[:, r * 128:(r + 1) * 128] for r in range(NSA_REP)]
    q_all = jnp.concatenate(qh, axis=0)

    kc = kcv_ref[:, :128]
    vc = kcv_ref[:, 128:]
    vis_c = relc_ref[...] >= -qs
    s = jnp.where(vis_c, _nt_dot(q_all, kc), NEG)
    p = jnp.exp(s - jnp.max(s, axis=-1, keepdims=True)) * jnp.where(vis_c, 1.0, 0.0)
    p = (p / jnp.maximum(jnp.sum(p, axis=-1, keepdims=True), 1e-30)).astype(BF16)
    o_cmp = jnp.dot(p, vc, preferred_element_type=F32)

    wlen = min(WINDOW + Q_BLOCK, seq)
    w0 = pl.multiple_of(jnp.maximum(qs + Q_BLOCK - wlen, 0), Q_BLOCK)
    rel_w = rel_ref[...]
    s = _nt_dot(q_all, kv_ref[pl.ds(w0, wlen), KW_OFF:KW_OFF + 128])
    s = jnp.where((rel_w >= w0 - qs) & (rel_w < w0 - qs + WINDOW), s, NEG)
    p_w = jnp.exp(s - jnp.max(s, axis=-1, keepdims=True)).astype(BF16)
    a_win = jnp.dot(p_w, kv_ref[pl.ds(w0, wlen), VW_OFF:VW_OFF + 128], preferred_element_type=F32)

    p_heads = jnp.concatenate([p[r * Q_BLOCK:(r + 1) * Q_BLOCK] for r in range(NSA_REP)], axis=1)
    imp = jnp.dot(p_heads, ov_ref[...], preferred_element_type=F32)
    coli = lax.broadcasted_iota(jnp.int32, (Q_BLOCK, 128), 1)
    t_q = qs + lax.broadcasted_iota(jnp.int32, (Q_BLOCK, 128), 0)
    cur = t_q >> 6
    forced = (coli == 0) | (coli == cur) | (coli == cur - 1)
    score = jnp.where(coli * SEL_BLOCK <= t_q, imp + jnp.where(forced, FORCE_BONUS, 0.0), NEG)
    nsl = seq // SEL_BLOCK
    st = score.T[:nsl]
    groups = [st[8 * v:8 * v + 8] for v in range(nsl // 8)]
    jsub = lax.broadcasted_iota(jnp.int32, (8, Q_BLOCK), 0)
    rank = [jnp.zeros((8, Q_BLOCK), F32) for _ in groups]
    for k in range(nsl):
        rk = st[k:k + 1, :]
        for v, sv in enumerate(groups):
            if v > k // 8:
                one = jnp.where(rk >= sv, 1.0, 0.0)
            elif v < k // 8:
                one = jnp.where(rk > sv, 1.0, 0.0)
            else:
                one = jnp.where(rk > sv, 1.0, jnp.where(jsub > k % 8, jnp.where(rk == sv, 1.0, 0.0), 0.0))
            rank[v] = rank[v] + one
    drop_t = jnp.where(jnp.concatenate(rank, axis=0) < float(min(SEL_TOPK, nsl)), 0.0, -MASK_BIG)
    if nsl < 128:
        drop_t = jnp.concatenate([drop_t, jnp.zeros((128 - nsl, Q_BLOCK), F32)], axis=0)
    drop = drop_t.T.astype(BF16)
    q_sel = jnp.concatenate([jnp.concatenate([qh[r], drop], axis=1) for r in range(NSA_REP)], axis=0)

    n_tiles = (qs + Q_BLOCK + SEL_TILE - 1) // SEL_TILE

    def scores(kt):
        k0 = pl.multiple_of(kt * SEL_TILE, SEL_TILE)
        return _nt_dot(q_sel, kv_ref[pl.ds(k0, SEL_TILE), KS_OFF:KS_OFF + 256])

    def values(kt):
        k0 = pl.multiple_of(kt * SEL_TILE, SEL_TILE)
        return kv_ref[pl.ds(k0, SEL_TILE), VS_OFF:VS_OFF + 128]

    def trip(diagonal, kt, m_old):
        if not diagonal:
            s_next = scores(kt + 1)
        acc[...] = a_buf[...] * acc[...] + jnp.dot(p_buf[...], values(jnp.maximum(kt - 1, 0)),
                                                   preferred_element_type=F32)
        s = s_buf[...]
        if diagonal:
            s = jnp.where(rel_ref[:, :SEL_TILE] >= kt * SEL_TILE - qs, s, NEG)
        m_new = jnp.maximum(m_old, jnp.max(s, axis=-1, keepdims=True))
        p_buf[...] = jnp.exp(s - m_new).astype(BF16)
        a_buf[...] = jnp.exp(m_old - m_new)
        if not diagonal:
            s_buf[...] = s_next
        return m_new

    acc[...] = jnp.zeros_like(acc)
    p_buf[...] = jnp.zeros_like(p_buf)
    a_buf[...] = jnp.ones_like(a_buf)
    s_buf[...] = scores(0)
    m_s = lax.fori_loop(0, n_tiles - 1, functools.partial(trip, False), jnp.full((rows, 1), NEG, F32))
    trip(True, n_tiles - 1, m_s)
    a_sel = a_buf[...] * acc[...] + jnp.dot(p_buf[...], values(n_tiles - 1), preferred_element_type=F32)

    gate = jax.nn.sigmoid(g_ref[...])
    outs = []
    for r in range(NSA_REP):
        rr = slice(r * Q_BLOCK, (r + 1) * Q_BLOCK)
        o_s = a_sel[rr, :dk] / a_sel[rr, ONE_LANE:ONE_LANE + 1]
        o_w = a_win[rr, :dk] / a_win[rr, ONE_LANE:ONE_LANE + 1]
        outs.append(gate[:, 3 * r:3 * r + 1] * o_cmp[rr, :dk] + gate[:, 3 * r + 1:3 * r + 2] * o_s
                    + gate[:, 3 * r + 2:3 * r + 3] * o_w)
    o_ref[...] = jnp.concatenate(outs, axis=1).astype(o_ref.dtype)


def _stage_attn(qp, kcv, kv, gates, bsz, seq):
    n = bsz * seq
    nq = seq // Q_BLOCK
    ncmp = seq // CMP_STRIDE
    nsl = seq // SEL_BLOCK
    assert seq % SEL_TILE == 0
    c_start = np.arange(ncmp) * CMP_STRIDE
    s_start = np.arange(nsl) * SEL_BLOCK
    ov = ((c_start[:, None] < s_start[None, :] + SEL_BLOCK) & (c_start[:, None] + CMP_BLOCK > s_start[None, :]))
    ov = np.pad(ov.astype(np.float32), ((0, 0), (0, 128 - nsl)))
    ov[ncmp - 1] = 0.0
    ov = jnp.asarray(np.tile(ov, (NSA_REP, 1)), BF16)
    width = NSA_REP * NSA_HEAD_DIM
    rows = NSA_REP * Q_BLOCK
    wlen = min(WINDOW + Q_BLOCK, seq)
    assert wlen >= SEL_TILE
    offs = jnp.arange(rows, dtype=jnp.int32)[:, None] % Q_BLOCK
    rel = offs - jnp.arange(wlen, dtype=jnp.int32)[None, :]
    relc = offs - (jnp.arange(ncmp, dtype=jnp.int32)[None, :] * CMP_STRIDE + (CMP_BLOCK - 1))
    return pl.pallas_call(
        _nsa_body, grid=(bsz, NSA_KV_GROUPS, nq),
        in_specs=[pl.BlockSpec((Q_BLOCK, NSA_REP * 128), lambda b, g, i: (b * nq + i, g)),
                  pl.BlockSpec((ncmp, 256), lambda b, g, i: (b, g)),
                  pl.BlockSpec((seq, KV_REC), lambda b, g, i: (b, g)),
                  pl.BlockSpec((Q_BLOCK, 128), lambda b, g, i: (b * nq + i, g)),
                  _const_spec(ov.shape), _const_spec(rel.shape), _const_spec(relc.shape)],
        out_specs=pl.BlockSpec((Q_BLOCK, width), lambda b, g, i: (b * nq + i, g)),
        out_shape=jax.ShapeDtypeStruct((n, NSA_KV_GROUPS * width), BF16),
        scratch_shapes=[pltpu.VMEM((rows, 128), F32), pltpu.VMEM((rows, SEL_TILE), F32),
                        pltpu.VMEM((rows, SEL_TILE), BF16), pltpu.VMEM((rows, 1), F32)],
        compiler_params=_cparams(("parallel", "parallel", "arbitrary")), name="nsa_attention")(
            qp, kcv, kv, gates, ov, rel, relc)


def _merge_body(x_ref, ya_ref, ob_ref, gm_ref, wnsa_ref, wout_ref, g_ref, b_ref, o_ref):
    yb = jnp.dot(ob_ref[...], wnsa_ref[...], preferred_element_type=F32)
    gm = gm_ref[...].astype(F32)
    mix_in = gm[:, :D_MODEL] * ya_ref[...].astype(F32) + gm[:, D_MODEL:] * yb
    mix = jnp.dot(mix_in.astype(BF16), wout_ref[...], preferred_element_type=F32)
    o_ref[...] = _layer_norm(DN_ALPHA * x_ref[...] + mix, g_ref[...], b_ref[...])


def _stage_merge(x2, ya, ob, gm, w_up_nsa, w_out, ln_g, ln_b):
    n = x2.shape[0]
    tm = 512
    row = lambda w: pl.BlockSpec((tm, w), lambda i: (i, 0))
    return pl.pallas_call(
        _merge_body, grid=(n // tm,),
        in_specs=[row(D_MODEL), row(D_MODEL), row(ob.shape[1]), row(2 * D_MODEL),
                  _const_spec(w_up_nsa.shape), _const_spec(w_out.shape),
                  _const_spec((1, D_MODEL)), _const_spec((1, D_MODEL))],
        out_specs=row(D_MODEL), out_shape=jax.ShapeDtypeStruct((n, D_MODEL), F32),
        compiler_params=_cparams(("parallel",)), name="merge_ln")(
            x2, ya, ob, gm, w_up_nsa.astype(BF16), w_out.astype(BF16),
            ln_g.reshape(1, -1).astype(F32), ln_b.reshape(1, -1).astype(F32))


BIG_NEG = -3.0e38


def _top_rows(s, k):
    rid = lax.broadcasted_iota(jnp.int32, s.shape, 0).astype(F32)
    rank = jnp.full(s.shape, float(k), F32)
    vals, idxs = [], []
    for r in range(k):
        m = jnp.max(s, axis=0, keepdims=True)
        idx = jnp.min(jnp.where(s == m, rid, float(s.shape[0])), axis=0, keepdims=True)
        hit = rid == idx
        rank = jnp.where(hit, float(r), rank)
        s = jnp.where(hit, BIG_NEG, s)
        vals.append(m)
        idxs.append(idx)
    return jnp.concatenate(vals, axis=0), jnp.concatenate(idxs, axis=0), rank


def _max_rounds(s, k, want_round=True):
    rnd = jnp.full(s.shape, float(k), F32) if want_round else None
    vals = []
    for r in range(k):
        m = jnp.max(s, axis=0, keepdims=True)
        eq = s == m
        if want_round:
            rnd = jnp.where(eq, float(r), rnd)
        s = jnp.where(eq, BIG_NEG, s)
        vals.append(m)
    return jnp.concatenate(vals, axis=0), rnd


def _pair_candidates(a, b):
    c8 = lax.broadcasted_iota(jnp.int32, (8, a.shape[1]), 0)
    parts = [a[0:1] + b[0:8], a[0:1] + b[8:16]]
    for r in range(1, 8):
        keep = PEER_TOPK // (r + 1)
        cand = a[r:r + 1] + b[0:8]
        parts.append(cand if keep >= 8 else jnp.where(c8 < keep, cand, BIG_NEG))
    parts.append(a[8:16] + b[0:1])
    return jnp.concatenate(parts, axis=0)


def _picks_per_rank(picked):
    rows = [jnp.sum(picked[0:16], axis=0, keepdims=True)]
    rows += [jnp.sum(picked[8 * (r + 1):8 * (r + 2)], axis=0, keepdims=True) for r in range(1, 8)]
    return jnp.concatenate(rows + [picked[72:80]], axis=0)


def _spread_by_rank(rank, per_rank):
    out = jnp.zeros(rank.shape, F32)
    for r in range(PEER_TOPK):
        out = out + jnp.where(rank == float(r), per_rank[r:r + 1], 0.0)
    return out


def _sort16_network():
    pairs, p = [], 1
    while p < 16:
        k = p
        while k >= 1:
            for j in range(k % p, 16 - k, 2 * k):
                for i in range(min(k, 16 - j - k)):
                    if (i + j) // (2 * p) == (i + j + k) // (2 * p):
                        pairs.append((i + j, i + j + k))
            k //= 2
        p *= 2
    return pairs


_SORT16 = _sort16_network()


def _sorted_top16(s):
    def exchange(v, i, j):
        v[i], v[j] = jnp.maximum(v[i], v[j]), jnp.minimum(v[i], v[j])

    v = [s[8 * k:8 * k + 8] for k in range(16)]
    for i, j in _SORT16:
        exchange(v, i, j)
    for shift in (4, 2, 1):
        v = [jnp.maximum(v[k], pltpu.roll(v[15 - k], shift, 0)) for k in range(16)]
        for d in (8, 4, 2, 1):
            for i in range(16):
                if i & d == 0:
                    exchange(v, i, i + d)
    return jnp.concatenate([x[0:1] for x in v], axis=0)


def _count_rows(vals, s, strict):
    rows = [jnp.broadcast_to(vals[r:r + 1], (8, s.shape[1])) for r in range(vals.shape[0])]
    assert len(rows) == 16
    outs = []
    for c in range(0, s.shape[0], 8):
        sc = s[c:c + 8]
        test = (lambda row: row > sc) if strict else (lambda row: row <= sc)
        c1 = test(rows[7])
        c2 = test(jnp.where(c1, rows[11], rows[3]))
        c3 = test(jnp.where(c1, jnp.where(c2, rows[13], rows[9]), jnp.where(c2, rows[5], rows[1])))
        hi = jnp.where(c2, jnp.where(c3, rows[14], rows[12]), jnp.where(c3, rows[10], rows[8]))
        lo = jnp.where(c2, jnp.where(c3, rows[6], rows[4]), jnp.where(c3, rows[2], rows[0]))
        c4 = test(jnp.where(c1, hi, lo))
        acc = (jnp.where(c1, 8.0, 0.0) + jnp.where(c2, 4.0, 0.0) + jnp.where(c3, 2.0, 0.0) + jnp.where(c4, 1.0, 0.0)
               + jnp.where(test(rows[15]), 1.0, 0.0))
        outs.append(acc)
    return jnp.concatenate(outs, axis=0)


def _route_fast(s0, s1):
    k = PEER_TOPK
    count = lambda v: jnp.sum(v, axis=0, keepdims=True)
    a, b = _sorted_top16(s0), _sorted_top16(s1)
    cand = _pair_candidates(a, b)
    best, _ = _max_rounds(cand, k, want_round=False)
    hit = cand >= best[k - 1:k]
    z = jnp.sum(jnp.where(hit, jnp.exp(cand - best[0:1]), 0.0), axis=0, keepdims=True)
    nrank = _picks_per_rank(jnp.where(hit, 1.0, 0.0))
    earns = jnp.concatenate([jnp.min(jnp.where(nrank >= float(v), a, -BIG_NEG), axis=0, keepdims=True)
                             for v in range(1, k + 1)], axis=0)
    n_i = _count_rows(earns, s0, strict=False)
    ties = lambda t, s: (count(jnp.where(t[:-1] == t[1:], 1.0, 0.0))
                         + jnp.abs(count(jnp.where(s >= t[k - 1:k], 1.0, 0.0)) - k))
    bad = ties(a, s0) + ties(b, s1) + jnp.abs(count(nrank) - k)
    return n_i, jnp.exp(s0 - a[0:1]) / z, _count_rows(b, s1, strict=True), jnp.exp(s1 - b[0:1]), bad


def _route_exact(s0, s1):
    k = PEER_TOPK
    a, _, rk0 = _top_rows(s0, k)
    b, _, rk1 = _top_rows(s1, k)
    cand = jnp.concatenate([a[r:r + 1] + b for r in range(k)], axis=0)
    best, bidx, _ = _top_rows(cand, k)
    brank = jnp.floor(bidx * (1.0 / k))
    z = jnp.sum(jnp.exp(best - best[0:1]), axis=0, keepdims=True)
    nrank = jnp.concatenate([jnp.sum(jnp.where(brank == float(r), 1.0, 0.0), axis=0, keepdims=True)
                             for r in range(k)], axis=0)
    return _spread_by_rank(rk0, nrank), jnp.exp(s0 - a[0:1]) / z, rk1, jnp.exp(s1 - b[0:1])


def _gelu_tanh(x):
    c = math.sqrt(2.0 / math.pi)
    return x / (1.0 + jnp.exp(x * (x * x * (-2.0 * c * 0.044715) - 2.0 * c)))


def _peer_body(x_ref, wq_ref, sk_ref, u_ref, v_ref, g_ref, b_ref, o_ref,
               xt_sc, acc_sc, n_sc, e0_sc, rk1_sc, e1_sc, p_sc, st_sc):
    e = pl.program_id(1)
    nk = PEER_NKEYS

    @pl.when(e == 0)
    def _():
        xt = x_ref[...].T.astype(BF16)
        xt_sc[...] = xt
        acc_sc[...] = jnp.zeros_like(acc_sc)
        qt = jnp.dot(wq_ref[...], xt, preferred_element_type=F32).astype(BF16)
        st_sc[...] = jnp.dot(sk_ref[...], qt, preferred_element_type=F32)

        @pl.loop(0, PEER_HEADS)
        def _(h):
            s0 = st_sc[pl.ds(pl.multiple_of(2 * h * nk, nk), nk), :]
            s1 = st_sc[pl.ds(pl.multiple_of((2 * h + 1) * nk, nk), nk), :]

            def put(n_i, e0, rk1, e1):
                n_sc[h] = n_i
                e0_sc[h] = e0
                rk1_sc[h] = rk1.astype(BF16)
                e1_sc[h] = e1.astype(BF16)

            n_i, e0, rk1, e1, bad = _route_fast(s0, s1)
            put(n_i, e0, rk1, e1)

            @pl.when(jnp.max(bad) > 0.0)
            def _():
                put(*_route_exact(s0, s1))

    rows = u_ref.shape[0] // nk
    ht = jnp.dot(u_ref[...], xt_sc[...], preferred_element_type=F32)
    act = _gelu_tanh(ht).astype(BF16)
    for ii in range(rows):
        w = jnp.zeros((nk, xt_sc.shape[1]), BF16)
        for h in range(PEER_HEADS):
            n_row = n_sc[h, pl.ds(e * rows + ii, 1), :].astype(BF16)
            e0_row = e0_sc[h, pl.ds(e * rows + ii, 1), :].astype(BF16)
            w = w + jnp.where(rk1_sc[h] < n_row, e0_row * e1_sc[h], 0)
        p_sc[ii * nk:(ii + 1) * nk, :] = w * act[ii * nk:(ii + 1) * nk]
    acc_sc[...] += _tn_dot(v_ref[...], p_sc[...])

    @pl.when(e == pl.num_programs(1) - 1)
    def _():
        o_ref[...] = _layer_norm(DN_ALPHA * x_ref[...] + acc_sc[...].T, g_ref[...], b_ref[...])


def _stage_peer(x1, w_q, subkeys, u_tab, v_tab, ln_g, ln_b, tt=512, et=1024):
    n = x1.shape[0]
    tt = min(tt, n)
    H, nk, half = PEER_HEADS, PEER_NKEYS, subkeys.shape[-1]
    wq_t = w_q.T.astype(BF16)
    eye = jnp.eye(2 * H, dtype=F32)
    sk = subkeys.reshape(2 * H, nk, half)
    sk_t = jnp.einsum('ckd,ce->cked', sk, eye).reshape(2 * H * nk, 2 * H * half).astype(BF16)
    ne = u_tab.shape[0]
    return pl.pallas_call(
        _peer_body, grid=(n // tt, ne // et),
        in_specs=[pl.BlockSpec((tt, D_MODEL), lambda t, e: (t, 0)),
                  _const_spec(wq_t.shape), _const_spec(sk_t.shape),
                  pl.BlockSpec((et, D_MODEL), lambda t, e: (e, 0)),
                  pl.BlockSpec((et, D_MODEL), lambda t, e: (e, 0)),
                  _const_spec((1, D_MODEL)), _const_spec((1, D_MODEL))],
        out_specs=pl.BlockSpec((tt, D_MODEL), lambda t, e: (t, 0)),
        out_shape=jax.ShapeDtypeStruct((n, D_MODEL), F32),
        scratch_shapes=[pltpu.VMEM((D_MODEL, tt), BF16), pltpu.VMEM((D_MODEL, tt), F32),
                        pltpu.VMEM((H, nk, tt), F32), pltpu.VMEM((H, nk, tt), F32),
                        pltpu.VMEM((H, nk, tt), BF16), pltpu.VMEM((H, nk, tt), BF16),
                        pltpu.VMEM((et, tt), BF16), pltpu.VMEM((2 * H * nk, tt), F32)],
        compiler_params=_cparams(("parallel", "arbitrary")), name="peer_ffn")(
            x1, wq_t, sk_t, u_tab.astype(BF16), v_tab.astype(BF16),
            ln_g.reshape(1, -1).astype(F32), ln_b.reshape(1, -1).astype(F32))


def _layer(x, w_in, b_in, ssm_a_re, ssm_a_im, ssm_log_dt, ssm_b_re, ssm_b_im, ssm_c_re, ssm_c_im, ssm_d,
           w_glu, w_up_ssm, nsa_pe_k, nsa_w1_k, nsa_w2_k, nsa_pe_v, nsa_w1_v, nsa_w2_v, w_up_nsa,
           w_out, ln1_g, ln1_b, peer_w_q, peer_subkeys, peer_u, peer_v, ln2_g, ln2_b):
    bsz, seq, _ = x.shape
    x2 = x.reshape(bsz * seq, D_MODEL)
    u, qp, kvc, kv, gates, gm = _stage_proj(x2, w_in, b_in, seq)
    ya = _stage_ssm(u, bsz, seq, ssm_a_re, ssm_a_im, ssm_log_dt, ssm_b_re, ssm_b_im, ssm_c_re, ssm_c_im,
                    ssm_d, w_glu, w_up_ssm)
    kcv = _stage_compress(kvc, bsz, seq, nsa_pe_k, nsa_w1_k, nsa_w2_k, nsa_pe_v, nsa_w1_v, nsa_w2_v)
    ob = _stage_attn(qp, kcv, kv, gates, bsz, seq)
    x1 = _stage_merge(x2, ya, ob, gm, w_up_nsa, w_out, ln1_g, ln1_b)
    out = _stage_peer(x1, peer_w_q, peer_subkeys, peer_u, peer_v, ln2_g, ln2_b)
    return out.reshape(bsz, seq, D_MODEL)


def kernel(x, w_in, b_in, ssm_a_re, ssm_a_im, ssm_log_dt, ssm_b_re, ssm_b_im, ssm_c_re, ssm_c_im, ssm_d, w_glu,
           w_up_ssm, nsa_pe_k, nsa_w1_k, nsa_w2_k, nsa_pe_v, nsa_w1_v, nsa_w2_v, w_up_nsa, w_out, ln1_g, ln1_b,
           peer_w_q, peer_subkeys, peer_u, peer_v, ln2_g, ln2_b):
    params = (w_in, b_in, ssm_a_re, ssm_a_im, ssm_log_dt, ssm_b_re, ssm_b_im, ssm_c_re, ssm_c_im, ssm_d, w_glu,
              w_up_ssm, nsa_pe_k, nsa_w1_k, nsa_w2_k, nsa_pe_v, nsa_w1_v, nsa_w2_v, w_up_nsa, w_out, ln1_g, ln1_b,
              peer_w_q, peer_subkeys, peer_u, peer_v, ln2_g, ln2_b)
    for layer in range(w_in.shape[0]):
        x = _layer(x, *[p[layer] for p in params])
    return x
```

```python
import functools
import math

import jax
import jax.numpy as jnp
import numpy as np
from jax import lax
from jax.experimental import pallas as pl
from jax.experimental.pallas import tpu as pltpu

F32 = jnp.float32
BF16 = jnp.bfloat16

D_MODEL = 1024
SSM_WIDTH = 512
SSM_GROUP = 16
SSM_GROUPS = 32
SSM_STATE = 64
SSM_CHUNK = 8
LANE_GROUPS = 8
N_LANE_TILES = SSM_WIDTH // 128
NSA_HEADS = 8
NSA_KV_GROUPS = 2
NSA_REP = NSA_HEADS // NSA_KV_GROUPS
NSA_HEAD_DIM = 64
KV_WIDTH = NSA_KV_GROUPS * NSA_HEAD_DIM
CMP_BLOCK = 32
CMP_STRIDE = 16
CMP_HIDDEN = 128
SEL_BLOCK = 64
SEL_TOPK = 16
WINDOW = 512
Q_BLOCK = 256
FORCE_BONUS = 1.0e4
PEER_HEADS = 8
PEER_NKEYS = 128
PEER_EXPERTS = PEER_NKEYS * PEER_NKEYS
PEER_TOPK = 16
DN_ALPHA = 2.0 ** 0.25
LN_EPS = 1e-5
NEG = -1e30
VMEM_LIMIT = 56 * 1024 * 1024


def _cparams(sem):
    return pltpu.CompilerParams(dimension_semantics=sem, vmem_limit_bytes=VMEM_LIMIT)


def _const_spec(shape):
    nd = len(shape)
    return pl.BlockSpec(shape, lambda *_: (0,) * nd, pipeline_mode=pl.Buffered(1))


def _layer_norm(z, g, b):
    mu = jnp.mean(z, axis=-1, keepdims=True)
    var = jnp.mean(jnp.square(z - mu), axis=-1, keepdims=True)
    return (z - mu) * lax.rsqrt(var + LN_EPS) * g + b


def _nt_dot(a, b):
    return lax.dot_general(a, b, (((1,), (1,)), ((), ())), preferred_element_type=F32)


def _tn_dot(a, b):
    return lax.dot_general(a, b, (((0,), (0,)), ((), ())), preferred_element_type=F32)


KV_REC = 640
KS_OFF, VS_OFF, KW_OFF, VW_OFF = 0, 256, 384, 512
POS_HI, POS_LO, ONE_LANE, BLOCK_LANE0 = 64, 65, 64, 128
assert SEL_BLOCK == 64


def _position_features(pos, lane):
    hi, lo = pos >> 6, pos & 63
    f = jnp.where(lane == KS_OFF + POS_HI, hi, 0) + jnp.where(lane == KW_OFF + POS_HI, hi, 0)
    f = f + jnp.where(lane == KS_OFF + POS_LO, lo, 0) + jnp.where(lane == KW_OFF + POS_LO, lo, 0)
    f = f + jnp.where(lane == KS_OFF + BLOCK_LANE0 + hi, 1, 0)
    f = f + jnp.where(lane == VS_OFF + ONE_LANE, 1, 0) + jnp.where(lane == VW_OFF + ONE_LANE, 1, 0)
    return f.astype(F32)


def _proj_body(seq, x_ref, wu, bu, wq, bq, wc, bc, wk, bk, wg, bg, wm, bm, aux_ref,
               u_o, q_o, c_o, k_o, g_o, m_o):
    xb = x_ref[...].astype(BF16)
    tm = x_ref.shape[0]
    dk = NSA_HEAD_DIM

    def lin(w, b):
        return jnp.dot(xb, w[...], preferred_element_type=F32) + b[...]

    def halves(a, tile):
        t = a[:, tile * 128:(tile + 1) * 128]
        return t, pltpu.roll(t, dk, 1)

    low = lax.broadcasted_iota(jnp.int32, (tm, 128), 1) < dk
    u_o[...] = lin(wu, bu)
    q = lin(wq, bq) * (dk ** -0.5)
    for tile in range(NSA_HEADS // 2):
        for half, piece in enumerate(halves(q, tile)):
            h = 2 * tile + half
            q_o[:, h * 128:(h + 1) * 128] = jnp.where(low, piece, aux_ref[h:h + 1, :]).astype(BF16)
    c_o[...] = lin(wc, bc)
    kv = lin(wk, bk)
    parts = [halves(kv, tile) for tile in range(4)]
    zero = jnp.zeros((tm, 128), F32)
    recs = []
    for g in range(NSA_KV_GROUPS):
        k_s, v_s, k_w, v_w = (jnp.where(low, parts[tile][g], 0.0) for tile in range(4))
        recs += [k_s, zero, v_s, k_w, v_w]
    pos0 = (pl.program_id(0) % (seq // tm)) * tm
    shape = (tm, NSA_KV_GROUPS * KV_REC)
    lane = lax.broadcasted_iota(jnp.int32, shape, 1)
    lane = jnp.where(lane >= KV_REC, lane - KV_REC, lane)
    feat = _position_features(pos0 + lax.broadcasted_iota(jnp.int32, shape, 0), lane)
    k_o[...] = (jnp.concatenate(recs, axis=1) + feat).astype(BF16)
    g_o[...] = lin(wg, bg)
    m_o[...] = jax.nn.sigmoid(lin(wm, bm)).astype(m_o.dtype)


def _stage_proj(x2, w_in, b_in, seq):
    n = x2.shape[0]
    tm = 256
    o0 = SSM_WIDTH
    o1 = o0 + NSA_HEADS * NSA_HEAD_DIM
    o2 = o1 + 6 * KV_WIDTH
    o3 = o2 + 3 * NSA_HEADS
    w_u, b_u = w_in[:, :o0], b_in[:o0]
    w_q, b_q = w_in[:, o0:o1], b_in[o0:o1]
    w_kv, b_kv = w_in[:, o1:o2], b_in[o1:o2]
    w_g, b_g = w_in[:, o2:o3], b_in[o2:o3]
    w_m, b_m = w_in[:, o3:], b_in[o3:]

    slope = 2.0 ** -(np.arange(NSA_HEADS, dtype=np.float32) + 1.0)
    q_aux = np.zeros((NSA_HEADS, 128), np.float32)
    q_aux[:, POS_HI] = slope * 64.0
    q_aux[:, POS_LO] = slope

    def pad_g(a):
        a = a.reshape(a.shape[:-1] + (NSA_KV_GROUPS, 3 * NSA_REP))
        a = jnp.concatenate([a, jnp.zeros(a.shape[:-1] + (128 - 3 * NSA_REP,), a.dtype)], axis=-1)
        return a.reshape(a.shape[:-2] + (NSA_KV_GROUPS * 128,))

    ws = [w_u, w_q, w_kv[:, :2 * KV_WIDTH], w_kv[:, 2 * KV_WIDTH:], pad_g(w_g), w_m]
    bs = [b_u, b_q, b_kv[:2 * KV_WIDTH], b_kv[2 * KV_WIDTH:], pad_g(b_g), b_m]
    odt = [F32, BF16, F32, BF16, F32, BF16]
    widths = [SSM_WIDTH, NSA_HEADS * 128, 2 * KV_WIDTH, NSA_KV_GROUPS * KV_REC, NSA_KV_GROUPS * 128, 2 * D_MODEL]
    args, in_specs = [x2], [pl.BlockSpec((tm, D_MODEL), lambda i: (i, 0))]
    for w, b in zip(ws, bs):
        args += [w.astype(BF16), b.reshape(1, -1).astype(F32)]
        in_specs += [_const_spec(w.shape), _const_spec((1, w.shape[1]))]
    args.append(jnp.asarray(q_aux))
    in_specs.append(_const_spec(q_aux.shape))
    out_shape = [jax.ShapeDtypeStruct((n, w), dt) for w, dt in zip(widths, odt)]
    out_specs = [pl.BlockSpec((tm, w), lambda i: (i, 0)) for w in widths]
    return pl.pallas_call(
        functools.partial(_proj_body, seq), grid=(n // tm,), in_specs=in_specs, out_specs=out_specs, out_shape=out_shape,
        compiler_params=_cparams(("parallel",)), name="in_proj")(*args)


def _ssm_weights(a_re, a_im, log_dt, b_re, b_im, c_re, c_im):
    L, G, P, H, A, J = SSM_CHUNK, SSM_GROUPS, SSM_STATE, SSM_GROUP, LANE_GROUPS, N_LANE_TILES
    lam = lax.complex(a_re.astype(F32), a_im.astype(F32))
    dt = jnp.exp(log_dt.astype(F32))[:, None]
    lam_bar = jnp.exp(lam * dt)
    b_bar = ((lam_bar - 1.0) / lam)[:, :, None] * lax.complex(b_re.astype(F32), b_im.astype(F32))
    c = lax.complex(c_re.astype(F32), c_im.astype(F32))
    k = jnp.arange(L + 1, dtype=F32)
    pw = jnp.exp((lam * dt)[None] * k[:, None, None])
    def spread(x, n_inner):
        src = np.arange(x.shape[2])
        dst = np.arange(x.shape[2] * A)
        sel = ((src[:, None] // n_inner == dst[None, :] // (A * n_inner))
               & (src[:, None] % n_inner == dst[None, :] % n_inner))
        return jnp.einsum('jrk,kc->jrc', x, jnp.asarray(sel, F32)), (dst % (A * n_inner)) // n_inner

    def group_mask(row_group, col_group):
        return jnp.asarray(row_group[:, None] == col_group[None, :], F32)

    in_rows = (np.arange(L * 128) % 128) // H
    st_rows = (np.arange(2 * A * P) % (A * P)) // P
    kern = jnp.real(jnp.einsum('ghp,kgp,gpj->kghj', c, pw[:L], b_bar))
    s_i, t_i = jnp.arange(L)[:, None], jnp.arange(L)[None, :]
    tau = jnp.clip(t_i - s_i, 0, L - 1)
    causal = (t_i >= s_i).astype(F32)
    k_st = (kern[tau] * causal[:, :, None, None, None]).reshape(L, L, J, A, H, H)
    k_st = k_st.transpose(2, 0, 3, 5, 1, 4).reshape(J, L * 128, L * H)
    w_intra, cg = spread(k_st, H)
    w_intra = w_intra * group_mask(in_rows, cg)
    q = (pw[:L][::-1][:, :, :, None] * b_bar[None]).reshape(L, J, A, P, H)
    q = jnp.stack([jnp.real(q), jnp.imag(q)], axis=0)
    q = q.transpose(2, 1, 3, 5, 0, 4).reshape(J, L * 128, 2 * P)
    m2, cg = spread(q, P)
    m2 = m2 * group_mask(in_rows, cg)
    cl = (c[None] * pw[1:][:, :, None, :]).reshape(L, J, A, H, P)
    cl = jnp.stack([jnp.real(cl), -jnp.imag(cl)], axis=0)
    cl = cl.transpose(2, 0, 3, 5, 1, 4).reshape(J, 2 * A * P, L * H)
    m1, cg = spread(cl, H)
    m1 = m1 * group_mask(st_rows, cg)
    w2 = jnp.concatenate([w_intra, m1], axis=1)
    lam8 = pw[L].reshape(G * P)
    return m2.astype(BF16), w2.astype(BF16), jnp.real(lam8), jnp.imag(lam8)


def _split_lane_tiles(src_ref, dst_sc):
    for j in range(dst_sc.shape[0]):
        dst_sc[j] = src_ref[:, j * 128:(j + 1) * 128]


def _chunk_inputs(u_sc, j, rows):
    return jnp.concatenate([u_sc[j, pl.ds(s, rows, stride=SSM_CHUNK), :] for s in range(SSM_CHUNK)], axis=1)


def _ssm_state_body(u_ref, m2_ref, zre_ref, zim_ref, u_sc):
    half = LANE_GROUPS * SSM_STATE
    _split_lane_tiles(u_ref, u_sc)
    for j in range(N_LANE_TILES):
        z = jnp.dot(_chunk_inputs(u_sc, j, zre_ref.shape[0]).astype(BF16), m2_ref[j], preferred_element_type=F32)
        zre_ref[:, j * half:(j + 1) * half] = z[:, :half]
        zim_ref[:, j * half:(j + 1) * half] = z[:, half:]


def _ssm_scan_body(zre_ref, zim_ref, lre_ref, lim_ref, xre_ref, xim_ref):
    lr, li = lre_ref[...], lim_ref[...]

    def step(c, carry):
        xr, xi = carry
        xre_ref[pl.ds(c, 1), :] = xr
        xim_ref[pl.ds(c, 1), :] = xi
        return lr * xr - li * xi + zre_ref[pl.ds(c, 1), :], lr * xi + li * xr + zim_ref[pl.ds(c, 1), :]

    zero = jnp.zeros(lr.shape, F32)
    lax.fori_loop(0, zre_ref.shape[0], step, (zero, zero))


def _ssm_out_body(u_ref, xre_ref, xim_ref, w2_ref, d_ref, wglu_ref, wup_ref, o_ref, u_sc, y_sc):
    half = LANE_GROUPS * SSM_STATE
    rows = xre_ref.shape[0]
    _split_lane_tiles(u_ref, u_sc)
    for j in range(N_LANE_TILES):
        lhs = jnp.concatenate(
            [_chunk_inputs(u_sc, j, rows), xre_ref[:, j * half:(j + 1) * half], xim_ref[:, j * half:(j + 1) * half]],
            axis=1).astype(BF16)
        yj = jnp.dot(lhs, w2_ref[j], preferred_element_type=F32)
        for t in range(SSM_CHUNK):
            y_sc[j, pl.ds(t, rows, stride=SSM_CHUNK), :] = yj[:, t * 128:(t + 1) * 128]
    y = jnp.concatenate([y_sc[j] for j in range(N_LANE_TILES)], axis=1) + d_ref[...] * u_ref[...]
    gl = jnp.dot(jax.nn.gelu(y).astype(BF16), wglu_ref[...], preferred_element_type=F32)
    v = gl[:, :SSM_WIDTH] * jax.nn.sigmoid(gl[:, SSM_WIDTH:])
    o_ref[...] = jnp.dot(v.astype(BF16), wup_ref[...], preferred_element_type=F32).astype(o_ref.dtype)


def _stage_ssm(u, bsz, seq, a_re, a_im, log_dt, b_re, b_im, c_re, c_im, d_skip, w_glu, w_up):
    n = bsz * seq
    L = SSM_CHUNK
    nck = seq // L
    gp = SSM_GROUPS * SSM_STATE
    m2, w2, l8re, l8im = _ssm_weights(a_re, a_im, log_dt, b_re, b_im, c_re, c_im)
    rs = min(nck, 512)
    zre, zim = pl.pallas_call(
        _ssm_state_body, grid=(bsz, nck // rs),
        in_specs=[pl.BlockSpec((rs * L, SSM_WIDTH), lambda b, i: (b * (nck // rs) + i, 0)),
                  _const_spec(m2.shape)],
        out_specs=[pl.BlockSpec((rs, gp), lambda b, i: (i, b))] * 2,
        out_shape=[jax.ShapeDtypeStruct((nck, bsz * gp), F32)] * 2,
        scratch_shapes=[pltpu.VMEM((N_LANE_TILES, rs * L, 128), F32)],
        compiler_params=_cparams(("parallel", "parallel")), name="ssm_chunk_state")(u, m2)
    cw = 1024
    ncol = bsz * gp // cw
    lre = jnp.tile(l8re, bsz).reshape(1, bsz * gp)
    lim = jnp.tile(l8im, bsz).reshape(1, bsz * gp)
    col = pl.BlockSpec((nck, cw), lambda i: (0, i))
    lam = pl.BlockSpec((1, cw), lambda i: (0, i))
    xre, xim = pl.pallas_call(
        _ssm_scan_body, grid=(ncol,), in_specs=[col, col, lam, lam], out_specs=[col, col],
        out_shape=[jax.ShapeDtypeStruct((nck, bsz * gp), F32)] * 2,
        compiler_params=_cparams(("parallel",)), name="ssm_carry_scan")(zre, zim, lre, lim)
    ro = min(nck, 128)
    d8 = d_skip.reshape(1, SSM_WIDTH).astype(F32)
    return pl.pallas_call(
        _ssm_out_body, grid=(bsz, nck // ro),
        in_specs=[pl.BlockSpec((ro * L, SSM_WIDTH), lambda b, i: (b * (nck // ro) + i, 0)),
                  pl.BlockSpec((ro, gp), lambda b, i: (i, b)),
                  pl.BlockSpec((ro, gp), lambda b, i: (i, b)),
                  _const_spec(w2.shape), _const_spec((1, SSM_WIDTH)),
                  _const_spec(w_glu.shape), _const_spec(w_up.shape)],
        out_specs=pl.BlockSpec((ro * L, D_MODEL), lambda b, i: (b * (nck // ro) + i, 0)),
        out_shape=jax.ShapeDtypeStruct((n, D_MODEL), BF16),
        scratch_shapes=[pltpu.VMEM((N_LANE_TILES, ro * L, 128), F32)] * 2,
        compiler_params=_cparams(("parallel", "parallel")), name="ssm_out")(
            u, xre, xim, w2, d8, w_glu.astype(BF16), w_up.astype(BF16))


def _compress_body(kv_ref, pea_ref, peb_ref, w1a_ref, w1b_ref, w2_ref, o_ref, kv_sc):
    nrow = o_ref.shape[0]
    _split_lane_tiles(kv_ref, kv_sc)
    f = jnp.concatenate([kv_sc[j, pl.ds(l, nrow, stride=CMP_STRIDE), :]
                         for l in range(CMP_STRIDE) for j in range(kv_sc.shape[0])], axis=1)
    a = jnp.dot((f + pea_ref[...]).astype(BF16), w1a_ref[...], preferred_element_type=F32)
    b = jnp.dot((f + peb_ref[...]).astype(BF16), w1b_ref[...], preferred_element_type=F32)
    pre = a + pltpu.roll(b, b.shape[0] - 1, 0)
    hid = jax.nn.gelu(pre)
    out = jnp.dot(hid.astype(BF16), w2_ref[...], preferred_element_type=F32)
    lane = lax.broadcasted_iota(jnp.int32, out.shape, 1) & 255
    c_end = lax.broadcasted_iota(jnp.int32, out.shape, 0) * CMP_STRIDE + (CMP_BLOCK - 1)
    feat = jnp.where(lane == POS_HI, c_end >> 6, 0) + jnp.where(lane == POS_LO, c_end & 63, 0)
    o_ref[...] = (out + feat.astype(F32)).astype(BF16)


def _stage_compress(kvc, bsz, seq, pe_k, w1_k, w2_k, pe_v, w1_v, w2_v):
    hl = CMP_STRIDE
    nrow = seq // hl
    G, dk, hid = NSA_KV_GROUPS, NSA_HEAD_DIM, CMP_HIDDEN
    eye = jnp.eye(2 * G, dtype=F32)

    def big_w1(lo):
        wk = w1_k.reshape(CMP_BLOCK, dk, hid)[lo:lo + hl]
        wv = w1_v.reshape(CMP_BLOCK, dk, hid)[lo:lo + hl]
        w = jnp.stack([wk, wk, wv, wv], axis=1)
        return jnp.einsum('lcdj,ce->lcdej', w, eye).reshape(hl * 2 * G * dk, 2 * G * hid)

    def big_pe(lo):
        pk, pv = pe_k[lo:lo + hl], pe_v[lo:lo + hl]
        return jnp.stack([pk, pk, pv, pv], axis=1).reshape(1, hl * 2 * G * dk)

    w2 = jnp.stack([w2_k, w2_k, w2_v, w2_v], axis=0)
    w2 = jnp.concatenate([w2, jnp.zeros((2 * G, hid, 128 - dk), w2.dtype)], axis=-1)
    slot = eye[np.array([kv * G + g for g in range(G) for kv in range(2)])].T
    w2 = jnp.einsum('cjd,ce->cjed', w2, slot).reshape(2 * G * hid, 2 * G * 128)
    width = hl * 2 * G * dk
    return pl.pallas_call(
        _compress_body, grid=(bsz,),
        in_specs=[pl.BlockSpec((seq, 2 * G * dk), lambda b: (b, 0)),
                  _const_spec((1, width)), _const_spec((1, width)),
                  _const_spec((width, 2 * G * hid)), _const_spec((width, 2 * G * hid)),
                  _const_spec((2 * G * hid, 2 * G * 128))],
        out_specs=pl.BlockSpec((nrow, 2 * G * 128), lambda b: (b, 0)),
        out_shape=jax.ShapeDtypeStruct((bsz * nrow, 2 * G * 128), BF16),
        scratch_shapes=[pltpu.VMEM((2 * G * dk // 128, seq, 128), F32)],
        compiler_params=_cparams(("parallel",)), name="nsa_compress")(
            kvc, big_pe(0).astype(F32), big_pe(hl).astype(F32),
            big_w1(0).astype(BF16), big_w1(hl).astype(BF16), w2.astype(BF16))


MASK_BIG = 2.0 ** 100
SEL_TILE = 512


def _nsa_body(q_ref, kcv_ref, kv_ref, g_ref, ov_ref, rel_ref, relc_ref, o_ref, acc, s_buf, p_buf, a_buf):
    i = pl.program_id(2)
    qs = i * Q_BLOCK
    seq = kv_ref.shape[0]
    rows = NSA_REP * Q_BLOCK
    dk = NSA_HEAD_DIM
    qh = [q_ref[:, r * 128:(r + 1) * 128] for r in range(NSA_REP)]
    q_all = jnp.concatenate(qh, axis=0)

    kc = kcv_ref[:, :128]
    vc = kcv_ref[:, 128:]
    vis_c = relc_ref[...] >= -qs
    s = jnp.where(vis_c, _nt_dot(q_all, kc), NEG)
    p = jnp.exp(s - jnp.max(s, axis=-1, keepdims=True)) * jnp.where(vis_c, 1.0, 0.0)
    p = (p / jnp.maximum(jnp.sum(p, axis=-1, keepdims=True), 1e-30)).astype(BF16)
    o_cmp = jnp.dot(p, vc, preferred_element_type=F32)

    wlen = min(WINDOW + Q_BLOCK, seq)
    w0 = pl.multiple_of(jnp.maximum(qs + Q_BLOCK - wlen, 0), Q_BLOCK)
    rel_w = rel_ref[...]
    s = _nt_dot(q_all, kv_ref[pl.ds(w0, wlen), KW_OFF:KW_OFF + 128])
    in_band = pltpu.bitcast(rel_w - (w0 - qs), jnp.uint32) < WINDOW
    s = jnp.where(in_band, s, NEG)
    p_w = jnp.exp(s - jnp.max(s, axis=-1, keepdims=True)).astype(BF16)
    a_win = jnp.dot(p_w, kv_ref[pl.ds(w0, wlen), VW_OFF:VW_OFF + 128], preferred_element_type=F32)

    p_heads = jnp.concatenate([p[r * Q_BLOCK:(r + 1) * Q_BLOCK] for r in range(NSA_REP)], axis=1)
    imp = jnp.dot(p_heads, ov_ref[...], preferred_element_type=F32)
    coli = lax.broadcasted_iota(jnp.int32, (Q_BLOCK, 128), 1)
    t_q = qs + lax.broadcasted_iota(jnp.int32, (Q_BLOCK, 128), 0)
    cur = t_q >> 6
    forced = (coli == 0) | (coli == cur) | (coli == cur - 1)
    score = jnp.where(coli * SEL_BLOCK <= t_q, imp + jnp.where(forced, FORCE_BONUS, 0.0), NEG)
    nsl = seq // SEL_BLOCK
    st = score.T[:nsl]
    groups = [st[8 * v:8 * v + 8] for v in range(nsl // 8)]
    jsub = lax.broadcasted_iota(jnp.int32, (8, Q_BLOCK), 0)
    rank = [jnp.zeros((8, Q_BLOCK), F32) for _ in groups]
    for k in range(nsl):
        rk = st[k:k + 1, :]
        for v, sv in enumerate(groups):
            if v > k // 8:
                one = jnp.where(rk >= sv, 1.0, 0.0)
            elif v < k // 8:
                one = jnp.where(rk > sv, 1.0, 0.0)
            else:
                one = jnp.where(rk > sv, 1.0, jnp.where(jsub > k % 8, jnp.where(rk == sv, 1.0, 0.0), 0.0))
            rank[v] = rank[v] + one
    drop_t = jnp.where(jnp.concatenate(rank, axis=0) < float(min(SEL_TOPK, nsl)), 0.0, -MASK_BIG)
    if nsl < 128:
        drop_t = jnp.concatenate([drop_t, jnp.zeros((128 - nsl, Q_BLOCK), F32)], axis=0)
    drop = drop_t.T.astype(BF16)
    q_sel = jnp.concatenate([jnp.concatenate([qh[r], drop], axis=1) for r in range(NSA_REP)], axis=0)

    n_tiles = (qs + Q_BLOCK + SEL_TILE - 1) // SEL_TILE

    def scores(kt):
        k0 = pl.multiple_of(kt * SEL_TILE, SEL_TILE)
        return _nt_dot(q_sel, kv_ref[pl.ds(k0, SEL_TILE), KS_OFF:KS_OFF + 256])

    def values(kt):
        k0 = pl.multiple_of(kt * SEL_TILE, SEL_TILE)
        return kv_ref[pl.ds(k0, SEL_TILE), VS_OFF:VS_OFF + 128]

    def trip(diagonal, kt, m_old):
        if not diagonal:
            s_next = scores(kt + 1)
        acc[...] = a_buf[...] * acc[...] + jnp.dot(p_buf[...], values(jnp.maximum(kt - 1, 0)),
                                                   preferred_element_type=F32)
        s = s_buf[...]
        if diagonal:
            s = jnp.where(rel_ref[:, :SEL_TILE] >= kt * SEL_TILE - qs, s, NEG)
        m_new = jnp.maximum(m_old, jnp.max(s, axis=-1, keepdims=True))
        p_buf[...] = jnp.exp(s - m_new).astype(BF16)
        a_buf[...] = jnp.exp(m_old - m_new)
        if not diagonal:
            s_buf[...] = s_next
        return m_new

    acc[...] = jnp.zeros_like(acc)
    p_buf[...] = jnp.zeros_like(p_buf)
    a_buf[...] = jnp.ones_like(a_buf)
    s_buf[...] = scores(0)
    m_s = lax.fori_loop(0, n_tiles - 1, functools.partial(trip, False), jnp.full((rows, 1), NEG, F32))
    trip(True, n_tiles - 1, m_s)
    a_sel = a_buf[...] * acc[...] + jnp.dot(p_buf[...], values(n_tiles - 1), preferred_element_type=F32)

    gate = jax.nn.sigmoid(g_ref[...])
    outs = []
    for r in range(NSA_REP):
        rr = slice(r * Q_BLOCK, (r + 1) * Q_BLOCK)
        o_s = a_sel[rr, :dk] / a_sel[rr, ONE_LANE:ONE_LANE + 1]
        o_w = a_win[rr, :dk] / a_win[rr, ONE_LANE:ONE_LANE + 1]
        outs.append(gate[:, 3 * r:3 * r + 1] * o_cmp[rr, :dk] + gate[:, 3 * r + 1:3 * r + 2] * o_s
                    + gate[:, 3 * r + 2:3 * r + 3] * o_w)
    o_ref[...] = jnp.concatenate(outs, axis=1).astype(o_ref.dtype)


def _stage_attn(qp, kcv, kv, gates, bsz, seq):
    n = bsz * seq
    nq = seq // Q_BLOCK
    ncmp = seq // CMP_STRIDE
    nsl = seq // SEL_BLOCK
    assert seq % SEL_TILE == 0
    c_start = np.arange(ncmp) * CMP_STRIDE
    s_start = np.arange(nsl) * SEL_BLOCK
    ov = ((c_start[:, None] < s_start[None, :] + SEL_BLOCK) & (c_start[:, None] + CMP_BLOCK > s_start[None, :]))
    ov = np.pad(ov.astype(np.float32), ((0, 0), (0, 128 - nsl)))
    ov[ncmp - 1] = 0.0
    ov = jnp.asarray(np.tile(ov, (NSA_REP, 1)), BF16)
    width = NSA_REP * NSA_HEAD_DIM
    rows = NSA_REP * Q_BLOCK
    wlen = min(WINDOW + Q_BLOCK, seq)
    assert wlen >= SEL_TILE
    offs = jnp.arange(rows, dtype=jnp.int32)[:, None] % Q_BLOCK
    rel = offs - jnp.arange(wlen, dtype=jnp.int32)[None, :]
    relc = offs - (jnp.arange(ncmp, dtype=jnp.int32)[None, :] * CMP_STRIDE + (CMP_BLOCK - 1))
    return pl.pallas_call(
        _nsa_body, grid=(bsz, NSA_KV_GROUPS, nq),
        in_specs=[pl.BlockSpec((Q_BLOCK, NSA_REP * 128), lambda b, g, i: (b * nq + i, g)),
                  pl.BlockSpec((ncmp, 256), lambda b, g, i: (b, g)),
                  pl.BlockSpec((seq, KV_REC), lambda b, g, i: (b, g)),
                  pl.BlockSpec((Q_BLOCK, 128), lambda b, g, i: (b * nq + i, g)),
                  _const_spec(ov.shape), _const_spec(rel.shape), _const_spec(relc.shape)],
        out_specs=pl.BlockSpec((Q_BLOCK, width), lambda b, g, i: (b * nq + i, g)),
        out_shape=jax.ShapeDtypeStruct((n, NSA_KV_GROUPS * width), BF16),
        scratch_shapes=[pltpu.VMEM((rows, 128), F32), pltpu.VMEM((rows, SEL_TILE), F32),
                        pltpu.VMEM((rows, SEL_TILE), BF16), pltpu.VMEM((rows, 1), F32)],
        compiler_params=_cparams(("parallel", "parallel", "arbitrary")), name="nsa_attention")(
            qp, kcv, kv, gates, ov, rel, relc)


def _merge_body(x_ref, ya_ref, ob_ref, gm_ref, wnsa_ref, wout_ref, g_ref, b_ref, o_ref):
    yb = jnp.dot(ob_ref[...], wnsa_ref[...], preferred_element_type=F32)
    gm = gm_ref[...].astype(F32)
    mix_in = gm[:, :D_MODEL] * ya_ref[...].astype(F32) + gm[:, D_MODEL:] * yb
    mix = jnp.dot(mix_in.astype(BF16), wout_ref[...], preferred_element_type=F32)
    o_ref[...] = _layer_norm(DN_ALPHA * x_ref[...] + mix, g_ref[...], b_ref[...])


def _stage_merge(x2, ya, ob, gm, w_up_nsa, w_out, ln_g, ln_b):
    n = x2.shape[0]
    tm = 512
    row = lambda w: pl.BlockSpec((tm, w), lambda i: (i, 0))
    return pl.pallas_call(
        _merge_body, grid=(n // tm,),
        in_specs=[row(D_MODEL), row(D_MODEL), row(ob.shape[1]), row(2 * D_MODEL),
                  _const_spec(w_up_nsa.shape), _const_spec(w_out.shape),
                  _const_spec((1, D_MODEL)), _const_spec((1, D_MODEL))],
        out_specs=row(D_MODEL), out_shape=jax.ShapeDtypeStruct((n, D_MODEL), F32),
        compiler_params=_cparams(("parallel",)), name="merge_ln")(
            x2, ya, ob, gm, w_up_nsa.astype(BF16), w_out.astype(BF16),
            ln_g.reshape(1, -1).astype(F32), ln_b.reshape(1, -1).astype(F32))


BIG_NEG = -3.0e38


def _top_rows(s, k):
    rid = lax.broadcasted_iota(jnp.int32, s.shape, 0).astype(F32)
    rank = jnp.full(s.shape, float(k), F32)
    vals, idxs = [], []
    for r in range(k):
        m = jnp.max(s, axis=0, keepdims=True)
        idx = jnp.min(jnp.where(s == m, rid, float(s.shape[0])), axis=0, keepdims=True)
        hit = rid == idx
        rank = jnp.where(hit, float(r), rank)
        s = jnp.where(hit, BIG_NEG, s)
        vals.append(m)
        idxs.append(idx)
    return jnp.concatenate(vals, axis=0), jnp.concatenate(idxs, axis=0), rank


def _max_rounds(s, k, want_round=True):
    rnd = jnp.full(s.shape, float(k), F32) if want_round else None
    vals = []
    for r in range(k):
        m = jnp.max(s, axis=0, keepdims=True)
        eq = s == m
        if want_round:
            rnd = jnp.where(eq, float(r), rnd)
        s = jnp.where(eq, BIG_NEG, s)
        vals.append(m)
    return jnp.concatenate(vals, axis=0), rnd


def _pair_candidates(a, b):
    c8 = lax.broadcasted_iota(jnp.int32, (8, a.shape[1]), 0)
    parts = [a[0:1] + b[0:8], a[0:1] + b[8:16]]
    for r in range(1, 8):
        keep = PEER_TOPK // (r + 1)
        cand = a[r:r + 1] + b[0:8]
        parts.append(cand if keep >= 8 else jnp.where(c8 < keep, cand, BIG_NEG))
    parts.append(a[8:16] + b[0:1])
    return jnp.concatenate(parts, axis=0)


def _picks_per_rank(picked):
    rows = [jnp.sum(picked[0:16], axis=0, keepdims=True)]
    rows += [jnp.sum(picked[8 * (r + 1):8 * (r + 2)], axis=0, keepdims=True) for r in range(1, 8)]
    return jnp.concatenate(rows + [picked[72:80]], axis=0)


def _spread_by_rank(rank, per_rank):
    out = jnp.zeros(rank.shape, F32)
    for r in range(PEER_TOPK):
        out = out + jnp.where(rank == float(r), per_rank[r:r + 1], 0.0)
    return out


def _sort16_network():
    pairs, p = [], 1
    while p < 16:
        k = p
        while k >= 1:
            for j in range(k % p, 16 - k, 2 * k):
                for i in range(min(k, 16 - j - k)):
                    if (i + j) // (2 * p) == (i + j + k) // (2 * p):
                        pairs.append((i + j, i + j + k))
            k //= 2
        p *= 2
    return pairs


_SORT16 = _sort16_network()


def _sorted_top16(s):
    def exchange(v, i, j):
        v[i], v[j] = jnp.maximum(v[i], v[j]), jnp.minimum(v[i], v[j])

    v = [s[8 * k:8 * k + 8] for k in range(16)]
    for i, j in _SORT16:
        exchange(v, i, j)
    for shift in (4, 2, 1):
        v = [jnp.maximum(v[k], pltpu.roll(v[15 - k], shift, 0)) for k in range(16)]
        for d in (8, 4, 2, 1):
            for i in range(16):
                if i & d == 0:
                    exchange(v, i, i + d)
    return jnp.concatenate([x[0:1] for x in v], axis=0)


def _count_rows(vals, s, strict):
    rows = [jnp.broadcast_to(vals[r:r + 1], (8, s.shape[1])) for r in range(vals.shape[0])]
    assert len(rows) == 16
    outs = []
    for c in range(0, s.shape[0], 8):
        sc = s[c:c + 8]
        test = (lambda row: row > sc) if strict else (lambda row: row <= sc)
        c1 = test(rows[7])
        c2 = test(jnp.where(c1, rows[11], rows[3]))
        c3 = test(jnp.where(c1, jnp.where(c2, rows[13], rows[9]), jnp.where(c2, rows[5], rows[1])))
        hi = jnp.where(c2, jnp.where(c3, rows[14], rows[12]), jnp.where(c3, rows[10], rows[8]))
        lo = jnp.where(c2, jnp.where(c3, rows[6], rows[4]), jnp.where(c3, rows[2], rows[0]))
        c4 = test(jnp.where(c1, hi, lo))
        acc = (jnp.where(c1, 8.0, 0.0) + jnp.where(c2, 4.0, 0.0) + jnp.where(c3, 2.0, 0.0) + jnp.where(c4, 1.0, 0.0)
               + jnp.where(test(rows[15]), 1.0, 0.0))
        outs.append(acc)
    return jnp.concatenate(outs, axis=0)


def _route_fast(s0, s1):
    k = PEER_TOPK
    count = lambda v: jnp.sum(v, axis=0, keepdims=True)
    a, b = _sorted_top16(s0), _sorted_top16(s1)
    cand = _pair_candidates(a, b)
    best, _ = _max_rounds(cand, k, want_round=False)
    hit = cand >= best[k - 1:k]
    z = jnp.sum(jnp.where(hit, jnp.exp(cand - best[0:1]), 0.0), axis=0, keepdims=True)
    nrank = _picks_per_rank(jnp.where(hit, 1.0, 0.0))
    earns = jnp.concatenate([jnp.min(jnp.where(nrank >= float(v), a, -BIG_NEG), axis=0, keepdims=True)
                             for v in range(1, k + 1)], axis=0)
    n_i = _count_rows(earns, s0, strict=False)
    ties = lambda t, s: (count(jnp.where(t[:-1] == t[1:], 1.0, 0.0))
                         + jnp.abs(count(jnp.where(s >= t[k - 1:k], 1.0, 0.0)) - k))
    bad = ties(a, s0) + ties(b, s1) + jnp.abs(count(nrank) - k)
    return n_i, jnp.exp(s0 - a[0:1]) / z, _count_rows(b, s1, strict=True), jnp.exp(s1 - b[0:1]), bad


def _route_exact(s0, s1):
    k = PEER_TOPK
    a, _, rk0 = _top_rows(s0, k)
    b, _, rk1 = _top_rows(s1, k)
    cand = jnp.concatenate([a[r:r + 1] + b for r in range(k)], axis=0)
    best, bidx, _ = _top_rows(cand, k)
    brank = jnp.floor(bidx * (1.0 / k))
    z = jnp.sum(jnp.exp(best - best[0:1]), axis=0, keepdims=True)
    nrank = jnp.concatenate([jnp.sum(jnp.where(brank == float(r), 1.0, 0.0), axis=0, keepdims=True)
                             for r in range(k)], axis=0)
    return _spread_by_rank(rk0, nrank), jnp.exp(s0 - a[0:1]) / z, rk1, jnp.exp(s1 - b[0:1])


def _gelu_tanh(x):
    c = math.sqrt(2.0 / math.pi)
    return x / (1.0 + jnp.exp(x * (x * x * (-2.0 * c * 0.044715) - 2.0 * c)))


def _peer_body(x_ref, wq_ref, sk_ref, u_ref, v_ref, g_ref, b_ref, o_ref,
               xt_sc, acc_sc, n_sc, e0_sc, rk1_sc, e1_sc, p_sc, st_sc):
    e = pl.program_id(1)
    nk = PEER_NKEYS

    @pl.when(e == 0)
    def _():
        xt = x_ref[...].T.astype(BF16)
        xt_sc[...] = xt
        acc_sc[...] = jnp.zeros_like(acc_sc)
        qt = jnp.dot(wq_ref[...], xt, preferred_element_type=F32).astype(BF16)
        half = sk_ref.shape[2]
        for c in range(sk_ref.shape[0]):
            st_sc[c * nk:(c + 1) * nk, :] = jnp.dot(sk_ref[c], qt[c * half:(c + 1) * half],
                                                    preferred_element_type=F32)

        @pl.loop(0, PEER_HEADS)
        def _(h):
            s0 = st_sc[pl.ds(pl.multiple_of(2 * h * nk, nk), nk), :]
            s1 = st_sc[pl.ds(pl.multiple_of((2 * h + 1) * nk, nk), nk), :]

            def put(n_i, e0, rk1, e1):
                n_sc[h] = n_i
                e0_sc[h] = e0
                rk1_sc[h] = rk1.astype(BF16)
                e1_sc[h] = e1.astype(BF16)

            n_i, e0, rk1, e1, bad = _route_fast(s0, s1)
            put(n_i, e0, rk1, e1)

            @pl.when(jnp.max(bad) > 0.0)
            def _():
                put(*_route_exact(s0, s1))

    rows = u_ref.shape[0] // nk
    ht = jnp.dot(u_ref[...], xt_sc[...], preferred_element_type=F32)
    act = _gelu_tanh(ht).astype(BF16)
    for ii in range(rows):
        w = jnp.zeros((nk, xt_sc.shape[1]), BF16)
        for h in range(PEER_HEADS):
            n_row = n_sc[h, pl.ds(e * rows + ii, 1), :].astype(BF16)
            e0_row = e0_sc[h, pl.ds(e * rows + ii, 1), :].astype(BF16)
            w = w + jnp.where(rk1_sc[h] < n_row, e0_row * e1_sc[h], 0)
        p_sc[ii * nk:(ii + 1) * nk, :] = w * act[ii * nk:(ii + 1) * nk]
    acc_sc[...] += _tn_dot(v_ref[...], p_sc[...])

    @pl.when(e == pl.num_programs(1) - 1)
    def _():
        o_ref[...] = _layer_norm(DN_ALPHA * x_ref[...] + acc_sc[...].T, g_ref[...], b_ref[...])


def _stage_peer(x1, w_q, subkeys, u_tab, v_tab, ln_g, ln_b, tt=512, et=1024):
    n = x1.shape[0]
    tt = min(tt, n)
    H, nk, half = PEER_HEADS, PEER_NKEYS, subkeys.shape[-1]
    wq_t = w_q.T.astype(BF16)
    sk_t = subkeys.reshape(2 * H, nk, half).astype(BF16)
    ne = u_tab.shape[0]
    return pl.pallas_call(
        _peer_body, grid=(n // tt, ne // et),
        in_specs=[pl.BlockSpec((tt, D_MODEL), lambda t, e: (t, 0)),
                  _const_spec(wq_t.shape), _const_spec(sk_t.shape),
                  pl.BlockSpec((et, D_MODEL), lambda t, e: (e, 0)),
                  pl.BlockSpec((et, D_MODEL), lambda t, e: (e, 0)),
                  _const_spec((1, D_MODEL)), _const_spec((1, D_MODEL))],
        out_specs=pl.BlockSpec((tt, D_MODEL), lambda t, e: (t, 0)),
        out_shape=jax.ShapeDtypeStruct((n, D_MODEL), F32),
        scratch_shapes=[pltpu.VMEM((D_MODEL, tt), BF16), pltpu.VMEM((D_MODEL, tt), F32),
                        pltpu.VMEM((H, nk, tt), F32), pltpu.VMEM((H, nk, tt), F32),
                        pltpu.VMEM((H, nk, tt), BF16), pltpu.VMEM((H, nk, tt), BF16),
                        pltpu.VMEM((et, tt), BF16), pltpu.VMEM((2 * H * nk, tt), F32)],
        compiler_params=_cparams(("parallel", "arbitrary")), name="peer_ffn")(
            x1, wq_t, sk_t, u_tab.astype(BF16), v_tab.astype(BF16),
            ln_g.reshape(1, -1).astype(F32), ln_b.reshape(1, -1).astype(F32))


def _layer(x, w_in, b_in, ssm_a_re, ssm_a_im, ssm_log_dt, ssm_b_re, ssm_b_im, ssm_c_re, ssm_c_im, ssm_d,
           w_glu, w_up_ssm, nsa_pe_k, nsa_w1_k, nsa_w2_k, nsa_pe_v, nsa_w1_v, nsa_w2_v, w_up_nsa,
           w_out, ln1_g, ln1_b, peer_w_q, peer_subkeys, peer_u, peer_v, ln2_g, ln2_b):
    bsz, seq, _ = x.shape
    x2 = x.reshape(bsz * seq, D_MODEL)
    u, qp, kvc, kv, gates, gm = _stage_proj(x2, w_in, b_in, seq)
    ya = _stage_ssm(u, bsz, seq, ssm_a_re, ssm_a_im, ssm_log_dt, ssm_b_re, ssm_b_im, ssm_c_re, ssm_c_im,
                    ssm_d, w_glu, w_up_ssm)
    kcv = _stage_compress(kvc, bsz, seq, nsa_pe_k, nsa_w1_k, nsa_w2_k, nsa_pe_v, nsa_w1_v, nsa_w2_v)
    ob = _stage_attn(qp, kcv, kv, gates, bsz, seq)
    x1 = _stage_merge(x2, ya, ob, gm, w_up_nsa, w_out, ln1_g, ln1_b)
    out = _stage_peer(x1, peer_w_q, peer_subkeys, peer_u, peer_v, ln2_g, ln2_b)
    return out.reshape(bsz, seq, D_MODEL)


def kernel(x, w_in, b_in, ssm_a_re, ssm_a_im, ssm_log_dt, ssm_b_re, ssm_b_im, ssm_c_re, ssm_c_im, ssm_d, w_glu,
           w_up_ssm, nsa_pe_k, nsa_w1_k, nsa_w2_k, nsa_pe_v, nsa_w1_v, nsa_w2_v, w_up_nsa, w_out, ln1_g, ln1_b,
           peer_w_q, peer_subkeys, peer_u, peer_v, ln2_g, ln2_b):
    params = (w_in, b_in, ssm_a_re, ssm_a_im, ssm_log_dt, ssm_b_re, ssm_b_im, ssm_c_re, ssm_c_im, ssm_d, w_glu,
              w_up_ssm, nsa_pe_k, nsa_w1_k, nsa_w2_k, nsa_pe_v, nsa_w1_v, nsa_w2_v, w_up_nsa, w_out, ln1_g, ln1_b,
              peer_w_q, peer_subkeys, peer_u, peer_v, ln2_g, ln2_b)
    for layer in range(w_in.shape[0]):
        x = _layer(x, *[p[layer] for p in params])
    return x
```

```python
import functools
import math

import jax
import jax.numpy as jnp
import numpy as np
from jax import lax
from jax.experimental import pallas as pl
from jax.experimental.pallas import tpu as pltpu

F32 = jnp.float32
BF16 = jnp.bfloat16

D_MODEL = 1024
SSM_WIDTH = 512
SSM_GROUP = 16
SSM_GROUPS = 32
SSM_STATE = 64
SSM_CHUNK = 8
LANE_GROUPS = 8
N_LANE_TILES = SSM_WIDTH // 128
NSA_HEADS = 8
NSA_KV_GROUPS = 2
NSA_REP = NSA_HEADS // NSA_KV_GROUPS
NSA_HEAD_DIM = 64
KV_WIDTH = NSA_KV_GROUPS * NSA_HEAD_DIM
CMP_BLOCK = 32
CMP_STRIDE = 16
CMP_HIDDEN = 128
SEL_BLOCK = 64
SEL_TOPK = 16
WINDOW = 512
Q_BLOCK = 256
FORCE_BONUS = 1.0e4
PEER_HEADS = 8
PEER_NKEYS = 128
PEER_EXPERTS = PEER_NKEYS * PEER_NKEYS
PEER_TOPK = 16
DN_ALPHA = 2.0 ** 0.25
LN_EPS = 1e-5
NEG = -1e30
VMEM_LIMIT = 56 * 1024 * 1024


def _cparams(sem):
    return pltpu.CompilerParams(dimension_semantics=sem, vmem_limit_bytes=VMEM_LIMIT)


def _const_spec(shape):
    nd = len(shape)
    return pl.BlockSpec(shape, lambda *_: (0,) * nd, pipeline_mode=pl.Buffered(1))


def _layer_norm(z, g, b):
    mu = jnp.mean(z, axis=-1, keepdims=True)
    var = jnp.mean(jnp.square(z - mu), axis=-1, keepdims=True)
    return (z - mu) * lax.rsqrt(var + LN_EPS) * g + b


def _nt_dot(a, b):
    return lax.dot_general(a, b, (((1,), (1,)), ((), ())), preferred_element_type=F32)


def _tn_dot(a, b):
    return lax.dot_general(a, b, (((0,), (0,)), ((), ())), preferred_element_type=F32)


KV_REC = 640
KS_OFF, VS_OFF, KW_OFF, VW_OFF = 0, 256, 384, 512
POS_HI, POS_LO, ONE_LANE, BLOCK_LANE0 = 64, 65, 64, 128
assert SEL_BLOCK == 64


def _position_features(pos, lane):
    hi, lo = pos >> 6, pos & 63
    f = jnp.where(lane == KS_OFF + POS_HI, hi, 0) + jnp.where(lane == KW_OFF + POS_HI, hi, 0)
    f = f + jnp.where(lane == KS_OFF + POS_LO, lo, 0) + jnp.where(lane == KW_OFF + POS_LO, lo, 0)
    f = f + jnp.where(lane == KS_OFF + BLOCK_LANE0 + hi, 1, 0)
    f = f + jnp.where(lane == VS_OFF + ONE_LANE, 1, 0) + jnp.where(lane == VW_OFF + ONE_LANE, 1, 0)
    return f.astype(F32)


def _proj_body(seq, x_ref, wu, bu, wq, bq, wc, bc, wk, bk, wg, bg, wm, bm, aux_ref,
               u_o, q_o, c_o, k_o, g_o, m_o):
    xb = x_ref[...].astype(BF16)
    tm = x_ref.shape[0]
    dk = NSA_HEAD_DIM

    def lin(w, b):
        return jnp.dot(xb, w[...], preferred_element_type=F32) + b[...]

    def halves(a, tile):
        t = a[:, tile * 128:(tile + 1) * 128]
        return t, pltpu.roll(t, dk, 1)

    low = lax.broadcasted_iota(jnp.int32, (tm, 128), 1) < dk
    u_o[...] = lin(wu, bu)
    q = lin(wq, bq) * (dk ** -0.5)
    for tile in range(NSA_HEADS // 2):
        for half, piece in enumerate(halves(q, tile)):
            h = 2 * tile + half
            q_o[:, h * 128:(h + 1) * 128] = jnp.where(low, piece, aux_ref[h:h + 1, :]).astype(BF16)
    c_o[...] = lin(wc, bc)
    kv = lin(wk, bk)
    parts = [halves(kv, tile) for tile in range(4)]
    zero = jnp.zeros((tm, 128), F32)
    recs = []
    for g in range(NSA_KV_GROUPS):
        k_s, v_s, k_w, v_w = (jnp.where(low, parts[tile][g], 0.0) for tile in range(4))
        recs += [k_s, zero, v_s, k_w, v_w]
    pos0 = (pl.program_id(0) % (seq // tm)) * tm
    shape = (tm, NSA_KV_GROUPS * KV_REC)
    lane = lax.broadcasted_iota(jnp.int32, shape, 1)
    lane = jnp.where(lane >= KV_REC, lane - KV_REC, lane)
    feat = _position_features(pos0 + lax.broadcasted_iota(jnp.int32, shape, 0), lane)
    k_o[...] = (jnp.concatenate(recs, axis=1) + feat).astype(BF16)
    g_o[...] = lin(wg, bg)
    m_o[...] = jax.nn.sigmoid(lin(wm, bm)).astype(m_o.dtype)


def _stage_proj(x2, w_in, b_in, seq):
    n = x2.shape[0]
    tm = 256
    o0 = SSM_WIDTH
    o1 = o0 + NSA_HEADS * NSA_HEAD_DIM
    o2 = o1 + 6 * KV_WIDTH
    o3 = o2 + 3 * NSA_HEADS
    w_u, b_u = w_in[:, :o0], b_in[:o0]
    w_q, b_q = w_in[:, o0:o1], b_in[o0:o1]
    w_kv, b_kv = w_in[:, o1:o2], b_in[o1:o2]
    w_g, b_g = w_in[:, o2:o3], b_in[o2:o3]
    w_m, b_m = w_in[:, o3:], b_in[o3:]

    slope = 2.0 ** -(np.arange(NSA_HEADS, dtype=np.float32) + 1.0)
    q_aux = np.zeros((NSA_HEADS, 128), np.float32)
    q_aux[:, POS_HI] = slope * 64.0
    q_aux[:, POS_LO] = slope

    def pad_g(a):
        a = a.reshape(a.shape[:-1] + (NSA_KV_GROUPS, 3 * NSA_REP))
        a = jnp.concatenate([a, jnp.zeros(a.shape[:-1] + (128 - 3 * NSA_REP,), a.dtype)], axis=-1)
        return a.reshape(a.shape[:-2] + (NSA_KV_GROUPS * 128,))

    ws = [w_u, w_q, w_kv[:, :2 * KV_WIDTH], w_kv[:, 2 * KV_WIDTH:], pad_g(w_g), w_m]
    bs = [b_u, b_q, b_kv[:2 * KV_WIDTH], b_kv[2 * KV_WIDTH:], pad_g(b_g), b_m]
    odt = [F32, BF16, F32, BF16, F32, BF16]
    widths = [SSM_WIDTH, NSA_HEADS * 128, 2 * KV_WIDTH, NSA_KV_GROUPS * KV_REC, NSA_KV_GROUPS * 128, 2 * D_MODEL]
    args, in_specs = [x2], [pl.BlockSpec((tm, D_MODEL), lambda i: (i, 0))]
    for w, b in zip(ws, bs):
        args += [w.astype(BF16), b.reshape(1, -1).astype(F32)]
        in_specs += [_const_spec(w.shape), _const_spec((1, w.shape[1]))]
    args.append(jnp.asarray(q_aux))
    in_specs.append(_const_spec(q_aux.shape))
    out_shape = [jax.ShapeDtypeStruct((n, w), dt) for w, dt in zip(widths, odt)]
    out_specs = [pl.BlockSpec((tm, w), lambda i: (i, 0)) for w in widths]
    return pl.pallas_call(
        functools.partial(_proj_body, seq), grid=(n // tm,), in_specs=in_specs, out_specs=out_specs, out_shape=out_shape,
        compiler_params=_cparams(("parallel",)), name="in_proj")(*args)


def _ssm_weights(a_re, a_im, log_dt, b_re, b_im, c_re, c_im):
    L, G, P, H, A, J = SSM_CHUNK, SSM_GROUPS, SSM_STATE, SSM_GROUP, LANE_GROUPS, N_LANE_TILES
    lam = lax.complex(a_re.astype(F32), a_im.astype(F32))
    dt = jnp.exp(log_dt.astype(F32))[:, None]
    lam_bar = jnp.exp(lam * dt)
    b_bar = ((lam_bar - 1.0) / lam)[:, :, None] * lax.complex(b_re.astype(F32), b_im.astype(F32))
    c = lax.complex(c_re.astype(F32), c_im.astype(F32))
    k = jnp.arange(L + 1, dtype=F32)
    pw = jnp.exp((lam * dt)[None] * k[:, None, None])
    def spread(x, n_inner):
        src = np.arange(x.shape[2])
        dst = np.arange(x.shape[2] * A)
        sel = ((src[:, None] // n_inner == dst[None, :] // (A * n_inner))
               & (src[:, None] % n_inner == dst[None, :] % n_inner))
        return jnp.einsum('jrk,kc->jrc', x, jnp.asarray(sel, F32)), (dst % (A * n_inner)) // n_inner

    def group_mask(row_group, col_group):
        return jnp.asarray(row_group[:, None] == col_group[None, :], F32)

    in_rows = (np.arange(L * 128) % 128) // H
    st_rows = (np.arange(2 * A * P) % (A * P)) // P
    kern = jnp.real(jnp.einsum('ghp,kgp,gpj->kghj', c, pw[:L], b_bar))
    s_i, t_i = jnp.arange(L)[:, None], jnp.arange(L)[None, :]
    tau = jnp.clip(t_i - s_i, 0, L - 1)
    causal = (t_i >= s_i).astype(F32)
    k_st = (kern[tau] * causal[:, :, None, None, None]).reshape(L, L, J, A, H, H)
    k_st = k_st.transpose(2, 0, 3, 5, 1, 4).reshape(J, L * 128, L * H)
    w_intra, cg = spread(k_st, H)
    w_intra = w_intra * group_mask(in_rows, cg)
    q = (pw[:L][::-1][:, :, :, None] * b_bar[None]).reshape(L, J, A, P, H)
    q = jnp.stack([jnp.real(q), jnp.imag(q)], axis=0)
    q = q.transpose(2, 1, 3, 5, 0, 4).reshape(J, L * 128, 2 * P)
    m2, cg = spread(q, P)
    m2 = m2 * group_mask(in_rows, cg)
    cl = (c[None] * pw[1:][:, :, None, :]).reshape(L, J, A, H, P)
    cl = jnp.stack([jnp.real(cl), -jnp.imag(cl)], axis=0)
    cl = cl.transpose(2, 0, 3, 5, 1, 4).reshape(J, 2 * A * P, L * H)
    m1, cg = spread(cl, H)
    m1 = m1 * group_mask(st_rows, cg)
    w2 = jnp.concatenate([w_intra, m1], axis=1)
    lam8 = pw[L].reshape(G * P)
    return m2.astype(BF16), w2.astype(BF16), jnp.real(lam8), jnp.imag(lam8)


def _split_lane_tiles(src_ref, dst_sc):
    for j in range(dst_sc.shape[0]):
        dst_sc[j] = src_ref[:, j * 128:(j + 1) * 128]


def _chunk_inputs(u_sc, j, rows):
    return jnp.concatenate([u_sc[j, pl.ds(s, rows, stride=SSM_CHUNK), :] for s in range(SSM_CHUNK)], axis=1)


def _ssm_state_body(u_ref, m2_ref, zre_ref, zim_ref, u_sc):
    half = LANE_GROUPS * SSM_STATE
    _split_lane_tiles(u_ref, u_sc)
    for j in range(N_LANE_TILES):
        z = jnp.dot(_chunk_inputs(u_sc, j, zre_ref.shape[0]).astype(BF16), m2_ref[j], preferred_element_type=F32)
        zre_ref[:, j * half:(j + 1) * half] = z[:, :half]
        zim_ref[:, j * half:(j + 1) * half] = z[:, half:]


def _ssm_scan_body(zre_ref, zim_ref, lre_ref, lim_ref, xre_ref, xim_ref):
    lr, li = lre_ref[...], lim_ref[...]

    def step(c, carry):
        xr, xi = carry
        xre_ref[pl.ds(c, 1), :] = xr
        xim_ref[pl.ds(c, 1), :] = xi
        return lr * xr - li * xi + zre_ref[pl.ds(c, 1), :], lr * xi + li * xr + zim_ref[pl.ds(c, 1), :]

    zero = jnp.zeros(lr.shape, F32)
    lax.fori_loop(0, zre_ref.shape[0], step, (zero, zero))


def _ssm_out_body(u_ref, xre_ref, xim_ref, w2_ref, d_ref, wglu_ref, wup_ref, o_ref, u_sc, y_sc):
    half = LANE_GROUPS * SSM_STATE
    rows = xre_ref.shape[0]
    _split_lane_tiles(u_ref, u_sc)
    for j in range(N_LANE_TILES):
        lhs = jnp.concatenate(
            [_chunk_inputs(u_sc, j, rows), xre_ref[:, j * half:(j + 1) * half], xim_ref[:, j * half:(j + 1) * half]],
            axis=1).astype(BF16)
        yj = jnp.dot(lhs, w2_ref[j], preferred_element_type=F32)
        for t in range(SSM_CHUNK):
            y_sc[j, pl.ds(t, rows, stride=SSM_CHUNK), :] = yj[:, t * 128:(t + 1) * 128]
    y = jnp.concatenate([y_sc[j] for j in range(N_LANE_TILES)], axis=1) + d_ref[...] * u_ref[...]
    gl = jnp.dot(jax.nn.gelu(y).astype(BF16), wglu_ref[...], preferred_element_type=F32)
    v = gl[:, :SSM_WIDTH] * jax.nn.sigmoid(gl[:, SSM_WIDTH:])
    o_ref[...] = jnp.dot(v.astype(BF16), wup_ref[...], preferred_element_type=F32).astype(o_ref.dtype)


def _stage_ssm(u, bsz, seq, a_re, a_im, log_dt, b_re, b_im, c_re, c_im, d_skip, w_glu, w_up):
    n = bsz * seq
    L = SSM_CHUNK
    nck = seq // L
    gp = SSM_GROUPS * SSM_STATE
    m2, w2, l8re, l8im = _ssm_weights(a_re, a_im, log_dt, b_re, b_im, c_re, c_im)
    rs = min(nck, 512)
    zre, zim = pl.pallas_call(
        _ssm_state_body, grid=(bsz, nck // rs),
        in_specs=[pl.BlockSpec((rs * L, SSM_WIDTH), lambda b, i: (b * (nck // rs) + i, 0)),
                  _const_spec(m2.shape)],
        out_specs=[pl.BlockSpec((rs, gp), lambda b, i: (i, b))] * 2,
        out_shape=[jax.ShapeDtypeStruct((nck, bsz * gp), F32)] * 2,
        scratch_shapes=[pltpu.VMEM((N_LANE_TILES, rs * L, 128), F32)],
        compiler_params=_cparams(("parallel", "parallel")), name="ssm_chunk_state")(u, m2)
    cw = 1024
    ncol = bsz * gp // cw
    lre = jnp.tile(l8re, bsz).reshape(1, bsz * gp)
    lim = jnp.tile(l8im, bsz).reshape(1, bsz * gp)
    col = pl.BlockSpec((nck, cw), lambda i: (0, i))
    lam = pl.BlockSpec((1, cw), lambda i: (0, i))
    xre, xim = pl.pallas_call(
        _ssm_scan_body, grid=(ncol,), in_specs=[col, col, lam, lam], out_specs=[col, col],
        out_shape=[jax.ShapeDtypeStruct((nck, bsz * gp), F32)] * 2,
        compiler_params=_cparams(("parallel",)), name="ssm_carry_scan")(zre, zim, lre, lim)
    ro = min(nck, 128)
    d8 = d_skip.reshape(1, SSM_WIDTH).astype(F32)
    return pl.pallas_call(
        _ssm_out_body, grid=(bsz, nck // ro),
        in_specs=[pl.BlockSpec((ro * L, SSM_WIDTH), lambda b, i: (b * (nck // ro) + i, 0)),
                  pl.BlockSpec((ro, gp), lambda b, i: (i, b)),
                  pl.BlockSpec((ro, gp), lambda b, i: (i, b)),
                  _const_spec(w2.shape), _const_spec((1, SSM_WIDTH)),
                  _const_spec(w_glu.shape), _const_spec(w_up.shape)],
        out_specs=pl.BlockSpec((ro * L, D_MODEL), lambda b, i: (b * (nck // ro) + i, 0)),
        out_shape=jax.ShapeDtypeStruct((n, D_MODEL), BF16),
        scratch_shapes=[pltpu.VMEM((N_LANE_TILES, ro * L, 128), F32)] * 2,
        compiler_params=_cparams(("parallel", "parallel")), name="ssm_out")(
            u, xre, xim, w2, d8, w_glu.astype(BF16), w_up.astype(BF16))


def _compress_body(kv_ref, pea_ref, peb_ref, w1a_ref, w1b_ref, w2_ref, o_ref, kv_sc):
    nrow = o_ref.shape[0]
    _split_lane_tiles(kv_ref, kv_sc)
    f = jnp.concatenate([kv_sc[j, pl.ds(l, nrow, stride=CMP_STRIDE), :]
                         for l in range(CMP_STRIDE) for j in range(kv_sc.shape[0])], axis=1)
    a = jnp.dot((f + pea_ref[...]).astype(BF16), w1a_ref[...], preferred_element_type=F32)
    b = jnp.dot((f + peb_ref[...]).astype(BF16), w1b_ref[...], preferred_element_type=F32)
    pre = a + pltpu.roll(b, b.shape[0] - 1, 0)
    hid = jax.nn.gelu(pre)
    out = jnp.dot(hid.astype(BF16), w2_ref[...], preferred_element_type=F32)
    lane = lax.broadcasted_iota(jnp.int32, out.shape, 1) & 255
    c_end = lax.broadcasted_iota(jnp.int32, out.shape, 0) * CMP_STRIDE + (CMP_BLOCK - 1)
    feat = jnp.where(lane == POS_HI, c_end >> 6, 0) + jnp.where(lane == POS_LO, c_end & 63, 0)
    o_ref[...] = (out + feat.astype(F32)).astype(BF16)


def _stage_compress(kvc, bsz, seq, pe_k, w1_k, w2_k, pe_v, w1_v, w2_v):
    hl = CMP_STRIDE
    nrow = seq // hl
    G, dk, hid = NSA_KV_GROUPS, NSA_HEAD_DIM, CMP_HIDDEN
    eye = jnp.eye(2 * G, dtype=F32)

    def big_w1(lo):
        wk = w1_k.reshape(CMP_BLOCK, dk, hid)[lo:lo + hl]
        wv = w1_v.reshape(CMP_BLOCK, dk, hid)[lo:lo + hl]
        w = jnp.stack([wk, wk, wv, wv], axis=1)
        return jnp.einsum('lcdj,ce->lcdej', w, eye).reshape(hl * 2 * G * dk, 2 * G * hid)

    def big_pe(lo):
        pk, pv = pe_k[lo:lo + hl], pe_v[lo:lo + hl]
        return jnp.stack([pk, pk, pv, pv], axis=1).reshape(1, hl * 2 * G * dk)

    w2 = jnp.stack([w2_k, w2_k, w2_v, w2_v], axis=0)
    w2 = jnp.concatenate([w2, jnp.zeros((2 * G, hid, 128 - dk), w2.dtype)], axis=-1)
    slot = eye[np.array([kv * G + g for g in range(G) for kv in range(2)])].T
    w2 = jnp.einsum('cjd,ce->cjed', w2, slot).reshape(2 * G * hid, 2 * G * 128)
    width = hl * 2 * G * dk
    return pl.pallas_call(
        _compress_body, grid=(bsz,),
        in_specs=[pl.BlockSpec((seq, 2 * G * dk), lambda b: (b, 0)),
                  _const_spec((1, width)), _const_spec((1, width)),
                  _const_spec((width, 2 * G * hid)), _const_spec((width, 2 * G * hid)),
                  _const_spec((2 * G * hid, 2 * G * 128))],
        out_specs=pl.BlockSpec((nrow, 2 * G * 128), lambda b: (b, 0)),
        out_shape=jax.ShapeDtypeStruct((bsz * nrow, 2 * G * 128), BF16),
        scratch_shapes=[pltpu.VMEM((2 * G * dk // 128, seq, 128), F32)],
        compiler_params=_cparams(("parallel",)), name="nsa_compress")(
            kvc, big_pe(0).astype(F32), big_pe(hl).astype(F32),
            big_w1(0).astype(BF16), big_w1(hl).astype(BF16), w2.astype(BF16))


MASK_BIG = 2.0 ** 100
SEL_TILE = 512


def _nsa_body(q_ref, kcv_ref, kv_ref, g_ref, ov_ref, rel_ref, relc_ref, o_ref, acc, s_buf, p_buf, a_buf):
    i = pl.program_id(2)
    qs = i * Q_BLOCK
    seq = kv_ref.shape[0]
    rows = NSA_REP * Q_BLOCK
    dk = NSA_HEAD_DIM
    qh = [q_ref[:, r * 128:(r + 1) * 128] for r in range(NSA_REP)]
    q_all = jnp.concatenate(qh, axis=0)

    kc = kcv_ref[:, :128]
    vc = kcv_ref[:, 128:]
    vis_c = relc_ref[...] >= -qs
    s = jnp.where(vis_c, _nt_dot(q_all, kc), NEG)
    p = jnp.exp(s - jnp.max(s, axis=-1, keepdims=True)) * jnp.where(vis_c, 1.0, 0.0)
    p = (p / jnp.maximum(jnp.sum(p, axis=-1, keepdims=True), 1e-30)).astype(BF16)
    o_cmp = jnp.dot(p, vc, preferred_element_type=F32)

    wlen = min(WINDOW + Q_BLOCK, seq)
    w0 = pl.multiple_of(jnp.maximum(qs + Q_BLOCK - wlen, 0), Q_BLOCK)
    rel_w = rel_ref[...]
    s = _nt_dot(q_all, kv_ref[pl.ds(w0, wlen), KW_OFF:KW_OFF + 128])
    in_band = pltpu.bitcast(rel_w - (w0 - qs), jnp.uint32) < WINDOW
    s = jnp.where(in_band, s, NEG)
    p_w = jnp.exp(s - jnp.max(s, axis=-1, keepdims=True)).astype(BF16)
    a_win = jnp.dot(p_w, kv_ref[pl.ds(w0, wlen), VW_OFF:VW_OFF + 128], preferred_element_type=F32)

    p_heads = jnp.concatenate([p[r * Q_BLOCK:(r + 1) * Q_BLOCK] for r in range(NSA_REP)], axis=1)
    imp = jnp.dot(p_heads, ov_ref[...], preferred_element_type=F32)
    coli = lax.broadcasted_iota(jnp.int32, (Q_BLOCK, 128), 1)
    t_q = qs + lax.broadcasted_iota(jnp.int32, (Q_BLOCK, 128), 0)
    cur = t_q >> 6
    forced = (coli == 0) | (coli == cur) | (coli == cur - 1)
    score = jnp.where(coli * SEL_BLOCK <= t_q, imp + jnp.where(forced, FORCE_BONUS, 0.0), NEG)
    nsl = seq // SEL_BLOCK
    st = score.T[:nsl]
    groups = [st[8 * v:8 * v + 8] for v in range(nsl // 8)]
    jsub = lax.broadcasted_iota(jnp.int32, (8, Q_BLOCK), 0)
    rank = [jnp.zeros((8, Q_BLOCK), F32) for _ in groups]
    for k in range(nsl):
        rk = st[k:k + 1, :]
        for v, sv in enumerate(groups):
            if v > k // 8:
                one = jnp.where(rk >= sv, 1.0, 0.0)
            elif v < k // 8:
                one = jnp.where(rk > sv, 1.0, 0.0)
            else:
                one = jnp.where(rk > sv, 1.0, jnp.where(jsub > k % 8, jnp.where(rk == sv, 1.0, 0.0), 0.0))
            rank[v] = rank[v] + one
    drop_t = jnp.where(jnp.concatenate(rank, axis=0) < float(min(SEL_TOPK, nsl)), 0.0, -MASK_BIG)
    if nsl < 128:
        drop_t = jnp.concatenate([drop_t, jnp.zeros((128 - nsl, Q_BLOCK), F32)], axis=0)
    drop = drop_t.T.astype(BF16)
    q_sel = jnp.concatenate([jnp.concatenate([qh[r], drop], axis=1) for r in range(NSA_REP)], axis=0)

    n_tiles = (qs + Q_BLOCK + SEL_TILE - 1) // SEL_TILE

    def scores(kt):
        k0 = pl.multiple_of(kt * SEL_TILE, SEL_TILE)
        return _nt_dot(q_sel, kv_ref[pl.ds(k0, SEL_TILE), KS_OFF:KS_OFF + 256])

    def values(kt):
        k0 = pl.multiple_of(kt * SEL_TILE, SEL_TILE)
        return kv_ref[pl.ds(k0, SEL_TILE), VS_OFF:VS_OFF + 128]

    def trip(diagonal, kt, m_old):
        if not diagonal:
            s_next = scores(kt + 1)
        acc[...] = a_buf[...] * acc[...] + jnp.dot(p_buf[...], values(jnp.maximum(kt - 1, 0)),
                                                   preferred_element_type=F32)
        s = s_buf[...]
        if diagonal:
            s = jnp.where(rel_ref[:, :SEL_TILE] >= kt * SEL_TILE - qs, s, NEG)
        m_new = jnp.maximum(m_old, jnp.max(s, axis=-1, keepdims=True))
        p_buf[...] = jnp.exp(s - m_new).astype(BF16)
        a_buf[...] = jnp.exp(m_old - m_new)
        if not diagonal:
            s_buf[...] = s_next
        return m_new

    acc[...] = jnp.zeros_like(acc)
    p_buf[...] = jnp.zeros_like(p_buf)
    a_buf[...] = jnp.ones_like(a_buf)
    s_buf[...] = scores(0)
    m_s = lax.fori_loop(0, n_tiles - 1, functools.partial(trip, False), jnp.full((rows, 1), NEG, F32))
    trip(True, n_tiles - 1, m_s)
    a_sel = a_buf[...] * acc[...] + jnp.dot(p_buf[...], values(n_tiles - 1), preferred_element_type=F32)

    gate = jax.nn.sigmoid(g_ref[...])
    outs = []
    for r in range(NSA_REP):
        rr = slice(r * Q_BLOCK, (r + 1) * Q_BLOCK)
        o_s = a_sel[rr, :dk] / a_sel[rr, ONE_LANE:ONE_LANE + 1]
        o_w = a_win[rr, :dk] / a_win[rr, ONE_LANE:ONE_LANE + 1]
        outs.append(gate[:, 3 * r:3 * r + 1] * o_cmp[rr, :dk] + gate[:, 3 * r + 1:3 * r + 2] * o_s
                    + gate[:, 3 * r + 2:3 * r + 3] * o_w)
    o_ref[...] = jnp.concatenate(outs, axis=1).astype(o_ref.dtype)


def _stage_attn(qp, kcv, kv, gates, bsz, seq):
    n = bsz * seq
    nq = seq // Q_BLOCK
    ncmp = seq // CMP_STRIDE
    nsl = seq // SEL_BLOCK
    assert seq % SEL_TILE == 0
    c_start = np.arange(ncmp) * CMP_STRIDE
    s_start = np.arange(nsl) * SEL_BLOCK
    ov = ((c_start[:, None] < s_start[None, :] + SEL_BLOCK) & (c_start[:, None] + CMP_BLOCK > s_start[None, :]))
    ov = np.pad(ov.astype(np.float32), ((0, 0), (0, 128 - nsl)))
    ov[ncmp - 1] = 0.0
    ov = jnp.asarray(np.tile(ov, (NSA_REP, 1)), BF16)
    width = NSA_REP * NSA_HEAD_DIM
    rows = NSA_REP * Q_BLOCK
    wlen = min(WINDOW + Q_BLOCK, seq)
    assert wlen >= SEL_TILE
    offs = jnp.arange(rows, dtype=jnp.int32)[:, None] % Q_BLOCK
    rel = offs - jnp.arange(wlen, dtype=jnp.int32)[None, :]
    relc = offs - (jnp.arange(ncmp, dtype=jnp.int32)[None, :] * CMP_STRIDE + (CMP_BLOCK - 1))
    return pl.pallas_call(
        _nsa_body, grid=(bsz, NSA_KV_GROUPS, nq),
        in_specs=[pl.BlockSpec((Q_BLOCK, NSA_REP * 128), lambda b, g, i: (b * nq + i, g)),
                  pl.BlockSpec((ncmp, 256), lambda b, g, i: (b, g)),
                  pl.BlockSpec((seq, KV_REC), lambda b, g, i: (b, g)),
                  pl.BlockSpec((Q_BLOCK, 128), lambda b, g, i: (b * nq + i, g)),
                  _const_spec(ov.shape), _const_spec(rel.shape), _const_spec(relc.shape)],
        out_specs=pl.BlockSpec((Q_BLOCK, width), lambda b, g, i: (b * nq + i, g)),
        out_shape=jax.ShapeDtypeStruct((n, NSA_KV_GROUPS * width), BF16),
        scratch_shapes=[pltpu.VMEM((rows, 128), F32), pltpu.VMEM((rows, SEL_TILE), F32),
                        pltpu.VMEM((rows, SEL_TILE), BF16), pltpu.VMEM((rows, 1), F32)],
        compiler_params=_cparams(("parallel", "parallel", "arbitrary")), name="nsa_attention")(
            qp, kcv, kv, gates, ov, rel, relc)


def _merge_body(x_ref, ya_ref, ob_ref, gm_ref, wnsa_ref, wout_ref, g_ref, b_ref, o_ref):
    yb = jnp.dot(ob_ref[...], wnsa_ref[...], preferred_element_type=F32)
    gm = gm_ref[...].astype(F32)
    mix_in = gm[:, :D_MODEL] * ya_ref[...].astype(F32) + gm[:, D_MODEL:] * yb
    mix = jnp.dot(mix_in.astype(BF16), wout_ref[...], preferred_element_type=F32)
    o_ref[...] = _layer_norm(DN_ALPHA * x_ref[...] + mix, g_ref[...], b_ref[...])


def _stage_merge(x2, ya, ob, gm, w_up_nsa, w_out, ln_g, ln_b):
    n = x2.shape[0]
    tm = 512
    row = lambda w: pl.BlockSpec((tm, w), lambda i: (i, 0))
    return pl.pallas_call(
        _merge_body, grid=(n // tm,),
        in_specs=[row(D_MODEL), row(D_MODEL), row(ob.shape[1]), row(2 * D_MODEL),
                  _const_spec(w_up_nsa.shape), _const_spec(w_out.shape),
                  _const_spec((1, D_MODEL)), _const_spec((1, D_MODEL))],
        out_specs=row(D_MODEL), out_shape=jax.ShapeDtypeStruct((n, D_MODEL), F32),
        compiler_params=_cparams(("parallel",)), name="merge_ln")(
            x2, ya, ob, gm, w_up_nsa.astype(BF16), w_out.astype(BF16),
            ln_g.reshape(1, -1).astype(F32), ln_b.reshape(1, -1).astype(F32))


BIG_NEG = -3.0e38


def _top_rows(s, k):
    rid = lax.broadcasted_iota(jnp.int32, s.shape, 0).astype(F32)
    rank = jnp.full(s.shape, float(k), F32)
    vals, idxs = [], []
    for r in range(k):
        m = jnp.max(s, axis=0, keepdims=True)
        idx = jnp.min(jnp.where(s == m, rid, float(s.shape[0])), axis=0, keepdims=True)
        hit = rid == idx
        rank = jnp.where(hit, float(r), rank)
        s = jnp.where(hit, BIG_NEG, s)
        vals.append(m)
        idxs.append(idx)
    return jnp.concatenate(vals, axis=0), jnp.concatenate(idxs, axis=0), rank


def _max_rounds(s, k, want_round=True):
    rnd = jnp.full(s.shape, float(k), F32) if want_round else None
    vals = []
    for r in range(k):
        m = jnp.max(s, axis=0, keepdims=True)
        eq = s == m
        if want_round:
            rnd = jnp.where(eq, float(r), rnd)
        s = jnp.where(eq, BIG_NEG, s)
        vals.append(m)
    return jnp.concatenate(vals, axis=0), rnd


def _pair_candidates(a, b):
    c8 = lax.broadcasted_iota(jnp.int32, (8, a.shape[1]), 0)
    parts = [a[0:1] + b[0:8], a[0:1] + b[8:16]]
    for r in range(1, 8):
        keep = PEER_TOPK // (r + 1)
        cand = a[r:r + 1] + b[0:8]
        parts.append(cand if keep >= 8 else jnp.where(c8 < keep, cand, BIG_NEG))
    parts.append(a[8:16] + b[0:1])
    return jnp.concatenate(parts, axis=0)


def _picks_per_rank(picked):
    rows = [jnp.sum(picked[0:16], axis=0, keepdims=True)]
    rows += [jnp.sum(picked[8 * (r + 1):8 * (r + 2)], axis=0, keepdims=True) for r in range(1, 8)]
    return jnp.concatenate(rows + [picked[72:80]], axis=0)


def _spread_by_rank(rank, per_rank):
    out = jnp.zeros(rank.shape, F32)
    for r in range(PEER_TOPK):
        out = out + jnp.where(rank == float(r), per_rank[r:r + 1], 0.0)
    return out


def _sort16_network():
    pairs, p = [], 1
    while p < 16:
        k = p
        while k >= 1:
            for j in range(k % p, 16 - k, 2 * k):
                for i in range(min(k, 16 - j - k)):
                    if (i + j) // (2 * p) == (i + j + k) // (2 * p):
                        pairs.append((i + j, i + j + k))
            k //= 2
        p *= 2
    return pairs


_SORT16 = _sort16_network()


def _sorted_top16(s):
    def exchange(v, i, j):
        v[i], v[j] = jnp.maximum(v[i], v[j]), jnp.minimum(v[i], v[j])

    v = [s[8 * k:8 * k + 8] for k in range(16)]
    for i, j in _SORT16:
        exchange(v, i, j)
    for shift in (4, 2, 1):
        v = [jnp.maximum(v[k], pltpu.roll(v[15 - k], shift, 0)) for k in range(16)]
        for d in (8, 4, 2, 1):
            for i in range(16):
                if i & d == 0:
                    exchange(v, i, i + d)
    return jnp.concatenate([x[0:1] for x in v], axis=0)


def _count_rows(vals, s, strict):
    rows = [jnp.broadcast_to(vals[r:r + 1], (8, s.shape[1])) for r in range(vals.shape[0])]
    assert len(rows) == 16
    outs = []
    for c in range(0, s.shape[0], 8):
        sc = s[c:c + 8]
        test = (lambda row: row > sc) if strict else (lambda row: row <= sc)
        c1 = test(rows[7])
        c2 = test(jnp.where(c1, rows[11], rows[3]))
        c3 = test(jnp.where(c1, jnp.where(c2, rows[13], rows[9]), jnp.where(c2, rows[5], rows[1])))
        hi = jnp.where(c2, jnp.where(c3, rows[14], rows[12]), jnp.where(c3, rows[10], rows[8]))
        lo = jnp.where(c2, jnp.where(c3, rows[6], rows[4]), jnp.where(c3, rows[2], rows[0]))
        c4 = test(jnp.where(c1, hi, lo))
        acc = (jnp.where(c1, 8.0, 0.0) + jnp.where(c2, 4.0, 0.0) + jnp.where(c3, 2.0, 0.0) + jnp.where(c4, 1.0, 0.0)
               + jnp.where(test(rows[15]), 1.0, 0.0))
        outs.append(acc)
    return jnp.concatenate(outs, axis=0)


def _route_fast(s0, s1):
    k = PEER_TOPK
    count = lambda v: jnp.sum(v, axis=0, keepdims=True)
    a, b = _sorted_top16(s0), _sorted_top16(s1)
    cand = _pair_candidates(a, b)
    best, _ = _max_rounds(cand, k, want_round=False)
    hit = cand >= best[k - 1:k]
    z = jnp.sum(jnp.where(hit, jnp.exp(cand - best[0:1]), 0.0), axis=0, keepdims=True)
    nrank = _picks_per_rank(jnp.where(hit, 1.0, 0.0))
    earns = jnp.concatenate([jnp.min(jnp.where(nrank >= float(v), a, -BIG_NEG), axis=0, keepdims=True)
                             for v in range(1, k + 1)], axis=0)
    n_i = _count_rows(earns, s0, strict=False)
    ties = lambda t, s: (count(jnp.where(t[:-1] == t[1:], 1.0, 0.0))
                         + jnp.abs(count(jnp.where(s >= t[k - 1:k], 1.0, 0.0)) - k))
    bad = ties(a, s0) + ties(b, s1) + jnp.abs(count(nrank) - k)
    return n_i, jnp.exp(s0 - a[0:1]) / z, _count_rows(b, s1, strict=True), jnp.exp(s1 - b[0:1]), bad


def _route_exact(s0, s1):
    k = PEER_TOPK
    a, _, rk0 = _top_rows(s0, k)
    b, _, rk1 = _top_rows(s1, k)
    cand = jnp.concatenate([a[r:r + 1] + b for r in range(k)], axis=0)
    best, bidx, _ = _top_rows(cand, k)
    brank = jnp.floor(bidx * (1.0 / k))
    z = jnp.sum(jnp.exp(best - best[0:1]), axis=0, keepdims=True)
    nrank = jnp.concatenate([jnp.sum(jnp.where(brank == float(r), 1.0, 0.0), axis=0, keepdims=True)
                             for r in range(k)], axis=0)
    return _spread_by_rank(rk0, nrank), jnp.exp(s0 - a[0:1]) / z, rk1, jnp.exp(s1 - b[0:1])


def _gelu_tanh(x):
    c = math.sqrt(2.0 / math.pi)
    return x / (1.0 + jnp.exp(x * (x * x * (-2.0 * c * 0.044715) - 2.0 * c)))


def _peer_body(x_ref, wq_ref, sk_ref, u_ref, v_ref, g_ref, b_ref, o_ref,
               xt_sc, acc_sc, n_sc, e0_sc, rk1_sc, e1_sc, p_sc, st_sc):
    e = pl.program_id(1)
    nk = PEER_NKEYS

    @pl.when(e == 0)
    def _():
        xt = x_ref[...].T.astype(BF16)
        xt_sc[...] = xt
        acc_sc[...] = jnp.zeros_like(acc_sc)
        qt = jnp.dot(wq_ref[...], xt, preferred_element_type=F32).astype(BF16)
        half = sk_ref.shape[2]
        for c in range(sk_ref.shape[0]):
            st_sc[c * nk:(c + 1) * nk, :] = jnp.dot(sk_ref[c], qt[c * half:(c + 1) * half],
                                                    preferred_element_type=F32)

        def scores_of(h):
            return (st_sc[pl.ds(pl.multiple_of(2 * h * nk, nk), nk), :],
                    st_sc[pl.ds(pl.multiple_of((2 * h + 1) * nk, nk), nk), :])

        def put(h, n_i, e0, rk1, e1):
            n_sc[h] = n_i
            e0_sc[h] = e0
            rk1_sc[h] = rk1.astype(BF16)
            e1_sc[h] = e1.astype(BF16)

        @pl.loop(0, PEER_HEADS // 2)
        def _(hh):
            heads = (2 * hh, 2 * hh + 1)
            flags = []
            for h in heads:
                *routing, bad = _route_fast(*scores_of(h))
                put(h, *routing)
                flags.append(jnp.max(bad) > 0.0)
            for h, flag in zip(heads, flags):
                @pl.when(flag)
                def _():
                    put(h, *_route_exact(*scores_of(h)))

    rows = u_ref.shape[0] // nk
    ht = jnp.dot(u_ref[...], xt_sc[...], preferred_element_type=F32)
    act = _gelu_tanh(ht).astype(BF16)
    for ii in range(rows):
        w = jnp.zeros((nk, xt_sc.shape[1]), BF16)
        for h in range(PEER_HEADS):
            n_row = n_sc[h, pl.ds(e * rows + ii, 1), :].astype(BF16)
            e0_row = e0_sc[h, pl.ds(e * rows + ii, 1), :].astype(BF16)
            w = w + jnp.where(rk1_sc[h] < n_row, e0_row * e1_sc[h], 0)
        p_sc[ii * nk:(ii + 1) * nk, :] = w * act[ii * nk:(ii + 1) * nk]
    acc_sc[...] += _tn_dot(v_ref[...], p_sc[...])

    @pl.when(e == pl.num_programs(1) - 1)
    def _():
        o_ref[...] = _layer_norm(DN_ALPHA * x_ref[...] + acc_sc[...].T, g_ref[...], b_ref[...])


def _stage_peer(x1, w_q, subkeys, u_tab, v_tab, ln_g, ln_b, tt=512, et=1024):
    n = x1.shape[0]
    tt = min(tt, n)
    H, nk, half = PEER_HEADS, PEER_NKEYS, subkeys.shape[-1]
    wq_t = w_q.T.astype(BF16)
    sk_t = subkeys.reshape(2 * H, nk, half).astype(BF16)
    ne = u_tab.shape[0]
    return pl.pallas_call(
        _peer_body, grid=(n // tt, ne // et),
        in_specs=[pl.BlockSpec((tt, D_MODEL), lambda t, e: (t, 0)),
                  _const_spec(wq_t.shape), _const_spec(sk_t.shape),
                  pl.BlockSpec((et, D_MODEL), lambda t, e: (e, 0)),
                  pl.BlockSpec((et, D_MODEL), lambda t, e: (e, 0)),
                  _const_spec((1, D_MODEL)), _const_spec((1, D_MODEL))],
        out_specs=pl.BlockSpec((tt, D_MODEL), lambda t, e: (t, 0)),
        out_shape=jax.ShapeDtypeStruct((n, D_MODEL), F32),
        scratch_shapes=[pltpu.VMEM((D_MODEL, tt), BF16), pltpu.VMEM((D_MODEL, tt), F32),
                        pltpu.VMEM((H, nk, tt), F32), pltpu.VMEM((H, nk, tt), F32),
                        pltpu.VMEM((H, nk, tt), BF16), pltpu.VMEM((H, nk, tt), BF16),
                        pltpu.VMEM((et, tt), BF16), pltpu.VMEM((2 * H * nk, tt), F32)],
        compiler_params=_cparams(("parallel", "arbitrary")), name="peer_ffn")(
            x1, wq_t, sk_t, u_tab.astype(BF16), v_tab.astype(BF16),
            ln_g.reshape(1, -1).astype(F32), ln_b.reshape(1, -1).astype(F32))


def _layer(x, w_in, b_in, ssm_a_re, ssm_a_im, ssm_log_dt, ssm_b_re, ssm_b_im, ssm_c_re, ssm_c_im, ssm_d,
           w_glu, w_up_ssm, nsa_pe_k, nsa_w1_k, nsa_w2_k, nsa_pe_v, nsa_w1_v, nsa_w2_v, w_up_nsa,
           w_out, ln1_g, ln1_b, peer_w_q, peer_subkeys, peer_u, peer_v, ln2_g, ln2_b):
    bsz, seq, _ = x.shape
    x2 = x.reshape(bsz * seq, D_MODEL)
    u, qp, kvc, kv, gates, gm = _stage_proj(x2, w_in, b_in, seq)
    ya = _stage_ssm(u, bsz, seq, ssm_a_re, ssm_a_im, ssm_log_dt, ssm_b_re, ssm_b_im, ssm_c_re, ssm_c_im,
                    ssm_d, w_glu, w_up_ssm)
    kcv = _stage_compress(kvc, bsz, seq, nsa_pe_k, nsa_w1_k, nsa_w2_k, nsa_pe_v, nsa_w1_v, nsa_w2_v)
    ob = _stage_attn(qp, kcv, kv, gates, bsz, seq)
    x1 = _stage_merge(x2, ya, ob, gm, w_up_nsa, w_out, ln1_g, ln1_b)
    out = _stage_peer(x1, peer_w_q, peer_subkeys, peer_u, peer_v, ln2_g, ln2_b)
    return out.reshape(bsz, seq, D_MODEL)


def kernel(x, w_in, b_in, ssm_a_re, ssm_a_im, ssm_log_dt, ssm_b_re, ssm_b_im, ssm_c_re, ssm_c_im, ssm_d, w_glu,
           w_up_ssm, nsa_pe_k, nsa_w1_k, nsa_w2_k, nsa_pe_v, nsa_w1_v, nsa_w2_v, w_up_nsa, w_out, ln1_g, ln1_b,
           peer_w_q, peer_subkeys, peer_u, peer_v, ln2_g, ln2_b):
    params = (w_in, b_in, ssm_a_re, ssm_a_im, ssm_log_dt, ssm_b_re, ssm_b_im, ssm_c_re, ssm_c_im, ssm_d, w_glu,
              w_up_ssm, nsa_pe_k, nsa_w1_k, nsa_w2_k, nsa_pe_v, nsa_w1_v, nsa_w2_v, w_up_nsa, w_out, ln1_g, ln1_b,
              peer_w_q, peer_subkeys, peer_u, peer_v, ln2_g, ln2_b)
    for layer in range(w_in.shape[0]):
        x = _layer(x, *[p[layer] for p in params])
    return x
```

```python
import functools
import math

import jax
import jax.numpy as jnp
import numpy as np
from jax import lax
from jax.experimental import pallas as pl
from jax.experimental.pallas import tpu as pltpu

F32 = jnp.float32
BF16 = jnp.bfloat16

D_MODEL = 1024
SSM_WIDTH = 512
SSM_GROUP = 16
SSM_GROUPS = 32
SSM_STATE = 64
SSM_CHUNK = 8
LANE_GROUPS = 8
N_LANE_TILES = SSM_WIDTH // 128
NSA_HEADS = 8
NSA_KV_GROUPS = 2
NSA_REP = NSA_HEADS // NSA_KV_GROUPS
NSA_HEAD_DIM = 64
KV_WIDTH = NSA_KV_GROUPS * NSA_HEAD_DIM
CMP_BLOCK = 32
CMP_STRIDE = 16
CMP_HIDDEN = 128
SEL_BLOCK = 64
SEL_TOPK = 16
WINDOW = 512
Q_BLOCK = 256
FORCE_BONUS = 1.0e4
PEER_HEADS = 8
PEER_NKEYS = 128
PEER_EXPERTS = PEER_NKEYS * PEER_NKEYS
PEER_TOPK = 16
DN_ALPHA = 2.0 ** 0.25
LN_EPS = 1e-5
NEG = -1e30
VMEM_LIMIT = 56 * 1024 * 1024


def _cparams(sem):
    return pltpu.CompilerParams(dimension_semantics=sem, vmem_limit_bytes=VMEM_LIMIT)


def _const_spec(shape):
    nd = len(shape)
    return pl.BlockSpec(shape, lambda *_: (0,) * nd, pipeline_mode=pl.Buffered(1))


def _layer_norm(z, g, b):
    mu = jnp.mean(z, axis=-1, keepdims=True)
    var = jnp.mean(jnp.square(z - mu), axis=-1, keepdims=True)
    return (z - mu) * lax.rsqrt(var + LN_EPS) * g + b


def _nt_dot(a, b):
    return lax.dot_general(a, b, (((1,), (1,)), ((), ())), preferred_element_type=F32)


def _tn_dot(a, b):
    return lax.dot_general(a, b, (((0,), (0,)), ((), ())), preferred_element_type=F32)


KV_REC = 640
KS_OFF, VS_OFF, KW_OFF, VW_OFF = 0, 256, 384, 512
POS_HI, POS_LO, ONE_LANE, BLOCK_LANE0 = 64, 65, 64, 128
assert SEL_BLOCK == 64


def _position_features(pos, lane):
    hi, lo = pos >> 6, pos & 63
    f = jnp.where(lane == KS_OFF + POS_HI, hi, 0) + jnp.where(lane == KW_OFF + POS_HI, hi, 0)
    f = f + jnp.where(lane == KS_OFF + POS_LO, lo, 0) + jnp.where(lane == KW_OFF + POS_LO, lo, 0)
    f = f + jnp.where(lane == KS_OFF + BLOCK_LANE0 + hi, 1, 0)
    f = f + jnp.where(lane == VS_OFF + ONE_LANE, 1, 0) + jnp.where(lane == VW_OFF + ONE_LANE, 1, 0)
    return f.astype(F32)


def _proj_body(seq, x_ref, wu, bu, wq, bq, wc, bc, wk, bk, wg, bg, wm, bm, aux_ref,
               u_o, q_o, c_o, k_o, g_o, m_o):
    xb = x_ref[...].astype(BF16)
    tm = x_ref.shape[0]
    dk = NSA_HEAD_DIM

    def lin(w, b):
        return jnp.dot(xb, w[...], preferred_element_type=F32) + b[...]

    def halves(a, tile):
        t = a[:, tile * 128:(tile + 1) * 128]
        return t, pltpu.roll(t, dk, 1)

    low = lax.broadcasted_iota(jnp.int32, (tm, 128), 1) < dk
    u_o[...] = lin(wu, bu)
    q = lin(wq, bq) * (dk ** -0.5)
    for tile in range(NSA_HEADS // 2):
        for half, piece in enumerate(halves(q, tile)):
            h = 2 * tile + half
            q_o[:, h * 128:(h + 1) * 128] = jnp.where(low, piece, aux_ref[h:h + 1, :]).astype(BF16)
    c_o[...] = lin(wc, bc)
    kv = lin(wk, bk)
    parts = [halves(kv, tile) for tile in range(4)]
    zero = jnp.zeros((tm, 128), F32)
    recs = []
    for g in range(NSA_KV_GROUPS):
        k_s, v_s, k_w, v_w = (jnp.where(low, parts[tile][g], 0.0) for tile in range(4))
        recs += [k_s, zero, v_s, k_w, v_w]
    pos0 = (pl.program_id(0) % (seq // tm)) * tm
    shape = (tm, NSA_KV_GROUPS * KV_REC)
    lane = lax.broadcasted_iota(jnp.int32, shape, 1)
    lane = jnp.where(lane >= KV_REC, lane - KV_REC, lane)
    feat = _position_features(pos0 + lax.broadcasted_iota(jnp.int32, shape, 0), lane)
    k_o[...] = (jnp.concatenate(recs, axis=1) + feat).astype(BF16)
    g_o[...] = lin(wg, bg)
    m_o[...] = jax.nn.sigmoid(lin(wm, bm)).astype(m_o.dtype)


def _stage_proj(x2, w_in, b_in, seq):
    n = x2.shape[0]
    tm = 256
    o0 = SSM_WIDTH
    o1 = o0 + NSA_HEADS * NSA_HEAD_DIM
    o2 = o1 + 6 * KV_WIDTH
    o3 = o2 + 3 * NSA_HEADS
    w_u, b_u = w_in[:, :o0], b_in[:o0]
    w_q, b_q = w_in[:, o0:o1], b_in[o0:o1]
    w_kv, b_kv = w_in[:, o1:o2], b_in[o1:o2]
    w_g, b_g = w_in[:, o2:o3], b_in[o2:o3]
    w_m, b_m = w_in[:, o3:], b_in[o3:]

    slope = 2.0 ** -(np.arange(NSA_HEADS, dtype=np.float32) + 1.0)
    q_aux = np.zeros((NSA_HEADS, 128), np.float32)
    q_aux[:, POS_HI] = slope * 64.0
    q_aux[:, POS_LO] = slope

    def pad_g(a):
        a = a.reshape(a.shape[:-1] + (NSA_KV_GROUPS, 3 * NSA_REP))
        a = jnp.concatenate([a, jnp.zeros(a.shape[:-1] + (128 - 3 * NSA_REP,), a.dtype)], axis=-1)
        return a.reshape(a.shape[:-2] + (NSA_KV_GROUPS * 128,))

    ws = [w_u, w_q, w_kv[:, :2 * KV_WIDTH], w_kv[:, 2 * KV_WIDTH:], pad_g(w_g), w_m]
    bs = [b_u, b_q, b_kv[:2 * KV_WIDTH], b_kv[2 * KV_WIDTH:], pad_g(b_g), b_m]
    odt = [F32, BF16, F32, BF16, F32, BF16]
    widths = [SSM_WIDTH, NSA_HEADS * 128, 2 * KV_WIDTH, NSA_KV_GROUPS * KV_REC, NSA_KV_GROUPS * 128, 2 * D_MODEL]
    args, in_specs = [x2], [pl.BlockSpec((tm, D_MODEL), lambda i: (i, 0))]
    for w, b in zip(ws, bs):
        args += [w.astype(BF16), b.reshape(1, -1).astype(F32)]
        in_specs += [_const_spec(w.shape), _const_spec((1, w.shape[1]))]
    args.append(jnp.asarray(q_aux))
    in_specs.append(_const_spec(q_aux.shape))
    out_shape = [jax.ShapeDtypeStruct((n, w), dt) for w, dt in zip(widths, odt)]
    out_specs = [pl.BlockSpec((tm, w), lambda i: (i, 0)) for w in widths]
    return pl.pallas_call(
        functools.partial(_proj_body, seq), grid=(n // tm,), in_specs=in_specs, out_specs=out_specs, out_shape=out_shape,
        compiler_params=_cparams(("parallel",)), name="in_proj")(*args)


def _ssm_weights(a_re, a_im, log_dt, b_re, b_im, c_re, c_im):
    L, G, P, H, A, J = SSM_CHUNK, SSM_GROUPS, SSM_STATE, SSM_GROUP, LANE_GROUPS, N_LANE_TILES
    lam = lax.complex(a_re.astype(F32), a_im.astype(F32))
    dt = jnp.exp(log_dt.astype(F32))[:, None]
    lam_bar = jnp.exp(lam * dt)
    b_bar = ((lam_bar - 1.0) / lam)[:, :, None] * lax.complex(b_re.astype(F32), b_im.astype(F32))
    c = lax.complex(c_re.astype(F32), c_im.astype(F32))
    k = jnp.arange(L + 1, dtype=F32)
    pw = jnp.exp((lam * dt)[None] * k[:, None, None])
    def spread(x, n_inner):
        src = np.arange(x.shape[2])
        dst = np.arange(x.shape[2] * A)
        sel = ((src[:, None] // n_inner == dst[None, :] // (A * n_inner))
               & (src[:, None] % n_inner == dst[None, :] % n_inner))
        return jnp.einsum('jrk,kc->jrc', x, jnp.asarray(sel, F32)), (dst % (A * n_inner)) // n_inner

    def group_mask(row_group, col_group):
        return jnp.asarray(row_group[:, None] == col_group[None, :], F32)

    in_rows = (np.arange(L * 128) % 128) // H
    st_rows = (np.arange(2 * A * P) % (A * P)) // P
    kern = jnp.real(jnp.einsum('ghp,kgp,gpj->kghj', c, pw[:L], b_bar))
    s_i, t_i = jnp.arange(L)[:, None], jnp.arange(L)[None, :]
    tau = jnp.clip(t_i - s_i, 0, L - 1)
    causal = (t_i >= s_i).astype(F32)
    k_st = (kern[tau] * causal[:, :, None, None, None]).reshape(L, L, J, A, H, H)
    k_st = k_st.transpose(2, 0, 3, 5, 1, 4).reshape(J, L * 128, L * H)
    w_intra, cg = spread(k_st, H)
    w_intra = w_intra * group_mask(in_rows, cg)
    q = (pw[:L][::-1][:, :, :, None] * b_bar[None]).reshape(L, J, A, P, H)
    q = jnp.stack([jnp.real(q), jnp.imag(q)], axis=0)
    q = q.transpose(2, 1, 3, 5, 0, 4).reshape(J, L * 128, 2 * P)
    m2, cg = spread(q, P)
    m2 = m2 * group_mask(in_rows, cg)
    cl = (c[None] * pw[1:][:, :, None, :]).reshape(L, J, A, H, P)
    cl = jnp.stack([jnp.real(cl), -jnp.imag(cl)], axis=0)
    cl = cl.transpose(2, 0, 3, 5, 1, 4).reshape(J, 2 * A * P, L * H)
    m1, cg = spread(cl, H)
    m1 = m1 * group_mask(st_rows, cg)
    w2 = jnp.concatenate([w_intra, m1], axis=1)
    lam8 = pw[L].reshape(G * P)
    return m2.astype(BF16), w2.astype(BF16), jnp.real(lam8), jnp.imag(lam8)


def _split_lane_tiles(src_ref, dst_sc):
    for j in range(dst_sc.shape[0]):
        dst_sc[j] = src_ref[:, j * 128:(j + 1) * 128]


def _chunk_inputs(u_sc, j, rows):
    return jnp.concatenate([u_sc[j, pl.ds(s, rows, stride=SSM_CHUNK), :] for s in range(SSM_CHUNK)], axis=1)


def _ssm_state_body(u_ref, m2_ref, zre_ref, zim_ref, u_sc):
    half = LANE_GROUPS * SSM_STATE
    _split_lane_tiles(u_ref, u_sc)
    for j in range(N_LANE_TILES):
        z = jnp.dot(_chunk_inputs(u_sc, j, zre_ref.shape[0]).astype(BF16), m2_ref[j], preferred_element_type=F32)
        zre_ref[:, j * half:(j + 1) * half] = z[:, :half]
        zim_ref[:, j * half:(j + 1) * half] = z[:, half:]


def _ssm_scan_body(zre_ref, zim_ref, lre_ref, lim_ref, xre_ref, xim_ref):
    lr, li = lre_ref[...], lim_ref[...]

    def step(c, carry):
        xr, xi = carry
        xre_ref[pl.ds(c, 1), :] = xr
        xim_ref[pl.ds(c, 1), :] = xi
        return lr * xr - li * xi + zre_ref[pl.ds(c, 1), :], lr * xi + li * xr + zim_ref[pl.ds(c, 1), :]

    zero = jnp.zeros(lr.shape, F32)
    lax.fori_loop(0, zre_ref.shape[0], step, (zero, zero))


def _ssm_out_body(u_ref, xre_ref, xim_ref, w2_ref, d_ref, wglu_ref, wup_ref, o_ref, u_sc, y_sc):
    half = LANE_GROUPS * SSM_STATE
    rows = xre_ref.shape[0]
    _split_lane_tiles(u_ref, u_sc)
    for j in range(N_LANE_TILES):
        lhs = jnp.concatenate(
            [_chunk_inputs(u_sc, j, rows), xre_ref[:, j * half:(j + 1) * half], xim_ref[:, j * half:(j + 1) * half]],
            axis=1).astype(BF16)
        yj = jnp.dot(lhs, w2_ref[j], preferred_element_type=F32)
        for t in range(SSM_CHUNK):
            y_sc[j, pl.ds(t, rows, stride=SSM_CHUNK), :] = yj[:, t * 128:(t + 1) * 128]
    y = jnp.concatenate([y_sc[j] for j in range(N_LANE_TILES)], axis=1) + d_ref[...] * u_ref[...]
    gl = jnp.dot(jax.nn.gelu(y).astype(BF16), wglu_ref[...], preferred_element_type=F32)
    v = gl[:, :SSM_WIDTH] * jax.nn.sigmoid(gl[:, SSM_WIDTH:])
    o_ref[...] = jnp.dot(v.astype(BF16), wup_ref[...], preferred_element_type=F32).astype(o_ref.dtype)


def _stage_ssm(u, bsz, seq, a_re, a_im, log_dt, b_re, b_im, c_re, c_im, d_skip, w_glu, w_up):
    n = bsz * seq
    L = SSM_CHUNK
    nck = seq // L
    gp = SSM_GROUPS * SSM_STATE
    m2, w2, l8re, l8im = _ssm_weights(a_re, a_im, log_dt, b_re, b_im, c_re, c_im)
    rs = min(nck, 512)
    zre, zim = pl.pallas_call(
        _ssm_state_body, grid=(bsz, nck // rs),
        in_specs=[pl.BlockSpec((rs * L, SSM_WIDTH), lambda b, i: (b * (nck // rs) + i, 0)),
                  _const_spec(m2.shape)],
        out_specs=[pl.BlockSpec((rs, gp), lambda b, i: (i, b))] * 2,
        out_shape=[jax.ShapeDtypeStruct((nck, bsz * gp), F32)] * 2,
        scratch_shapes=[pltpu.VMEM((N_LANE_TILES, rs * L, 128), F32)],
        compiler_params=_cparams(("parallel", "parallel")), name="ssm_chunk_state")(u, m2)
    cw = 1024
    ncol = bsz * gp // cw
    lre = jnp.tile(l8re, bsz).reshape(1, bsz * gp)
    lim = jnp.tile(l8im, bsz).reshape(1, bsz * gp)
    col = pl.BlockSpec((nck, cw), lambda i: (0, i))
    lam = pl.BlockSpec((1, cw), lambda i: (0, i))
    xre, xim = pl.pallas_call(
        _ssm_scan_body, grid=(ncol,), in_specs=[col, col, lam, lam], out_specs=[col, col],
        out_shape=[jax.ShapeDtypeStruct((nck, bsz * gp), F32)] * 2,
        compiler_params=_cparams(("parallel",)), name="ssm_carry_scan")(zre, zim, lre, lim)
    ro = min(nck, 128)
    d8 = d_skip.reshape(1, SSM_WIDTH).astype(F32)
    return pl.pallas_call(
        _ssm_out_body, grid=(bsz, nck // ro),
        in_specs=[pl.BlockSpec((ro * L, SSM_WIDTH), lambda b, i: (b * (nck // ro) + i, 0)),
                  pl.BlockSpec((ro, gp), lambda b, i: (i, b)),
                  pl.BlockSpec((ro, gp), lambda b, i: (i, b)),
                  _const_spec(w2.shape), _const_spec((1, SSM_WIDTH)),
                  _const_spec(w_glu.shape), _const_spec(w_up.shape)],
        out_specs=pl.BlockSpec((ro * L, D_MODEL), lambda b, i: (b * (nck // ro) + i, 0)),
        out_shape=jax.ShapeDtypeStruct((n, D_MODEL), BF16),
        scratch_shapes=[pltpu.VMEM((N_LANE_TILES, ro * L, 128), F32)] * 2,
        compiler_params=_cparams(("parallel", "parallel")), name="ssm_out")(
            u, xre, xim, w2, d8, w_glu.astype(BF16), w_up.astype(BF16))


def _compress_body(kv_ref, pea_ref, peb_ref, w1a_ref, w1b_ref, w2_ref, o_ref, kv_sc):
    nrow = o_ref.shape[0]
    _split_lane_tiles(kv_ref, kv_sc)
    f = jnp.concatenate([kv_sc[j, pl.ds(l, nrow, stride=CMP_STRIDE), :]
                         for l in range(CMP_STRIDE) for j in range(kv_sc.shape[0])], axis=1)
    a = jnp.dot((f + pea_ref[...]).astype(BF16), w1a_ref[...], preferred_element_type=F32)
    b = jnp.dot((f + peb_ref[...]).astype(BF16), w1b_ref[...], preferred_element_type=F32)
    pre = a + pltpu.roll(b, b.shape[0] - 1, 0)
    hid = jax.nn.gelu(pre)
    out = jnp.dot(hid.astype(BF16), w2_ref[...], preferred_element_type=F32)
    lane = lax.broadcasted_iota(jnp.int32, out.shape, 1) & 255
    c_end = lax.broadcasted_iota(jnp.int32, out.shape, 0) * CMP_STRIDE + (CMP_BLOCK - 1)
    feat = jnp.where(lane == POS_HI, c_end >> 6, 0) + jnp.where(lane == POS_LO, c_end & 63, 0)
    o_ref[...] = (out + feat.astype(F32)).astype(BF16)


def _stage_compress(kvc, bsz, seq, pe_k, w1_k, w2_k, pe_v, w1_v, w2_v):
    hl = CMP_STRIDE
    nrow = seq // hl
    G, dk, hid = NSA_KV_GROUPS, NSA_HEAD_DIM, CMP_HIDDEN
    eye = jnp.eye(2 * G, dtype=F32)

    def big_w1(lo):
        wk = w1_k.reshape(CMP_BLOCK, dk, hid)[lo:lo + hl]
        wv = w1_v.reshape(CMP_BLOCK, dk, hid)[lo:lo + hl]
        w = jnp.stack([wk, wk, wv, wv], axis=1)
        return jnp.einsum('lcdj,ce->lcdej', w, eye).reshape(hl * 2 * G * dk, 2 * G * hid)

    def big_pe(lo):
        pk, pv = pe_k[lo:lo + hl], pe_v[lo:lo + hl]
        return jnp.stack([pk, pk, pv, pv], axis=1).reshape(1, hl * 2 * G * dk)

    w2 = jnp.stack([w2_k, w2_k, w2_v, w2_v], axis=0)
    w2 = jnp.concatenate([w2, jnp.zeros((2 * G, hid, 128 - dk), w2.dtype)], axis=-1)
    slot = eye[np.array([kv * G + g for g in range(G) for kv in range(2)])].T
    w2 = jnp.einsum('cjd,ce->cjed', w2, slot).reshape(2 * G * hid, 2 * G * 128)
    width = hl * 2 * G * dk
    return pl.pallas_call(
        _compress_body, grid=(bsz,),
        in_specs=[pl.BlockSpec((seq, 2 * G * dk), lambda b: (b, 0)),
                  _const_spec((1, width)), _const_spec((1, width)),
                  _const_spec((width, 2 * G * hid)), _const_spec((width, 2 * G * hid)),
                  _const_spec((2 * G * hid, 2 * G * 128))],
        out_specs=pl.BlockSpec((nrow, 2 * G * 128), lambda b: (b, 0)),
        out_shape=jax.ShapeDtypeStruct((bsz * nrow, 2 * G * 128), BF16),
        scratch_shapes=[pltpu.VMEM((2 * G * dk // 128, seq, 128), F32)],
        compiler_params=_cparams(("parallel",)), name="nsa_compress")(
            kvc, big_pe(0).astype(F32), big_pe(hl).astype(F32),
            big_w1(0).astype(BF16), big_w1(hl).astype(BF16), w2.astype(BF16))


MASK_BIG = 2.0 ** 100
SEL_TILE = 512


def _nsa_body(q_ref, kcv_ref, kv_ref, g_ref, ov_ref, rel_ref, relc_ref, o_ref, acc, s_buf, p_buf, a_buf):
    i = pl.program_id(2)
    qs = i * Q_BLOCK
    seq = kv_ref.shape[0]
    rows = NSA_REP * Q_BLOCK
    dk = NSA_HEAD_DIM
    qh = [q_ref[:, r * 128:(r + 1) * 128] for r in range(NSA_REP)]
    q_all = jnp.concatenate(qh, axis=0)

    kc = kcv_ref[:, :128]
    vc = kcv_ref[:, 128:]
    vis_c = relc_ref[...] >= -qs
    s = jnp.where(vis_c, _nt_dot(q_all, kc), NEG)
    p = jnp.exp(s - jnp.max(s, axis=-1, keepdims=True)) * jnp.where(vis_c, 1.0, 0.0)
    p = (p / jnp.maximum(jnp.sum(p, axis=-1, keepdims=True), 1e-30)).astype(BF16)
    o_cmp = jnp.dot(p, vc, preferred_element_type=F32)

    wlen = min(WINDOW + Q_BLOCK, seq)
    w0 = pl.multiple_of(jnp.maximum(qs + Q_BLOCK - wlen, 0), Q_BLOCK)
    rel_w = rel_ref[...]
    s = _nt_dot(q_all, kv_ref[pl.ds(w0, wlen), KW_OFF:KW_OFF + 128])
    in_band = pltpu.bitcast(rel_w - (w0 - qs), jnp.uint32) < WINDOW
    s = jnp.where(in_band, s, NEG)
    p_w = jnp.exp(s - jnp.max(s, axis=-1, keepdims=True)).astype(BF16)
    a_win = jnp.dot(p_w, kv_ref[pl.ds(w0, wlen), VW_OFF:VW_OFF + 128], preferred_element_type=F32)

    p_heads = jnp.concatenate([p[r * Q_BLOCK:(r + 1) * Q_BLOCK] for r in range(NSA_REP)], axis=1)
    imp = jnp.dot(p_heads, ov_ref[...], preferred_element_type=F32)
    coli = lax.broadcasted_iota(jnp.int32, (Q_BLOCK, 128), 1)
    t_q = qs + lax.broadcasted_iota(jnp.int32, (Q_BLOCK, 128), 0)
    cur = t_q >> 6
    forced = (coli == 0) | (coli == cur) | (coli == cur - 1)
    score = jnp.where(coli * SEL_BLOCK <= t_q, imp + jnp.where(forced, FORCE_BONUS, 0.0), NEG)
    nsl = seq // SEL_BLOCK
    st = score.T[:nsl]
    groups = [st[8 * v:8 * v + 8] for v in range(nsl // 8)]
    jsub = lax.broadcasted_iota(jnp.int32, (8, Q_BLOCK), 0)
    rank = [jnp.zeros((8, Q_BLOCK), F32) for _ in groups]
    for k in range(nsl):
        rk = st[k:k + 1, :]
        for v, sv in enumerate(groups):
            if v > k // 8:
                one = jnp.where(rk >= sv, 1.0, 0.0)
            elif v < k // 8:
                one = jnp.where(rk > sv, 1.0, 0.0)
            else:
                one = jnp.where(rk > sv, 1.0, jnp.where(jsub > k % 8, jnp.where(rk == sv, 1.0, 0.0), 0.0))
            rank[v] = rank[v] + one
    drop_t = jnp.where(jnp.concatenate(rank, axis=0) < float(min(SEL_TOPK, nsl)), 0.0, -MASK_BIG)
    if nsl < 128:
        drop_t = jnp.concatenate([drop_t, jnp.zeros((128 - nsl, Q_BLOCK), F32)], axis=0)
    drop = drop_t.T.astype(BF16)
    q_sel = jnp.concatenate([jnp.concatenate([qh[r], drop], axis=1) for r in range(NSA_REP)], axis=0)

    n_tiles = (qs + Q_BLOCK + SEL_TILE - 1) // SEL_TILE

    def scores(kt):
        k0 = pl.multiple_of(kt * SEL_TILE, SEL_TILE)
        return _nt_dot(q_sel, kv_ref[pl.ds(k0, SEL_TILE), KS_OFF:KS_OFF + 256])

    def values(kt):
        k0 = pl.multiple_of(kt * SEL_TILE, SEL_TILE)
        return kv_ref[pl.ds(k0, SEL_TILE), VS_OFF:VS_OFF + 128]

    def trip(diagonal, kt, m_old):
        if not diagonal:
            s_next = scores(kt + 1)
        acc[...] = a_buf[...] * acc[...] + jnp.dot(p_buf[...], values(jnp.maximum(kt - 1, 0)),
                                                   preferred_element_type=F32)
        s = s_buf[...]
        if diagonal:
            s = jnp.where(rel_ref[:, :SEL_TILE] >= kt * SEL_TILE - qs, s, NEG)
        m_new = jnp.maximum(m_old, jnp.max(s, axis=-1, keepdims=True))
        p_buf[...] = jnp.exp(s - m_new).astype(BF16)
        a_buf[...] = jnp.exp(m_old - m_new)
        if not diagonal:
            s_buf[...] = s_next
        return m_new

    acc[...] = jnp.zeros_like(acc)
    p_buf[...] = jnp.zeros_like(p_buf)
    a_buf[...] = jnp.ones_like(a_buf)
    s_buf[...] = scores(0)
    m_s = lax.fori_loop(0, n_tiles - 1, functools.partial(trip, False), jnp.full((rows, 1), NEG, F32))
    trip(True, n_tiles - 1, m_s)
    a_sel = a_buf[...] * acc[...] + jnp.dot(p_buf[...], values(n_tiles - 1), preferred_element_type=F32)

    gate = jax.nn.sigmoid(g_ref[...])
    outs = []
    for r in range(NSA_REP):
        rr = slice(r * Q_BLOCK, (r + 1) * Q_BLOCK)
        o_s = a_sel[rr, :dk] / a_sel[rr, ONE_LANE:ONE_LANE + 1]
        o_w = a_win[rr, :dk] / a_win[rr, ONE_LANE:ONE_LANE + 1]
        outs.append(gate[:, 3 * r:3 * r + 1] * o_cmp[rr, :dk] + gate[:, 3 * r + 1:3 * r + 2] * o_s
                    + gate[:, 3 * r + 2:3 * r + 3] * o_w)
    o_ref[...] = jnp.concatenate(outs, axis=1).astype(o_ref.dtype)


def _stage_attn(qp, kcv, kv, gates, bsz, seq):
    n = bsz * seq
    nq = seq // Q_BLOCK
    ncmp = seq // CMP_STRIDE
    nsl = seq // SEL_BLOCK
    assert seq % SEL_TILE == 0
    c_start = np.arange(ncmp) * CMP_STRIDE
    s_start = np.arange(nsl) * SEL_BLOCK
    ov = ((c_start[:, None] < s_start[None, :] + SEL_BLOCK) & (c_start[:, None] + CMP_BLOCK > s_start[None, :]))
    ov = np.pad(ov.astype(np.float32), ((0, 0), (0, 128 - nsl)))
    ov[ncmp - 1] = 0.0
    ov = jnp.asarray(np.tile(ov, (NSA_REP, 1)), BF16)
    width = NSA_REP * NSA_HEAD_DIM
    rows = NSA_REP * Q_BLOCK
    wlen = min(WINDOW + Q_BLOCK, seq)
    assert wlen >= SEL_TILE
    offs = jnp.arange(rows, dtype=jnp.int32)[:, None] % Q_BLOCK
    rel = offs - jnp.arange(wlen, dtype=jnp.int32)[None, :]
    relc = offs - (jnp.arange(ncmp, dtype=jnp.int32)[None, :] * CMP_STRIDE + (CMP_BLOCK - 1))
    return pl.pallas_call(
        _nsa_body, grid=(bsz, NSA_KV_GROUPS, nq),
        in_specs=[pl.BlockSpec((Q_BLOCK, NSA_REP * 128), lambda b, g, i: (b * nq + i, g)),
                  pl.BlockSpec((ncmp, 256), lambda b, g, i: (b, g)),
                  pl.BlockSpec((seq, KV_REC), lambda b, g, i: (b, g)),
                  pl.BlockSpec((Q_BLOCK, 128), lambda b, g, i: (b * nq + i, g)),
                  _const_spec(ov.shape), _const_spec(rel.shape), _const_spec(relc.shape)],
        out_specs=pl.BlockSpec((Q_BLOCK, width), lambda b, g, i: (b * nq + i, g)),
        out_shape=jax.ShapeDtypeStruct((n, NSA_KV_GROUPS * width), BF16),
        scratch_shapes=[pltpu.VMEM((rows, 128), F32), pltpu.VMEM((rows, SEL_TILE), F32),
                        pltpu.VMEM((rows, SEL_TILE), BF16), pltpu.VMEM((rows, 1), F32)],
        compiler_params=_cparams(("parallel", "parallel", "arbitrary")), name="nsa_attention")(
            qp, kcv, kv, gates, ov, rel, relc)


def _merge_body(x_ref, ya_ref, ob_ref, gm_ref, wnsa_ref, wout_ref, g_ref, b_ref, o_ref):
    yb = jnp.dot(ob_ref[...], wnsa_ref[...], preferred_element_type=F32)
    gm = gm_ref[...].astype(F32)
    mix_in = gm[:, :D_MODEL] * ya_ref[...].astype(F32) + gm[:, D_MODEL:] * yb
    mix = jnp.dot(mix_in.astype(BF16), wout_ref[...], preferred_element_type=F32)
    o_ref[...] = _layer_norm(DN_ALPHA * x_ref[...] + mix, g_ref[...], b_ref[...])


def _stage_merge(x2, ya, ob, gm, w_up_nsa, w_out, ln_g, ln_b):
    n = x2.shape[0]
    tm = 512
    row = lambda w: pl.BlockSpec((tm, w), lambda i: (i, 0))
    return pl.pallas_call(
        _merge_body, grid=(n // tm,),
        in_specs=[row(D_MODEL), row(D_MODEL), row(ob.shape[1]), row(2 * D_MODEL),
                  _const_spec(w_up_nsa.shape), _const_spec(w_out.shape),
                  _const_spec((1, D_MODEL)), _const_spec((1, D_MODEL))],
        out_specs=row(D_MODEL), out_shape=jax.ShapeDtypeStruct((n, D_MODEL), F32),
        compiler_params=_cparams(("parallel",)), name="merge_ln")(
            x2, ya, ob, gm, w_up_nsa.astype(BF16), w_out.astype(BF16),
            ln_g.reshape(1, -1).astype(F32), ln_b.reshape(1, -1).astype(F32))


BIG_NEG = -3.0e38


def _top_rows(s, k):
    rid = lax.broadcasted_iota(jnp.int32, s.shape, 0).astype(F32)
    rank = jnp.full(s.shape, float(k), F32)
    vals, idxs = [], []
    for r in range(k):
        m = jnp.max(s, axis=0, keepdims=True)
        idx = jnp.min(jnp.where(s == m, rid, float(s.shape[0])), axis=0, keepdims=True)
        hit = rid == idx
        rank = jnp.where(hit, float(r), rank)
        s = jnp.where(hit, BIG_NEG, s)
        vals.append(m)
        idxs.append(idx)
    return jnp.concatenate(vals, axis=0), jnp.concatenate(idxs, axis=0), rank


def _max_rounds(s, k, want_round=True):
    rnd = jnp.full(s.shape, float(k), F32) if want_round else None
    vals = []
    for r in range(k):
        m = jnp.max(s, axis=0, keepdims=True)
        eq = s == m
        if want_round:
            rnd = jnp.where(eq, float(r), rnd)
        s = jnp.where(eq, BIG_NEG, s)
        vals.append(m)
    return jnp.concatenate(vals, axis=0), rnd


def _pair_candidates(a, b):
    c8 = lax.broadcasted_iota(jnp.int32, (8, a.shape[1]), 0)
    parts = [a[0:1] + b[0:8], a[0:1] + b[8:16]]
    for r in range(1, 8):
        keep = PEER_TOPK // (r + 1)
        cand = a[r:r + 1] + b[0:8]
        parts.append(cand if keep >= 8 else jnp.where(c8 < keep, cand, BIG_NEG))
    parts.append(a[8:16] + b[0:1])
    return jnp.concatenate(parts, axis=0)


def _picks_per_rank(picked):
    rows = [jnp.sum(picked[0:16], axis=0, keepdims=True)]
    rows += [jnp.sum(picked[8 * (r + 1):8 * (r + 2)], axis=0, keepdims=True) for r in range(1, 8)]
    return jnp.concatenate(rows + [picked[72:80]], axis=0)


def _spread_by_rank(rank, per_rank):
    out = jnp.zeros(rank.shape, F32)
    for r in range(PEER_TOPK):
        out = out + jnp.where(rank == float(r), per_rank[r:r + 1], 0.0)
    return out


def _sort16_network():
    pairs, p = [], 1
    while p < 16:
        k = p
        while k >= 1:
            for j in range(k % p, 16 - k, 2 * k):
                for i in range(min(k, 16 - j - k)):
                    if (i + j) // (2 * p) == (i + j + k) // (2 * p):
                        pairs.append((i + j, i + j + k))
            k //= 2
        p *= 2
    return pairs


_SORT16 = _sort16_network()


def _sorted_top16(s):
    def exchange(v, i, j):
        v[i], v[j] = jnp.maximum(v[i], v[j]), jnp.minimum(v[i], v[j])

    v = [s[8 * k:8 * k + 8] for k in range(16)]
    for i, j in _SORT16:
        exchange(v, i, j)
    for shift in (4, 2, 1):
        v = [jnp.maximum(v[k], pltpu.roll(v[15 - k], shift, 0)) for k in range(16)]
        for d in (8, 4, 2, 1):
            for i in range(16):
                if i & d == 0:
                    exchange(v, i, i + d)
    return jnp.concatenate([x[0:1] for x in v], axis=0)


def _count_rows(vals, s, strict):
    rows = [jnp.broadcast_to(vals[r:r + 1], (8, s.shape[1])) for r in range(vals.shape[0])]
    assert len(rows) == 16
    outs = []
    for c in range(0, s.shape[0], 8):
        sc = s[c:c + 8]
        test = (lambda row: row > sc) if strict else (lambda row: row <= sc)
        c1 = test(rows[7])
        c2 = test(jnp.where(c1, rows[11], rows[3]))
        c3 = test(jnp.where(c1, jnp.where(c2, rows[13], rows[9]), jnp.where(c2, rows[5], rows[1])))
        hi = jnp.where(c2, jnp.where(c3, rows[14], rows[12]), jnp.where(c3, rows[10], rows[8]))
        lo = jnp.where(c2, jnp.where(c3, rows[6], rows[4]), jnp.where(c3, rows[2], rows[0]))
        c4 = test(jnp.where(c1, hi, lo))
        acc = (jnp.where(c1, 8.0, 0.0) + jnp.where(c2, 4.0, 0.0) + jnp.where(c3, 2.0, 0.0) + jnp.where(c4, 1.0, 0.0)
               + jnp.where(test(rows[15]), 1.0, 0.0))
        outs.append(acc)
    return jnp.concatenate(outs, axis=0)


def _route_fast(s0, s1):
    k = PEER_TOPK
    count = lambda v: jnp.sum(v, axis=0, keepdims=True)
    a, b = _sorted_top16(s0), _sorted_top16(s1)
    cand = _pair_candidates(a, b)
    best, _ = _max_rounds(cand, k, want_round=False)
    hit = cand >= best[k - 1:k]
    z = jnp.sum(jnp.where(hit, jnp.exp(cand - best[0:1]), 0.0), axis=0, keepdims=True)
    nrank = _picks_per_rank(jnp.where(hit, 1.0, 0.0))
    earns = jnp.concatenate([jnp.min(jnp.where(nrank >= float(v), a, -BIG_NEG), axis=0, keepdims=True)
                             for v in range(1, k + 1)], axis=0)
    n_i = _count_rows(earns, s0, strict=False)
    ties = lambda t, s: (count(jnp.where(t[:-1] == t[1:], 1.0, 0.0))
                         + jnp.abs(count(jnp.where(s >= t[k - 1:k], 1.0, 0.0)) - k))
    bad = ties(a, s0) + ties(b, s1) + jnp.abs(count(nrank) - k)
    return n_i, jnp.exp(s0 - a[0:1]) / z, _count_rows(b, s1, strict=True), jnp.exp(s1 - b[0:1]), bad


def _route_exact(s0, s1):
    k = PEER_TOPK
    a, _, rk0 = _top_rows(s0, k)
    b, _, rk1 = _top_rows(s1, k)
    cand = jnp.concatenate([a[r:r + 1] + b for r in range(k)], axis=0)
    best, bidx, _ = _top_rows(cand, k)
    brank = jnp.floor(bidx * (1.0 / k))
    z = jnp.sum(jnp.exp(best - best[0:1]), axis=0, keepdims=True)
    nrank = jnp.concatenate([jnp.sum(jnp.where(brank == float(r), 1.0, 0.0), axis=0, keepdims=True)
                             for r in range(k)], axis=0)
    return _spread_by_rank(rk0, nrank), jnp.exp(s0 - a[0:1]) / z, rk1, jnp.exp(s1 - b[0:1])


def _gelu_tanh(x):
    c = math.sqrt(2.0 / math.pi)
    return x / (1.0 + jnp.exp(x * (x * x * (-2.0 * c * 0.044715) - 2.0 * c)))


def _peer_body(x_ref, wq_ref, sk_ref, u_ref, v_ref, g_ref, b_ref, o_ref,
               xt_sc, acc_sc, n_sc, e0_sc, rk1_sc, e1_sc, p_sc, st_sc):
    e = pl.program_id(1)
    nk = PEER_NKEYS

    @pl.when(e == 0)
    def _():
        xt = x_ref[...].T.astype(BF16)
        xt_sc[...] = xt
        acc_sc[...] = jnp.zeros_like(acc_sc)
        qt = jnp.dot(wq_ref[...], xt, preferred_element_type=F32).astype(BF16)
        half = sk_ref.shape[2]
        for c in range(sk_ref.shape[0]):
            st_sc[c * nk:(c + 1) * nk, :] = jnp.dot(sk_ref[c], qt[c * half:(c + 1) * half],
                                                    preferred_element_type=F32)

        def scores_of(h):
            return (st_sc[pl.ds(pl.multiple_of(2 * h * nk, nk), nk), :],
                    st_sc[pl.ds(pl.multiple_of((2 * h + 1) * nk, nk), nk), :])

        def put(h, n_i, e0, rk1, e1):
            n_sc[h] = n_i
            e0_sc[h] = e0
            rk1_sc[h] = rk1.astype(BF16)
            e1_sc[h] = e1.astype(BF16)

        @pl.loop(0, PEER_HEADS // 2)
        def _(hh):
            heads = (2 * hh, 2 * hh + 1)
            flags = []
            for h in heads:
                *routing, bad = _route_fast(*scores_of(h))
                put(h, *routing)
                flags.append(jnp.max(bad) > 0.0)
            for h, flag in zip(heads, flags):
                @pl.when(flag)
                def _():
                    put(h, *_route_exact(*scores_of(h)))

    rows = u_ref.shape[0] // nk
    ht = jnp.dot(u_ref[...], xt_sc[...], preferred_element_type=F32)
    act = _gelu_tanh(ht.astype(BF16))
    for ii in range(rows):
        w = jnp.zeros((nk, xt_sc.shape[1]), BF16)
        for h in range(PEER_HEADS):
            n_row = n_sc[h, pl.ds(e * rows + ii, 1), :].astype(BF16)
            e0_row = e0_sc[h, pl.ds(e * rows + ii, 1), :].astype(BF16)
            w = w + jnp.where(rk1_sc[h] < n_row, e0_row * e1_sc[h], 0)
        p_sc[ii * nk:(ii + 1) * nk, :] = w * act[ii * nk:(ii + 1) * nk]
    acc_sc[...] += _tn_dot(v_ref[...], p_sc[...])

    @pl.when(e == pl.num_programs(1) - 1)
    def _():
        o_ref[...] = _layer_norm(DN_ALPHA * x_ref[...] + acc_sc[...].T, g_ref[...], b_ref[...])


def _stage_peer(x1, w_q, subkeys, u_tab, v_tab, ln_g, ln_b, tt=512, et=1024):
    n = x1.shape[0]
    tt = min(tt, n)
    H, nk, half = PEER_HEADS, PEER_NKEYS, subkeys.shape[-1]
    wq_t = w_q.T.astype(BF16)
    sk_t = subkeys.reshape(2 * H, nk, half).astype(BF16)
    ne = u_tab.shape[0]
    return pl.pallas_call(
        _peer_body, grid=(n // tt, ne // et),
        in_specs=[pl.BlockSpec((tt, D_MODEL), lambda t, e: (t, 0)),
                  _const_spec(wq_t.shape), _const_spec(sk_t.shape),
                  pl.BlockSpec((et, D_MODEL), lambda t, e: (e, 0)),
                  pl.BlockSpec((et, D_MODEL), lambda t, e: (e, 0)),
                  _const_spec((1, D_MODEL)), _const_spec((1, D_MODEL))],
        out_specs=pl.BlockSpec((tt, D_MODEL), lambda t, e: (t, 0)),
        out_shape=jax.ShapeDtypeStruct((n, D_MODEL), F32),
        scratch_shapes=[pltpu.VMEM((D_MODEL, tt), BF16), pltpu.VMEM((D_MODEL, tt), F32),
                        pltpu.VMEM((H, nk, tt), F32), pltpu.VMEM((H, nk, tt), F32),
                        pltpu.VMEM((H, nk, tt), BF16), pltpu.VMEM((H, nk, tt), BF16),
                        pltpu.VMEM((et, tt), BF16), pltpu.VMEM((2 * H * nk, tt), F32)],
        compiler_params=_cparams(("parallel", "arbitrary")), name="peer_ffn")(
            x1, wq_t, sk_t, u_tab.astype(BF16), v_tab.astype(BF16),
            ln_g.reshape(1, -1).astype(F32), ln_b.reshape(1, -1).astype(F32))


def _layer(x, w_in, b_in, ssm_a_re, ssm_a_im, ssm_log_dt, ssm_b_re, ssm_b_im, ssm_c_re, ssm_c_im, ssm_d,
           w_glu, w_up_ssm, nsa_pe_k, nsa_w1_k, nsa_w2_k, nsa_pe_v, nsa_w1_v, nsa_w2_v, w_up_nsa,
           w_out, ln1_g, ln1_b, peer_w_q, peer_subkeys, peer_u, peer_v, ln2_g, ln2_b):
    bsz, seq, _ = x.shape
    x2 = x.reshape(bsz * seq, D_MODEL)
    u, qp, kvc, kv, gates, gm = _stage_proj(x2, w_in, b_in, seq)
    ya = _stage_ssm(u, bsz, seq, ssm_a_re, ssm_a_im, ssm_log_dt, ssm_b_re, ssm_b_im, ssm_c_re, ssm_c_im,
                    ssm_d, w_glu, w_up_ssm)
    kcv = _stage_compress(kvc, bsz, seq, nsa_pe_k, nsa_w1_k, nsa_w2_k, nsa_pe_v, nsa_w1_v, nsa_w2_v)
    ob = _stage_attn(qp, kcv, kv, gates, bsz, seq)
    x1 = _stage_merge(x2, ya, ob, gm, w_up_nsa, w_out, ln1_g, ln1_b)
    out = _stage_peer(x1, peer_w_q, peer_subkeys, peer_u, peer_v, ln2_g, ln2_b)
    return out.reshape(bsz, seq, D_MODEL)


def kernel(x, w_in, b_in, ssm_a_re, ssm_a_im, ssm_log_dt, ssm_b_re, ssm_b_im, ssm_c_re, ssm_c_im, ssm_d, w_glu,
           w_up_ssm, nsa_pe_k, nsa_w1_k, nsa_w2_k, nsa_pe_v, nsa_w1_v, nsa_w2_v, w_up_nsa, w_out, ln1_g, ln1_b,
           peer_w_q, peer_subkeys, peer_u, peer_v, ln2_g, ln2_b):
    params = (w_in, b_in, ssm_a_re, ssm_a_im, ssm_log_dt, ssm_b_re, ssm_b_im, ssm_c_re, ssm_c_im, ssm_d, w_glu,
              w_up_ssm, nsa_pe_k, nsa_w1_k, nsa_w2_k, nsa_pe_v, nsa_w1_v, nsa_w2_v, w_up_nsa, w_out, ln1_g, ln1_b,
              peer_w_q, peer_subkeys, peer_u, peer_v, ln2_g, ln2_b)
    for layer in range(w_in.shape[0]):
        x = _layer(x, *[p[layer] for p in params])
    return x
```

```python
import functools
import math

import jax
import jax.numpy as jnp
import numpy as np
from jax import lax
from jax.experimental import pallas as pl
from jax.experimental.pallas import tpu as pltpu

F32 = jnp.float32
BF16 = jnp.bfloat16

D_MODEL = 1024
SSM_WIDTH = 512
SSM_GROUP = 16
SSM_GROUPS = 32
SSM_STATE = 64
SSM_CHUNK = 8
LANE_GROUPS = 8
N_LANE_TILES = SSM_WIDTH // 128
NSA_HEADS = 8
NSA_KV_GROUPS = 2
NSA_REP = NSA_HEADS // NSA_KV_GROUPS
NSA_HEAD_DIM = 64
KV_WIDTH = NSA_KV_GROUPS * NSA_HEAD_DIM
CMP_BLOCK = 32
CMP_STRIDE = 16
CMP_HIDDEN = 128
SEL_BLOCK = 64
SEL_TOPK = 16
WINDOW = 512
Q_BLOCK = 256
FORCE_BONUS = 1.0e4
PEER_HEADS = 8
PEER_NKEYS = 128
PEER_EXPERTS = PEER_NKEYS * PEER_NKEYS
PEER_TOPK = 16
DN_ALPHA = 2.0 ** 0.25
LN_EPS = 1e-5
NEG = -1e30
VMEM_LIMIT = 56 * 1024 * 1024


def _cparams(sem):
    return pltpu.CompilerParams(dimension_semantics=sem, vmem_limit_bytes=VMEM_LIMIT)


def _const_spec(shape):
    nd = len(shape)
    return pl.BlockSpec(shape, lambda *_: (0,) * nd, pipeline_mode=pl.Buffered(1))


def _layer_norm(z, g, b):
    mu = jnp.mean(z, axis=-1, keepdims=True)
    var = jnp.mean(jnp.square(z - mu), axis=-1, keepdims=True)
    return (z - mu) * lax.rsqrt(var + LN_EPS) * g + b


def _nt_dot(a, b):
    return lax.dot_general(a, b, (((1,), (1,)), ((), ())), preferred_element_type=F32)


def _tn_dot(a, b):
    return lax.dot_general(a, b, (((0,), (0,)), ((), ())), preferred_element_type=F32)


KV_REC = 640
KS_OFF, VS_OFF, KW_OFF, VW_OFF = 0, 256, 384, 512
POS_HI, POS_LO, ONE_LANE, BLOCK_LANE0 = 64, 65, 64, 128
assert SEL_BLOCK == 64


def _position_features(pos, lane):
    hi, lo = pos >> 6, pos & 63
    f = jnp.where(lane == KS_OFF + POS_HI, hi, 0) + jnp.where(lane == KW_OFF + POS_HI, hi, 0)
    f = f + jnp.where(lane == KS_OFF + POS_LO, lo, 0) + jnp.where(lane == KW_OFF + POS_LO, lo, 0)
    f = f + jnp.where(lane == KS_OFF + BLOCK_LANE0 + hi, 1, 0)
    f = f + jnp.where(lane == VS_OFF + ONE_LANE, 1, 0) + jnp.where(lane == VW_OFF + ONE_LANE, 1, 0)
    return f.astype(F32)


def _proj_body(seq, x_ref, wu, bu, wq, bq, wc, bc, wk, bk, wg, bg, wm, bm, aux_ref,
               u_o, q_o, c_o, k_o, g_o, m_o):
    xb = x_ref[...].astype(BF16)
    tm = x_ref.shape[0]
    dk = NSA_HEAD_DIM

    def lin(w, b):
        return jnp.dot(xb, w[...], preferred_element_type=F32) + b[...]

    def halves(a, tile):
        t = a[:, tile * 128:(tile + 1) * 128]
        return t, pltpu.roll(t, dk, 1)

    low = lax.broadcasted_iota(jnp.int32, (tm, 128), 1) < dk
    u_o[...] = lin(wu, bu)
    q = lin(wq, bq) * (dk ** -0.5)
    for tile in range(NSA_HEADS // 2):
        for half, piece in enumerate(halves(q, tile)):
            h = 2 * tile + half
            q_o[:, h * 128:(h + 1) * 128] = jnp.where(low, piece, aux_ref[h:h + 1, :]).astype(BF16)
    c_o[...] = lin(wc, bc)
    kv = lin(wk, bk)
    parts = [halves(kv, tile) for tile in range(4)]
    zero = jnp.zeros((tm, 128), F32)
    recs = []
    for g in range(NSA_KV_GROUPS):
        k_s, v_s, k_w, v_w = (jnp.where(low, parts[tile][g], 0.0) for tile in range(4))
        recs += [k_s, zero, v_s, k_w, v_w]
    pos0 = (pl.program_id(0) % (seq // tm)) * tm
    shape = (tm, NSA_KV_GROUPS * KV_REC)
    lane = lax.broadcasted_iota(jnp.int32, shape, 1)
    lane = jnp.where(lane >= KV_REC, lane - KV_REC, lane)
    feat = _position_features(pos0 + lax.broadcasted_iota(jnp.int32, shape, 0), lane)
    k_o[...] = (jnp.concatenate(recs, axis=1) + feat).astype(BF16)
    g_o[...] = lin(wg, bg)
    m_o[...] = jax.nn.sigmoid(lin(wm, bm)).astype(m_o.dtype)


def _stage_proj(x2, w_in, b_in, seq):
    n = x2.shape[0]
    tm = 256
    o0 = SSM_WIDTH
    o1 = o0 + NSA_HEADS * NSA_HEAD_DIM
    o2 = o1 + 6 * KV_WIDTH
    o3 = o2 + 3 * NSA_HEADS
    w_u, b_u = w_in[:, :o0], b_in[:o0]
    w_q, b_q = w_in[:, o0:o1], b_in[o0:o1]
    w_kv, b_kv = w_in[:, o1:o2], b_in[o1:o2]
    w_g, b_g = w_in[:, o2:o3], b_in[o2:o3]
    w_m, b_m = w_in[:, o3:], b_in[o3:]

    slope = 2.0 ** -(np.arange(NSA_HEADS, dtype=np.float32) + 1.0)
    q_aux = np.zeros((NSA_HEADS, 128), np.float32)
    q_aux[:, POS_HI] = slope * 64.0
    q_aux[:, POS_LO] = slope

    def pad_g(a):
        a = a.reshape(a.shape[:-1] + (NSA_KV_GROUPS, 3 * NSA_REP))
        a = jnp.concatenate([a, jnp.zeros(a.shape[:-1] + (128 - 3 * NSA_REP,), a.dtype)], axis=-1)
        return a.reshape(a.shape[:-2] + (NSA_KV_GROUPS * 128,))

    ws = [w_u, w_q, w_kv[:, :2 * KV_WIDTH], w_kv[:, 2 * KV_WIDTH:], pad_g(w_g), w_m]
    bs = [b_u, b_q, b_kv[:2 * KV_WIDTH], b_kv[2 * KV_WIDTH:], pad_g(b_g), b_m]
    odt = [F32, BF16, F32, BF16, F32, BF16]
    widths = [SSM_WIDTH, NSA_HEADS * 128, 2 * KV_WIDTH, NSA_KV_GROUPS * KV_REC, NSA_KV_GROUPS * 128, 2 * D_MODEL]
    args, in_specs = [x2], [pl.BlockSpec((tm, D_MODEL), lambda i: (i, 0))]
    for w, b in zip(ws, bs):
        args += [w.astype(BF16), b.reshape(1, -1).astype(F32)]
        in_specs += [_const_spec(w.shape), _const_spec((1, w.shape[1]))]
    args.append(jnp.asarray(q_aux))
    in_specs.append(_const_spec(q_aux.shape))
    out_shape = [jax.ShapeDtypeStruct((n, w), dt) for w, dt in zip(widths, odt)]
    out_specs = [pl.BlockSpec((tm, w), lambda i: (i, 0)) for w in widths]
    return pl.pallas_call(
        functools.partial(_proj_body, seq), grid=(n // tm,), in_specs=in_specs, out_specs=out_specs, out_shape=out_shape,
        compiler_params=_cparams(("parallel",)), name="in_proj")(*args)


def _ssm_weights(a_re, a_im, log_dt, b_re, b_im, c_re, c_im):
    L, G, P, H, A, J = SSM_CHUNK, SSM_GROUPS, SSM_STATE, SSM_GROUP, LANE_GROUPS, N_LANE_TILES
    lam = lax.complex(a_re.astype(F32), a_im.astype(F32))
    dt = jnp.exp(log_dt.astype(F32))[:, None]
    lam_bar = jnp.exp(lam * dt)
    b_bar = ((lam_bar - 1.0) / lam)[:, :, None] * lax.complex(b_re.astype(F32), b_im.astype(F32))
    c = lax.complex(c_re.astype(F32), c_im.astype(F32))
    k = jnp.arange(L + 1, dtype=F32)
    pw = jnp.exp((lam * dt)[None] * k[:, None, None])
    def spread(x, n_inner):
        src = np.arange(x.shape[2])
        dst = np.arange(x.shape[2] * A)
        sel = ((src[:, None] // n_inner == dst[None, :] // (A * n_inner))
               & (src[:, None] % n_inner == dst[None, :] % n_inner))
        spread_x = jnp.einsum('jrk,kc->jrc', x.astype(BF16), jnp.asarray(sel, BF16), preferred_element_type=BF16)
        return spread_x, (dst % (A * n_inner)) // n_inner

    def group_mask(row_group, col_group):
        return jnp.asarray(row_group[:, None] == col_group[None, :], BF16)

    in_rows = (np.arange(L * 128) % 128) // H
    st_rows = (np.arange(2 * A * P) % (A * P)) // P
    kern = jnp.real(jnp.einsum('ghp,kgp,gpj->kghj', c, pw[:L], b_bar))
    s_i, t_i = jnp.arange(L)[:, None], jnp.arange(L)[None, :]
    tau = jnp.clip(t_i - s_i, 0, L - 1)
    causal = (t_i >= s_i).astype(F32)
    k_st = (kern[tau] * causal[:, :, None, None, None]).reshape(L, L, J, A, H, H)
    k_st = k_st.transpose(2, 0, 3, 5, 1, 4).reshape(J, L * 128, L * H)
    w_intra, cg = spread(k_st, H)
    w_intra = w_intra * group_mask(in_rows, cg)
    q = (pw[:L][::-1][:, :, :, None] * b_bar[None]).reshape(L, J, A, P, H)
    q = jnp.stack([jnp.real(q), jnp.imag(q)], axis=0)
    q = q.transpose(2, 1, 3, 5, 0, 4).reshape(J, L * 128, 2 * P)
    m2, cg = spread(q, P)
    m2 = m2 * group_mask(in_rows, cg)
    cl = (c[None] * pw[1:][:, :, None, :]).reshape(L, J, A, H, P)
    cl = jnp.stack([jnp.real(cl), -jnp.imag(cl)], axis=0)
    cl = cl.transpose(2, 0, 3, 5, 1, 4).reshape(J, 2 * A * P, L * H)
    m1, cg = spread(cl, H)
    m1 = m1 * group_mask(st_rows, cg)
    w2 = jnp.concatenate([w_intra, m1], axis=1)
    lam8 = pw[L].reshape(G * P)
    return m2.astype(BF16), w2.astype(BF16), jnp.real(lam8), jnp.imag(lam8)


def _split_lane_tiles(src_ref, dst_sc):
    for j in range(dst_sc.shape[0]):
        dst_sc[j] = src_ref[:, j * 128:(j + 1) * 128]


def _chunk_inputs(u_sc, j, rows):
    return jnp.concatenate([u_sc[j, pl.ds(s, rows, stride=SSM_CHUNK), :] for s in range(SSM_CHUNK)], axis=1)


def _ssm_state_body(u_ref, m2_ref, zre_ref, zim_ref, u_sc):
    half = LANE_GROUPS * SSM_STATE
    _split_lane_tiles(u_ref, u_sc)
    for j in range(N_LANE_TILES):
        z = jnp.dot(_chunk_inputs(u_sc, j, zre_ref.shape[0]).astype(BF16), m2_ref[j], preferred_element_type=F32)
        zre_ref[:, j * half:(j + 1) * half] = z[:, :half]
        zim_ref[:, j * half:(j + 1) * half] = z[:, half:]


def _ssm_scan_body(zre_ref, zim_ref, lre_ref, lim_ref, xre_ref, xim_ref):
    lr, li = lre_ref[...], lim_ref[...]

    def step(c, carry):
        xr, xi = carry
        xre_ref[pl.ds(c, 1), :] = xr
        xim_ref[pl.ds(c, 1), :] = xi
        return lr * xr - li * xi + zre_ref[pl.ds(c, 1), :], lr * xi + li * xr + zim_ref[pl.ds(c, 1), :]

    zero = jnp.zeros(lr.shape, F32)
    lax.fori_loop(0, zre_ref.shape[0], step, (zero, zero))


def _ssm_out_body(u_ref, xre_ref, xim_ref, w2_ref, d_ref, wglu_ref, wup_ref, o_ref, u_sc, y_sc):
    half = LANE_GROUPS * SSM_STATE
    rows = xre_ref.shape[0]
    _split_lane_tiles(u_ref, u_sc)
    for j in range(N_LANE_TILES):
        lhs = jnp.concatenate(
            [_chunk_inputs(u_sc, j, rows), xre_ref[:, j * half:(j + 1) * half], xim_ref[:, j * half:(j + 1) * half]],
            axis=1).astype(BF16)
        yj = jnp.dot(lhs, w2_ref[j], preferred_element_type=F32)
        for t in range(SSM_CHUNK):
            y_sc[j, pl.ds(t, rows, stride=SSM_CHUNK), :] = yj[:, t * 128:(t + 1) * 128]
    y = jnp.concatenate([y_sc[j] for j in range(N_LANE_TILES)], axis=1) + d_ref[...] * u_ref[...]
    gl = jnp.dot(jax.nn.gelu(y).astype(BF16), wglu_ref[...], preferred_element_type=F32)
    v = gl[:, :SSM_WIDTH] * jax.nn.sigmoid(gl[:, SSM_WIDTH:])
    o_ref[...] = jnp.dot(v.astype(BF16), wup_ref[...], preferred_element_type=F32).astype(o_ref.dtype)


def _stage_ssm(u, bsz, seq, a_re, a_im, log_dt, b_re, b_im, c_re, c_im, d_skip, w_glu, w_up):
    n = bsz * seq
    L = SSM_CHUNK
    nck = seq // L
    gp = SSM_GROUPS * SSM_STATE
    m2, w2, l8re, l8im = _ssm_weights(a_re, a_im, log_dt, b_re, b_im, c_re, c_im)
    rs = min(nck, 512)
    zre, zim = pl.pallas_call(
        _ssm_state_body, grid=(bsz, nck // rs),
        in_specs=[pl.BlockSpec((rs * L, SSM_WIDTH), lambda b, i: (b * (nck // rs) + i, 0)),
                  _const_spec(m2.shape)],
        out_specs=[pl.BlockSpec((rs, gp), lambda b, i: (i, b))] * 2,
        out_shape=[jax.ShapeDtypeStruct((nck, bsz * gp), F32)] * 2,
        scratch_shapes=[pltpu.VMEM((N_LANE_TILES, rs * L, 128), F32)],
        compiler_params=_cparams(("parallel", "parallel")), name="ssm_chunk_state")(u, m2)
    cw = 1024
    ncol = bsz * gp // cw
    lre = jnp.tile(l8re, bsz).reshape(1, bsz * gp)
    lim = jnp.tile(l8im, bsz).reshape(1, bsz * gp)
    col = pl.BlockSpec((nck, cw), lambda i: (0, i))
    lam = pl.BlockSpec((1, cw), lambda i: (0, i))
    xre, xim = pl.pallas_call(
        _ssm_scan_body, grid=(ncol,), in_specs=[col, col, lam, lam], out_specs=[col, col],
        out_shape=[jax.ShapeDtypeStruct((nck, bsz * gp), F32)] * 2,
        compiler_params=_cparams(("parallel",)), name="ssm_carry_scan")(zre, zim, lre, lim)
    ro = min(nck, 128)
    d8 = d_skip.reshape(1, SSM_WIDTH).astype(F32)
    return pl.pallas_call(
        _ssm_out_body, grid=(bsz, nck // ro),
        in_specs=[pl.BlockSpec((ro * L, SSM_WIDTH), lambda b, i: (b * (nck // ro) + i, 0)),
                  pl.BlockSpec((ro, gp), lambda b, i: (i, b)),
                  pl.BlockSpec((ro, gp), lambda b, i: (i, b)),
                  _const_spec(w2.shape), _const_spec((1, SSM_WIDTH)),
                  _const_spec(w_glu.shape), _const_spec(w_up.shape)],
        out_specs=pl.BlockSpec((ro * L, D_MODEL), lambda b, i: (b * (nck // ro) + i, 0)),
        out_shape=jax.ShapeDtypeStruct((n, D_MODEL), BF16),
        scratch_shapes=[pltpu.VMEM((N_LANE_TILES, ro * L, 128), F32)] * 2,
        compiler_params=_cparams(("parallel", "parallel")), name="ssm_out")(
            u, xre, xim, w2, d8, w_glu.astype(BF16), w_up.astype(BF16))


def _compress_body(kv_ref, pea_ref, peb_ref, w1a_ref, w1b_ref, w2_ref, o_ref, kv_sc):
    nrow = o_ref.shape[0]
    _split_lane_tiles(kv_ref, kv_sc)
    f = jnp.concatenate([kv_sc[j, pl.ds(l, nrow, stride=CMP_STRIDE), :]
                         for l in range(CMP_STRIDE) for j in range(kv_sc.shape[0])], axis=1)
    a = jnp.dot((f + pea_ref[...]).astype(BF16), w1a_ref[...], preferred_element_type=F32)
    b = jnp.dot((f + peb_ref[...]).astype(BF16), w1b_ref[...], preferred_element_type=F32)
    pre = a + pltpu.roll(b, b.shape[0] - 1, 0)
    hid = jax.nn.gelu(pre)
    out = jnp.dot(hid.astype(BF16), w2_ref[...], preferred_element_type=F32)
    lane = lax.broadcasted_iota(jnp.int32, out.shape, 1) & 255
    c_end = lax.broadcasted_iota(jnp.int32, out.shape, 0) * CMP_STRIDE + (CMP_BLOCK - 1)
    feat = jnp.where(lane == POS_HI, c_end >> 6, 0) + jnp.where(lane == POS_LO, c_end & 63, 0)
    o_ref[...] = (out + feat.astype(F32)).astype(BF16)


def _stage_compress(kvc, bsz, seq, pe_k, w1_k, w2_k, pe_v, w1_v, w2_v):
    hl = CMP_STRIDE
    nrow = seq // hl
    G, dk, hid = NSA_KV_GROUPS, NSA_HEAD_DIM, CMP_HIDDEN
    eye = jnp.eye(2 * G, dtype=F32)

    def big_w1(lo):
        wk = w1_k.reshape(CMP_BLOCK, dk, hid)[lo:lo + hl]
        wv = w1_v.reshape(CMP_BLOCK, dk, hid)[lo:lo + hl]
        w = jnp.stack([wk, wk, wv, wv], axis=1)
        return jnp.einsum('lcdj,ce->lcdej', w, eye).reshape(hl * 2 * G * dk, 2 * G * hid)

    def big_pe(lo):
        pk, pv = pe_k[lo:lo + hl], pe_v[lo:lo + hl]
        return jnp.stack([pk, pk, pv, pv], axis=1).reshape(1, hl * 2 * G * dk)

    w2 = jnp.stack([w2_k, w2_k, w2_v, w2_v], axis=0)
    w2 = jnp.concatenate([w2, jnp.zeros((2 * G, hid, 128 - dk), w2.dtype)], axis=-1)
    slot = eye[np.array([kv * G + g for g in range(G) for kv in range(2)])].T
    w2 = jnp.einsum('cjd,ce->cjed', w2, slot).reshape(2 * G * hid, 2 * G * 128)
    width = hl * 2 * G * dk
    return pl.pallas_call(
        _compress_body, grid=(bsz,),
        in_specs=[pl.BlockSpec((seq, 2 * G * dk), lambda b: (b, 0)),
                  _const_spec((1, width)), _const_spec((1, width)),
                  _const_spec((width, 2 * G * hid)), _const_spec((width, 2 * G * hid)),
                  _const_spec((2 * G * hid, 2 * G * 128))],
        out_specs=pl.BlockSpec((nrow, 2 * G * 128), lambda b: (b, 0)),
        out_shape=jax.ShapeDtypeStruct((bsz * nrow, 2 * G * 128), BF16),
        scratch_shapes=[pltpu.VMEM((2 * G * dk // 128, seq, 128), F32)],
        compiler_params=_cparams(("parallel",)), name="nsa_compress")(
            kvc, big_pe(0).astype(F32), big_pe(hl).astype(F32),
            big_w1(0).astype(BF16), big_w1(hl).astype(BF16), w2.astype(BF16))


MASK_BIG = 2.0 ** 100
SEL_TILE = 512


def _nsa_body(q_ref, kcv_ref, kv_ref, g_ref, ov_ref, rel_ref, relc_ref, o_ref, acc, s_buf, p_buf, a_buf):
    i = pl.program_id(2)
    qs = i * Q_BLOCK
    seq = kv_ref.shape[0]
    rows = NSA_REP * Q_BLOCK
    dk = NSA_HEAD_DIM
    qh = [q_ref[:, r * 128:(r + 1) * 128] for r in range(NSA_REP)]
    q_all = jnp.concatenate(qh, axis=0)

    kc = kcv_ref[:, :128]
    vc = kcv_ref[:, 128:]
    vis_c = relc_ref[...] >= -qs
    s = jnp.where(vis_c, _nt_dot(q_all, kc), NEG)
    p = jnp.exp(s - jnp.max(s, axis=-1, keepdims=True)) * jnp.where(vis_c, 1.0, 0.0)
    p = (p / jnp.maximum(jnp.sum(p, axis=-1, keepdims=True), 1e-30)).astype(BF16)
    o_cmp = jnp.dot(p, vc, preferred_element_type=F32)

    wlen = min(WINDOW + Q_BLOCK, seq)
    w0 = pl.multiple_of(jnp.maximum(qs + Q_BLOCK - wlen, 0), Q_BLOCK)
    rel_w = rel_ref[...]
    s = _nt_dot(q_all, kv_ref[pl.ds(w0, wlen), KW_OFF:KW_OFF + 128])
    in_band = pltpu.bitcast(rel_w - (w0 - qs), jnp.uint32) < WINDOW
    s = jnp.where(in_band, s, NEG)
    p_w = jnp.exp(s - jnp.max(s, axis=-1, keepdims=True)).astype(BF16)
    a_win = jnp.dot(p_w, kv_ref[pl.ds(w0, wlen), VW_OFF:VW_OFF + 128], preferred_element_type=F32)

    p_heads = jnp.concatenate([p[r * Q_BLOCK:(r + 1) * Q_BLOCK] for r in range(NSA_REP)], axis=1)
    imp = jnp.dot(p_heads, ov_ref[...], preferred_element_type=F32)
    coli = lax.broadcasted_iota(jnp.int32, (Q_BLOCK, 128), 1)
    t_q = qs + lax.broadcasted_iota(jnp.int32, (Q_BLOCK, 128), 0)
    cur = t_q >> 6
    forced = (coli == 0) | (coli == cur) | (coli == cur - 1)
    score = jnp.where(coli * SEL_BLOCK <= t_q, imp + jnp.where(forced, FORCE_BONUS, 0.0), NEG)
    nsl = seq // SEL_BLOCK
    st = score.T[:nsl]
    groups = [st[8 * v:8 * v + 8] for v in range(nsl // 8)]
    jsub = lax.broadcasted_iota(jnp.int32, (8, Q_BLOCK), 0)
    rank = [jnp.zeros((8, Q_BLOCK), F32) for _ in groups]
    for k in range(nsl):
        rk = st[k:k + 1, :]
        for v, sv in enumerate(groups):
            if v > k // 8:
                one = jnp.where(rk >= sv, 1.0, 0.0)
            elif v < k // 8:
                one = jnp.where(rk > sv, 1.0, 0.0)
            else:
                one = jnp.where(rk > sv, 1.0, jnp.where(jsub > k % 8, jnp.where(rk == sv, 1.0, 0.0), 0.0))
            rank[v] = rank[v] + one
    drop_t = jnp.where(jnp.concatenate(rank, axis=0) < float(min(SEL_TOPK, nsl)), 0.0, -MASK_BIG)
    if nsl < 128:
        drop_t = jnp.concatenate([drop_t, jnp.zeros((128 - nsl, Q_BLOCK), F32)], axis=0)
    drop = drop_t.T.astype(BF16)
    q_sel = jnp.concatenate([jnp.concatenate([qh[r], drop], axis=1) for r in range(NSA_REP)], axis=0)

    n_tiles = (qs + Q_BLOCK + SEL_TILE - 1) // SEL_TILE

    def scores(kt):
        k0 = pl.multiple_of(kt * SEL_TILE, SEL_TILE)
        return _nt_dot(q_sel, kv_ref[pl.ds(k0, SEL_TILE), KS_OFF:KS_OFF + 256])

    def values(kt):
        k0 = pl.multiple_of(kt * SEL_TILE, SEL_TILE)
        return kv_ref[pl.ds(k0, SEL_TILE), VS_OFF:VS_OFF + 128]

    def trip(diagonal, kt, m_old):
        if not diagonal:
            s_next = scores(kt + 1)
        acc[...] = a_buf[...] * acc[...] + jnp.dot(p_buf[...], values(jnp.maximum(kt - 1, 0)),
                                                   preferred_element_type=F32)
        s = s_buf[...]
        if diagonal:
            s = jnp.where(rel_ref[:, :SEL_TILE] >= kt * SEL_TILE - qs, s, NEG)
        m_new = jnp.maximum(m_old, jnp.max(s, axis=-1, keepdims=True))
        p_buf[...] = jnp.exp(s - m_new).astype(BF16)
        a_buf[...] = jnp.exp(m_old - m_new)
        if not diagonal:
            s_buf[...] = s_next
        return m_new

    acc[...] = jnp.zeros_like(acc)
    p_buf[...] = jnp.zeros_like(p_buf)
    a_buf[...] = jnp.ones_like(a_buf)
    s_buf[...] = scores(0)
    m_s = lax.fori_loop(0, n_tiles - 1, functools.partial(trip, False), jnp.full((rows, 1), NEG, F32))
    trip(True, n_tiles - 1, m_s)
    a_sel = a_buf[...] * acc[...] + jnp.dot(p_buf[...], values(n_tiles - 1), preferred_element_type=F32)

    gate = jax.nn.sigmoid(g_ref[...])
    outs = []
    for r in range(NSA_REP):
        rr = slice(r * Q_BLOCK, (r + 1) * Q_BLOCK)
        o_s = a_sel[rr, :dk] / a_sel[rr, ONE_LANE:ONE_LANE + 1]
        o_w = a_win[rr, :dk] / a_win[rr, ONE_LANE:ONE_LANE + 1]
        outs.append(gate[:, 3 * r:3 * r + 1] * o_cmp[rr, :dk] + gate[:, 3 * r + 1:3 * r + 2] * o_s
                    + gate[:, 3 * r + 2:3 * r + 3] * o_w)
    o_ref[...] = jnp.concatenate(outs, axis=1).astype(o_ref.dtype)


def _stage_attn(qp, kcv, kv, gates, bsz, seq):
    n = bsz * seq
    nq = seq // Q_BLOCK
    ncmp = seq // CMP_STRIDE
    nsl = seq // SEL_BLOCK
    assert seq % SEL_TILE == 0
    c_start = np.arange(ncmp) * CMP_STRIDE
    s_start = np.arange(nsl) * SEL_BLOCK
    ov = ((c_start[:, None] < s_start[None, :] + SEL_BLOCK) & (c_start[:, None] + CMP_BLOCK > s_start[None, :]))
    ov = np.pad(ov.astype(np.float32), ((0, 0), (0, 128 - nsl)))
    ov[ncmp - 1] = 0.0
    ov = jnp.asarray(np.tile(ov, (NSA_REP, 1)), BF16)
    width = NSA_REP * NSA_HEAD_DIM
    rows = NSA_REP * Q_BLOCK
    wlen = min(WINDOW + Q_BLOCK, seq)
    assert wlen >= SEL_TILE
    offs = jnp.arange(rows, dtype=jnp.int32)[:, None] % Q_BLOCK
    rel = offs - jnp.arange(wlen, dtype=jnp.int32)[None, :]
    relc = offs - (jnp.arange(ncmp, dtype=jnp.int32)[None, :] * CMP_STRIDE + (CMP_BLOCK - 1))
    return pl.pallas_call(
        _nsa_body, grid=(bsz, NSA_KV_GROUPS, nq),
        in_specs=[pl.BlockSpec((Q_BLOCK, NSA_REP * 128), lambda b, g, i: (b * nq + i, g)),
                  pl.BlockSpec((ncmp, 256), lambda b, g, i: (b, g)),
                  pl.BlockSpec((seq, KV_REC), lambda b, g, i: (b, g)),
                  pl.BlockSpec((Q_BLOCK, 128), lambda b, g, i: (b * nq + i, g)),
                  _const_spec(ov.shape), _const_spec(rel.shape), _const_spec(relc.shape)],
        out_specs=pl.BlockSpec((Q_BLOCK, width), lambda b, g, i: (b * nq + i, g)),
        out_shape=jax.ShapeDtypeStruct((n, NSA_KV_GROUPS * width), BF16),
        scratch_shapes=[pltpu.VMEM((rows, 128), F32), pltpu.VMEM((rows, SEL_TILE), F32),
                        pltpu.VMEM((rows, SEL_TILE), BF16), pltpu.VMEM((rows, 1), F32)],
        compiler_params=_cparams(("parallel", "parallel", "arbitrary")), name="nsa_attention")(
            qp, kcv, kv, gates, ov, rel, relc)


def _merge_body(x_ref, ya_ref, ob_ref, gm_ref, wnsa_ref, wout_ref, g_ref, b_ref, o_ref):
    yb = jnp.dot(ob_ref[...], wnsa_ref[...], preferred_element_type=F32)
    gm = gm_ref[...].astype(F32)
    mix_in = gm[:, :D_MODEL] * ya_ref[...].astype(F32) + gm[:, D_MODEL:] * yb
    mix = jnp.dot(mix_in.astype(BF16), wout_ref[...], preferred_element_type=F32)
    o_ref[...] = _layer_norm(DN_ALPHA * x_ref[...] + mix, g_ref[...], b_ref[...])


def _stage_merge(x2, ya, ob, gm, w_up_nsa, w_out, ln_g, ln_b):
    n = x2.shape[0]
    tm = 512
    row = lambda w: pl.BlockSpec((tm, w), lambda i: (i, 0))
    return pl.pallas_call(
        _merge_body, grid=(n // tm,),
        in_specs=[row(D_MODEL), row(D_MODEL), row(ob.shape[1]), row(2 * D_MODEL),
                  _const_spec(w_up_nsa.shape), _const_spec(w_out.shape),
                  _const_spec((1, D_MODEL)), _const_spec((1, D_MODEL))],
        out_specs=row(D_MODEL), out_shape=jax.ShapeDtypeStruct((n, D_MODEL), F32),
        compiler_params=_cparams(("parallel",)), name="merge_ln")(
            x2, ya, ob, gm, w_up_nsa.astype(BF16), w_out.astype(BF16),
            ln_g.reshape(1, -1).astype(F32), ln_b.reshape(1, -1).astype(F32))


BIG_NEG = -3.0e38


def _top_rows(s, k):
    rid = lax.broadcasted_iota(jnp.int32, s.shape, 0).astype(F32)
    rank = jnp.full(s.shape, float(k), F32)
    vals, idxs = [], []
    for r in range(k):
        m = jnp.max(s, axis=0, keepdims=True)
        idx = jnp.min(jnp.where(s == m, rid, float(s.shape[0])), axis=0, keepdims=True)
        hit = rid == idx
        rank = jnp.where(hit, float(r), rank)
        s = jnp.where(hit, BIG_NEG, s)
        vals.append(m)
        idxs.append(idx)
    return jnp.concatenate(vals, axis=0), jnp.concatenate(idxs, axis=0), rank


def _max_rounds(s, k, want_round=True):
    rnd = jnp.full(s.shape, float(k), F32) if want_round else None
    vals = []
    for r in range(k):
        m = jnp.max(s, axis=0, keepdims=True)
        eq = s == m
        if want_round:
            rnd = jnp.where(eq, float(r), rnd)
        s = jnp.where(eq, BIG_NEG, s)
        vals.append(m)
    return jnp.concatenate(vals, axis=0), rnd


def _pair_candidates(a, b):
    c8 = lax.broadcasted_iota(jnp.int32, (8, a.shape[1]), 0)
    parts = [a[0:1] + b[0:8], a[0:1] + b[8:16]]
    for r in range(1, 8):
        keep = PEER_TOPK // (r + 1)
        cand = a[r:r + 1] + b[0:8]
        parts.append(cand if keep >= 8 else jnp.where(c8 < keep, cand, BIG_NEG))
    parts.append(a[8:16] + b[0:1])
    return jnp.concatenate(parts, axis=0)


def _picks_per_rank(picked):
    rows = [jnp.sum(picked[0:16], axis=0, keepdims=True)]
    rows += [jnp.sum(picked[8 * (r + 1):8 * (r + 2)], axis=0, keepdims=True) for r in range(1, 8)]
    return jnp.concatenate(rows + [picked[72:80]], axis=0)


def _spread_by_rank(rank, per_rank):
    out = jnp.zeros(rank.shape, F32)
    for r in range(PEER_TOPK):
        out = out + jnp.where(rank == float(r), per_rank[r:r + 1], 0.0)
    return out


def _sort16_network():
    pairs, p = [], 1
    while p < 16:
        k = p
        while k >= 1:
            for j in range(k % p, 16 - k, 2 * k):
                for i in range(min(k, 16 - j - k)):
                    if (i + j) // (2 * p) == (i + j + k) // (2 * p):
                        pairs.append((i + j, i + j + k))
            k //= 2
        p *= 2
    return pairs


_SORT16 = _sort16_network()


def _sorted_top16(s):
    def exchange(v, i, j):
        v[i], v[j] = jnp.maximum(v[i], v[j]), jnp.minimum(v[i], v[j])

    v = [s[8 * k:8 * k + 8] for k in range(16)]
    for i, j in _SORT16:
        exchange(v, i, j)
    for shift in (4, 2, 1):
        v = [jnp.maximum(v[k], pltpu.roll(v[15 - k], shift, 0)) for k in range(16)]
        for d in (8, 4, 2, 1):
            for i in range(16):
                if i & d == 0:
                    exchange(v, i, i + d)
    return jnp.concatenate([x[0:1] for x in v], axis=0)


def _count_rows(vals, s, strict):
    rows = [jnp.broadcast_to(vals[r:r + 1], (8, s.shape[1])) for r in range(vals.shape[0])]
    assert len(rows) == 16
    outs = []
    for c in range(0, s.shape[0], 8):
        sc = s[c:c + 8]
        test = (lambda row: row > sc) if strict else (lambda row: row <= sc)
        c1 = test(rows[7])
        c2 = test(jnp.where(c1, rows[11], rows[3]))
        c3 = test(jnp.where(c1, jnp.where(c2, rows[13], rows[9]), jnp.where(c2, rows[5], rows[1])))
        hi = jnp.where(c2, jnp.where(c3, rows[14], rows[12]), jnp.where(c3, rows[10], rows[8]))
        lo = jnp.where(c2, jnp.where(c3, rows[6], rows[4]), jnp.where(c3, rows[2], rows[0]))
        c4 = test(jnp.where(c1, hi, lo))
        acc = (jnp.where(c1, 8.0, 0.0) + jnp.where(c2, 4.0, 0.0) + jnp.where(c3, 2.0, 0.0) + jnp.where(c4, 1.0, 0.0)
               + jnp.where(test(rows[15]), 1.0, 0.0))
        outs.append(acc)
    return jnp.concatenate(outs, axis=0)


def _route_fast(s0, s1):
    k = PEER_TOPK
    count = lambda v: jnp.sum(v, axis=0, keepdims=True)
    a, b = _sorted_top16(s0), _sorted_top16(s1)
    cand = _pair_candidates(a, b)
    best, _ = _max_rounds(cand, k, want_round=False)
    hit = cand >= best[k - 1:k]
    z = jnp.sum(jnp.where(hit, jnp.exp(cand - best[0:1]), 0.0), axis=0, keepdims=True)
    nrank = _picks_per_rank(jnp.where(hit, 1.0, 0.0))
    earns = jnp.concatenate([jnp.min(jnp.where(nrank >= float(v), a, -BIG_NEG), axis=0, keepdims=True)
                             for v in range(1, k + 1)], axis=0)
    n_i = _count_rows(earns, s0, strict=False)
    ties = lambda t, s: (count(jnp.where(t[:-1] == t[1:], 1.0, 0.0))
                         + jnp.abs(count(jnp.where(s >= t[k - 1:k], 1.0, 0.0)) - k))
    bad = ties(a, s0) + ties(b, s1) + jnp.abs(count(nrank) - k)
    return n_i, jnp.exp(s0 - a[0:1]) / z, _count_rows(b, s1, strict=True), jnp.exp(s1 - b[0:1]), bad


def _route_exact(s0, s1):
    k = PEER_TOPK
    a, _, rk0 = _top_rows(s0, k)
    b, _, rk1 = _top_rows(s1, k)
    cand = jnp.concatenate([a[r:r + 1] + b for r in range(k)], axis=0)
    best, bidx, _ = _top_rows(cand, k)
    brank = jnp.floor(bidx * (1.0 / k))
    z = jnp.sum(jnp.exp(best - best[0:1]), axis=0, keepdims=True)
    nrank = jnp.concatenate([jnp.sum(jnp.where(brank == float(r), 1.0, 0.0), axis=0, keepdims=True)
                             for r in range(k)], axis=0)
    return _spread_by_rank(rk0, nrank), jnp.exp(s0 - a[0:1]) / z, rk1, jnp.exp(s1 - b[0:1])


def _gelu_tanh(x):
    c = math.sqrt(2.0 / math.pi)
    return x / (1.0 + jnp.exp(x * (x * x * (-2.0 * c * 0.044715) - 2.0 * c)))


def _peer_body(x_ref, wq_ref, sk_ref, u_ref, v_ref, g_ref, b_ref, o_ref,
               xt_sc, acc_sc, n_sc, e0_sc, rk1_sc, e1_sc, p_sc, st_sc):
    e = pl.program_id(1)
    nk = PEER_NKEYS

    @pl.when(e == 0)
    def _():
        xt = x_ref[...].T.astype(BF16)
        xt_sc[...] = xt
        acc_sc[...] = jnp.zeros_like(acc_sc)
        qt = jnp.dot(wq_ref[...], xt, preferred_element_type=F32).astype(BF16)
        half = sk_ref.shape[2]
        for c in range(sk_ref.shape[0]):
            st_sc[c * nk:(c + 1) * nk, :] = jnp.dot(sk_ref[c], qt[c * half:(c + 1) * half],
                                                    preferred_element_type=F32)

        def scores_of(h):
            return (st_sc[pl.ds(pl.multiple_of(2 * h * nk, nk), nk), :],
                    st_sc[pl.ds(pl.multiple_of((2 * h + 1) * nk, nk), nk), :])

        def put(h, n_i, e0, rk1, e1):
            n_sc[h] = n_i
            e0_sc[h] = e0
            rk1_sc[h] = rk1.astype(BF16)
            e1_sc[h] = e1.astype(BF16)

        @pl.loop(0, PEER_HEADS // 2)
        def _(hh):
            heads = (2 * hh, 2 * hh + 1)
            flags = []
            for h in heads:
                *routing, bad = _route_fast(*scores_of(h))
                put(h, *routing)
                flags.append(jnp.max(bad) > 0.0)
            for h, flag in zip(heads, flags):
                @pl.when(flag)
                def _():
                    put(h, *_route_exact(*scores_of(h)))

    rows = u_ref.shape[0] // nk
    ht = jnp.dot(u_ref[...], xt_sc[...], preferred_element_type=F32)
    act = _gelu_tanh(ht.astype(BF16))
    for ii in range(rows):
        w = jnp.zeros((nk, xt_sc.shape[1]), BF16)
        for h in range(PEER_HEADS):
            n_row = n_sc[h, pl.ds(e * rows + ii, 1), :].astype(BF16)
            e0_row = e0_sc[h, pl.ds(e * rows + ii, 1), :].astype(BF16)
            w = w + jnp.where(rk1_sc[h] < n_row, e0_row * e1_sc[h], 0)
        p_sc[ii * nk:(ii + 1) * nk, :] = w * act[ii * nk:(ii + 1) * nk]
    acc_sc[...] += _tn_dot(v_ref[...], p_sc[...])

    @pl.when(e == pl.num_programs(1) - 1)
    def _():
        o_ref[...] = _layer_norm(DN_ALPHA * x_ref[...] + acc_sc[...].T, g_ref[...], b_ref[...])


def _stage_peer(x1, w_q, subkeys, u_tab, v_tab, ln_g, ln_b, tt=512, et=1024):
    n = x1.shape[0]
    tt = min(tt, n)
    H, nk, half = PEER_HEADS, PEER_NKEYS, subkeys.shape[-1]
    wq_t = w_q.T.astype(BF16)
    sk_t = subkeys.reshape(2 * H, nk, half).astype(BF16)
    ne = u_tab.shape[0]
    return pl.pallas_call(
        _peer_body, grid=(n // tt, ne // et),
        in_specs=[pl.BlockSpec((tt, D_MODEL), lambda t, e: (t, 0)),
                  _const_spec(wq_t.shape), _const_spec(sk_t.shape),
                  pl.BlockSpec((et, D_MODEL), lambda t, e: (e, 0)),
                  pl.BlockSpec((et, D_MODEL), lambda t, e: (e, 0)),
                  _const_spec((1, D_MODEL)), _const_spec((1, D_MODEL))],
        out_specs=pl.BlockSpec((tt, D_MODEL), lambda t, e: (t, 0)),
        out_shape=jax.ShapeDtypeStruct((n, D_MODEL), F32),
        scratch_shapes=[pltpu.VMEM((D_MODEL, tt), BF16), pltpu.VMEM((D_MODEL, tt), F32),
                        pltpu.VMEM((H, nk, tt), F32), pltpu.VMEM((H, nk, tt), F32),
                        pltpu.VMEM((H, nk, tt), BF16), pltpu.VMEM((H, nk, tt), BF16),
                        pltpu.VMEM((et, tt), BF16), pltpu.VMEM((2 * H * nk, tt), F32)],
        compiler_params=_cparams(("parallel", "arbitrary")), name="peer_ffn")(
            x1, wq_t, sk_t, u_tab.astype(BF16), v_tab.astype(BF16),
            ln_g.reshape(1, -1).astype(F32), ln_b.reshape(1, -1).astype(F32))


def _layer(x, w_in, b_in, ssm_a_re, ssm_a_im, ssm_log_dt, ssm_b_re, ssm_b_im, ssm_c_re, ssm_c_im, ssm_d,
           w_glu, w_up_ssm, nsa_pe_k, nsa_w1_k, nsa_w2_k, nsa_pe_v, nsa_w1_v, nsa_w2_v, w_up_nsa,
           w_out, ln1_g, ln1_b, peer_w_q, peer_subkeys, peer_u, peer_v, ln2_g, ln2_b):
    bsz, seq, _ = x.shape
    x2 = x.reshape(bsz * seq, D_MODEL)
    u, qp, kvc, kv, gates, gm = _stage_proj(x2, w_in, b_in, seq)
    ya = _stage_ssm(u, bsz, seq, ssm_a_re, ssm_a_im, ssm_log_dt, ssm_b_re, ssm_b_im, ssm_c_re, ssm_c_im,
                    ssm_d, w_glu, w_up_ssm)
    kcv = _stage_compress(kvc, bsz, seq, nsa_pe_k, nsa_w1_k, nsa_w2_k, nsa_pe_v, nsa_w1_v, nsa_w2_v)
    ob = _stage_attn(qp, kcv, kv, gates, bsz, seq)
    x1 = _stage_merge(x2, ya, ob, gm, w_up_nsa, w_out, ln1_g, ln1_b)
    out = _stage_peer(x1, peer_w_q, peer_subkeys, peer_u, peer_v, ln2_g, ln2_b)
    return out.reshape(bsz, seq, D_MODEL)


def kernel(x, w_in, b_in, ssm_a_re, ssm_a_im, ssm_log_dt, ssm_b_re, ssm_b_im, ssm_c_re, ssm_c_im, ssm_d, w_glu,
           w_up_ssm, nsa_pe_k, nsa_w1_k, nsa_w2_k, nsa_pe_v, nsa_w1_v, nsa_w2_v, w_up_nsa, w_out, ln1_g, ln1_b,
           peer_w_q, peer_subkeys, peer_u, peer_v, ln2_g, ln2_b):
    params = (w_in, b_in, ssm_a_re, ssm_a_im, ssm_log_dt, ssm_b_re, ssm_b_im, ssm_c_re, ssm_c_im, ssm_d, w_glu,
              w_up_ssm, nsa_pe_k, nsa_w1_k, nsa_w2_k, nsa_pe_v, nsa_w1_v, nsa_w2_v, w_up_nsa, w_out, ln1_g, ln1_b,
              peer_w_q, peer_subkeys, peer_u, peer_v, ln2_g, ln2_b)
    for layer in range(w_in.shape[0]):
        x = _layer(x, *[p[layer] for p in params])
    return x
```

```python
import functools
import math

import jax
import jax.numpy as jnp
import numpy as np
from jax import lax
from jax.experimental import pallas as pl
from jax.experimental.pallas import tpu as pltpu

F32 = jnp.float32
BF16 = jnp.bfloat16

D_MODEL = 1024
SSM_WIDTH = 512
SSM_GROUP = 16
SSM_GROUPS = 32
SSM_STATE = 64
SSM_CHUNK = 8
LANE_GROUPS = 8
N_LANE_TILES = SSM_WIDTH // 128
NSA_HEADS = 8
NSA_KV_GROUPS = 2
NSA_REP = NSA_HEADS // NSA_KV_GROUPS
NSA_HEAD_DIM = 64
KV_WIDTH = NSA_KV_GROUPS * NSA_HEAD_DIM
CMP_BLOCK = 32
CMP_STRIDE = 16
CMP_HIDDEN = 128
SEL_BLOCK = 64
SEL_TOPK = 16
WINDOW = 512
Q_BLOCK = 256
FORCE_BONUS = 1.0e4
PEER_HEADS = 8
PEER_NKEYS = 128
PEER_EXPERTS = PEER_NKEYS * PEER_NKEYS
PEER_TOPK = 16
DN_ALPHA = 2.0 ** 0.25
LN_EPS = 1e-5
NEG = -1e30
VMEM_LIMIT = 56 * 1024 * 1024


def _cparams(sem):
    return pltpu.CompilerParams(dimension_semantics=sem, vmem_limit_bytes=VMEM_LIMIT)


def _const_spec(shape):
    nd = len(shape)
    return pl.BlockSpec(shape, lambda *_: (0,) * nd, pipeline_mode=pl.Buffered(1))


def _layer_norm(z, g, b):
    mu = jnp.mean(z, axis=-1, keepdims=True)
    var = jnp.mean(jnp.square(z - mu), axis=-1, keepdims=True)
    return (z - mu) * lax.rsqrt(var + LN_EPS) * g + b


def _nt_dot(a, b):
    return lax.dot_general(a, b, (((1,), (1,)), ((), ())), preferred_element_type=F32)


def _tn_dot(a, b):
    return lax.dot_general(a, b, (((0,), (0,)), ((), ())), preferred_element_type=F32)


KV_REC = 640
KS_OFF, VS_OFF, KW_OFF, VW_OFF = 0, 256, 384, 512
POS_HI, POS_LO, ONE_LANE, BLOCK_LANE0 = 64, 65, 64, 128
assert SEL_BLOCK == 64


def _position_features(pos, lane):
    hi, lo = pos >> 6, pos & 63
    f = jnp.where(lane == KS_OFF + POS_HI, hi, 0) + jnp.where(lane == KW_OFF + POS_HI, hi, 0)
    f = f + jnp.where(lane == KS_OFF + POS_LO, lo, 0) + jnp.where(lane == KW_OFF + POS_LO, lo, 0)
    f = f + jnp.where(lane == KS_OFF + BLOCK_LANE0 + hi, 1, 0)
    f = f + jnp.where(lane == VS_OFF + ONE_LANE, 1, 0) + jnp.where(lane == VW_OFF + ONE_LANE, 1, 0)
    return f.astype(F32)


def _proj_body(seq, x_ref, wu, bu, wq, bq, wc, bc, wk, bk, wg, bg, wm, bm, aux_ref,
               u_o, q_o, c_o, k_o, g_o, m_o):
    xb = x_ref[...].astype(BF16)
    tm = x_ref.shape[0]
    dk = NSA_HEAD_DIM

    def lin(w, b):
        return jnp.dot(xb, w[...], preferred_element_type=F32) + b[...]

    def halves(a, tile):
        t = a[:, tile * 128:(tile + 1) * 128]
        return t, pltpu.roll(t, dk, 1)

    low = lax.broadcasted_iota(jnp.int32, (tm, 128), 1) < dk
    u_o[...] = lin(wu, bu)
    q = lin(wq, bq) * (dk ** -0.5)
    for tile in range(NSA_HEADS // 2):
        for half, piece in enumerate(halves(q, tile)):
            h = 2 * tile + half
            q_o[:, h * 128:(h + 1) * 128] = jnp.where(low, piece, aux_ref[h:h + 1, :]).astype(BF16)
    c_o[...] = lin(wc, bc)
    kv = lin(wk, bk)
    parts = [halves(kv, tile) for tile in range(4)]
    zero = jnp.zeros((tm, 128), F32)
    recs = []
    for g in range(NSA_KV_GROUPS):
        k_s, v_s, k_w, v_w = (jnp.where(low, parts[tile][g], 0.0) for tile in range(4))
        recs += [k_s, zero, v_s, k_w, v_w]
    pos0 = (pl.program_id(0) % (seq // tm)) * tm
    shape = (tm, NSA_KV_GROUPS * KV_REC)
    lane = lax.broadcasted_iota(jnp.int32, shape, 1)
    lane = jnp.where(lane >= KV_REC, lane - KV_REC, lane)
    feat = _position_features(pos0 + lax.broadcasted_iota(jnp.int32, shape, 0), lane)
    k_o[...] = (jnp.concatenate(recs, axis=1) + feat).astype(BF16)
    g_o[...] = lin(wg, bg)
    m_o[...] = jax.nn.sigmoid(lin(wm, bm)).astype(m_o.dtype)


def _stage_proj(x2, w_in, b_in, seq):
    n = x2.shape[0]
    tm = 256
    o0 = SSM_WIDTH
    o1 = o0 + NSA_HEADS * NSA_HEAD_DIM
    o2 = o1 + 6 * KV_WIDTH
    o3 = o2 + 3 * NSA_HEADS
    w_u, b_u = w_in[:, :o0], b_in[:o0]
    w_q, b_q = w_in[:, o0:o1], b_in[o0:o1]
    w_kv, b_kv = w_in[:, o1:o2], b_in[o1:o2]
    w_g, b_g = w_in[:, o2:o3], b_in[o2:o3]
    w_m, b_m = w_in[:, o3:], b_in[o3:]

    slope = 2.0 ** -(np.arange(NSA_HEADS, dtype=np.float32) + 1.0)
    q_aux = np.zeros((NSA_HEADS, 128), np.float32)
    q_aux[:, POS_HI] = slope * 64.0
    q_aux[:, POS_LO] = slope

    def pad_g(a):
        a = a.reshape(a.shape[:-1] + (NSA_KV_GROUPS, 3 * NSA_REP))
        a = jnp.concatenate([a, jnp.zeros(a.shape[:-1] + (128 - 3 * NSA_REP,), a.dtype)], axis=-1)
        return a.reshape(a.shape[:-2] + (NSA_KV_GROUPS * 128,))

    ws = [w_u, w_q, w_kv[:, :2 * KV_WIDTH], w_kv[:, 2 * KV_WIDTH:], pad_g(w_g), w_m]
    bs = [b_u, b_q, b_kv[:2 * KV_WIDTH], b_kv[2 * KV_WIDTH:], pad_g(b_g), b_m]
    odt = [F32, BF16, F32, BF16, F32, BF16]
    widths = [SSM_WIDTH, NSA_HEADS * 128, 2 * KV_WIDTH, NSA_KV_GROUPS * KV_REC, NSA_KV_GROUPS * 128, 2 * D_MODEL]
    args, in_specs = [x2], [pl.BlockSpec((tm, D_MODEL), lambda i: (i, 0))]
    for w, b in zip(ws, bs):
        args += [w.astype(BF16), b.reshape(1, -1).astype(F32)]
        in_specs += [_const_spec(w.shape), _const_spec((1, w.shape[1]))]
    args.append(jnp.asarray(q_aux))
    in_specs.append(_const_spec(q_aux.shape))
    out_shape = [jax.ShapeDtypeStruct((n, w), dt) for w, dt in zip(widths, odt)]
    out_specs = [pl.BlockSpec((tm, w), lambda i: (i, 0)) for w in widths]
    return pl.pallas_call(
        functools.partial(_proj_body, seq), grid=(n // tm,), in_specs=in_specs, out_specs=out_specs, out_shape=out_shape,
        compiler_params=_cparams(("parallel",)), name="in_proj")(*args)


def _ssm_weights(a_re, a_im, log_dt, b_re, b_im, c_re, c_im):
    L, G, P, H, A, J = SSM_CHUNK, SSM_GROUPS, SSM_STATE, SSM_GROUP, LANE_GROUPS, N_LANE_TILES
    lam = lax.complex(a_re.astype(F32), a_im.astype(F32))
    dt = jnp.exp(log_dt.astype(F32))[:, None]
    lam_bar = jnp.exp(lam * dt)
    b_bar = ((lam_bar - 1.0) / lam)[:, :, None] * lax.complex(b_re.astype(F32), b_im.astype(F32))
    c = lax.complex(c_re.astype(F32), c_im.astype(F32))
    k = jnp.arange(L + 1, dtype=F32)
    pw = jnp.exp((lam * dt)[None] * k[:, None, None])
    def spread(x, n_inner):
        src = np.arange(x.shape[2])
        dst = np.arange(x.shape[2] * A)
        sel = ((src[:, None] // n_inner == dst[None, :] // (A * n_inner))
               & (src[:, None] % n_inner == dst[None, :] % n_inner))
        return jnp.einsum('jrk,kc->jrc', x, jnp.asarray(sel, F32)), (dst % (A * n_inner)) // n_inner

    def group_mask(row_group, col_group):
        return jnp.asarray(row_group[:, None] == col_group[None, :], F32)

    in_rows = (np.arange(L * 128) % 128) // H
    st_rows = (np.arange(2 * A * P) % (A * P)) // P
    kern = jnp.real(jnp.einsum('ghp,kgp,gpj->kghj', c, pw[:L], b_bar))
    s_i, t_i = jnp.arange(L)[:, None], jnp.arange(L)[None, :]
    tau = jnp.clip(t_i - s_i, 0, L - 1)
    causal = (t_i >= s_i).astype(F32)
    k_st = (kern[tau] * causal[:, :, None, None, None]).reshape(L, L, J, A, H, H)
    k_st = k_st.transpose(2, 0, 3, 5, 1, 4).reshape(J, L * 128, L * H)
    w_intra, cg = spread(k_st, H)
    w_intra = w_intra * group_mask(in_rows, cg)
    q = (pw[:L][::-1][:, :, :, None] * b_bar[None]).reshape(L, J, A, P, H)
    q = jnp.stack([jnp.real(q), jnp.imag(q)], axis=0)
    q = q.transpose(2, 1, 3, 5, 0, 4).reshape(J, L * 128, 2 * P)
    m2, cg = spread(q, P)
    m2 = m2 * group_mask(in_rows, cg)
    cl = (c[None] * pw[1:][:, :, None, :]).reshape(L, J, A, H, P)
    cl = jnp.stack([jnp.real(cl), -jnp.imag(cl)], axis=0)
    cl = cl.transpose(2, 0, 3, 5, 1, 4).reshape(J, 2 * A * P, L * H)
    m1, cg = spread(cl, H)
    m1 = m1 * group_mask(st_rows, cg)
    w2 = jnp.concatenate([w_intra, m1], axis=1)
    lam8 = pw[L].reshape(G * P)
    return m2.astype(BF16), w2.astype(BF16), jnp.real(lam8), jnp.imag(lam8)


def _split_lane_tiles(src_ref, dst_sc):
    for j in range(dst_sc.shape[0]):
        dst_sc[j] = src_ref[:, j * 128:(j + 1) * 128]


def _chunk_inputs(u_sc, j, rows):
    return jnp.concatenate([u_sc[j, pl.ds(s, rows, stride=SSM_CHUNK), :] for s in range(SSM_CHUNK)], axis=1)


def _ssm_state_body(u_ref, m2_ref, zre_ref, zim_ref, u_sc):
    half = LANE_GROUPS * SSM_STATE
    _split_lane_tiles(u_ref, u_sc)
    for j in range(N_LANE_TILES):
        z = jnp.dot(_chunk_inputs(u_sc, j, zre_ref.shape[0]).astype(BF16), m2_ref[j], preferred_element_type=F32)
        zre_ref[:, j * half:(j + 1) * half] = z[:, :half]
        zim_ref[:, j * half:(j + 1) * half] = z[:, half:]


def _ssm_scan_body(zre_ref, zim_ref, lre_ref, lim_ref, xre_ref, xim_ref):
    lr, li = lre_ref[...], lim_ref[...]

    def step(c, carry):
        xr, xi = carry
        xre_ref[pl.ds(c, 1), :] = xr
        xim_ref[pl.ds(c, 1), :] = xi
        return lr * xr - li * xi + zre_ref[pl.ds(c, 1), :], lr * xi + li * xr + zim_ref[pl.ds(c, 1), :]

    zero = jnp.zeros(lr.shape, F32)
    lax.fori_loop(0, zre_ref.shape[0], step, (zero, zero))


def _ssm_out_body(u_ref, xre_ref, xim_ref, w2_ref, d_ref, wglu_ref, wup_ref, o_ref, u_sc, y_sc):
    half = LANE_GROUPS * SSM_STATE
    rows = xre_ref.shape[0]
    _split_lane_tiles(u_ref, u_sc)
    for j in range(N_LANE_TILES):
        lhs = jnp.concatenate(
            [_chunk_inputs(u_sc, j, rows), xre_ref[:, j * half:(j + 1) * half], xim_ref[:, j * half:(j + 1) * half]],
            axis=1).astype(BF16)
        yj = jnp.dot(lhs, w2_ref[j], preferred_element_type=F32)
        for t in range(SSM_CHUNK):
            y_sc[j, pl.ds(t, rows, stride=SSM_CHUNK), :] = yj[:, t * 128:(t + 1) * 128]
    y = jnp.concatenate([y_sc[j] for j in range(N_LANE_TILES)], axis=1) + d_ref[...] * u_ref[...]
    gl = jnp.dot(jax.nn.gelu(y).astype(BF16), wglu_ref[...], preferred_element_type=F32)
    v = gl[:, :SSM_WIDTH] * jax.nn.sigmoid(gl[:, SSM_WIDTH:])
    o_ref[...] = jnp.dot(v.astype(BF16), wup_ref[...], preferred_element_type=F32).astype(o_ref.dtype)


def _stage_ssm(u, bsz, seq, a_re, a_im, log_dt, b_re, b_im, c_re, c_im, d_skip, w_glu, w_up):
    n = bsz * seq
    L = SSM_CHUNK
    nck = seq // L
    gp = SSM_GROUPS * SSM_STATE
    m2, w2, l8re, l8im = _ssm_weights(a_re, a_im, log_dt, b_re, b_im, c_re, c_im)
    rs = min(nck, 512)
    zre, zim = pl.pallas_call(
        _ssm_state_body, grid=(bsz, nck // rs),
        in_specs=[pl.BlockSpec((rs * L, SSM_WIDTH), lambda b, i: (b * (nck // rs) + i, 0)),
                  _const_spec(m2.shape)],
        out_specs=[pl.BlockSpec((rs, gp), lambda b, i: (i, b))] * 2,
        out_shape=[jax.ShapeDtypeStruct((nck, bsz * gp), F32)] * 2,
        scratch_shapes=[pltpu.VMEM((N_LANE_TILES, rs * L, 128), F32)],
        compiler_params=_cparams(("parallel", "parallel")), name="ssm_chunk_state")(u, m2)
    cw = 1024
    ncol = bsz * gp // cw
    lre = jnp.tile(l8re, bsz).reshape(1, bsz * gp)
    lim = jnp.tile(l8im, bsz).reshape(1, bsz * gp)
    col = pl.BlockSpec((nck, cw), lambda i: (0, i))
    lam = pl.BlockSpec((1, cw), lambda i: (0, i))
    xre, xim = pl.pallas_call(
        _ssm_scan_body, grid=(ncol,), in_specs=[col, col, lam, lam], out_specs=[col, col],
        out_shape=[jax.ShapeDtypeStruct((nck, bsz * gp), F32)] * 2,
        compiler_params=_cparams(("parallel",)), name="ssm_carry_scan")(zre, zim, lre, lim)
    ro = min(nck, 128)
    d8 = d_skip.reshape(1, SSM_WIDTH).astype(F32)
    return pl.pallas_call(
        _ssm_out_body, grid=(bsz, nck // ro),
        in_specs=[pl.BlockSpec((ro * L, SSM_WIDTH), lambda b, i: (b * (nck // ro) + i, 0)),
                  pl.BlockSpec((ro, gp), lambda b, i: (i, b)),
                  pl.BlockSpec((ro, gp), lambda b, i: (i, b)),
                  _const_spec(w2.shape), _const_spec((1, SSM_WIDTH)),
                  _const_spec(w_glu.shape), _const_spec(w_up.shape)],
        out_specs=pl.BlockSpec((ro * L, D_MODEL), lambda b, i: (b * (nck // ro) + i, 0)),
        out_shape=jax.ShapeDtypeStruct((n, D_MODEL), BF16),
        scratch_shapes=[pltpu.VMEM((N_LANE_TILES, ro * L, 128), F32)] * 2,
        compiler_params=_cparams(("parallel", "parallel")), name="ssm_out")(
            u, xre, xim, w2, d8, w_glu.astype(BF16), w_up.astype(BF16))


def _compress_body(kv_ref, pea_ref, peb_ref, w1a_ref, w1b_ref, w2_ref, o_ref, kv_sc):
    nrow = o_ref.shape[0]
    _split_lane_tiles(kv_ref, kv_sc)
    f = jnp.concatenate([kv_sc[j, pl.ds(l, nrow, stride=CMP_STRIDE), :]
                         for l in range(CMP_STRIDE) for j in range(kv_sc.shape[0])], axis=1)
    a = jnp.dot((f + pea_ref[...]).astype(BF16), w1a_ref[...], preferred_element_type=F32)
    b = jnp.dot((f + peb_ref[...]).astype(BF16), w1b_ref[...], preferred_element_type=F32)
    pre = a + pltpu.roll(b, b.shape[0] - 1, 0)
    hid = jax.nn.gelu(pre)
    out = jnp.dot(hid.astype(BF16), w2_ref[...], preferred_element_type=F32)
    lane = lax.broadcasted_iota(jnp.int32, out.shape, 1) & 255
    c_end = lax.broadcasted_iota(jnp.int32, out.shape, 0) * CMP_STRIDE + (CMP_BLOCK - 1)
    feat = jnp.where(lane == POS_HI, c_end >> 6, 0) + jnp.where(lane == POS_LO, c_end & 63, 0)
    o_ref[...] = (out + feat.astype(F32)).astype(BF16)


def _stage_compress(kvc, bsz, seq, pe_k, w1_k, w2_k, pe_v, w1_v, w2_v):
    hl = CMP_STRIDE
    nrow = seq // hl
    G, dk, hid = NSA_KV_GROUPS, NSA_HEAD_DIM, CMP_HIDDEN
    eye = jnp.eye(2 * G, dtype=F32)

    def big_w1(lo):
        wk = w1_k.reshape(CMP_BLOCK, dk, hid)[lo:lo + hl]
        wv = w1_v.reshape(CMP_BLOCK, dk, hid)[lo:lo + hl]
        w = jnp.stack([wk, wk, wv, wv], axis=1)
        return jnp.einsum('lcdj,ce->lcdej', w, eye).reshape(hl * 2 * G * dk, 2 * G * hid)

    def big_pe(lo):
        pk, pv = pe_k[lo:lo + hl], pe_v[lo:lo + hl]
        return jnp.stack([pk, pk, pv, pv], axis=1).reshape(1, hl * 2 * G * dk)

    w2 = jnp.stack([w2_k, w2_k, w2_v, w2_v], axis=0)
    w2 = jnp.concatenate([w2, jnp.zeros((2 * G, hid, 128 - dk), w2.dtype)], axis=-1)
    slot = eye[np.array([kv * G + g for g in range(G) for kv in range(2)])].T
    w2 = jnp.einsum('cjd,ce->cjed', w2, slot).reshape(2 * G * hid, 2 * G * 128)
    width = hl * 2 * G * dk
    return pl.pallas_call(
        _compress_body, grid=(bsz,),
        in_specs=[pl.BlockSpec((seq, 2 * G * dk), lambda b: (b, 0)),
                  _const_spec((1, width)), _const_spec((1, width)),
                  _const_spec((width, 2 * G * hid)), _const_spec((width, 2 * G * hid)),
                  _const_spec((2 * G * hid, 2 * G * 128))],
        out_specs=pl.BlockSpec((nrow, 2 * G * 128), lambda b: (b, 0)),
        out_shape=jax.ShapeDtypeStruct((bsz * nrow, 2 * G * 128), BF16),
        scratch_shapes=[pltpu.VMEM((2 * G * dk // 128, seq, 128), F32)],
        compiler_params=_cparams(("parallel",)), name="nsa_compress")(
            kvc, big_pe(0).astype(F32), big_pe(hl).astype(F32),
            big_w1(0).astype(BF16), big_w1(hl).astype(BF16), w2.astype(BF16))


MASK_BIG = 2.0 ** 100
SEL_TILE = 512


def _nsa_body(q_ref, kcv_ref, kv_ref, g_ref, ov_ref, rel_ref, relc_ref, o_ref, acc, s_buf, p_buf, a_buf):
    i = pl.program_id(2)
    qs = i * Q_BLOCK
    seq = kv_ref.shape[0]
    rows = NSA_REP * Q_BLOCK
    dk = NSA_HEAD_DIM
    qh = [q_ref[:, r * 128:(r + 1) * 128] for r in range(NSA_REP)]
    q_all = jnp.concatenate(qh, axis=0)

    kc = kcv_ref[:, :128]
    vc = kcv_ref[:, 128:]
    vis_c = relc_ref[...] >= -qs
    s = jnp.where(vis_c, _nt_dot(q_all, kc), NEG)
    p = jnp.exp(s - jnp.max(s, axis=-1, keepdims=True)) * jnp.where(vis_c, 1.0, 0.0)
    p = (p / jnp.maximum(jnp.sum(p, axis=-1, keepdims=True), 1e-30)).astype(BF16)
    o_cmp = jnp.dot(p, vc, preferred_element_type=F32)

    wlen = min(WINDOW + Q_BLOCK, seq)
    w0 = pl.multiple_of(jnp.maximum(qs + Q_BLOCK - wlen, 0), Q_BLOCK)
    rel_w = rel_ref[...]
    s = _nt_dot(q_all, kv_ref[pl.ds(w0, wlen), KW_OFF:KW_OFF + 128])
    in_band = pltpu.bitcast(rel_w - (w0 - qs), jnp.uint32) < WINDOW
    s = jnp.where(in_band, s, NEG)
    p_w = jnp.exp(s - jnp.max(s, axis=-1, keepdims=True)).astype(BF16)
    a_win = jnp.dot(p_w, kv_ref[pl.ds(w0, wlen), VW_OFF:VW_OFF + 128], preferred_element_type=F32)

    p_heads = jnp.concatenate([p[r * Q_BLOCK:(r + 1) * Q_BLOCK] for r in range(NSA_REP)], axis=1)
    imp = jnp.dot(p_heads, ov_ref[...], preferred_element_type=F32)
    coli = lax.broadcasted_iota(jnp.int32, (Q_BLOCK, 128), 1)
    t_q = qs + lax.broadcasted_iota(jnp.int32, (Q_BLOCK, 128), 0)
    cur = t_q >> 6
    forced = (coli == 0) | (coli == cur) | (coli == cur - 1)
    score = jnp.where(coli * SEL_BLOCK <= t_q, imp + jnp.where(forced, FORCE_BONUS, 0.0), NEG)
    nsl = seq // SEL_BLOCK
    st = score.T[:nsl]
    groups = [st[8 * v:8 * v + 8] for v in range(nsl // 8)]
    jsub = lax.broadcasted_iota(jnp.int32, (8, Q_BLOCK), 0)
    rank = [jnp.zeros((8, Q_BLOCK), F32) for _ in groups]
    for k in range(nsl):
        rk = st[k:k + 1, :]
        for v, sv in enumerate(groups):
            if v > k // 8:
                one = jnp.where(rk >= sv, 1.0, 0.0)
            elif v < k // 8:
                one = jnp.where(rk > sv, 1.0, 0.0)
            else:
                one = jnp.where(rk > sv, 1.0, jnp.where(jsub > k % 8, jnp.where(rk == sv, 1.0, 0.0), 0.0))
            rank[v] = rank[v] + one
    drop_t = jnp.where(jnp.concatenate(rank, axis=0) < float(min(SEL_TOPK, nsl)), 0.0, -MASK_BIG)
    if nsl < 128:
        drop_t = jnp.concatenate([drop_t, jnp.zeros((128 - nsl, Q_BLOCK), F32)], axis=0)
    drop = drop_t.T.astype(BF16)
    q_sel = jnp.concatenate([jnp.concatenate([qh[r], drop], axis=1) for r in range(NSA_REP)], axis=0)

    n_tiles = (qs + Q_BLOCK + SEL_TILE - 1) // SEL_TILE

    def scores(kt):
        k0 = pl.multiple_of(kt * SEL_TILE, SEL_TILE)
        return _nt_dot(q_sel, kv_ref[pl.ds(k0, SEL_TILE), KS_OFF:KS_OFF + 256])

    def values(kt):
        k0 = pl.multiple_of(kt * SEL_TILE, SEL_TILE)
        return kv_ref[pl.ds(k0, SEL_TILE), VS_OFF:VS_OFF + 128]

    def trip(diagonal, kt, m_old):
        if not diagonal:
            s_next = scores(kt + 1)
        acc[...] = a_buf[...] * acc[...] + jnp.dot(p_buf[...], values(jnp.maximum(kt - 1, 0)),
                                                   preferred_element_type=F32)
        s = s_buf[...]
        if diagonal:
            s = jnp.where(rel_ref[:, :SEL_TILE] >= kt * SEL_TILE - qs, s, NEG)
        m_new = jnp.maximum(m_old, jnp.max(s, axis=-1, keepdims=True))
        p_buf[...] = jnp.exp(s - m_new).astype(BF16)
        a_buf[...] = jnp.exp(m_old - m_new)
        if not diagonal:
            s_buf[...] = s_next
        return m_new

    acc[...] = jnp.zeros_like(acc)
    p_buf[...] = jnp.zeros_like(p_buf)
    a_buf[...] = jnp.ones_like(a_buf)
    s_buf[...] = scores(0)
    m_s = lax.fori_loop(0, n_tiles - 1, functools.partial(trip, False), jnp.full((rows, 1), NEG, F32))
    trip(True, n_tiles - 1, m_s)
    a_sel = a_buf[...] * acc[...] + jnp.dot(p_buf[...], values(n_tiles - 1), preferred_element_type=F32)

    gate = jax.nn.sigmoid(g_ref[...])
    outs = []
    for r in range(NSA_REP):
        rr = slice(r * Q_BLOCK, (r + 1) * Q_BLOCK)
        o_s = a_sel[rr, :dk] / a_sel[rr, ONE_LANE:ONE_LANE + 1]
        o_w = a_win[rr, :dk] / a_win[rr, ONE_LANE:ONE_LANE + 1]
        outs.append(gate[:, 3 * r:3 * r + 1] * o_cmp[rr, :dk] + gate[:, 3 * r + 1:3 * r + 2] * o_s
                    + gate[:, 3 * r + 2:3 * r + 3] * o_w)
    o_ref[...] = jnp.concatenate(outs, axis=1).astype(o_ref.dtype)


def _stage_attn(qp, kcv, kv, gates, bsz, seq):
    n = bsz * seq
    nq = seq // Q_BLOCK
    ncmp = seq // CMP_STRIDE
    nsl = seq // SEL_BLOCK
    assert seq % SEL_TILE == 0
    c_start = np.arange(ncmp) * CMP_STRIDE
    s_start = np.arange(nsl) * SEL_BLOCK
    ov = ((c_start[:, None] < s_start[None, :] + SEL_BLOCK) & (c_start[:, None] + CMP_BLOCK > s_start[None, :]))
    ov = np.pad(ov.astype(np.float32), ((0, 0), (0, 128 - nsl)))
    ov[ncmp - 1] = 0.0
    ov = jnp.asarray(np.tile(ov, (NSA_REP, 1)), BF16)
    width = NSA_REP * NSA_HEAD_DIM
    rows = NSA_REP * Q_BLOCK
    wlen = min(WINDOW + Q_BLOCK, seq)
    assert wlen >= SEL_TILE
    offs = jnp.arange(rows, dtype=jnp.int32)[:, None] % Q_BLOCK
    rel = offs - jnp.arange(wlen, dtype=jnp.int32)[None, :]
    relc = offs - (jnp.arange(ncmp, dtype=jnp.int32)[None, :] * CMP_STRIDE + (CMP_BLOCK - 1))
    return pl.pallas_call(
        _nsa_body, grid=(bsz, NSA_KV_GROUPS, nq),
        in_specs=[pl.BlockSpec((Q_BLOCK, NSA_REP * 128), lambda b, g, i: (b * nq + i, g)),
                  pl.BlockSpec((ncmp, 256), lambda b, g, i: (b, g)),
                  pl.BlockSpec((seq, KV_REC), lambda b, g, i: (b, g)),
                  pl.BlockSpec((Q_BLOCK, 128), lambda b, g, i: (b * nq + i, g)),
                  _const_spec(ov.shape), _const_spec(rel.shape), _const_spec(relc.shape)],
        out_specs=pl.BlockSpec((Q_BLOCK, width), lambda b, g, i: (b * nq + i, g)),
        out_shape=jax.ShapeDtypeStruct((n, NSA_KV_GROUPS * width), BF16),
        scratch_shapes=[pltpu.VMEM((rows, 128), F32), pltpu.VMEM((rows, SEL_TILE), F32),
                        pltpu.VMEM((rows, SEL_TILE), BF16), pltpu.VMEM((rows, 1), F32)],
        compiler_params=_cparams(("parallel", "parallel", "arbitrary")), name="nsa_attention")(
            qp, kcv, kv, gates, ov, rel, relc)


def _merge_body(x_ref, ya_ref, ob_ref, gm_ref, wnsa_ref, wout_ref, g_ref, b_ref, o_ref):
    yb = jnp.dot(ob_ref[...], wnsa_ref[...], preferred_element_type=F32)
    gm = gm_ref[...].astype(F32)
    mix_in = gm[:, :D_MODEL] * ya_ref[...].astype(F32) + gm[:, D_MODEL:] * yb
    mix = jnp.dot(mix_in.astype(BF16), wout_ref[...], preferred_element_type=F32)
    o_ref[...] = _layer_norm(DN_ALPHA * x_ref[...] + mix, g_ref[...], b_ref[...])


def _stage_merge(x2, ya, ob, gm, w_up_nsa, w_out, ln_g, ln_b):
    n = x2.shape[0]
    tm = 512
    row = lambda w: pl.BlockSpec((tm, w), lambda i: (i, 0))
    return pl.pallas_call(
        _merge_body, grid=(n // tm,),
        in_specs=[row(D_MODEL), row(D_MODEL), row(ob.shape[1]), row(2 * D_MODEL),
                  _const_spec(w_up_nsa.shape), _const_spec(w_out.shape),
                  _const_spec((1, D_MODEL)), _const_spec((1, D_MODEL))],
        out_specs=row(D_MODEL), out_shape=jax.ShapeDtypeStruct((n, D_MODEL), F32),
        compiler_params=_cparams(("parallel",)), name="merge_ln")(
            x2, ya, ob, gm, w_up_nsa.astype(BF16), w_out.astype(BF16),
            ln_g.reshape(1, -1).astype(F32), ln_b.reshape(1, -1).astype(F32))


BIG_NEG = -3.0e38


def _top_rows(s, k):
    rid = lax.broadcasted_iota(jnp.int32, s.shape, 0).astype(F32)
    rank = jnp.full(s.shape, float(k), F32)
    vals, idxs = [], []
    for r in range(k):
        m = jnp.max(s, axis=0, keepdims=True)
        idx = jnp.min(jnp.where(s == m, rid, float(s.shape[0])), axis=0, keepdims=True)
        hit = rid == idx
        rank = jnp.where(hit, float(r), rank)
        s = jnp.where(hit, BIG_NEG, s)
        vals.append(m)
        idxs.append(idx)
    return jnp.concatenate(vals, axis=0), jnp.concatenate(idxs, axis=0), rank


def _max_rounds(s, k, want_round=True):
    rnd = jnp.full(s.shape, float(k), F32) if want_round else None
    vals = []
    for r in range(k):
        m = jnp.max(s, axis=0, keepdims=True)
        eq = s == m
        if want_round:
            rnd = jnp.where(eq, float(r), rnd)
        s = jnp.where(eq, BIG_NEG, s)
        vals.append(m)
    return jnp.concatenate(vals, axis=0), rnd


def _pair_candidates(a, b):
    c8 = lax.broadcasted_iota(jnp.int32, (8, a.shape[1]), 0)
    parts = [a[0:1] + b[0:8], a[0:1] + b[8:16]]
    for r in range(1, 8):
        keep = PEER_TOPK // (r + 1)
        cand = a[r:r + 1] + b[0:8]
        parts.append(cand if keep >= 8 else jnp.where(c8 < keep, cand, BIG_NEG))
    parts.append(a[8:16] + b[0:1])
    return jnp.concatenate(parts, axis=0)


def _picks_per_rank(picked):
    rows = [jnp.sum(picked[0:16], axis=0, keepdims=True)]
    rows += [jnp.sum(picked[8 * (r + 1):8 * (r + 2)], axis=0, keepdims=True) for r in range(1, 8)]
    return jnp.concatenate(rows + [picked[72:80]], axis=0)


def _spread_by_rank(rank, per_rank):
    out = jnp.zeros(rank.shape, F32)
    for r in range(PEER_TOPK):
        out = out + jnp.where(rank == float(r), per_rank[r:r + 1], 0.0)
    return out


def _sort16_network():
    pairs, p = [], 1
    while p < 16:
        k = p
        while k >= 1:
            for j in range(k % p, 16 - k, 2 * k):
                for i in range(min(k, 16 - j - k)):
                    if (i + j) // (2 * p) == (i + j + k) // (2 * p):
                        pairs.append((i + j, i + j + k))
            k //= 2
        p *= 2
    return pairs


_SORT16 = _sort16_network()


def _sorted_top16(s):
    def exchange(v, i, j):
        v[i], v[j] = jnp.maximum(v[i], v[j]), jnp.minimum(v[i], v[j])

    v = [s[8 * k:8 * k + 8] for k in range(16)]
    for i, j in _SORT16:
        exchange(v, i, j)
    for shift in (4, 2, 1):
        v = [jnp.maximum(v[k], pltpu.roll(v[15 - k], shift, 0)) for k in range(16)]
        for d in (8, 4, 2, 1):
            for i in range(16):
                if i & d == 0:
                    exchange(v, i, i + d)
    return jnp.concatenate([x[0:1] for x in v], axis=0)


def _count_rows(vals, s, strict):
    rows = [jnp.broadcast_to(vals[r:r + 1], (8, s.shape[1])) for r in range(vals.shape[0])]
    assert len(rows) == 16
    outs = []
    for c in range(0, s.shape[0], 8):
        sc = s[c:c + 8]
        test = (lambda row: row > sc) if strict else (lambda row: row <= sc)
        c1 = test(rows[7])
        c2 = test(jnp.where(c1, rows[11], rows[3]))
        c3 = test(jnp.where(c1, jnp.where(c2, rows[13], rows[9]), jnp.where(c2, rows[5], rows[1])))
        hi = jnp.where(c2, jnp.where(c3, rows[14], rows[12]), jnp.where(c3, rows[10], rows[8]))
        lo = jnp.where(c2, jnp.where(c3, rows[6], rows[4]), jnp.where(c3, rows[2], rows[0]))
        c4 = test(jnp.where(c1, hi, lo))
        acc = (jnp.where(c1, 8.0, 0.0) + jnp.where(c2, 4.0, 0.0) + jnp.where(c3, 2.0, 0.0) + jnp.where(c4, 1.0, 0.0)
               + jnp.where(test(rows[15]), 1.0, 0.0))
        outs.append(acc)
    return jnp.concatenate(outs, axis=0)


def _route_fast(s0, s1):
    k = PEER_TOPK
    count = lambda v: jnp.sum(v, axis=0, keepdims=True)
    a, b = _sorted_top16(s0), _sorted_top16(s1)
    cand = _pair_candidates(a, b)
    best, _ = _max_rounds(cand, k, want_round=False)
    hit = cand >= best[k - 1:k]
    z = jnp.sum(jnp.where(hit, jnp.exp(cand - best[0:1]), 0.0), axis=0, keepdims=True)
    nrank = _picks_per_rank(jnp.where(hit, 1.0, 0.0))
    earns = jnp.concatenate([jnp.min(jnp.where(nrank >= float(v), a, -BIG_NEG), axis=0, keepdims=True)
                             for v in range(1, k + 1)], axis=0)
    n_i = _count_rows(earns, s0, strict=False)
    ties = lambda t, s: (count(jnp.where(t[:-1] == t[1:], 1.0, 0.0))
                         + jnp.abs(count(jnp.where(s >= t[k - 1:k], 1.0, 0.0)) - k))
    bad = ties(a, s0) + ties(b, s1) + jnp.abs(count(nrank) - k)
    return n_i, jnp.exp(s0 - a[0:1]) / z, _count_rows(b, s1, strict=True), jnp.exp(s1 - b[0:1]), bad


def _route_exact(s0, s1):
    k = PEER_TOPK
    a, _, rk0 = _top_rows(s0, k)
    b, _, rk1 = _top_rows(s1, k)
    cand = jnp.concatenate([a[r:r + 1] + b for r in range(k)], axis=0)
    best, bidx, _ = _top_rows(cand, k)
    brank = jnp.floor(bidx * (1.0 / k))
    z = jnp.sum(jnp.exp(best - best[0:1]), axis=0, keepdims=True)
    nrank = jnp.concatenate([jnp.sum(jnp.where(brank == float(r), 1.0, 0.0), axis=0, keepdims=True)
                             for r in range(k)], axis=0)
    return _spread_by_rank(rk0, nrank), jnp.exp(s0 - a[0:1]) / z, rk1, jnp.exp(s1 - b[0:1])


def _gelu_tanh(x):
    c = math.sqrt(2.0 / math.pi)
    return x / (1.0 + jnp.exp(x * (x * x * (-2.0 * c * 0.044715) - 2.0 * c)))


def _peer_body(x_ref, wq_ref, sk_ref, u_ref, v_ref, g_ref, b_ref, o_ref,
               xt_sc, acc_sc, n_sc, e0_sc, rk1_sc, e1_sc, p_sc, st_sc):
    e = pl.program_id(1)
    nk = PEER_NKEYS

    @pl.when(e == 0)
    def _():
        xt = x_ref[...].T.astype(BF16)
        xt_sc[...] = xt
        acc_sc[...] = jnp.zeros_like(acc_sc)
        st_sc[...] = jnp.dot(wq_ref[...], xt, preferred_element_type=F32).astype(BF16)
        half = sk_ref.shape[2]

        def scores_of(h):
            def one(c):
                return jnp.dot(sk_ref[c], st_sc[pl.ds(pl.multiple_of(c * half, half), half), :],
                               preferred_element_type=F32)
            return one(2 * h), one(2 * h + 1)

        def put(h, n_i, e0, rk1, e1):
            n_sc[h] = n_i
            e0_sc[h] = e0
            rk1_sc[h] = rk1.astype(BF16)
            e1_sc[h] = e1.astype(BF16)

        @pl.loop(0, PEER_HEADS // 2)
        def _(hh):
            heads = (2 * hh, 2 * hh + 1)
            flags = []
            for h in heads:
                *routing, bad = _route_fast(*scores_of(h))
                put(h, *routing)
                flags.append(jnp.max(bad) > 0.0)
            for h, flag in zip(heads, flags):
                @pl.when(flag)
                def _():
                    put(h, *_route_exact(*scores_of(h)))

    rows = u_ref.shape[0] // nk
    ht = jnp.dot(u_ref[...], xt_sc[...], preferred_element_type=F32)
    act = _gelu_tanh(ht.astype(BF16))
    for ii in range(rows):
        w = jnp.zeros((nk, xt_sc.shape[1]), BF16)
        for h in range(PEER_HEADS):
            n_row = n_sc[h, pl.ds(e * rows + ii, 1), :].astype(BF16)
            e0_row = e0_sc[h, pl.ds(e * rows + ii, 1), :].astype(BF16)
            w = w + jnp.where(rk1_sc[h] < n_row, e0_row * e1_sc[h], 0)
        p_sc[ii * nk:(ii + 1) * nk, :] = w * act[ii * nk:(ii + 1) * nk]
    acc_sc[...] += _tn_dot(v_ref[...], p_sc[...])

    @pl.when(e == pl.num_programs(1) - 1)
    def _():
        o_ref[...] = _layer_norm(DN_ALPHA * x_ref[...] + acc_sc[...].T, g_ref[...], b_ref[...])


def _stage_peer(x1, w_q, subkeys, u_tab, v_tab, ln_g, ln_b, tt=1024, et=512):
    n = x1.shape[0]
    tt = min(tt, n)
    H, nk, half = PEER_HEADS, PEER_NKEYS, subkeys.shape[-1]
    wq_t = w_q.T.astype(BF16)
    sk_t = subkeys.reshape(2 * H, nk, half).astype(BF16)
    ne = u_tab.shape[0]
    return pl.pallas_call(
        _peer_body, grid=(n // tt, ne // et),
        in_specs=[pl.BlockSpec((tt, D_MODEL), lambda t, e: (t, 0)),
                  _const_spec(wq_t.shape), _const_spec(sk_t.shape),
                  pl.BlockSpec((et, D_MODEL), lambda t, e: (e, 0)),
                  pl.BlockSpec((et, D_MODEL), lambda t, e: (e, 0)),
                  _const_spec((1, D_MODEL)), _const_spec((1, D_MODEL))],
        out_specs=pl.BlockSpec((tt, D_MODEL), lambda t, e: (t, 0)),
        out_shape=jax.ShapeDtypeStruct((n, D_MODEL), F32),
        scratch_shapes=[pltpu.VMEM((D_MODEL, tt), BF16), pltpu.VMEM((D_MODEL, tt), F32),
                        pltpu.VMEM((H, nk, tt), F32), pltpu.VMEM((H, nk, tt), F32),
                        pltpu.VMEM((H, nk, tt), BF16), pltpu.VMEM((H, nk, tt), BF16),
                        pltpu.VMEM((et, tt), BF16), pltpu.VMEM((2 * H * half, tt), BF16)],
        compiler_params=_cparams(("parallel", "arbitrary")), name="peer_ffn")(
            x1, wq_t, sk_t, u_tab.astype(BF16), v_tab.astype(BF16),
            ln_g.reshape(1, -1).astype(F32), ln_b.reshape(1, -1).astype(F32))


def _layer(x, w_in, b_in, ssm_a_re, ssm_a_im, ssm_log_dt, ssm_b_re, ssm_b_im, ssm_c_re, ssm_c_im, ssm_d,
           w_glu, w_up_ssm, nsa_pe_k, nsa_w1_k, nsa_w2_k, nsa_pe_v, nsa_w1_v, nsa_w2_v, w_up_nsa,
           w_out, ln1_g, ln1_b, peer_w_q, peer_subkeys, peer_u, peer_v, ln2_g, ln2_b):
    bsz, seq, _ = x.shape
    x2 = x.reshape(bsz * seq, D_MODEL)
    u, qp, kvc, kv, gates, gm = _stage_proj(x2, w_in, b_in, seq)
    ya = _stage_ssm(u, bsz, seq, ssm_a_re, ssm_a_im, ssm_log_dt, ssm_b_re, ssm_b_im, ssm_c_re, ssm_c_im,
                    ssm_d, w_glu, w_up_ssm)
    kcv = _stage_compress(kvc, bsz, seq, nsa_pe_k, nsa_w1_k, nsa_w2_k, nsa_pe_v, nsa_w1_v, nsa_w2_v)
    ob = _stage_attn(qp, kcv, kv, gates, bsz, seq)
    x1 = _stage_merge(x2, ya, ob, gm, w_up_nsa, w_out, ln1_g, ln1_b)
    out = _stage_peer(x1, peer_w_q, peer_subkeys, peer_u, peer_v, ln2_g, ln2_b)
    return out.reshape(bsz, seq, D_MODEL)


def kernel(x, w_in, b_in, ssm_a_re, ssm_a_im, ssm_log_dt, ssm_b_re, ssm_b_im, ssm_c_re, ssm_c_im, ssm_d, w_glu,
           w_up_ssm, nsa_pe_k, nsa_w1_k, nsa_w2_k, nsa_pe_v, nsa_w1_v, nsa_w2_v, w_up_nsa, w_out, ln1_g, ln1_b,
           peer_w_q, peer_subkeys, peer_u, peer_v, ln2_g, ln2_b):
    params = (w_in, b_in, ssm_a_re, ssm_a_im, ssm_log_dt, ssm_b_re, ssm_b_im, ssm_c_re, ssm_c_im, ssm_d, w_glu,
              w_up_ssm, nsa_pe_k, nsa_w1_k, nsa_w2_k, nsa_pe_v, nsa_w1_v, nsa_w2_v, w_up_nsa, w_out, ln1_g, ln1_b,
              peer_w_q, peer_subkeys, peer_u, peer_v, ln2_g, ln2_b)
    for layer in range(w_in.shape[0]):
        x = _layer(x, *[p[layer] for p in params])
    return x
```

```python
import functools
import math

import jax
import jax.numpy as jnp
import numpy as np
from jax import lax
from jax.experimental import pallas as pl
from jax.experimental.pallas import tpu as pltpu

F32 = jnp.float32
BF16 = jnp.bfloat16

D_MODEL = 1024
SSM_WIDTH = 512
SSM_GROUP = 16
SSM_GROUPS = 32
SSM_STATE = 64
SSM_CHUNK = 8
LANE_GROUPS = 8
N_LANE_TILES = SSM_WIDTH // 128
NSA_HEADS = 8
NSA_KV_GROUPS = 2
NSA_REP = NSA_HEADS // NSA_KV_GROUPS
NSA_HEAD_DIM = 64
KV_WIDTH = NSA_KV_GROUPS * NSA_HEAD_DIM
CMP_BLOCK = 32
CMP_STRIDE = 16
CMP_HIDDEN = 128
SEL_BLOCK = 64
SEL_TOPK = 16
WINDOW = 512
Q_BLOCK = 256
FORCE_BONUS = 1.0e4
PEER_HEADS = 8
PEER_NKEYS = 128
PEER_EXPERTS = PEER_NKEYS * PEER_NKEYS
PEER_TOPK = 16
DN_ALPHA = 2.0 ** 0.25
LN_EPS = 1e-5
NEG = -1e30
VMEM_LIMIT = 56 * 1024 * 1024


def _cparams(sem):
    return pltpu.CompilerParams(dimension_semantics=sem, vmem_limit_bytes=VMEM_LIMIT)


def _const_spec(shape):
    nd = len(shape)
    return pl.BlockSpec(shape, lambda *_: (0,) * nd, pipeline_mode=pl.Buffered(1))


def _layer_norm(z, g, b):
    mu = jnp.mean(z, axis=-1, keepdims=True)
    var = jnp.mean(jnp.square(z - mu), axis=-1, keepdims=True)
    return (z - mu) * lax.rsqrt(var + LN_EPS) * g + b


def _nt_dot(a, b):
    return lax.dot_general(a, b, (((1,), (1,)), ((), ())), preferred_element_type=F32)


def _tn_dot(a, b):
    return lax.dot_general(a, b, (((0,), (0,)), ((), ())), preferred_element_type=F32)


KV_REC = 640
KS_OFF, VS_OFF, KW_OFF, VW_OFF = 0, 256, 384, 512
POS_HI, POS_LO, ONE_LANE, BLOCK_LANE0 = 64, 65, 64, 128
assert SEL_BLOCK == 64


def _position_features(pos, lane):
    hi, lo = pos >> 6, pos & 63
    f = jnp.where(lane == KS_OFF + POS_HI, hi, 0) + jnp.where(lane == KW_OFF + POS_HI, hi, 0)
    f = f + jnp.where(lane == KS_OFF + POS_LO, lo, 0) + jnp.where(lane == KW_OFF + POS_LO, lo, 0)
    f = f + jnp.where(lane == KS_OFF + BLOCK_LANE0 + hi, 1, 0)
    f = f + jnp.where(lane == VS_OFF + ONE_LANE, 1, 0) + jnp.where(lane == VW_OFF + ONE_LANE, 1, 0)
    return f.astype(F32)


def _proj_body(seq, x_ref, wu, bu, wq, bq, wc, bc, wk, bk, wg, bg, wm, bm, aux_ref,
               u_o, q_o, c_o, k_o, g_o, m_o):
    xb = x_ref[...].astype(BF16)
    tm = x_ref.shape[0]
    dk = NSA_HEAD_DIM

    def lin(w, b):
        return jnp.dot(xb, w[...], preferred_element_type=F32) + b[...]

    def halves(a, tile):
        t = a[:, tile * 128:(tile + 1) * 128]
        return t, pltpu.roll(t, dk, 1)

    low = lax.broadcasted_iota(jnp.int32, (tm, 128), 1) < dk
    u_o[...] = lin(wu, bu)
    q = lin(wq, bq) * (dk ** -0.5)
    for tile in range(NSA_HEADS // 2):
        for half, piece in enumerate(halves(q, tile)):
            h = 2 * tile + half
            q_o[:, h * 128:(h + 1) * 128] = jnp.where(low, piece, aux_ref[h:h + 1, :]).astype(BF16)
    c_o[...] = lin(wc, bc)
    kv = lin(wk, bk)
    parts = [halves(kv, tile) for tile in range(4)]
    zero = jnp.zeros((tm, 128), F32)
    recs = []
    for g in range(NSA_KV_GROUPS):
        k_s, v_s, k_w, v_w = (jnp.where(low, parts[tile][g], 0.0) for tile in range(4))
        recs += [k_s, zero, v_s, k_w, v_w]
    pos0 = (pl.program_id(0) % (seq // tm)) * tm
    shape = (tm, NSA_KV_GROUPS * KV_REC)
    lane = lax.broadcasted_iota(jnp.int32, shape, 1)
    lane = jnp.where(lane >= KV_REC, lane - KV_REC, lane)
    feat = _position_features(pos0 + lax.broadcasted_iota(jnp.int32, shape, 0), lane)
    k_o[...] = (jnp.concatenate(recs, axis=1) + feat).astype(BF16)
    g_o[...] = lin(wg, bg)
    m_o[...] = jax.nn.sigmoid(lin(wm, bm)).astype(m_o.dtype)


def _stage_proj(x2, w_in, b_in, seq):
    n = x2.shape[0]
    tm = 256
    o0 = SSM_WIDTH
    o1 = o0 + NSA_HEADS * NSA_HEAD_DIM
    o2 = o1 + 6 * KV_WIDTH
    o3 = o2 + 3 * NSA_HEADS
    w_u, b_u = w_in[:, :o0], b_in[:o0]
    w_q, b_q = w_in[:, o0:o1], b_in[o0:o1]
    w_kv, b_kv = w_in[:, o1:o2], b_in[o1:o2]
    w_g, b_g = w_in[:, o2:o3], b_in[o2:o3]
    w_m, b_m = w_in[:, o3:], b_in[o3:]

    slope = 2.0 ** -(np.arange(NSA_HEADS, dtype=np.float32) + 1.0)
    q_aux = np.zeros((NSA_HEADS, 128), np.float32)
    q_aux[:, POS_HI] = slope * 64.0
    q_aux[:, POS_LO] = slope

    def pad_g(a):
        a = a.reshape(a.shape[:-1] + (NSA_KV_GROUPS, 3 * NSA_REP))
        a = jnp.concatenate([a, jnp.zeros(a.shape[:-1] + (128 - 3 * NSA_REP,), a.dtype)], axis=-1)
        return a.reshape(a.shape[:-2] + (NSA_KV_GROUPS * 128,))

    ws = [w_u, w_q, w_kv[:, :2 * KV_WIDTH], w_kv[:, 2 * KV_WIDTH:], pad_g(w_g), w_m]
    bs = [b_u, b_q, b_kv[:2 * KV_WIDTH], b_kv[2 * KV_WIDTH:], pad_g(b_g), b_m]
    odt = [F32, BF16, F32, BF16, F32, BF16]
    widths = [SSM_WIDTH, NSA_HEADS * 128, 2 * KV_WIDTH, NSA_KV_GROUPS * KV_REC, NSA_KV_GROUPS * 128, 2 * D_MODEL]
    args, in_specs = [x2], [pl.BlockSpec((tm, D_MODEL), lambda i: (i, 0))]
    for w, b in zip(ws, bs):
        args += [w.astype(BF16), b.reshape(1, -1).astype(F32)]
        in_specs += [_const_spec(w.shape), _const_spec((1, w.shape[1]))]
    args.append(jnp.asarray(q_aux))
    in_specs.append(_const_spec(q_aux.shape))
    out_shape = [jax.ShapeDtypeStruct((n, w), dt) for w, dt in zip(widths, odt)]
    out_specs = [pl.BlockSpec((tm, w), lambda i: (i, 0)) for w in widths]
    return pl.pallas_call(
        functools.partial(_proj_body, seq), grid=(n // tm,), in_specs=in_specs, out_specs=out_specs, out_shape=out_shape,
        compiler_params=_cparams(("parallel",)), name="in_proj")(*args)


def _ssm_weights(a_re, a_im, log_dt, b_re, b_im, c_re, c_im):
    L, G, P, H, A, J = SSM_CHUNK, SSM_GROUPS, SSM_STATE, SSM_GROUP, LANE_GROUPS, N_LANE_TILES
    lam = lax.complex(a_re.astype(F32), a_im.astype(F32))
    dt = jnp.exp(log_dt.astype(F32))[:, None]
    lam_bar = jnp.exp(lam * dt)
    b_bar = ((lam_bar - 1.0) / lam)[:, :, None] * lax.complex(b_re.astype(F32), b_im.astype(F32))
    c = lax.complex(c_re.astype(F32), c_im.astype(F32))
    k = jnp.arange(L + 1, dtype=F32)
    pw = jnp.exp((lam * dt)[None] * k[:, None, None])
    def spread(x, n_inner):
        src = np.arange(x.shape[2])
        dst = np.arange(x.shape[2] * A)
        sel = ((src[:, None] // n_inner == dst[None, :] // (A * n_inner))
               & (src[:, None] % n_inner == dst[None, :] % n_inner))
        return jnp.einsum('jrk,kc->jrc', x, jnp.asarray(sel, F32)), (dst % (A * n_inner)) // n_inner

    def group_mask(row_group, col_group):
        return jnp.asarray(row_group[:, None] == col_group[None, :], F32)

    in_rows = (np.arange(L * 128) % 128) // H
    st_rows = (np.arange(2 * A * P) % (A * P)) // P
    kern = jnp.real(jnp.einsum('ghp,kgp,gpj->kghj', c, pw[:L], b_bar))
    s_i, t_i = jnp.arange(L)[:, None], jnp.arange(L)[None, :]
    tau = jnp.clip(t_i - s_i, 0, L - 1)
    causal = (t_i >= s_i).astype(F32)
    k_st = (kern[tau] * causal[:, :, None, None, None]).reshape(L, L, J, A, H, H)
    k_st = k_st.transpose(2, 0, 3, 5, 1, 4).reshape(J, L * 128, L * H)
    w_intra, cg = spread(k_st, H)
    w_intra = w_intra * group_mask(in_rows, cg)
    q = (pw[:L][::-1][:, :, :, None] * b_bar[None]).reshape(L, J, A, P, H)
    q = jnp.stack([jnp.real(q), jnp.imag(q)], axis=0)
    q = q.transpose(2, 1, 3, 5, 0, 4).reshape(J, L * 128, 2 * P)
    m2, cg = spread(q, P)
    m2 = m2 * group_mask(in_rows, cg)
    cl = (c[None] * pw[1:][:, :, None, :]).reshape(L, J, A, H, P)
    cl = jnp.stack([jnp.real(cl), -jnp.imag(cl)], axis=0)
    cl = cl.transpose(2, 0, 3, 5, 1, 4).reshape(J, 2 * A * P, L * H)
    m1, cg = spread(cl, H)
    m1 = m1 * group_mask(st_rows, cg)
    w2 = jnp.concatenate([w_intra, m1], axis=1)
    lam8 = pw[L].reshape(G * P)
    return m2.astype(BF16), w2.astype(BF16), jnp.real(lam8), jnp.imag(lam8)


def _split_lane_tiles(src_ref, dst_sc):
    for j in range(dst_sc.shape[0]):
        dst_sc[j] = src_ref[:, j * 128:(j + 1) * 128]


def _chunk_inputs(u_sc, j, rows):
    return jnp.concatenate([u_sc[j, pl.ds(s, rows, stride=SSM_CHUNK), :] for s in range(SSM_CHUNK)], axis=1)


def _ssm_state_body(u_ref, m2_ref, zre_ref, zim_ref, u_sc):
    half = LANE_GROUPS * SSM_STATE
    _split_lane_tiles(u_ref, u_sc)
    for j in range(N_LANE_TILES):
        z = jnp.dot(_chunk_inputs(u_sc, j, zre_ref.shape[0]).astype(BF16), m2_ref[j], preferred_element_type=F32)
        zre_ref[:, j * half:(j + 1) * half] = z[:, :half]
        zim_ref[:, j * half:(j + 1) * half] = z[:, half:]


def _ssm_scan_body(zre_ref, zim_ref, lre_ref, lim_ref, xre_ref, xim_ref):
    lr, li = lre_ref[...], lim_ref[...]

    def step(c, carry):
        xr, xi = carry
        xre_ref[pl.ds(c, 1), :] = xr
        xim_ref[pl.ds(c, 1), :] = xi
        return lr * xr - li * xi + zre_ref[pl.ds(c, 1), :], lr * xi + li * xr + zim_ref[pl.ds(c, 1), :]

    zero = jnp.zeros(lr.shape, F32)
    lax.fori_loop(0, zre_ref.shape[0], step, (zero, zero))


def _ssm_out_body(u_ref, xre_ref, xim_ref, w2_ref, d_ref, wglu_ref, wup_ref, o_ref, u_sc, y_sc):
    half = LANE_GROUPS * SSM_STATE
    rows = xre_ref.shape[0]
    _split_lane_tiles(u_ref, u_sc)
    for j in range(N_LANE_TILES):
        lhs = jnp.concatenate(
            [_chunk_inputs(u_sc, j, rows), xre_ref[:, j * half:(j + 1) * half], xim_ref[:, j * half:(j + 1) * half]],
            axis=1).astype(BF16)
        yj = jnp.dot(lhs, w2_ref[j], preferred_element_type=F32)
        for t in range(SSM_CHUNK):
            y_sc[j, pl.ds(t, rows, stride=SSM_CHUNK), :] = yj[:, t * 128:(t + 1) * 128]
    y = jnp.concatenate([y_sc[j] for j in range(N_LANE_TILES)], axis=1) + d_ref[...] * u_ref[...]
    gl = jnp.dot(jax.nn.gelu(y).astype(BF16), wglu_ref[...], preferred_element_type=F32)
    v = gl[:, :SSM_WIDTH] * jax.nn.sigmoid(gl[:, SSM_WIDTH:])
    o_ref[...] = jnp.dot(v.astype(BF16), wup_ref[...], preferred_element_type=F32).astype(o_ref.dtype)


def _stage_ssm(u, bsz, seq, a_re, a_im, log_dt, b_re, b_im, c_re, c_im, d_skip, w_glu, w_up):
    n = bsz * seq
    L = SSM_CHUNK
    nck = seq // L
    gp = SSM_GROUPS * SSM_STATE
    m2, w2, l8re, l8im = _ssm_weights(a_re, a_im, log_dt, b_re, b_im, c_re, c_im)
    rs = min(nck, 512)
    zre, zim = pl.pallas_call(
        _ssm_state_body, grid=(bsz, nck // rs),
        in_specs=[pl.BlockSpec((rs * L, SSM_WIDTH), lambda b, i: (b * (nck // rs) + i, 0)),
                  _const_spec(m2.shape)],
        out_specs=[pl.BlockSpec((rs, gp), lambda b, i: (i, b))] * 2,
        out_shape=[jax.ShapeDtypeStruct((nck, bsz * gp), F32)] * 2,
        scratch_shapes=[pltpu.VMEM((N_LANE_TILES, rs * L, 128), F32)],
        compiler_params=_cparams(("parallel", "parallel")), name="ssm_chunk_state")(u, m2)
    cw = 1024
    ncol = bsz * gp // cw
    lre = jnp.tile(l8re, bsz).reshape(1, bsz * gp)
    lim = jnp.tile(l8im, bsz).reshape(1, bsz * gp)
    col = pl.BlockSpec((nck, cw), lambda i: (0, i))
    lam = pl.BlockSpec((1, cw), lambda i: (0, i))
    xre, xim = pl.pallas_call(
        _ssm_scan_body, grid=(ncol,), in_specs=[col, col, lam, lam], out_specs=[col, col],
        out_shape=[jax.ShapeDtypeStruct((nck, bsz * gp), F32)] * 2,
        compiler_params=_cparams(("parallel",)), name="ssm_carry_scan")(zre, zim, lre, lim)
    ro = min(nck, 128)
    d8 = d_skip.reshape(1, SSM_WIDTH).astype(F32)
    return pl.pallas_call(
        _ssm_out_body, grid=(bsz, nck // ro),
        in_specs=[pl.BlockSpec((ro * L, SSM_WIDTH), lambda b, i: (b * (nck // ro) + i, 0)),
                  pl.BlockSpec((ro, gp), lambda b, i: (i, b)),
                  pl.BlockSpec((ro, gp), lambda b, i: (i, b)),
                  _const_spec(w2.shape), _const_spec((1, SSM_WIDTH)),
                  _const_spec(w_glu.shape), _const_spec(w_up.shape)],
        out_specs=pl.BlockSpec((ro * L, D_MODEL), lambda b, i: (b * (nck // ro) + i, 0)),
        out_shape=jax.ShapeDtypeStruct((n, D_MODEL), BF16),
        scratch_shapes=[pltpu.VMEM((N_LANE_TILES, ro * L, 128), F32)] * 2,
        compiler_params=_cparams(("parallel", "parallel")), name="ssm_out")(
            u, xre, xim, w2, d8, w_glu.astype(BF16), w_up.astype(BF16))


def _compress_body(kv_ref, pea_ref, peb_ref, w1a_ref, w1b_ref, w2_ref, o_ref, kv_sc):
    nrow = o_ref.shape[0]
    _split_lane_tiles(kv_ref, kv_sc)
    f = jnp.concatenate([kv_sc[j, pl.ds(l, nrow, stride=CMP_STRIDE), :]
                         for l in range(CMP_STRIDE) for j in range(kv_sc.shape[0])], axis=1)
    a = jnp.dot((f + pea_ref[...]).astype(BF16), w1a_ref[...], preferred_element_type=F32)
    b = jnp.dot((f + peb_ref[...]).astype(BF16), w1b_ref[...], preferred_element_type=F32)
    pre = a + pltpu.roll(b, b.shape[0] - 1, 0)
    hid = jax.nn.gelu(pre)
    out = jnp.dot(hid.astype(BF16), w2_ref[...], preferred_element_type=F32)
    lane = lax.broadcasted_iota(jnp.int32, out.shape, 1) & 255
    c_end = lax.broadcasted_iota(jnp.int32, out.shape, 0) * CMP_STRIDE + (CMP_BLOCK - 1)
    feat = jnp.where(lane == POS_HI, c_end >> 6, 0) + jnp.where(lane == POS_LO, c_end & 63, 0)
    o_ref[...] = (out + feat.astype(F32)).astype(BF16)


def _stage_compress(kvc, bsz, seq, pe_k, w1_k, w2_k, pe_v, w1_v, w2_v):
    hl = CMP_STRIDE
    nrow = seq // hl
    G, dk, hid = NSA_KV_GROUPS, NSA_HEAD_DIM, CMP_HIDDEN
    eye = jnp.eye(2 * G, dtype=F32)

    def big_w1(lo):
        wk = w1_k.reshape(CMP_BLOCK, dk, hid)[lo:lo + hl]
        wv = w1_v.reshape(CMP_BLOCK, dk, hid)[lo:lo + hl]
        w = jnp.stack([wk, wk, wv, wv], axis=1)
        return jnp.einsum('lcdj,ce->lcdej', w, eye).reshape(hl * 2 * G * dk, 2 * G * hid)

    def big_pe(lo):
        pk, pv = pe_k[lo:lo + hl], pe_v[lo:lo + hl]
        return jnp.stack([pk, pk, pv, pv], axis=1).reshape(1, hl * 2 * G * dk)

    w2 = jnp.stack([w2_k, w2_k, w2_v, w2_v], axis=0)
    w2 = jnp.concatenate([w2, jnp.zeros((2 * G, hid, 128 - dk), w2.dtype)], axis=-1)
    slot = eye[np.array([kv * G + g for g in range(G) for kv in range(2)])].T
    w2 = jnp.einsum('cjd,ce->cjed', w2, slot).reshape(2 * G * hid, 2 * G * 128)
    width = hl * 2 * G * dk
    return pl.pallas_call(
        _compress_body, grid=(bsz,),
        in_specs=[pl.BlockSpec((seq, 2 * G * dk), lambda b: (b, 0)),
                  _const_spec((1, width)), _const_spec((1, width)),
                  _const_spec((width, 2 * G * hid)), _const_spec((width, 2 * G * hid)),
                  _const_spec((2 * G * hid, 2 * G * 128))],
        out_specs=pl.BlockSpec((nrow, 2 * G * 128), lambda b: (b, 0)),
        out_shape=jax.ShapeDtypeStruct((bsz * nrow, 2 * G * 128), BF16),
        scratch_shapes=[pltpu.VMEM((2 * G * dk // 128, seq, 128), F32)],
        compiler_params=_cparams(("parallel",)), name="nsa_compress")(
            kvc, big_pe(0).astype(F32), big_pe(hl).astype(F32),
            big_w1(0).astype(BF16), big_w1(hl).astype(BF16), w2.astype(BF16))


MASK_BIG = 2.0 ** 100
SEL_TILE = 512


def _nsa_body(q_ref, kcv_ref, kv_ref, g_ref, ov_ref, rel_ref, relc_ref, o_ref, acc, s_buf, p_buf, a_buf):
    i = pl.program_id(2)
    qs = i * Q_BLOCK
    seq = kv_ref.shape[0]
    rows = NSA_REP * Q_BLOCK
    dk = NSA_HEAD_DIM
    qh = [q_ref[:, r * 128:(r + 1) * 128] for r in range(NSA_REP)]
    q_all = jnp.concatenate(qh, axis=0)

    kc = kcv_ref[:, :128]
    vc = kcv_ref[:, 128:]
    vis_c = relc_ref[...] >= -qs
    s = jnp.where(vis_c, _nt_dot(q_all, kc), NEG)
    p = jnp.exp(s - jnp.max(s, axis=-1, keepdims=True)) * jnp.where(vis_c, 1.0, 0.0)
    p = (p / jnp.maximum(jnp.sum(p, axis=-1, keepdims=True), 1e-30)).astype(BF16)
    o_cmp = jnp.dot(p, vc, preferred_element_type=F32)

    wlen = min(WINDOW + Q_BLOCK, seq)
    w0 = pl.multiple_of(jnp.maximum(qs + Q_BLOCK - wlen, 0), Q_BLOCK)
    rel_w = rel_ref[...]
    s = _nt_dot(q_all, kv_ref[pl.ds(w0, wlen), KW_OFF:KW_OFF + 128])
    in_band = pltpu.bitcast(rel_w - (w0 - qs), jnp.uint32) < WINDOW
    s = jnp.where(in_band, s, NEG)
    p_w = jnp.exp(s - jnp.max(s, axis=-1, keepdims=True)).astype(BF16)
    a_win = jnp.dot(p_w, kv_ref[pl.ds(w0, wlen), VW_OFF:VW_OFF + 128], preferred_element_type=F32)

    p_heads = jnp.concatenate([p[r * Q_BLOCK:(r + 1) * Q_BLOCK] for r in range(NSA_REP)], axis=1)
    imp = jnp.dot(p_heads, ov_ref[...], preferred_element_type=F32)
    coli = lax.broadcasted_iota(jnp.int32, (Q_BLOCK, 128), 1)
    t_q = qs + lax.broadcasted_iota(jnp.int32, (Q_BLOCK, 128), 0)
    cur = t_q >> 6
    forced = (coli == 0) | (coli == cur) | (coli == cur - 1)
    score = jnp.where(coli * SEL_BLOCK <= t_q, imp + jnp.where(forced, FORCE_BONUS, 0.0), NEG)
    nsl = seq // SEL_BLOCK
    st = score.T[:nsl]
    groups = [st[8 * v:8 * v + 8] for v in range(nsl // 8)]
    jsub = lax.broadcasted_iota(jnp.int32, (8, Q_BLOCK), 0)
    rank = [jnp.zeros((8, Q_BLOCK), F32) for _ in groups]
    for k in range(nsl):
        rk = st[k:k + 1, :]
        for v, sv in enumerate(groups):
            if v > k // 8:
                one = jnp.where(rk >= sv, 1.0, 0.0)
            elif v < k // 8:
                one = jnp.where(rk > sv, 1.0, 0.0)
            else:
                one = jnp.where(rk > sv, 1.0, jnp.where(jsub > k % 8, jnp.where(rk == sv, 1.0, 0.0), 0.0))
            rank[v] = rank[v] + one
    drop_t = jnp.where(jnp.concatenate(rank, axis=0) < float(min(SEL_TOPK, nsl)), 0.0, -MASK_BIG)
    if nsl < 128:
        drop_t = jnp.concatenate([drop_t, jnp.zeros((128 - nsl, Q_BLOCK), F32)], axis=0)
    drop = drop_t.T.astype(BF16)
    q_sel = jnp.concatenate([jnp.concatenate([qh[r], drop], axis=1) for r in range(NSA_REP)], axis=0)

    n_tiles = (qs + Q_BLOCK + SEL_TILE - 1) // SEL_TILE

    def scores(kt):
        k0 = pl.multiple_of(kt * SEL_TILE, SEL_TILE)
        return _nt_dot(q_sel, kv_ref[pl.ds(k0, SEL_TILE), KS_OFF:KS_OFF + 256])

    def values(kt):
        k0 = pl.multiple_of(kt * SEL_TILE, SEL_TILE)
        return kv_ref[pl.ds(k0, SEL_TILE), VS_OFF:VS_OFF + 128]

    def trip(diagonal, kt, m_old):
        if not diagonal:
            s_next = scores(kt + 1)
        acc[...] = a_buf[...] * acc[...] + jnp.dot(p_buf[...], values(jnp.maximum(kt - 1, 0)),
                                                   preferred_element_type=F32)
        s = s_buf[...]
        if diagonal:
            s = jnp.where(rel_ref[:, :SEL_TILE] >= kt * SEL_TILE - qs, s, NEG)
        m_new = jnp.maximum(m_old, jnp.max(s, axis=-1, keepdims=True))
        p_buf[...] = jnp.exp(s - m_new).astype(BF16)
        a_buf[...] = jnp.exp(m_old - m_new)
        if not diagonal:
            s_buf[...] = s_next
        return m_new

    acc[...] = jnp.zeros_like(acc)
    p_buf[...] = jnp.zeros_like(p_buf)
    a_buf[...] = jnp.ones_like(a_buf)
    s_buf[...] = scores(0)
    m_s = lax.fori_loop(0, n_tiles - 1, functools.partial(trip, False), jnp.full((rows, 1), NEG, F32))
    trip(True, n_tiles - 1, m_s)
    a_sel = a_buf[...] * acc[...] + jnp.dot(p_buf[...], values(n_tiles - 1), preferred_element_type=F32)

    gate = jax.nn.sigmoid(g_ref[...])
    outs = []
    for r in range(NSA_REP):
        rr = slice(r * Q_BLOCK, (r + 1) * Q_BLOCK)
        o_s = a_sel[rr, :dk] / a_sel[rr, ONE_LANE:ONE_LANE + 1]
        o_w = a_win[rr, :dk] / a_win[rr, ONE_LANE:ONE_LANE + 1]
        outs.append(gate[:, 3 * r:3 * r + 1] * o_cmp[rr, :dk] + gate[:, 3 * r + 1:3 * r + 2] * o_s
                    + gate[:, 3 * r + 2:3 * r + 3] * o_w)
    o_ref[...] = jnp.concatenate(outs, axis=1).astype(o_ref.dtype)


def _stage_attn(qp, kcv, kv, gates, bsz, seq):
    n = bsz * seq
    nq = seq // Q_BLOCK
    ncmp = seq // CMP_STRIDE
    nsl = seq // SEL_BLOCK
    assert seq % SEL_TILE == 0
    c_start = np.arange(ncmp) * CMP_STRIDE
    s_start = np.arange(nsl) * SEL_BLOCK
    ov = ((c_start[:, None] < s_start[None, :] + SEL_BLOCK) & (c_start[:, None] + CMP_BLOCK > s_start[None, :]))
    ov = np.pad(ov.astype(np.float32), ((0, 0), (0, 128 - nsl)))
    ov[ncmp - 1] = 0.0
    ov = jnp.asarray(np.tile(ov, (NSA_REP, 1)), BF16)
    width = NSA_REP * NSA_HEAD_DIM
    rows = NSA_REP * Q_BLOCK
    wlen = min(WINDOW + Q_BLOCK, seq)
    assert wlen >= SEL_TILE
    offs = jnp.arange(rows, dtype=jnp.int32)[:, None] % Q_BLOCK
    rel = offs - jnp.arange(wlen, dtype=jnp.int32)[None, :]
    relc = offs - (jnp.arange(ncmp, dtype=jnp.int32)[None, :] * CMP_STRIDE + (CMP_BLOCK - 1))
    return pl.pallas_call(
        _nsa_body, grid=(bsz, NSA_KV_GROUPS, nq),
        in_specs=[pl.BlockSpec((Q_BLOCK, NSA_REP * 128), lambda b, g, i: (b * nq + i, g)),
                  pl.BlockSpec((ncmp, 256), lambda b, g, i: (b, g)),
                  pl.BlockSpec((seq, KV_REC), lambda b, g, i: (b, g)),
                  pl.BlockSpec((Q_BLOCK, 128), lambda b, g, i: (b * nq + i, g)),
                  _const_spec(ov.shape), _const_spec(rel.shape), _const_spec(relc.shape)],
        out_specs=pl.BlockSpec((Q_BLOCK, width), lambda b, g, i: (b * nq + i, g)),
        out_shape=jax.ShapeDtypeStruct((n, NSA_KV_GROUPS * width), BF16),
        scratch_shapes=[pltpu.VMEM((rows, 128), F32), pltpu.VMEM((rows, SEL_TILE), F32),
                        pltpu.VMEM((rows, SEL_TILE), BF16), pltpu.VMEM((rows, 1), F32)],
        compiler_params=_cparams(("parallel", "parallel", "arbitrary")), name="nsa_attention")(
            qp, kcv, kv, gates, ov, rel, relc)


def _merge_body(x_ref, ya_ref, ob_ref, gm_ref, wnsa_ref, wout_ref, g_ref, b_ref, o_ref):
    yb = jnp.dot(ob_ref[...], wnsa_ref[...], preferred_element_type=F32)
    gm = gm_ref[...].astype(F32)
    mix_in = gm[:, :D_MODEL] * ya_ref[...].astype(F32) + gm[:, D_MODEL:] * yb
    mix = jnp.dot(mix_in.astype(BF16), wout_ref[...], preferred_element_type=F32)
    o_ref[...] = _layer_norm(DN_ALPHA * x_ref[...] + mix, g_ref[...], b_ref[...])


def _stage_merge(x2, ya, ob, gm, w_up_nsa, w_out, ln_g, ln_b):
    n = x2.shape[0]
    tm = 512
    row = lambda w: pl.BlockSpec((tm, w), lambda i: (i, 0))
    return pl.pallas_call(
        _merge_body, grid=(n // tm,),
        in_specs=[row(D_MODEL), row(D_MODEL), row(ob.shape[1]), row(2 * D_MODEL),
                  _const_spec(w_up_nsa.shape), _const_spec(w_out.shape),
                  _const_spec((1, D_MODEL)), _const_spec((1, D_MODEL))],
        out_specs=row(D_MODEL), out_shape=jax.ShapeDtypeStruct((n, D_MODEL), F32),
        compiler_params=_cparams(("parallel",)), name="merge_ln")(
            x2, ya, ob, gm, w_up_nsa.astype(BF16), w_out.astype(BF16),
            ln_g.reshape(1, -1).astype(F32), ln_b.reshape(1, -1).astype(F32))


BIG_NEG = -3.0e38


def _top_rows(s, k):
    rid = lax.broadcasted_iota(jnp.int32, s.shape, 0).astype(F32)
    rank = jnp.full(s.shape, float(k), F32)
    vals, idxs = [], []
    for r in range(k):
        m = jnp.max(s, axis=0, keepdims=True)
        idx = jnp.min(jnp.where(s == m, rid, float(s.shape[0])), axis=0, keepdims=True)
        hit = rid == idx
        rank = jnp.where(hit, float(r), rank)
        s = jnp.where(hit, BIG_NEG, s)
        vals.append(m)
        idxs.append(idx)
    return jnp.concatenate(vals, axis=0), jnp.concatenate(idxs, axis=0), rank


def _max_rounds(s, k, want_round=True):
    rnd = jnp.full(s.shape, float(k), F32) if want_round else None
    vals = []
    for r in range(k):
        m = jnp.max(s, axis=0, keepdims=True)
        eq = s == m
        if want_round:
            rnd = jnp.where(eq, float(r), rnd)
        s = jnp.where(eq, BIG_NEG, s)
        vals.append(m)
    return jnp.concatenate(vals, axis=0), rnd


def _pair_candidates(a, b):
    c8 = lax.broadcasted_iota(jnp.int32, (8, a.shape[1]), 0)
    parts = [a[0:1] + b[0:8], a[0:1] + b[8:16]]
    for r in range(1, 8):
        keep = PEER_TOPK // (r + 1)
        cand = a[r:r + 1] + b[0:8]
        parts.append(cand if keep >= 8 else jnp.where(c8 < keep, cand, BIG_NEG))
    parts.append(a[8:16] + b[0:1])
    return jnp.concatenate(parts, axis=0)


def _picks_per_rank(picked):
    rows = [jnp.sum(picked[0:16], axis=0, keepdims=True)]
    rows += [jnp.sum(picked[8 * (r + 1):8 * (r + 2)], axis=0, keepdims=True) for r in range(1, 8)]
    return jnp.concatenate(rows + [picked[72:80]], axis=0)


def _spread_by_rank(rank, per_rank):
    out = jnp.zeros(rank.shape, F32)
    for r in range(PEER_TOPK):
        out = out + jnp.where(rank == float(r), per_rank[r:r + 1], 0.0)
    return out


def _sort16_network():
    pairs, p = [], 1
    while p < 16:
        k = p
        while k >= 1:
            for j in range(k % p, 16 - k, 2 * k):
                for i in range(min(k, 16 - j - k)):
                    if (i + j) // (2 * p) == (i + j + k) // (2 * p):
                        pairs.append((i + j, i + j + k))
            k //= 2
        p *= 2
    return pairs


_SORT16 = _sort16_network()


def _sorted_top16(s):
    def exchange(v, i, j):
        v[i], v[j] = jnp.maximum(v[i], v[j]), jnp.minimum(v[i], v[j])

    v = [s[8 * k:8 * k + 8] for k in range(16)]
    for i, j in _SORT16:
        exchange(v, i, j)
    for shift in (4, 2, 1):
        v = [jnp.maximum(v[k], pltpu.roll(v[15 - k], shift, 0)) for k in range(16)]
        for d in (8, 4, 2, 1):
            for i in range(16):
                if i & d == 0:
                    exchange(v, i, i + d)
    return jnp.concatenate([x[0:1] for x in v], axis=0)


def _count_rows(vals, s, strict):
    rows = [jnp.broadcast_to(vals[r:r + 1], (8, s.shape[1])) for r in range(vals.shape[0])]
    assert len(rows) == 16
    outs = []
    for c in range(0, s.shape[0], 8):
        sc = s[c:c + 8]
        test = (lambda row: row > sc) if strict else (lambda row: row <= sc)
        c1 = test(rows[7])
        c2 = test(jnp.where(c1, rows[11], rows[3]))
        c3 = test(jnp.where(c1, jnp.where(c2, rows[13], rows[9]), jnp.where(c2, rows[5], rows[1])))
        hi = jnp.where(c2, jnp.where(c3, rows[14], rows[12]), jnp.where(c3, rows[10], rows[8]))
        lo = jnp.where(c2, jnp.where(c3, rows[6], rows[4]), jnp.where(c3, rows[2], rows[0]))
        c4 = test(jnp.where(c1, hi, lo))
        acc = (jnp.where(c1, 8.0, 0.0) + jnp.where(c2, 4.0, 0.0) + jnp.where(c3, 2.0, 0.0) + jnp.where(c4, 1.0, 0.0)
               + jnp.where(test(rows[15]), 1.0, 0.0))
        outs.append(acc)
    return jnp.concatenate(outs, axis=0)


def _route_fast(s0, s1):
    k = PEER_TOPK
    count = lambda v: jnp.sum(v, axis=0, keepdims=True)
    a, b = _sorted_top16(s0), _sorted_top16(s1)
    cand = _pair_candidates(a, b)
    best, _ = _max_rounds(cand, k, want_round=False)
    hit = cand >= best[k - 1:k]
    z = jnp.sum(jnp.where(hit, jnp.exp(cand - best[0:1]), 0.0), axis=0, keepdims=True)
    nrank = _picks_per_rank(jnp.where(hit, 1.0, 0.0))
    earns = jnp.concatenate([jnp.min(jnp.where(nrank >= float(v), a, -BIG_NEG), axis=0, keepdims=True)
                             for v in range(1, k + 1)], axis=0)
    n_i = _count_rows(earns, s0, strict=False)
    ties = lambda t, s: (count(jnp.where(t[:-1] == t[1:], 1.0, 0.0))
                         + jnp.abs(count(jnp.where(s >= t[k - 1:k], 1.0, 0.0)) - k))
    bad = ties(a, s0) + ties(b, s1) + jnp.abs(count(nrank) - k)
    return n_i, jnp.exp(s0 - a[0:1]) / z, _count_rows(b, s1, strict=True), jnp.exp(s1 - b[0:1]), bad


def _route_exact(s0, s1):
    k = PEER_TOPK
    a, _, rk0 = _top_rows(s0, k)
    b, _, rk1 = _top_rows(s1, k)
    cand = jnp.concatenate([a[r:r + 1] + b for r in range(k)], axis=0)
    best, bidx, _ = _top_rows(cand, k)
    brank = jnp.floor(bidx * (1.0 / k))
    z = jnp.sum(jnp.exp(best - best[0:1]), axis=0, keepdims=True)
    nrank = jnp.concatenate([jnp.sum(jnp.where(brank == float(r), 1.0, 0.0), axis=0, keepdims=True)
                             for r in range(k)], axis=0)
    return _spread_by_rank(rk0, nrank), jnp.exp(s0 - a[0:1]) / z, rk1, jnp.exp(s1 - b[0:1])


def _gelu_tanh(x):
    c = math.sqrt(2.0 / math.pi)
    return x / (1.0 + jnp.exp(x * (x * x * (-2.0 * c * 0.044715) - 2.0 * c)))


def _peer_body(x_ref, wq_ref, sk_ref, u_ref, v_ref, g_ref, b_ref, o_ref,
               xt_sc, acc_sc, n_sc, e0_sc, rk1_sc, e1_sc, p_sc, st_sc):
    e = pl.program_id(1)
    nk = PEER_NKEYS

    @pl.when(e == 0)
    def _():
        xt = x_ref[...].T.astype(BF16)
        xt_sc[...] = xt
        acc_sc[...] = jnp.zeros_like(acc_sc)
        qt = jnp.dot(wq_ref[...], xt, preferred_element_type=F32).astype(BF16)
        half = sk_ref.shape[2]
        for c in range(sk_ref.shape[0]):
            st_sc[c * nk:(c + 1) * nk, :] = jnp.dot(sk_ref[c], qt[c * half:(c + 1) * half],
                                                    preferred_element_type=F32)

        def scores_of(h):
            return (st_sc[pl.ds(pl.multiple_of(2 * h * nk, nk), nk), :],
                    st_sc[pl.ds(pl.multiple_of((2 * h + 1) * nk, nk), nk), :])

        def put(h, n_i, e0, rk1, e1):
            n_sc[h] = n_i
            e0_sc[h] = e0
            rk1_sc[h] = rk1.astype(BF16)
            e1_sc[h] = e1.astype(BF16)

        @pl.loop(0, PEER_HEADS // 2)
        def _(hh):
            heads = (2 * hh, 2 * hh + 1)
            flags = []
            for h in heads:
                *routing, bad = _route_fast(*scores_of(h))
                put(h, *routing)
                flags.append(jnp.max(bad) > 0.0)
            for h, flag in zip(heads, flags):
                @pl.when(flag)
                def _():
                    put(h, *_route_exact(*scores_of(h)))

    rows = u_ref.shape[0] // nk
    ht = jnp.dot(u_ref[...], xt_sc[...], preferred_element_type=F32)
    act = _gelu_tanh(ht.astype(BF16))
    for ii in range(rows):
        w = jnp.zeros((nk, xt_sc.shape[1]), BF16)
        for h in range(PEER_HEADS):
            n_row = n_sc[h, pl.ds(e * rows + ii, 1), :].astype(BF16)
            e0_row = e0_sc[h, pl.ds(e * rows + ii, 1), :].astype(BF16)
            w = w + jnp.where(rk1_sc[h] < n_row, e0_row * e1_sc[h], 0)
        p_sc[ii * nk:(ii + 1) * nk, :] = w * act[ii * nk:(ii + 1) * nk]
    acc_sc[...] += _tn_dot(v_ref[...], p_sc[...])

    @pl.when(e == pl.num_programs(1) - 1)
    def _():
        o_ref[...] = _layer_norm(DN_ALPHA * x_ref[...] + acc_sc[...].T, g_ref[...], b_ref[...])


def _stage_peer(x1, w_q, subkeys, u_tab, v_tab, ln_g, ln_b, tt=512, et=512):
    n = x1.shape[0]
    tt = min(tt, n)
    H, nk, half = PEER_HEADS, PEER_NKEYS, subkeys.shape[-1]
    wq_t = w_q.T.astype(BF16)
    sk_t = subkeys.reshape(2 * H, nk, half).astype(BF16)
    ne = u_tab.shape[0]
    return pl.pallas_call(
        _peer_body, grid=(n // tt, ne // et),
        in_specs=[pl.BlockSpec((tt, D_MODEL), lambda t, e: (t, 0)),
                  _const_spec(wq_t.shape), _const_spec(sk_t.shape),
                  pl.BlockSpec((et, D_MODEL), lambda t, e: (e, 0)),
                  pl.BlockSpec((et, D_MODEL), lambda t, e: (e, 0)),
                  _const_spec((1, D_MODEL)), _const_spec((1, D_MODEL))],
        out_specs=pl.BlockSpec((tt, D_MODEL), lambda t, e: (t, 0)),
        out_shape=jax.ShapeDtypeStruct((n, D_MODEL), F32),
        scratch_shapes=[pltpu.VMEM((D_MODEL, tt), BF16), pltpu.VMEM((D_MODEL, tt), F32),
                        pltpu.VMEM((H, nk, tt), F32), pltpu.VMEM((H, nk, tt), F32),
                        pltpu.VMEM((H, nk, tt), BF16), pltpu.VMEM((H, nk, tt), BF16),
                        pltpu.VMEM((et, tt), BF16), pltpu.VMEM((2 * H * nk, tt), F32)],
        compiler_params=_cparams(("parallel", "arbitrary")), name="peer_ffn")(
            x1, wq_t, sk_t, u_tab.astype(BF16), v_tab.astype(BF16),
            ln_g.reshape(1, -1).astype(F32), ln_b.reshape(1, -1).astype(F32))


def _layer(x, w_in, b_in, ssm_a_re, ssm_a_im, ssm_log_dt, ssm_b_re, ssm_b_im, ssm_c_re, ssm_c_im, ssm_d,
           w_glu, w_up_ssm, nsa_pe_k, nsa_w1_k, nsa_w2_k, nsa_pe_v, nsa_w1_v, nsa_w2_v, w_up_nsa,
           w_out, ln1_g, ln1_b, peer_w_q, peer_subkeys, peer_u, peer_v, ln2_g, ln2_b):
    bsz, seq, _ = x.shape
    x2 = x.reshape(bsz * seq, D_MODEL)
    u, qp, kvc, kv, gates, gm = _stage_proj(x2, w_in, b_in, seq)
    ya = _stage_ssm(u, bsz, seq, ssm_a_re, ssm_a_im, ssm_log_dt, ssm_b_re, ssm_b_im, ssm_c_re, ssm_c_im,
                    ssm_d, w_glu, w_up_ssm)
    kcv = _stage_compress(kvc, bsz, seq, nsa_pe_k, nsa_w1_k, nsa_w2_k, nsa_pe_v, nsa_w1_v, nsa_w2_v)
    ob = _stage_attn(qp, kcv, kv, gates, bsz, seq)
    x1 = _stage_merge(x2, ya, ob, gm, w_up_nsa, w_out, ln1_g, ln1_b)
    out = _stage_peer(x1, peer_w_q, peer_subkeys, peer_u, peer_v, ln2_g, ln2_b)
    return out.reshape(bsz, seq, D_MODEL)


def kernel(x, w_in, b_in, ssm_a_re, ssm_a_im, ssm_log_dt, ssm_b_re, ssm_b_im, ssm_c_re, ssm_c_im, ssm_d, w_glu,
           w_up_ssm, nsa_pe_k, nsa_w1_k, nsa_w2_k, nsa_pe_v, nsa_w1_v, nsa_w2_v, w_up_nsa, w_out, ln1_g, ln1_b,
           peer_w_q, peer_subkeys, peer_u, peer_v, ln2_g, ln2_b):
    params = (w_in, b_in, ssm_a_re, ssm_a_im, ssm_log_dt, ssm_b_re, ssm_b_im, ssm_c_re, ssm_c_im, ssm_d, w_glu,
              w_up_ssm, nsa_pe_k, nsa_w1_k, nsa_w2_k, nsa_pe_v, nsa_w1_v, nsa_w2_v, w_up_nsa, w_out, ln1_g, ln1_b,
              peer_w_q, peer_subkeys, peer_u, peer_v, ln2_g, ln2_b)
    for layer in range(w_in.shape[0]):
        x = _layer(x, *[p[layer] for p in params])
    return x
```
